```python
import jax, jax.numpy as jnp
from jax import lax
import numpy as np

D_MODEL = 2048
BATCH = 2
SEQ = 8192
DEPTH = 1

CHUNK = 64
SB_HEADS = 8
SB_HEAD_DIM = 128
SB_WIDTH = SB_HEADS * SB_HEAD_DIM
Q_BLOCK = 128
SGU_GROUPS = 8
SGU_GROUP_DIM = 128
SGU_WIDTH = SGU_GROUPS * SGU_GROUP_DIM
SGU_BLOCK = 128
N_BRANCHES = 2
IN_COLS = 3 * SB_WIDTH + 2 * SGU_WIDTH + N_BRANCHES * D_MODEL
N_GROUPS = 4
EXPERTS_PER_GROUP = 8
N_EXPERTS = N_GROUPS * EXPERTS_PER_GROUP
TOP_K_INNER = 2
D_EXPERT = 512
EXPERT_BLOCK = 128
LN_EPS = 1e-5
DEEPNORM_ALPHA = (2.0 * DEPTH) ** 0.25
DEEPNORM_BETA = (8.0 * DEPTH) ** -0.25

kernel_name = "chunk_causal_hybrid_sb_sgu_hmoe_deepnorm"


def layer_norm(x, gain=None, bias=None):
    xf = x.astype(jnp.float32)
    mu = jnp.mean(xf, axis=-1, keepdims=True)
    var = jnp.mean(jnp.square(xf - mu), axis=-1, keepdims=True)
    y = (xf - mu) * lax.rsqrt(var + LN_EPS)
    if gain is not None:
        y = y * gain.astype(jnp.float32) + bias.astype(jnp.float32)
    return y.astype(x.dtype)


def modulate(x, shift, scale):
    return layer_norm(x) * (1 + scale[:, None, :]) + shift[:, None, :]


def stick_breaking_attention(q, k, v):
    b, h, s, dh = q.shape
    n_qb = s // Q_BLOCK
    kf = k.astype(jnp.float32)
    vf = v.astype(jnp.float32)
    qb = q.astype(jnp.float32).reshape(b, h, n_qb, Q_BLOCK, dh).transpose(2, 0, 1, 3, 4)
    key_pos = jnp.arange(s)
    scale = dh ** -0.5

    def one_block(args):
        q_i, blk = args
        z = jnp.einsum('bhqd,bhkd->bhqk', q_i, kf) * scale
        q_pos = blk * Q_BLOCK + jnp.arange(Q_BLOCK)
        causal = key_pos[None, :] < q_pos[:, None]
        log_keep = jnp.where(causal, jax.nn.log_sigmoid(-z), 0.0)
        between = lax.cumsum(log_keep, axis=3, reverse=True) - log_keep
        w = jnp.where(causal, jnp.exp(jax.nn.log_sigmoid(z) + between), 0.0)
        return jnp.einsum('bhqk,bhkd->bhqd', w, vf)

    out = lax.map(one_block, (qb, jnp.arange(n_qb)))
    return out.transpose(1, 0, 3, 2, 4).reshape(b, s, h * dh).astype(q.dtype)


def spatial_gating(u, v, w_s, b_s, ln_g, ln_b):
    b, s, _ = v.shape
    n_blk = s // SGU_BLOCK
    v = layer_norm(v, ln_g, ln_b)
    pos = jnp.arange(SGU_BLOCK)
    mask = (pos[None, :] // CHUNK) <= (pos[:, None] // CHUNK)
    w = jnp.where(mask[None], w_s, 0.0).astype(v.dtype)
    vb = v.reshape(b, n_blk, SGU_BLOCK, SGU_GROUPS, SGU_GROUP_DIM)
    mixed = jnp.einsum('gts,bnsgd->bntgd', w, vb) + b_s.T[None, None, :, :, None]
    return u * mixed.reshape(b, s, SGU_WIDTH)


def hierarchical_moe(h, w_group, b_group, w_router, b_router, w_gate, w_up, w_down):
    b, s, d = h.shape
    n = b * s
    xf = h.reshape(n, d)
    group_logits = (xf @ w_group).astype(jnp.float32) + b_group.astype(jnp.float32)
    group_prob = jax.nn.softmax(group_logits, axis=-1)
    _, g_sel = lax.top_k(group_logits, 1)
    p_group = jnp.take_along_axis(group_prob, g_sel, axis=1)
    expert_logits = ((xf @ w_router).astype(jnp.float32) + b_router.astype(jnp.float32)
                     ).reshape(n, N_GROUPS, EXPERTS_PER_GROUP)
    in_group = jnp.take_along_axis(expert_logits, g_sel[:, :, None], axis=1)[:, 0]
    top_val, top_idx = lax.top_k(in_group, TOP_K_INNER)
    weights = p_group * jax.nn.softmax(top_val, axis=-1)
    expert_id = g_sel * EXPERTS_PER_GROUP + top_idx

    a = n * TOP_K_INNER
    exp_flat = expert_id.reshape(a)
    tok_flat = jnp.arange(a) // TOP_K_INNER
    w_flat = weights.reshape(a)
    order = jnp.argsort(exp_flat)
    exp_sorted = exp_flat[order]
    tok_sorted = tok_flat[order]
    w_sorted = w_flat[order]
    counts = jax.ops.segment_sum(jnp.ones((a,), jnp.int32), exp_flat, num_segments=N_EXPERTS)
    padded = (counts + EXPERT_BLOCK - 1) // EXPERT_BLOCK * EXPERT_BLOCK
    starts = jnp.cumsum(counts) - counts
    pad_ends = jnp.cumsum(padded)
    pad_starts = pad_ends - padded
    dest = pad_starts[exp_sorted] + jnp.arange(a) - starts[exp_sorted]
    n_rows = a + N_EXPERTS * EXPERT_BLOCK
    n_blocks = n_rows // EXPERT_BLOCK
    row_tok = jnp.zeros((n_rows,), jnp.int32).at[dest].set(tok_sorted)
    row_w = jnp.zeros((n_rows,), w_sorted.dtype).at[dest].set(w_sorted)
    block_expert = jnp.minimum(
        jnp.searchsorted(pad_ends, jnp.arange(n_blocks) * EXPERT_BLOCK, side='right'),
        N_EXPERTS - 1)
    xs = xf[row_tok].reshape(n_blocks, EXPERT_BLOCK, d)

    def expert_block(args):
        xb, e = args
        hb = jax.nn.silu(xb @ w_gate[e]) * (xb @ w_up[e])
        return hb @ w_down[e]

    ys = lax.map(expert_block, (xs, block_expert)).reshape(n_rows, d)
    out = jax.ops.segment_sum(ys * row_w[:, None].astype(ys.dtype), row_tok, num_segments=n)
    return out.reshape(b, s, d)


def setup_inputs(seed: int = 0) -> dict:
    key = jax.random.key(seed)
    ks = jax.random.split(key, 24)
    f32 = jnp.float32
    d = D_MODEL

    def nrm(k, shape, scale):
        return jax.random.normal(k, shape, f32) * scale

    return {
        "x": nrm(ks[0], (BATCH, SEQ, d), 1.0),
        "c": nrm(ks[1], (BATCH, d), 1.0),
        "w_ada": nrm(ks[2], (DEPTH, d, 6 * d), d ** -0.5),
        "b_ada": nrm(ks[3], (DEPTH, 6 * d), 0.01),
        "w_in": nrm(ks[4], (DEPTH, d, IN_COLS), d ** -0.5),
        "sgu_w": nrm(ks[5], (DEPTH, SGU_GROUPS, SGU_BLOCK, SGU_BLOCK), 0.5 * SGU_BLOCK ** -0.5),
        "sgu_b": 1.0 + nrm(ks[6], (DEPTH, SGU_GROUPS, SGU_BLOCK), 0.02),
        "sgu_ln_g": 1.0 + nrm(ks[7], (DEPTH, SGU_WIDTH), 0.02),
        "sgu_ln_b": nrm(ks[8], (DEPTH, SGU_WIDTH), 0.02),
        "w_proj_a": nrm(ks[9], (DEPTH, SB_WIDTH, d), SB_WIDTH ** -0.5),
        "w_proj_b": nrm(ks[10], (DEPTH, SGU_WIDTH, d), SGU_WIDTH ** -0.5),
        "w_out": nrm(ks[11], (DEPTH, d, d), DEEPNORM_BETA * d ** -0.5),
        "ln1_g": 1.0 + nrm(ks[12], (DEPTH, d), 0.02),
        "ln1_b": nrm(ks[13], (DEPTH, d), 0.02),
        "w_group": nrm(ks[14], (DEPTH, d, N_GROUPS), d ** -0.5),
        "b_group": nrm(ks[15], (DEPTH, N_GROUPS), 0.01),
        "w_router": nrm(ks[16], (DEPTH, d, N_EXPERTS), d ** -0.5),
        "b_router": nrm(ks[17], (DEPTH, N_EXPERTS), 0.01),
        "w_gate": nrm(ks[18], (DEPTH, N_EXPERTS, d, D_EXPERT), d ** -0.5),
        "w_up": nrm(ks[19], (DEPTH, N_EXPERTS, d, D_EXPERT), d ** -0.5),
        "w_down": nrm(ks[20], (DEPTH, N_EXPERTS, D_EXPERT, d), DEEPNORM_BETA * D_EXPERT ** -0.5),
        "ln2_g": 1.0 + nrm(ks[21], (DEPTH, d), 0.02),
        "ln2_b": nrm(ks[22], (DEPTH, d), 0.02),
    }


def reference(x, c, w_ada, b_ada, w_in, sgu_w, sgu_b, sgu_ln_g, sgu_ln_b, w_proj_a, w_proj_b,
              w_out, ln1_g, ln1_b, w_group, b_group, w_router, b_router, w_gate, w_up, w_down,
              ln2_g, ln2_b):
    b, s, d = x.shape
    split_at = [SB_WIDTH, 2 * SB_WIDTH, 3 * SB_WIDTH, 3 * SB_WIDTH + SGU_WIDTH,
                3 * SB_WIDTH + 2 * SGU_WIDTH, 3 * SB_WIDTH + 2 * SGU_WIDTH + d]
    c_act = jax.nn.silu(c)
    for l in range(DEPTH):
        mod = c_act @ w_ada[l] + b_ada[l]
        shift1, scale1, gate1, shift2, scale2, gate2 = jnp.split(mod, 6, axis=-1)

        h = modulate(x, shift1, scale1)
        proj = h @ w_in[l]
        q, k, v, su, sv, ga, gb = jnp.split(proj, split_at, axis=-1)
        to_heads = lambda t: t.reshape(b, s, SB_HEADS, SB_HEAD_DIM).transpose(0, 2, 1, 3)
        out_a = stick_breaking_attention(to_heads(q), to_heads(k), to_heads(v))
        out_b = spatial_gating(jax.nn.gelu(su, approximate=False), jax.nn.gelu(sv, approximate=False),
                               sgu_w[l], sgu_b[l], sgu_ln_g[l], sgu_ln_b[l])
        merged = jax.nn.sigmoid(ga) * (out_a @ w_proj_a[l]) + jax.nn.sigmoid(gb) * (out_b @ w_proj_b[l])
        y = merged @ w_out[l]
        x = layer_norm(DEEPNORM_ALPHA * x + gate1[:, None, :] * y, ln1_g[l], ln1_b[l])

        h = modulate(x, shift2, scale2)
        y = hierarchical_moe(h, w_group[l], b_group[l], w_router[l], b_router[l],
                             w_gate[l], w_up[l], w_down[l])
        x = layer_norm(DEEPNORM_ALPHA * x + gate2[:, None, :] * y, ln2_g[l], ln2_b[l])
    return x
```

```python
import functools
import math

import jax
import jax.numpy as jnp
from jax import lax
from jax.experimental import pallas as pl
from jax.experimental.pallas import tpu as pltpu

F32 = jnp.float32
BF16 = jnp.bfloat16
I32 = jnp.int32

CHUNK = 64
SB_HEADS = 8
SB_HEAD_DIM = 128
SB_WIDTH = SB_HEADS * SB_HEAD_DIM
SGU_GROUPS = 8
SGU_GROUP_DIM = 128
SGU_WIDTH = SGU_GROUPS * SGU_GROUP_DIM
SGU_BLOCK = 128
N_GROUPS = 4
EXPERTS_PER_GROUP = 8
N_EXPERTS = N_GROUPS * EXPERTS_PER_GROUP
D_EXPERT = 512
LN_EPS = 1e-5

LANES = 128
SUBLANES = 8
VMEM_LIMIT_BYTES = 60000 * 1024

GROUP_LANE0 = 0
EXPERT_LANE0 = N_GROUPS

LOG2E = 1.4426950408889634
NEG_BIG = -1e30

_COL_GA = 0


def _COL_GB(d):
    return d


def _COL_Q(d):
    return 2 * d


def _COL_K(d):
    return 2 * d + SB_WIDTH


def _COL_V(d):
    return 2 * d + 2 * SB_WIDTH


def _COL_SU(d):
    return 2 * d + 3 * SB_WIDTH


def _COL_SV(d):
    return 2 * d + 3 * SB_WIDTH + SGU_WIDTH


def _cparams(*sem):
    return pltpu.CompilerParams(dimension_semantics=sem, vmem_limit_bytes=VMEM_LIMIT_BYTES)


def _ln_rows(x):
    mu = jnp.mean(x, axis=-1, keepdims=True)
    xc = x - mu
    var = jnp.mean(xc * xc, axis=-1, keepdims=True)
    return xc * lax.rsqrt(var + LN_EPS)


def _sigmoid(x):
    return 1.0 / (1.0 + jnp.exp(-x))


def _ada_kernel(c_ref, w_ref, b_ref, o_ref):
    c = c_ref[...]
    ca = c * _sigmoid(c)
    o_ref[...] = jnp.dot(ca, w_ref[...], preferred_element_type=F32,
                         precision=lax.Precision.HIGHEST) + b_ref[...]


def _ada(c, w_ada, b_ada, tn=1024):
    b, d = c.shape
    n_out = w_ada.shape[1]
    cp = jnp.zeros((SUBLANES, d), F32).at[:b].set(c)
    out = pl.pallas_call(
        _ada_kernel,
        grid=(n_out // tn,),
        in_specs=[pl.BlockSpec((SUBLANES, d), lambda j: (0, 0)),
                  pl.BlockSpec((d, tn), lambda j: (0, j)),
                  pl.BlockSpec((1, tn), lambda j: (0, j))],
        out_specs=pl.BlockSpec((SUBLANES, tn), lambda j: (0, j)),
        out_shape=jax.ShapeDtypeStruct((SUBLANES, n_out), F32),
        compiler_params=_cparams("arbitrary"),
        name="ada",
    )(cp, w_ada, b_ada.reshape(1, n_out))
    return out[:b]


def _inproj_kernel(x_ref, sh_ref, sc_ref, w_ref, cs_ref, o_ref, h_ref):
    @pl.when(pl.program_id(1) == 0)
    def _():
        h = _ln_rows(x_ref[...]) * (1.0 + sc_ref[0]) + sh_ref[0]
        h_ref[...] = h.astype(BF16)

    acc = jnp.dot(h_ref[...], w_ref[...], preferred_element_type=F32)
    o_ref[...] = (acc * cs_ref[...]).astype(BF16)


def _inproj(x2, mod3, w_in_bf, colscale, seq, tm=1024, tn=1024):
    n, d = x2.shape
    n_cols = w_in_bf.shape[1]
    tm = min(tm, seq)
    return pl.pallas_call(
        _inproj_kernel,
        grid=(n // tm, n_cols // tn),
        in_specs=[pl.BlockSpec((tm, d), lambda i, j: (i, 0)),
                  pl.BlockSpec((1, 1, d), lambda i, j: (((i * tm) // seq) * 6 + 0, 0, 0)),
                  pl.BlockSpec((1, 1, d), lambda i, j: (((i * tm) // seq) * 6 + 1, 0, 0)),
                  pl.BlockSpec((d, tn), lambda i, j: (0, j)),
                  pl.BlockSpec((1, tn), lambda i, j: (0, j))],
        out_specs=pl.BlockSpec((tm, tn), lambda i, j: (i, j)),
        out_shape=jax.ShapeDtypeStruct((n, n_cols), BF16),
        scratch_shapes=[pltpu.VMEM((tm, d), BF16)],
        compiler_params=_cparams("arbitrary", "arbitrary"),
        name="inproj",
    )(x2, mod3, mod3, w_in_bf, colscale)


def _attn_kernel(q_ref, k_ref, v_ref, o_ref, *, tb):
    qi = pl.program_id(2)
    q = q_ref[0]
    row = lax.broadcasted_iota(I32, (tb, tb), 0)
    col = lax.broadcasted_iota(I32, (tb, tb), 1)
    tri = (row >= col).astype(BF16)
    causal = col < row

    def block(j, run, acc, masked):
        off = pl.multiple_of(j * tb, tb)
        kb = k_ref[0, pl.ds(off, tb), :]
        vb = v_ref[0, pl.ds(off, tb), :]
        z = lax.dot_general(q, kb, (((1,), (1,)), ((), ())), preferred_element_type=F32)
        sp = jnp.maximum(z, 0.0) + jnp.log2(1.0 + jnp.exp2(-jnp.abs(z)))
        if masked:
            sp = jnp.where(causal, sp, 0.0)
        hi = sp.astype(BF16)
        lo = (sp - hi.astype(F32)).astype(BF16)
        cum = (jnp.dot(hi, tri, preferred_element_type=F32)
               + jnp.dot(lo, tri, preferred_element_type=F32)) + run
        w = jnp.exp2(z - cum)
        if masked:
            w = jnp.where(causal, w, 0.0)
        acc = acc + jnp.dot(w.astype(BF16), vb, preferred_element_type=F32)
        return cum[:, 0:1], acc

    run0 = jnp.zeros((tb, 1), F32)
    acc0 = jnp.zeros((tb, SB_HEAD_DIM), F32)
    run, acc = block(qi, run0, acc0, True)

    def body(t, carry):
        return block(qi - 1 - t, carry[0], carry[1], False)

    run, acc = lax.fori_loop(0, qi, body, (run, acc))
    o_ref[0] = acc.astype(BF16)


def _attention(proj3, d, tb=256):
    b, s, _ = proj3.shape
    tb = min(tb, s)
    qc = _COL_Q(d) // SB_HEAD_DIM
    kc = _COL_K(d) // SB_HEAD_DIM
    vc = _COL_V(d) // SB_HEAD_DIM
    return pl.pallas_call(
        functools.partial(_attn_kernel, tb=tb),
        grid=(b, SB_HEADS, s // tb),
        in_specs=[pl.BlockSpec((1, tb, SB_HEAD_DIM), lambda bi, h, qi: (bi, qi, qc + h)),
                  pl.BlockSpec((1, s, SB_HEAD_DIM), lambda bi, h, qi: (bi, 0, kc + h)),
                  pl.BlockSpec((1, s, SB_HEAD_DIM), lambda bi, h, qi: (bi, 0, vc + h))],
        out_specs=pl.BlockSpec((1, tb, SB_HEAD_DIM), lambda bi, h, qi: (bi, qi, h)),
        out_shape=jax.ShapeDtypeStruct((b, s, SB_WIDTH), BF16),
        compiler_params=_cparams("arbitrary", "arbitrary", "arbitrary"),
        name="attn",
    )(proj3, proj3, proj3)


def _erf(x):
    return lax.erf(x)


def _gelu(x):
    return 0.5 * x * (1.0 + _erf(x * (1.0 / math.sqrt(2.0))))


def _mix_kernel(oa_ref, su_ref, sv_ref, ga_ref, gb_ref, x_ref,
                sw_ref, sbias_ref, sg_ref, sb_ref,
                wpa_ref, wpb_ref, wo_ref,
                g1_ref, ln1g_ref, ln1b_ref, sh2_ref, sc2_ref,
                wr_ref, br_ref,
                x1_ref, h2_ref, route_ref, oh1_ref, oh2_ref, *, tm, alpha):
    u = _gelu(su_ref[...].astype(F32))
    v = _gelu(sv_ref[...].astype(F32))
    v = _ln_rows(v) * sg_ref[...] + sb_ref[...]
    vb = v.astype(BF16)
    pos_t = lax.broadcasted_iota(I32, (SGU_BLOCK, SGU_BLOCK), 0)
    pos_s = lax.broadcasted_iota(I32, (SGU_BLOCK, SGU_BLOCK), 1)
    chunk_causal = (pos_s // CHUNK) <= (pos_t // CHUNK)
    blocks = []
    for nb in range(tm // SGU_BLOCK):
        r0 = nb * SGU_BLOCK
        cols = []
        for g in range(SGU_GROUPS):
            c0 = g * SGU_GROUP_DIM
            wg = jnp.where(chunk_causal, sw_ref[g], 0.0).astype(BF16)
            cols.append(jnp.dot(wg, vb[r0:r0 + SGU_BLOCK, c0:c0 + SGU_GROUP_DIM],
                                preferred_element_type=F32))
        blocks.append(jnp.concatenate(cols, axis=1) + sbias_ref[...])
    mixed = jnp.concatenate(blocks, axis=0)
    out_b = (u * mixed).astype(BF16)

    pa = jnp.dot(oa_ref[...], wpa_ref[...], preferred_element_type=F32)
    pb = jnp.dot(out_b, wpb_ref[...], preferred_element_type=F32)
    merged = (_sigmoid(ga_ref[...].astype(F32)) * pa
              + _sigmoid(gb_ref[...].astype(F32)) * pb)
    y = jnp.dot(merged.astype(BF16), wo_ref[...], preferred_element_type=F32)

    r = alpha * x_ref[...] + g1_ref[0] * y
    x1 = _ln_rows(r) * ln1g_ref[...] + ln1b_ref[...]
    x1_ref[...] = x1
    h2 = _ln_rows(x1) * (1.0 + sc2_ref[0]) + sh2_ref[0]
    h2_ref[...] = h2.astype(BF16)

    logits = jnp.dot(h2, wr_ref[...], preferred_element_type=F32,
                     precision=lax.Precision.HIGHEST) + br_ref[...]
    lane = lax.broadcasted_iota(I32, logits.shape, 1)
    gl = jnp.where(lane < N_GROUPS, logits, NEG_BIG)
    gmax = jnp.max(gl, axis=1, keepdims=True)
    g_sel = jnp.min(jnp.where(gl == gmax, lane, LANES), axis=1, keepdims=True)
    p_group = 1.0 / jnp.sum(jnp.exp(gl - gmax), axis=1, keepdims=True)
    lo_lane = EXPERT_LANE0 + g_sel * EXPERTS_PER_GROUP
    in_group = (lane >= lo_lane) & (lane < lo_lane + EXPERTS_PER_GROUP)
    el = jnp.where(in_group, logits, NEG_BIG)
    m1 = jnp.max(el, axis=1, keepdims=True)
    i1 = jnp.min(jnp.where(el == m1, lane, LANES), axis=1, keepdims=True)
    el2 = jnp.where(lane == i1, NEG_BIG, el)
    m2 = jnp.max(el2, axis=1, keepdims=True)
    i2 = jnp.min(jnp.where(el2 == m2, lane, LANES), axis=1, keepdims=True)
    e21 = jnp.exp(m2 - m1)
    w1 = p_group / (1.0 + e21)
    w2 = p_group * e21 / (1.0 + e21)
    route = jnp.where(lane == 0, w1, 0.0)
    route = jnp.where(lane == 1, w2, route)
    route = jnp.where(lane == 2, (i1 - EXPERT_LANE0).astype(F32), route)
    route = jnp.where(lane == 3, (i2 - EXPERT_LANE0).astype(F32), route)
    route_ref[...] = route
    oh1_ref[...] = jnp.where(lane == i1, 1.0, 0.0).astype(BF16)
    oh2_ref[...] = jnp.where(lane == i2, 1.0, 0.0).astype(BF16)


def _mix(out_a, proj, x2, mod3, sgu_w, sgu_bias_full, sgu_g, sgu_b, wpa, wpb, wo,
         ln1_g, ln1_b, wr, br, seq, alpha, tm=256):
    n, d = x2.shape
    tm = min(tm, seq)
    def pcol(start, width):
        assert start % width == 0
        return start // width
    su_c = pcol(_COL_SU(d), SGU_WIDTH)
    sv_c = pcol(_COL_SV(d), SGU_WIDTH)
    ga_c = pcol(_COL_GA, d)
    gb_c = pcol(_COL_GB(d), d)
    bidx = lambda i: (i * tm) // seq
    row = lambda i: (i, 0)
    const2 = lambda i: (0, 0)
    const3 = lambda i: (0, 0, 0)
    return pl.pallas_call(
        functools.partial(_mix_kernel, tm=tm, alpha=alpha),
        grid=(n // tm,),
        in_specs=[pl.BlockSpec((tm, SB_WIDTH), row),
                  pl.BlockSpec((tm, SGU_WIDTH), lambda i: (i, su_c)),
                  pl.BlockSpec((tm, SGU_WIDTH), lambda i: (i, sv_c)),
                  pl.BlockSpec((tm, d), lambda i: (i, ga_c)),
                  pl.BlockSpec((tm, d), lambda i: (i, gb_c)),
                  pl.BlockSpec((tm, d), row),
                  pl.BlockSpec((SGU_GROUPS, SGU_BLOCK, SGU_BLOCK), const3),
                  pl.BlockSpec((SGU_BLOCK, SGU_WIDTH), const2),
                  pl.BlockSpec((1, SGU_WIDTH), const2),
                  pl.BlockSpec((1, SGU_WIDTH), const2),
                  pl.BlockSpec((SB_WIDTH, d), const2),
                  pl.BlockSpec((SGU_WIDTH, d), const2),
                  pl.BlockSpec((d, d), const2),
                  pl.BlockSpec((1, 1, d), lambda i: (bidx(i) * 6 + 2, 0, 0)),
                  pl.BlockSpec((1, d), const2),
                  pl.BlockSpec((1, d), const2),
                  pl.BlockSpec((1, 1, d), lambda i: (bidx(i) * 6 + 3, 0, 0)),
                  pl.BlockSpec((1, 1, d), lambda i: (bidx(i) * 6 + 4, 0, 0)),
                  pl.BlockSpec((d, LANES), const2),
                  pl.BlockSpec((1, LANES), const2)],
        out_specs=[pl.BlockSpec((tm, d), row),
                   pl.BlockSpec((tm, d), row),
                   pl.BlockSpec((tm, LANES), row),
                   pl.BlockSpec((tm, LANES), row),
                   pl.BlockSpec((tm, LANES), row)],
        out_shape=[jax.ShapeDtypeStruct((n, d), F32),
                   jax.ShapeDtypeStruct((n, d), BF16),
                   jax.ShapeDtypeStruct((n, LANES), F32),
                   jax.ShapeDtypeStruct((n, LANES), BF16),
                   jax.ShapeDtypeStruct((n, LANES), BF16)],
        compiler_params=_cparams("arbitrary"),
        name="mix",
    )(out_a, proj, proj, proj, proj, x2, sgu_w, sgu_bias_full, sgu_g, sgu_b,
      wpa, wpb, wo, mod3, ln1_g, ln1_b, mod3, mod3, wr, br)


def _plan_kernel(oh1_ref, oh2_ref, dest_ref, blk_ref, cnt_ref, start_ref, *, tm, bm, nblk_pad):
    phase = pl.program_id(0)
    i = pl.program_id(1)
    oh1 = oh1_ref[...]
    oh2 = oh2_ref[...]
    both = oh1 + oh2
    lane = lax.broadcasted_iota(I32, (SUBLANES, LANES), 1)

    @pl.when((phase == 0) & (i == 0))
    def _():
        cnt_ref[...] = jnp.zeros_like(cnt_ref)

    @pl.when(phase == 0)
    def _():
        ones = jnp.ones((SUBLANES, tm), BF16)
        cnt_ref[...] += jnp.dot(ones, both, preferred_element_type=F32)

    @pl.when((phase == 1) & (i == 0))
    def _():
        cnt = cnt_ref[...]
        padded = jnp.floor((cnt + (bm - 1)) * (1.0 / bm)) * bm
        r = lax.broadcasted_iota(I32, (LANES, LANES), 0)
        c = lax.broadcasted_iota(I32, (LANES, LANES), 1)
        upper = (r < c).astype(F32)
        starts = jnp.dot(padded, upper, preferred_element_type=F32,
                         precision=lax.Precision.HIGHEST)
        start_ref[...] = starts
        cnt_ref[...] = jnp.zeros_like(cnt_ref)
        ends = starts + padded
        is_exp = (lane[0:1] >= EXPERT_LANE0) & (lane[0:1] < EXPERT_LANE0 + N_EXPERTS)
        bstart = (lax.broadcasted_iota(I32, (nblk_pad, LANES), 0) * bm).astype(F32)
        done = jnp.where(is_exp & (ends[0:1] <= bstart), 1.0, 0.0)
        bexp = jnp.minimum(jnp.sum(done, axis=1, keepdims=True), N_EXPERTS - 1.0)
        total = jnp.sum(jnp.where(is_exp, padded[0:1], 0.0), axis=1, keepdims=True)
        blane = lax.broadcasted_iota(I32, (nblk_pad, LANES), 1)
        blk = jnp.where(blane == 0, bexp, jnp.where(blane == 1, total * (1.0 / bm), 0.0))
        blk_ref[...] = blk.astype(I32)

    @pl.when(phase == 1)
    def _():
        r = lax.broadcasted_iota(I32, (tm, tm), 0)
        c = lax.broadcasted_iota(I32, (tm, tm), 1)
        lower = (c < r).astype(BF16)
        before = jnp.dot(lower, both, preferred_element_type=F32) + cnt_ref[0:1]
        pos = before + start_ref[0:1]
        d1 = jnp.sum(pos * oh1.astype(F32), axis=1, keepdims=True)
        d2 = jnp.sum(pos * oh2.astype(F32), axis=1, keepdims=True)
        dlane = lax.broadcasted_iota(I32, (tm, LANES), 1)
        dest = jnp.where(dlane == 0, d1, jnp.where(dlane == 1, d2, 0.0))
        dest_ref[...] = dest.astype(I32)
        ones = jnp.ones((SUBLANES, tm), BF16)
        cnt_ref[...] += jnp.dot(ones, both, preferred_element_type=F32)


def _plan(oh1, oh2, bm, nblk, tm=512):
    n = oh1.shape[0]
    tm = min(tm, n)
    nblk_pad = -(-nblk // SUBLANES) * SUBLANES
    dest, blk = pl.pallas_call(
        functools.partial(_plan_kernel, tm=tm, bm=bm, nblk_pad=nblk_pad),
        grid=(2, n // tm),
        in_specs=[pl.BlockSpec((tm, LANES), lambda p, i: (i, 0)),
                  pl.BlockSpec((tm, LANES), lambda p, i: (i, 0))],
        out_specs=[pl.BlockSpec((tm, LANES), lambda p, i: (i * p, 0)),
                   pl.BlockSpec((nblk_pad, LANES), lambda p, i: (0, 0))],
        out_shape=[jax.ShapeDtypeStruct((n, LANES), I32),
                   jax.ShapeDtypeStruct((nblk_pad, LANES), I32)],
        scratch_shapes=[pltpu.VMEM((SUBLANES, LANES), F32),
                        pltpu.VMEM((SUBLANES, LANES), F32)],
        compiler_params=_cparams("arbitrary", "arbitrary"),
        name="plan",
    )(oh1, oh2)
    return dest, blk


def _dispatch_kernel(d1_ref, d2_ref, h_ref, xs_in_ref, xs_ref, sem, *, tm):
    del xs_in_ref
    base = pl.program_id(0) * tm

    def row_copy(t, dref):
        return pltpu.make_async_copy(h_ref.at[t], xs_ref.at[dref[base + t]], sem)

    def issue(t, carry):
        row_copy(t, d1_ref).start()
        row_copy(t, d2_ref).start()
        return carry

    def drain(t, carry):
        row_copy(t, d1_ref).wait()
        row_copy(t, d2_ref).wait()
        return carry

    lax.fori_loop(0, tm, issue, 0)
    lax.fori_loop(0, tm, drain, 0)


def _dispatch(dest1, dest2, h2r, n_rows, tm=256):
    n, sub, lanes = h2r.shape
    tm = min(tm, n)
    xs0 = jnp.zeros((n_rows, sub, lanes), h2r.dtype)
    return pl.pallas_call(
        functools.partial(_dispatch_kernel, tm=tm),
        grid_spec=pltpu.PrefetchScalarGridSpec(
            num_scalar_prefetch=2,
            grid=(n // tm,),
            in_specs=[pl.BlockSpec((tm, sub, lanes), lambda i, d1, d2: (i, 0, 0)),
                      pl.BlockSpec(memory_space=pl.ANY)],
            out_specs=pl.BlockSpec(memory_space=pl.ANY),
            scratch_shapes=[pltpu.SemaphoreType.DMA(())],
        ),
        out_shape=jax.ShapeDtypeStruct((n_rows, sub, lanes), h2r.dtype),
        input_output_aliases={3: 0},
        compiler_params=_cparams("arbitrary"),
        name="dispatch",
    )(dest1, dest2, h2r, xs0)


def _experts_kernel(be_ref, na_ref, x_ref, wg_ref, wu_ref, wd_ref, o_ref, wgb, wub, wdb):
    i = pl.program_id(0)

    @pl.when(i < na_ref[0])
    def _():
        prev = be_ref[jnp.maximum(i - 1, 0)]

        @pl.when((i == 0) | (be_ref[i] != prev))
        def _():
            wgb[...] = wg_ref[...].astype(BF16)
            wub[...] = wu_ref[...].astype(BF16)
            wdb[...] = wd_ref[...].astype(BF16)

        x = x_ref[...]
        g = jnp.dot(x, wgb[...], preferred_element_type=F32)
        u = jnp.dot(x, wub[...], preferred_element_type=F32)
        hb = (g * _sigmoid(g) * u).astype(BF16)
        o_ref[...] = jnp.dot(hb, wdb[...], preferred_element_type=F32).astype(o_ref.dtype)

    @pl.when(i >= na_ref[0])
    def _():
        o_ref[...] = jnp.zeros_like(o_ref)


def _experts(block_expert, n_active, xs, w_gate, w_up, w_down, bm):
    n_rows, d = xs.shape
    nblk = n_rows // bm
    de = w_gate.shape[2]

    def blk(i, be, na):
        return jnp.minimum(i, na[0] - 1)

    return pl.pallas_call(
        _experts_kernel,
        grid_spec=pltpu.PrefetchScalarGridSpec(
            num_scalar_prefetch=2,
            grid=(nblk,),
            in_specs=[pl.BlockSpec((bm, d), lambda i, be, na: (blk(i, be, na), 0)),
                      pl.BlockSpec((None, d, de), lambda i, be, na: (be[blk(i, be, na)], 0, 0)),
                      pl.BlockSpec((None, d, de), lambda i, be, na: (be[blk(i, be, na)], 0, 0)),
                      pl.BlockSpec((None, de, d), lambda i, be, na: (be[blk(i, be, na)], 0, 0))],
            out_specs=pl.BlockSpec((bm, d), lambda i, be, na: (i, 0)),
            scratch_shapes=[pltpu.VMEM((d, de), BF16),
                            pltpu.VMEM((d, de), BF16),
                            pltpu.VMEM((de, d), BF16)],
        ),
        out_shape=jax.ShapeDtypeStruct((n_rows, d), BF16),
        compiler_params=_cparams("arbitrary"),
        name="experts",
    )(block_expert, n_active, xs, w_gate, w_up, w_down)


def _combine_kernel(d1_ref, d2_ref, ys_ref, x1_ref, w1_ref, w2_ref, g2_ref, lg_ref, lb_ref,
                    o_ref, y1buf, y2buf, sem, *, tm, alpha):
    base = pl.program_id(0) * tm

    def row_copy(t, dref, buf):
        return pltpu.make_async_copy(ys_ref.at[dref[base + t]], buf.at[t], sem)

    def issue(t, carry):
        row_copy(t, d1_ref, y1buf).start()
        row_copy(t, d2_ref, y2buf).start()
        return carry

    def drain(t, carry):
        row_copy(t, d1_ref, y1buf).wait()
        row_copy(t, d2_ref, y2buf).wait()
        return carry

    lax.fori_loop(0, tm, issue, 0)
    lax.fori_loop(0, tm, drain, 0)

    y = w1_ref[...] * y1buf[...].astype(F32) + w2_ref[...] * y2buf[...].astype(F32)
    r = alpha * x1_ref[...] + g2_ref[...] * y
    inv_d = 1.0 / (r.shape[1] * r.shape[2])
    mu = jnp.sum(jnp.sum(r, axis=2, keepdims=True), axis=1, keepdims=True) * inv_d
    rc = r - mu
    var = jnp.sum(jnp.sum(rc * rc, axis=2, keepdims=True), axis=1, keepdims=True) * inv_d
    o_ref[...] = rc * lax.rsqrt(var + LN_EPS) * lg_ref[...] + lb_ref[...]


def _combine(dest1, dest2, ys3, x1r, w1b, w2b, gate2r, ln2_g, ln2_b, seq, alpha, tm=256):
    n, sub, lanes = x1r.shape
    tm = min(tm, seq)
    tok = lambda i, d1, d2: (i, 0, 0)
    const = lambda i, d1, d2: (0, 0, 0)
    return pl.pallas_call(
        functools.partial(_combine_kernel, tm=tm, alpha=alpha),
        grid_spec=pltpu.PrefetchScalarGridSpec(
            num_scalar_prefetch=2,
            grid=(n // tm,),
            in_specs=[pl.BlockSpec(memory_space=pl.ANY),
                      pl.BlockSpec((tm, sub, lanes), tok),
                      pl.BlockSpec((tm, 1, lanes), tok),
                      pl.BlockSpec((tm, 1, lanes), tok),
                      pl.BlockSpec((1, sub, lanes), lambda i, d1, d2: ((i * tm) // seq, 0, 0)),
                      pl.BlockSpec((1, sub, lanes), const),
                      pl.BlockSpec((1, sub, lanes), const)],
            out_specs=pl.BlockSpec((tm, sub, lanes), tok),
            scratch_shapes=[pltpu.VMEM((tm, sub, lanes), ys3.dtype),
                            pltpu.VMEM((tm, sub, lanes), ys3.dtype),
                            pltpu.SemaphoreType.DMA(())],
        ),
        out_shape=jax.ShapeDtypeStruct((n, sub, lanes), F32),
        compiler_params=_cparams("arbitrary"),
        name="combine",
    )(dest1, dest2, ys3, x1r, w1b, w2b, gate2r, ln2_g, ln2_b)


EXPERT_BLOCK_ROWS = 256


def kernel(x, c, w_ada, b_ada, w_in, sgu_w, sgu_b, sgu_ln_g, sgu_ln_b, w_proj_a, w_proj_b,
           w_out, ln1_g, ln1_b, w_group, b_group, w_router, b_router, w_gate, w_up, w_down,
           ln2_g, ln2_b):
    b, s, d = x.shape
    n = b * s
    depth = w_ada.shape[0]
    alpha = (2.0 * depth) ** 0.25
    sub = d // LANES
    bm = EXPERT_BLOCK_ROWS
    n_assign = 2 * n
    n_rows = n_assign + N_EXPERTS * bm
    nblk = n_rows // bm

    in_cols = w_in.shape[2]
    colscale = jnp.ones((1, in_cols), F32).at[:, _COL_Q(d):_COL_K(d)].set(
        SB_HEAD_DIM ** -0.5 * LOG2E)
    n_mix_cols = 3 * SB_WIDTH + 2 * SGU_WIDTH

    for l in range(depth):
        mod = _ada(c, w_ada[l], b_ada[l])
        mod3 = mod.reshape(b * 6, 1, d)

        x2 = x.reshape(n, d)
        w_in_l = jnp.concatenate([w_in[l][:, n_mix_cols:], w_in[l][:, :n_mix_cols]],
                                 axis=1).astype(BF16)
        proj = _inproj(x2, mod3, w_in_l, colscale, s)
        out_a = _attention(proj.reshape(b, s, in_cols), d).reshape(n, SB_WIDTH)

        wr = jnp.zeros((d, LANES), F32)
        wr = wr.at[:, GROUP_LANE0:GROUP_LANE0 + N_GROUPS].set(w_group[l])
        wr = wr.at[:, EXPERT_LANE0:EXPERT_LANE0 + N_EXPERTS].set(w_router[l])
        br = jnp.zeros((1, LANES), F32)
        br = br.at[0, GROUP_LANE0:GROUP_LANE0 + N_GROUPS].set(b_group[l])
        br = br.at[0, EXPERT_LANE0:EXPERT_LANE0 + N_EXPERTS].set(b_router[l])
        sgu_bias_full = jnp.repeat(sgu_b[l].T, SGU_GROUP_DIM, axis=1)

        x1, h2, route, oh1, oh2 = _mix(
            out_a, proj, x2, mod3, sgu_w[l], sgu_bias_full,
            sgu_ln_g[l].reshape(1, -1), sgu_ln_b[l].reshape(1, -1),
            w_proj_a[l].astype(BF16), w_proj_b[l].astype(BF16), w_out[l].astype(BF16),
            ln1_g[l].reshape(1, d), ln1_b[l].reshape(1, d), wr, br, s, alpha)

        dest, blk = _plan(oh1, oh2, bm, nblk)
        dest1 = dest[:, 0]
        dest2 = dest[:, 1]
        block_expert = blk[:nblk, 0]
        n_active = blk[0:1, 1]

        xs = _dispatch(dest1, dest2, h2.reshape(n, sub, LANES), n_rows)
        ys = _experts(block_expert, n_active, xs.reshape(n_rows, d),
                      w_gate[l], w_up[l], w_down[l], bm)

        w1b = jnp.broadcast_to(route[:, 0][:, None, None], (n, 1, LANES))
        w2b = jnp.broadcast_to(route[:, 1][:, None, None], (n, 1, LANES))
        gate2r = mod[:, 5 * d:6 * d].reshape(b, sub, LANES)
        out = _combine(dest1, dest2, ys.reshape(n_rows, sub, LANES), x1.reshape(n, sub, LANES),
                       w1b, w2b, gate2r, ln2_g[l].reshape(1, sub, LANES),
                       ln2_b[l].reshape(1, sub, LANES), s, alpha)
        x = out.reshape(b, s, d)
    return x
```

```python
import functools
import math

import jax
import jax.numpy as jnp
from jax import lax
from jax.experimental import pallas as pl
from jax.experimental.pallas import tpu as pltpu

F32 = jnp.float32
BF16 = jnp.bfloat16
I32 = jnp.int32

CHUNK = 64
SB_HEADS = 8
SB_HEAD_DIM = 128
SB_WIDTH = SB_HEADS * SB_HEAD_DIM
SGU_GROUPS = 8
SGU_GROUP_DIM = 128
SGU_WIDTH = SGU_GROUPS * SGU_GROUP_DIM
SGU_BLOCK = 128
N_GROUPS = 4
EXPERTS_PER_GROUP = 8
N_EXPERTS = N_GROUPS * EXPERTS_PER_GROUP
D_EXPERT = 512
LN_EPS = 1e-5

LANES = 128
SUBLANES = 8
VMEM_LIMIT_BYTES = 60000 * 1024

GROUP_LANE0 = 0
EXPERT_LANE0 = N_GROUPS

LOG2E = 1.4426950408889634
NEG_BIG = -1e30

_COL_GA = 0


def _COL_GB(d):
    return d


def _COL_Q(d):
    return 2 * d


def _COL_K(d):
    return 2 * d + SB_WIDTH


def _COL_V(d):
    return 2 * d + 2 * SB_WIDTH


def _COL_SU(d):
    return 2 * d + 3 * SB_WIDTH


def _COL_SV(d):
    return 2 * d + 3 * SB_WIDTH + SGU_WIDTH


def _cparams(*sem):
    return pltpu.CompilerParams(dimension_semantics=sem, vmem_limit_bytes=VMEM_LIMIT_BYTES)


def _ln_rows(x):
    mu = jnp.mean(x, axis=-1, keepdims=True)
    xc = x - mu
    var = jnp.mean(xc * xc, axis=-1, keepdims=True)
    return xc * lax.rsqrt(var + LN_EPS)


def _sigmoid(x):
    return 1.0 / (1.0 + jnp.exp(-x))


def _ada_kernel(c_ref, w_ref, b_ref, o_ref):
    c = c_ref[...]
    ca = c * _sigmoid(c)
    o_ref[...] = jnp.dot(ca, w_ref[...], preferred_element_type=F32,
                         precision=lax.Precision.HIGHEST) + b_ref[...]


def _ada(c, w_ada, b_ada, tn=1024):
    b, d = c.shape
    n_out = w_ada.shape[1]
    cp = jnp.zeros((SUBLANES, d), F32).at[:b].set(c)
    out = pl.pallas_call(
        _ada_kernel,
        grid=(n_out // tn,),
        in_specs=[pl.BlockSpec((SUBLANES, d), lambda j: (0, 0)),
                  pl.BlockSpec((d, tn), lambda j: (0, j)),
                  pl.BlockSpec((1, tn), lambda j: (0, j))],
        out_specs=pl.BlockSpec((SUBLANES, tn), lambda j: (0, j)),
        out_shape=jax.ShapeDtypeStruct((SUBLANES, n_out), F32),
        compiler_params=_cparams("arbitrary"),
        name="ada",
    )(cp, w_ada, b_ada.reshape(1, n_out))
    return out[:b]


def _inproj_kernel(x_ref, sh_ref, sc_ref, w_ref, cs_ref, o_ref, h_ref):
    @pl.when(pl.program_id(1) == 0)
    def _():
        h = _ln_rows(x_ref[...]) * (1.0 + sc_ref[0]) + sh_ref[0]
        h_ref[...] = h.astype(BF16)

    acc = jnp.dot(h_ref[...], w_ref[...], preferred_element_type=F32)
    o_ref[...] = (acc * cs_ref[...]).astype(BF16)


def _inproj(x2, mod3, w_in_bf, colscale, seq, tm=1024, tn=1024):
    n, d = x2.shape
    n_cols = w_in_bf.shape[1]
    tm = min(tm, seq)
    return pl.pallas_call(
        _inproj_kernel,
        grid=(n // tm, n_cols // tn),
        in_specs=[pl.BlockSpec((tm, d), lambda i, j: (i, 0)),
                  pl.BlockSpec((1, 1, d), lambda i, j: (((i * tm) // seq) * 6 + 0, 0, 0)),
                  pl.BlockSpec((1, 1, d), lambda i, j: (((i * tm) // seq) * 6 + 1, 0, 0)),
                  pl.BlockSpec((d, tn), lambda i, j: (0, j)),
                  pl.BlockSpec((1, tn), lambda i, j: (0, j))],
        out_specs=pl.BlockSpec((tm, tn), lambda i, j: (i, j)),
        out_shape=jax.ShapeDtypeStruct((n, n_cols), BF16),
        scratch_shapes=[pltpu.VMEM((tm, d), BF16)],
        compiler_params=_cparams("arbitrary", "arbitrary"),
        name="inproj",
    )(x2, mod3, mod3, w_in_bf, colscale)


Z2_MAX = 100.0


def _attn_kernel(q_ref, k_ref, v_ref, o_ref, *, tq, tk, heads):
    qi = pl.program_id(2)
    nd = tq // tk
    jd = qi * nd
    trow = lax.broadcasted_iota(I32, (tk, tk), 0)
    tcol = lax.broadcasted_iota(I32, (tk, tk), 1)
    tri = (trow > tcol).astype(BF16)
    hd = SB_HEAD_DIM
    qs = [q_ref[0, :, h * hd:(h + 1) * hd] for h in range(heads)]

    def block(j, runs, accs, causal):
        off = pl.multiple_of(j * tk, tk)
        new_runs, new_accs = [], []
        for h in range(heads):
            kb = k_ref[0, pl.ds(off, tk), h * hd:(h + 1) * hd]
            vb = v_ref[0, pl.ds(off, tk), h * hd:(h + 1) * hd]
            z2 = lax.dot_general(qs[h], kb, (((1,), (1,)), ((), ())),
                                 preferred_element_type=F32)
            z2 = lax.clamp(-Z2_MAX, z2, Z2_MAX)
            sp = jnp.log(1.0 + jnp.exp2(z2))
            if causal is not None:
                sp = jnp.where(causal, sp, 0.0)
            later = jnp.dot(sp.astype(BF16), tri, preferred_element_type=F32)
            w = jnp.exp2(z2 - LOG2E * (sp + (later + runs[h])))
            if causal is not None:
                w = jnp.where(causal, w, 0.0)
            new_accs.append(accs[h] + jnp.dot(w.astype(BF16), vb, preferred_element_type=F32))
            new_runs.append(runs[h] + jnp.sum(sp, axis=1, keepdims=True))
        return tuple(new_runs), tuple(new_accs)

    def diag_body(t, carry):
        c = nd - 1 - t
        row = lax.broadcasted_iota(I32, (tq, tk), 0)
        col = lax.broadcasted_iota(I32, (tq, tk), 1)
        causal = col + c * tk < row
        return block(jd + c, carry[0], carry[1], causal)

    def body(t, carry):
        return block(jd - 1 - t, carry[0], carry[1], None)

    runs = tuple(jnp.zeros((tq, 1), F32) for _ in range(heads))
    accs = tuple(jnp.zeros((tq, hd), F32) for _ in range(heads))
    runs, accs = lax.fori_loop(0, nd, diag_body, (runs, accs))
    runs, accs = lax.fori_loop(0, jd, body, (runs, accs))
    for h in range(heads):
        o_ref[0, :, h * hd:(h + 1) * hd] = accs[h].astype(BF16)


def _attention(proj3, d, tq=1024, tk=256, heads=2):
    b, s, _ = proj3.shape
    tk = min(tk, s)
    tq = min(tq, s)
    assert tq % tk == 0 and s % tq == 0
    width = heads * SB_HEAD_DIM
    qc = _COL_Q(d) // width
    kc = _COL_K(d) // width
    vc = _COL_V(d) // width
    return pl.pallas_call(
        functools.partial(_attn_kernel, tq=tq, tk=tk, heads=heads),
        grid=(b, SB_HEADS // heads, s // tq),
        in_specs=[pl.BlockSpec((1, tq, width), lambda bi, g, qi: (bi, qi, qc + g)),
                  pl.BlockSpec((1, s, width), lambda bi, g, qi: (bi, 0, kc + g)),
                  pl.BlockSpec((1, s, width), lambda bi, g, qi: (bi, 0, vc + g))],
        out_specs=pl.BlockSpec((1, tq, width), lambda bi, g, qi: (bi, qi, g)),
        out_shape=jax.ShapeDtypeStruct((b, s, SB_WIDTH), BF16),
        compiler_params=_cparams("arbitrary", "arbitrary", "arbitrary"),
        name="attn",
    )(proj3, proj3, proj3)


def _erf(x):
    return lax.erf(x)


def _gelu(x):
    return 0.5 * x * (1.0 + _erf(x * (1.0 / math.sqrt(2.0))))


def _mix_kernel(oa_ref, su_ref, sv_ref, ga_ref, gb_ref, x_ref,
                sw_ref, sbias_ref, sg_ref, sb_ref,
                wpa_ref, wpb_ref, wo_ref,
                g1_ref, ln1g_ref, ln1b_ref, sh2_ref, sc2_ref,
                wr_ref, br_ref,
                x1_ref, h2_ref, route_ref, oh1_ref, oh2_ref, *, tm, alpha):
    u = _gelu(su_ref[...].astype(F32))
    v = _gelu(sv_ref[...].astype(F32))
    v = _ln_rows(v) * sg_ref[...] + sb_ref[...]
    vb = v.astype(BF16)
    pos_t = lax.broadcasted_iota(I32, (SGU_BLOCK, SGU_BLOCK), 0)
    pos_s = lax.broadcasted_iota(I32, (SGU_BLOCK, SGU_BLOCK), 1)
    chunk_causal = (pos_s // CHUNK) <= (pos_t // CHUNK)
    blocks = []
    for nb in range(tm // SGU_BLOCK):
        r0 = nb * SGU_BLOCK
        cols = []
        for g in range(SGU_GROUPS):
            c0 = g * SGU_GROUP_DIM
            wg = jnp.where(chunk_causal, sw_ref[g], 0.0).astype(BF16)
            cols.append(jnp.dot(wg, vb[r0:r0 + SGU_BLOCK, c0:c0 + SGU_GROUP_DIM],
                                preferred_element_type=F32))
        blocks.append(jnp.concatenate(cols, axis=1) + sbias_ref[...])
    mixed = jnp.concatenate(blocks, axis=0)
    out_b = (u * mixed).astype(BF16)

    pa = jnp.dot(oa_ref[...], wpa_ref[...], preferred_element_type=F32)
    pb = jnp.dot(out_b, wpb_ref[...], preferred_element_type=F32)
    merged = (_sigmoid(ga_ref[...].astype(F32)) * pa
              + _sigmoid(gb_ref[...].astype(F32)) * pb)
    y = jnp.dot(merged.astype(BF16), wo_ref[...], preferred_element_type=F32)

    r = alpha * x_ref[...] + g1_ref[0] * y
    x1 = _ln_rows(r) * ln1g_ref[...] + ln1b_ref[...]
    x1_ref[...] = x1
    h2 = _ln_rows(x1) * (1.0 + sc2_ref[0]) + sh2_ref[0]
    h2_ref[...] = h2.astype(BF16)

    logits = jnp.dot(h2, wr_ref[...], preferred_element_type=F32,
                     precision=lax.Precision.HIGHEST) + br_ref[...]
    lane = lax.broadcasted_iota(I32, logits.shape, 1)
    gl = jnp.where(lane < N_GROUPS, logits, NEG_BIG)
    gmax = jnp.max(gl, axis=1, keepdims=True)
    g_sel = jnp.min(jnp.where(gl == gmax, lane, LANES), axis=1, keepdims=True)
    p_group = 1.0 / jnp.sum(jnp.exp(gl - gmax), axis=1, keepdims=True)
    lo_lane = EXPERT_LANE0 + g_sel * EXPERTS_PER_GROUP
    in_group = (lane >= lo_lane) & (lane < lo_lane + EXPERTS_PER_GROUP)
    el = jnp.where(in_group, logits, NEG_BIG)
    m1 = jnp.max(el, axis=1, keepdims=True)
    i1 = jnp.min(jnp.where(el == m1, lane, LANES), axis=1, keepdims=True)
    el2 = jnp.where(lane == i1, NEG_BIG, el)
    m2 = jnp.max(el2, axis=1, keepdims=True)
    i2 = jnp.min(jnp.where(el2 == m2, lane, LANES), axis=1, keepdims=True)
    e21 = jnp.exp(m2 - m1)
    w1 = p_group / (1.0 + e21)
    w2 = p_group * e21 / (1.0 + e21)
    route = jnp.where(lane == 0, w1, 0.0)
    route = jnp.where(lane == 1, w2, route)
    route = jnp.where(lane == 2, (i1 - EXPERT_LANE0).astype(F32), route)
    route = jnp.where(lane == 3, (i2 - EXPERT_LANE0).astype(F32), route)
    route_ref[...] = route
    oh1_ref[...] = jnp.where(lane == i1, 1.0, 0.0).astype(BF16)
    oh2_ref[...] = jnp.where(lane == i2, 1.0, 0.0).astype(BF16)


def _mix(out_a, proj, x2, mod3, sgu_w, sgu_bias_full, sgu_g, sgu_b, wpa, wpb, wo,
         ln1_g, ln1_b, wr, br, seq, alpha, tm=256):
    n, d = x2.shape
    tm = min(tm, seq)
    def pcol(start, width):
        assert start % width == 0
        return start // width
    su_c = pcol(_COL_SU(d), SGU_WIDTH)
    sv_c = pcol(_COL_SV(d), SGU_WIDTH)
    ga_c = pcol(_COL_GA, d)
    gb_c = pcol(_COL_GB(d), d)
    bidx = lambda i: (i * tm) // seq
    row = lambda i: (i, 0)
    const2 = lambda i: (0, 0)
    const3 = lambda i: (0, 0, 0)
    return pl.pallas_call(
        functools.partial(_mix_kernel, tm=tm, alpha=alpha),
        grid=(n // tm,),
        in_specs=[pl.BlockSpec((tm, SB_WIDTH), row),
                  pl.BlockSpec((tm, SGU_WIDTH), lambda i: (i, su_c)),
                  pl.BlockSpec((tm, SGU_WIDTH), lambda i: (i, sv_c)),
                  pl.BlockSpec((tm, d), lambda i: (i, ga_c)),
                  pl.BlockSpec((tm, d), lambda i: (i, gb_c)),
                  pl.BlockSpec((tm, d), row),
                  pl.BlockSpec((SGU_GROUPS, SGU_BLOCK, SGU_BLOCK), const3),
                  pl.BlockSpec((SGU_BLOCK, SGU_WIDTH), const2),
                  pl.BlockSpec((1, SGU_WIDTH), const2),
                  pl.BlockSpec((1, SGU_WIDTH), const2),
                  pl.BlockSpec((SB_WIDTH, d), const2),
                  pl.BlockSpec((SGU_WIDTH, d), const2),
                  pl.BlockSpec((d, d), const2),
                  pl.BlockSpec((1, 1, d), lambda i: (bidx(i) * 6 + 2, 0, 0)),
                  pl.BlockSpec((1, d), const2),
                  pl.BlockSpec((1, d), const2),
                  pl.BlockSpec((1, 1, d), lambda i: (bidx(i) * 6 + 3, 0, 0)),
                  pl.BlockSpec((1, 1, d), lambda i: (bidx(i) * 6 + 4, 0, 0)),
                  pl.BlockSpec((d, LANES), const2),
                  pl.BlockSpec((1, LANES), const2)],
        out_specs=[pl.BlockSpec((tm, d), row),
                   pl.BlockSpec((tm, d), row),
                   pl.BlockSpec((tm, LANES), row),
                   pl.BlockSpec((tm, LANES), row),
                   pl.BlockSpec((tm, LANES), row)],
        out_shape=[jax.ShapeDtypeStruct((n, d), F32),
                   jax.ShapeDtypeStruct((n, d), BF16),
                   jax.ShapeDtypeStruct((n, LANES), F32),
                   jax.ShapeDtypeStruct((n, LANES), BF16),
                   jax.ShapeDtypeStruct((n, LANES), BF16)],
        compiler_params=_cparams("arbitrary"),
        name="mix",
    )(out_a, proj, proj, proj, proj, x2, sgu_w, sgu_bias_full, sgu_g, sgu_b,
      wpa, wpb, wo, mod3, ln1_g, ln1_b, mod3, mod3, wr, br)


def _plan_kernel(oh1_ref, oh2_ref, dest_ref, blk_ref, cnt_ref, start_ref, *, tm, bm, nblk_pad):
    phase = pl.program_id(0)
    i = pl.program_id(1)
    oh1 = oh1_ref[...]
    oh2 = oh2_ref[...]
    both = oh1 + oh2
    lane = lax.broadcasted_iota(I32, (SUBLANES, LANES), 1)

    @pl.when((phase == 0) & (i == 0))
    def _():
        cnt_ref[...] = jnp.zeros_like(cnt_ref)

    @pl.when(phase == 0)
    def _():
        ones = jnp.ones((SUBLANES, tm), BF16)
        cnt_ref[...] += jnp.dot(ones, both, preferred_element_type=F32)

    @pl.when((phase == 1) & (i == 0))
    def _():
        cnt = cnt_ref[...]
        padded = jnp.floor((cnt + (bm - 1)) * (1.0 / bm)) * bm
        r = lax.broadcasted_iota(I32, (LANES, LANES), 0)
        c = lax.broadcasted_iota(I32, (LANES, LANES), 1)
        upper = (r < c).astype(F32)
        starts = jnp.dot(padded, upper, preferred_element_type=F32,
                         precision=lax.Precision.HIGHEST)
        start_ref[...] = starts
        cnt_ref[...] = jnp.zeros_like(cnt_ref)
        ends = starts + padded
        is_exp = (lane[0:1] >= EXPERT_LANE0) & (lane[0:1] < EXPERT_LANE0 + N_EXPERTS)
        bstart = (lax.broadcasted_iota(I32, (nblk_pad, LANES), 0) * bm).astype(F32)
        done = jnp.where(is_exp & (ends[0:1] <= bstart), 1.0, 0.0)
        bexp = jnp.minimum(jnp.sum(done, axis=1, keepdims=True), N_EXPERTS - 1.0)
        total = jnp.sum(jnp.where(is_exp, padded[0:1], 0.0), axis=1, keepdims=True)
        blane = lax.broadcasted_iota(I32, (nblk_pad, LANES), 1)
        blk = jnp.where(blane == 0, bexp, jnp.where(blane == 1, total * (1.0 / bm), 0.0))
        blk_ref[...] = blk.astype(I32)

    @pl.when(phase == 1)
    def _():
        r = lax.broadcasted_iota(I32, (tm, tm), 0)
        c = lax.broadcasted_iota(I32, (tm, tm), 1)
        lower = (c < r).astype(BF16)
        before = jnp.dot(lower, both, preferred_element_type=F32) + cnt_ref[0:1]
        pos = before + start_ref[0:1]
        d1 = jnp.sum(pos * oh1.astype(F32), axis=1, keepdims=True)
        d2 = jnp.sum(pos * oh2.astype(F32), axis=1, keepdims=True)
        dlane = lax.broadcasted_iota(I32, (tm, LANES), 1)
        dest = jnp.where(dlane == 0, d1, jnp.where(dlane == 1, d2, 0.0))
        dest_ref[...] = dest.astype(I32)
        ones = jnp.ones((SUBLANES, tm), BF16)
        cnt_ref[...] += jnp.dot(ones, both, preferred_element_type=F32)


def _plan(oh1, oh2, bm, nblk, tm=512):
    n = oh1.shape[0]
    tm = min(tm, n)
    nblk_pad = -(-nblk // SUBLANES) * SUBLANES
    dest, blk = pl.pallas_call(
        functools.partial(_plan_kernel, tm=tm, bm=bm, nblk_pad=nblk_pad),
        grid=(2, n // tm),
        in_specs=[pl.BlockSpec((tm, LANES), lambda p, i: (i, 0)),
                  pl.BlockSpec((tm, LANES), lambda p, i: (i, 0))],
        out_specs=[pl.BlockSpec((tm, LANES), lambda p, i: (i * p, 0)),
                   pl.BlockSpec((nblk_pad, LANES), lambda p, i: (0, 0))],
        out_shape=[jax.ShapeDtypeStruct((n, LANES), I32),
                   jax.ShapeDtypeStruct((nblk_pad, LANES), I32)],
        scratch_shapes=[pltpu.VMEM((SUBLANES, LANES), F32),
                        pltpu.VMEM((SUBLANES, LANES), F32)],
        compiler_params=_cparams("arbitrary", "arbitrary"),
        name="plan",
    )(oh1, oh2)
    return dest, blk


def _dispatch_kernel(d1_ref, d2_ref, h_ref, xs_in_ref, xs_ref, sem, *, tm):
    del xs_in_ref
    base = pl.program_id(0) * tm

    def row_copy(t, dref):
        return pltpu.make_async_copy(h_ref.at[t], xs_ref.at[dref[base + t]], sem)

    def issue(t, carry):
        row_copy(t, d1_ref).start()
        row_copy(t, d2_ref).start()
        return carry

    def drain(t, carry):
        row_copy(t, d1_ref).wait()
        row_copy(t, d2_ref).wait()
        return carry

    lax.fori_loop(0, tm, issue, 0)
    lax.fori_loop(0, tm, drain, 0)


def _dispatch(dest1, dest2, h2r, n_rows, tm=256):
    n, sub, lanes = h2r.shape
    tm = min(tm, n)
    xs0 = jnp.zeros((n_rows, sub, lanes), h2r.dtype)
    return pl.pallas_call(
        functools.partial(_dispatch_kernel, tm=tm),
        grid_spec=pltpu.PrefetchScalarGridSpec(
            num_scalar_prefetch=2,
            grid=(n // tm,),
            in_specs=[pl.BlockSpec((tm, sub, lanes), lambda i, d1, d2: (i, 0, 0)),
                      pl.BlockSpec(memory_space=pl.ANY)],
            out_specs=pl.BlockSpec(memory_space=pl.ANY),
            scratch_shapes=[pltpu.SemaphoreType.DMA(())],
        ),
        out_shape=jax.ShapeDtypeStruct((n_rows, sub, lanes), h2r.dtype),
        input_output_aliases={3: 0},
        compiler_params=_cparams("arbitrary"),
        name="dispatch",
    )(dest1, dest2, h2r, xs0)


def _experts_kernel(be_ref, na_ref, x_ref, wg_ref, wu_ref, wd_ref, o_ref, wgb, wub, wdb):
    i = pl.program_id(0)

    @pl.when(i < na_ref[0])
    def _():
        prev = be_ref[jnp.maximum(i - 1, 0)]

        @pl.when((i == 0) | (be_ref[i] != prev))
        def _():
            wgb[...] = wg_ref[...].astype(BF16)
            wub[...] = wu_ref[...].astype(BF16)
            wdb[...] = wd_ref[...].astype(BF16)

        x = x_ref[...]
        g = jnp.dot(x, wgb[...], preferred_element_type=F32)
        u = jnp.dot(x, wub[...], preferred_element_type=F32)
        hb = (g * _sigmoid(g) * u).astype(BF16)
        o_ref[...] = jnp.dot(hb, wdb[...], preferred_element_type=F32).astype(o_ref.dtype)

    @pl.when(i >= na_ref[0])
    def _():
        o_ref[...] = jnp.zeros_like(o_ref)


def _experts(block_expert, n_active, xs, w_gate, w_up, w_down, bm):
    n_rows, d = xs.shape
    nblk = n_rows // bm
    de = w_gate.shape[2]

    def blk(i, be, na):
        return jnp.minimum(i, na[0] - 1)

    return pl.pallas_call(
        _experts_kernel,
        grid_spec=pltpu.PrefetchScalarGridSpec(
            num_scalar_prefetch=2,
            grid=(nblk,),
            in_specs=[pl.BlockSpec((bm, d), lambda i, be, na: (blk(i, be, na), 0)),
                      pl.BlockSpec((None, d, de), lambda i, be, na: (be[blk(i, be, na)], 0, 0)),
                      pl.BlockSpec((None, d, de), lambda i, be, na: (be[blk(i, be, na)], 0, 0)),
                      pl.BlockSpec((None, de, d), lambda i, be, na: (be[blk(i, be, na)], 0, 0))],
            out_specs=pl.BlockSpec((bm, d), lambda i, be, na: (i, 0)),
            scratch_shapes=[pltpu.VMEM((d, de), BF16),
                            pltpu.VMEM((d, de), BF16),
                            pltpu.VMEM((de, d), BF16)],
        ),
        out_shape=jax.ShapeDtypeStruct((n_rows, d), BF16),
        compiler_params=_cparams("arbitrary"),
        name="experts",
    )(block_expert, n_active, xs, w_gate, w_up, w_down)


def _combine_kernel(d1_ref, d2_ref, ys_ref, x1_ref, w1_ref, w2_ref, g2_ref, lg_ref, lb_ref,
                    o_ref, y1buf, y2buf, sem, *, tm, alpha):
    base = pl.program_id(0) * tm

    def row_copy(t, dref, buf):
        return pltpu.make_async_copy(ys_ref.at[dref[base + t]], buf.at[t], sem)

    def issue(t, carry):
        row_copy(t, d1_ref, y1buf).start()
        row_copy(t, d2_ref, y2buf).start()
        return carry

    def drain(t, carry):
        row_copy(t, d1_ref, y1buf).wait()
        row_copy(t, d2_ref, y2buf).wait()
        return carry

    lax.fori_loop(0, tm, issue, 0)
    lax.fori_loop(0, tm, drain, 0)

    y = w1_ref[...] * y1buf[...].astype(F32) + w2_ref[...] * y2buf[...].astype(F32)
    r = alpha * x1_ref[...] + g2_ref[...] * y
    inv_d = 1.0 / (r.shape[1] * r.shape[2])
    mu = jnp.sum(jnp.sum(r, axis=2, keepdims=True), axis=1, keepdims=True) * inv_d
    rc = r - mu
    var = jnp.sum(jnp.sum(rc * rc, axis=2, keepdims=True), axis=1, keepdims=True) * inv_d
    o_ref[...] = rc * lax.rsqrt(var + LN_EPS) * lg_ref[...] + lb_ref[...]


def _combine(dest1, dest2, ys3, x1r, w1b, w2b, gate2r, ln2_g, ln2_b, seq, alpha, tm=256):
    n, sub, lanes = x1r.shape
    tm = min(tm, seq)
    tok = lambda i, d1, d2: (i, 0, 0)
    const = lambda i, d1, d2: (0, 0, 0)
    return pl.pallas_call(
        functools.partial(_combine_kernel, tm=tm, alpha=alpha),
        grid_spec=pltpu.PrefetchScalarGridSpec(
            num_scalar_prefetch=2,
            grid=(n // tm,),
            in_specs=[pl.BlockSpec(memory_space=pl.ANY),
                      pl.BlockSpec((tm, sub, lanes), tok),
                      pl.BlockSpec((tm, 1, lanes), tok),
                      pl.BlockSpec((tm, 1, lanes), tok),
                      pl.BlockSpec((1, sub, lanes), lambda i, d1, d2: ((i * tm) // seq, 0, 0)),
                      pl.BlockSpec((1, sub, lanes), const),
                      pl.BlockSpec((1, sub, lanes), const)],
            out_specs=pl.BlockSpec((tm, sub, lanes), tok),
            scratch_shapes=[pltpu.VMEM((tm, sub, lanes), ys3.dtype),
                            pltpu.VMEM((tm, sub, lanes), ys3.dtype),
                            pltpu.SemaphoreType.DMA(())],
        ),
        out_shape=jax.ShapeDtypeStruct((n, sub, lanes), F32),
        compiler_params=_cparams("arbitrary"),
        name="combine",
    )(dest1, dest2, ys3, x1r, w1b, w2b, gate2r, ln2_g, ln2_b)


EXPERT_BLOCK_ROWS = 256


def kernel(x, c, w_ada, b_ada, w_in, sgu_w, sgu_b, sgu_ln_g, sgu_ln_b, w_proj_a, w_proj_b,
           w_out, ln1_g, ln1_b, w_group, b_group, w_router, b_router, w_gate, w_up, w_down,
           ln2_g, ln2_b):
    b, s, d = x.shape
    n = b * s
    depth = w_ada.shape[0]
    alpha = (2.0 * depth) ** 0.25
    sub = d // LANES
    bm = EXPERT_BLOCK_ROWS
    n_assign = 2 * n
    n_rows = n_assign + N_EXPERTS * bm
    nblk = n_rows // bm

    in_cols = w_in.shape[2]
    colscale = jnp.ones((1, in_cols), F32).at[:, _COL_Q(d):_COL_K(d)].set(
        SB_HEAD_DIM ** -0.5 * LOG2E)
    n_mix_cols = 3 * SB_WIDTH + 2 * SGU_WIDTH

    for l in range(depth):
        mod = _ada(c, w_ada[l], b_ada[l])
        mod3 = mod.reshape(b * 6, 1, d)

        x2 = x.reshape(n, d)
        w_in_l = jnp.concatenate([w_in[l][:, n_mix_cols:], w_in[l][:, :n_mix_cols]],
                                 axis=1).astype(BF16)
        proj = _inproj(x2, mod3, w_in_l, colscale, s)
        out_a = _attention(proj.reshape(b, s, in_cols), d).reshape(n, SB_WIDTH)

        wr = jnp.zeros((d, LANES), F32)
        wr = wr.at[:, GROUP_LANE0:GROUP_LANE0 + N_GROUPS].set(w_group[l])
        wr = wr.at[:, EXPERT_LANE0:EXPERT_LANE0 + N_EXPERTS].set(w_router[l])
        br = jnp.zeros((1, LANES), F32)
        br = br.at[0, GROUP_LANE0:GROUP_LANE0 + N_GROUPS].set(b_group[l])
        br = br.at[0, EXPERT_LANE0:EXPERT_LANE0 + N_EXPERTS].set(b_router[l])
        sgu_bias_full = jnp.repeat(sgu_b[l].T, SGU_GROUP_DIM, axis=1)

        x1, h2, route, oh1, oh2 = _mix(
            out_a, proj, x2, mod3, sgu_w[l], sgu_bias_full,
            sgu_ln_g[l].reshape(1, -1), sgu_ln_b[l].reshape(1, -1),
            w_proj_a[l].astype(BF16), w_proj_b[l].astype(BF16), w_out[l].astype(BF16),
            ln1_g[l].reshape(1, d), ln1_b[l].reshape(1, d), wr, br, s, alpha)

        dest, blk = _plan(oh1, oh2, bm, nblk)
        dest1 = dest[:, 0]
        dest2 = dest[:, 1]
        block_expert = blk[:nblk, 0]
        n_active = blk[0:1, 1]

        xs = _dispatch(dest1, dest2, h2.reshape(n, sub, LANES), n_rows)
        ys = _experts(block_expert, n_active, xs.reshape(n_rows, d),
                      w_gate[l], w_up[l], w_down[l], bm)

        w1b = jnp.broadcast_to(route[:, 0][:, None, None], (n, 1, LANES))
        w2b = jnp.broadcast_to(route[:, 1][:, None, None], (n, 1, LANES))
        gate2r = mod[:, 5 * d:6 * d].reshape(b, sub, LANES)
        out = _combine(dest1, dest2, ys.reshape(n_rows, sub, LANES), x1.reshape(n, sub, LANES),
                       w1b, w2b, gate2r, ln2_g[l].reshape(1, sub, LANES),
                       ln2_b[l].reshape(1, sub, LANES), s, alpha)
        x = out.reshape(b, s, d)
    return x
```

```python
import functools
import math

import jax
import jax.numpy as jnp
from jax import lax
from jax.experimental import pallas as pl
from jax.experimental.pallas import tpu as pltpu

F32 = jnp.float32
BF16 = jnp.bfloat16
I32 = jnp.int32

CHUNK = 64
SB_HEADS = 8
SB_HEAD_DIM = 128
SB_WIDTH = SB_HEADS * SB_HEAD_DIM
SGU_GROUPS = 8
SGU_GROUP_DIM = 128
SGU_WIDTH = SGU_GROUPS * SGU_GROUP_DIM
SGU_BLOCK = 128
N_GROUPS = 4
EXPERTS_PER_GROUP = 8
N_EXPERTS = N_GROUPS * EXPERTS_PER_GROUP
D_EXPERT = 512
LN_EPS = 1e-5

LANES = 128
SUBLANES = 8
VMEM_LIMIT_BYTES = 60000 * 1024

GROUP_LANE0 = 0
EXPERT_LANE0 = N_GROUPS

LOG2E = 1.4426950408889634
NEG_BIG = -1e30

_COL_GA = 0


def _COL_GB(d):
    return d


def _COL_Q(d):
    return 2 * d


def _COL_K(d):
    return 2 * d + SB_WIDTH


def _COL_V(d):
    return 2 * d + 2 * SB_WIDTH


def _COL_SU(d):
    return 2 * d + 3 * SB_WIDTH


def _COL_SV(d):
    return 2 * d + 3 * SB_WIDTH + SGU_WIDTH


def _cparams(*sem):
    return pltpu.CompilerParams(dimension_semantics=sem, vmem_limit_bytes=VMEM_LIMIT_BYTES)


def _ln_rows(x):
    mu = jnp.mean(x, axis=-1, keepdims=True)
    xc = x - mu
    var = jnp.mean(xc * xc, axis=-1, keepdims=True)
    return xc * lax.rsqrt(var + LN_EPS)


def _sigmoid(x):
    return 1.0 / (1.0 + jnp.exp(-x))


def _ada_kernel(c_ref, w_ref, b_ref, o_ref):
    c = c_ref[...]
    ca = c * _sigmoid(c)
    o_ref[...] = jnp.dot(ca, w_ref[...], preferred_element_type=F32,
                         precision=lax.Precision.HIGHEST) + b_ref[...]


def _ada(c, w_ada, b_ada, tn=1024):
    b, d = c.shape
    n_out = w_ada.shape[1]
    cp = jnp.zeros((SUBLANES, d), F32).at[:b].set(c)
    out = pl.pallas_call(
        _ada_kernel,
        grid=(n_out // tn,),
        in_specs=[pl.BlockSpec((SUBLANES, d), lambda j: (0, 0)),
                  pl.BlockSpec((d, tn), lambda j: (0, j)),
                  pl.BlockSpec((1, tn), lambda j: (0, j))],
        out_specs=pl.BlockSpec((SUBLANES, tn), lambda j: (0, j)),
        out_shape=jax.ShapeDtypeStruct((SUBLANES, n_out), F32),
        compiler_params=_cparams("arbitrary"),
        name="ada",
    )(cp, w_ada, b_ada.reshape(1, n_out))
    return out[:b]


def _inproj_kernel(x_ref, sh_ref, sc_ref, w_ref, cs_ref, o_ref, h_ref):
    @pl.when(pl.program_id(1) == 0)
    def _():
        h = _ln_rows(x_ref[...]) * (1.0 + sc_ref[0]) + sh_ref[0]
        h_ref[...] = h.astype(BF16)

    acc = jnp.dot(h_ref[...], w_ref[...], preferred_element_type=F32)
    o_ref[...] = (acc * cs_ref[...]).astype(BF16)


def _inproj(x2, mod3, w_in_bf, colscale, seq, tm=1024, tn=1024):
    n, d = x2.shape
    n_cols = w_in_bf.shape[1]
    tm = min(tm, seq)
    return pl.pallas_call(
        _inproj_kernel,
        grid=(n // tm, n_cols // tn),
        in_specs=[pl.BlockSpec((tm, d), lambda i, j: (i, 0)),
                  pl.BlockSpec((1, 1, d), lambda i, j: (((i * tm) // seq) * 6 + 0, 0, 0)),
                  pl.BlockSpec((1, 1, d), lambda i, j: (((i * tm) // seq) * 6 + 1, 0, 0)),
                  pl.BlockSpec((d, tn), lambda i, j: (0, j)),
                  pl.BlockSpec((1, tn), lambda i, j: (0, j))],
        out_specs=pl.BlockSpec((tm, tn), lambda i, j: (i, j)),
        out_shape=jax.ShapeDtypeStruct((n, n_cols), BF16),
        scratch_shapes=[pltpu.VMEM((tm, d), BF16)],
        compiler_params=_cparams("arbitrary", "arbitrary"),
        name="inproj",
    )(x2, mod3, mod3, w_in_bf, colscale)


Z2_MAX = 100.0
STICK_EXIT_LOG2 = 150.0


def _attn_kernel(q_ref, k_ref, v_ref, o_ref, *, tq, tk, heads):
    qi = pl.program_id(2)
    nd = tq // tk
    jd = qi * nd
    trow = lax.broadcasted_iota(I32, (tk, tk), 0)
    tcol = lax.broadcasted_iota(I32, (tk, tk), 1)
    tri = (trow > tcol).astype(BF16)
    hd = SB_HEAD_DIM
    qs = [q_ref[0, :, h * hd:(h + 1) * hd] for h in range(heads)]

    def block(j, runs, accs, causal):
        off = pl.multiple_of(j * tk, tk)
        hs = range(heads)
        z2s = [lax.dot_general(qs[h], k_ref[0, pl.ds(off, tk), h * hd:(h + 1) * hd],
                               (((1,), (1,)), ((), ())), preferred_element_type=F32) for h in hs]
        z2s = [lax.clamp(-Z2_MAX, z2, Z2_MAX) for z2 in z2s]
        sps = [jnp.log(1.0 + jnp.exp2(z2)) for z2 in z2s]
        if causal is not None:
            sps = [jnp.where(causal, sp, 0.0) for sp in sps]
        laters = [jnp.dot(sp.astype(BF16), tri, preferred_element_type=F32) for sp in sps]
        ws = [jnp.exp2(z2s[h] - LOG2E * (sps[h] + (laters[h] + runs[h]))) for h in hs]
        if causal is not None:
            ws = [jnp.where(causal, w, 0.0) for w in ws]
        new_accs = [accs[h] + jnp.dot(ws[h].astype(BF16),
                                      v_ref[0, pl.ds(off, tk), h * hd:(h + 1) * hd],
                                      preferred_element_type=F32) for h in hs]
        new_runs = [runs[h] + jnp.sum(sps[h], axis=1, keepdims=True) for h in hs]
        return tuple(new_runs), tuple(new_accs)

    def mass_left(runs):
        low = runs[0]
        for r in runs[1:]:
            low = jnp.minimum(low, r)
        return (jnp.min(low) * LOG2E < STICK_EXIT_LOG2).astype(I32)

    def diag_body(t, carry):
        c = nd - 1 - t
        row = lax.broadcasted_iota(I32, (tq, tk), 0)
        col = lax.broadcasted_iota(I32, (tq, tk), 1)
        causal = col + c * tk < row
        return block(jd + c, carry[0], carry[1], causal)

    def more(carry):
        t, go, _, _ = carry
        return (t < jd) & (go > 0)

    def body(carry):
        t, _, runs, accs = carry
        runs, accs = block(jd - 1 - t, runs, accs, None)
        return t + 1, mass_left(runs), runs, accs

    runs = tuple(jnp.zeros((tq, 1), F32) for _ in range(heads))
    accs = tuple(jnp.zeros((tq, hd), F32) for _ in range(heads))
    runs, accs = lax.fori_loop(0, nd, diag_body, (runs, accs))
    _, _, runs, accs = lax.while_loop(more, body, (jnp.int32(0), mass_left(runs), runs, accs))
    for h in range(heads):
        o_ref[0, :, h * hd:(h + 1) * hd] = accs[h].astype(BF16)


def _attention(proj3, d, tq=256, tk=256, heads=4):
    b, s, _ = proj3.shape
    tk = min(tk, s)
    tq = min(tq, s)
    assert tq % tk == 0 and s % tq == 0
    width = heads * SB_HEAD_DIM
    qc = _COL_Q(d) // width
    kc = _COL_K(d) // width
    vc = _COL_V(d) // width
    return pl.pallas_call(
        functools.partial(_attn_kernel, tq=tq, tk=tk, heads=heads),
        grid=(b, SB_HEADS // heads, s // tq),
        in_specs=[pl.BlockSpec((1, tq, width), lambda bi, g, qi: (bi, qi, qc + g)),
                  pl.BlockSpec((1, s, width), lambda bi, g, qi: (bi, 0, kc + g)),
                  pl.BlockSpec((1, s, width), lambda bi, g, qi: (bi, 0, vc + g))],
        out_specs=pl.BlockSpec((1, tq, width), lambda bi, g, qi: (bi, qi, g)),
        out_shape=jax.ShapeDtypeStruct((b, s, SB_WIDTH), BF16),
        compiler_params=_cparams("arbitrary", "arbitrary", "arbitrary"),
        name="attn",
    )(proj3, proj3, proj3)


def _erf(x):
    return lax.erf(x)


def _gelu(x):
    return 0.5 * x * (1.0 + _erf(x * (1.0 / math.sqrt(2.0))))


def _mix_kernel(oa_ref, su_ref, sv_ref, ga_ref, gb_ref, x_ref,
                sw_ref, sbias_ref, sg_ref, sb_ref,
                wpa_ref, wpb_ref, wo_ref,
                g1_ref, ln1g_ref, ln1b_ref, sh2_ref, sc2_ref,
                wr_ref, br_ref,
                x1_ref, h2_ref, route_ref, oh1_ref, oh2_ref, *, tm, alpha):
    u = _gelu(su_ref[...].astype(F32))
    v = _gelu(sv_ref[...].astype(F32))
    v = _ln_rows(v) * sg_ref[...] + sb_ref[...]
    vb = v.astype(BF16)
    pos_t = lax.broadcasted_iota(I32, (SGU_BLOCK, SGU_BLOCK), 0)
    pos_s = lax.broadcasted_iota(I32, (SGU_BLOCK, SGU_BLOCK), 1)
    chunk_causal = (pos_s // CHUNK) <= (pos_t // CHUNK)
    blocks = []
    for nb in range(tm // SGU_BLOCK):
        r0 = nb * SGU_BLOCK
        cols = []
        for g in range(SGU_GROUPS):
            c0 = g * SGU_GROUP_DIM
            wg = jnp.where(chunk_causal, sw_ref[g], 0.0).astype(BF16)
            cols.append(jnp.dot(wg, vb[r0:r0 + SGU_BLOCK, c0:c0 + SGU_GROUP_DIM],
                                preferred_element_type=F32))
        blocks.append(jnp.concatenate(cols, axis=1) + sbias_ref[...])
    mixed = jnp.concatenate(blocks, axis=0)
    out_b = (u * mixed).astype(BF16)

    pa = jnp.dot(oa_ref[...], wpa_ref[...], preferred_element_type=F32)
    pb = jnp.dot(out_b, wpb_ref[...], preferred_element_type=F32)
    merged = (_sigmoid(ga_ref[...].astype(F32)) * pa
              + _sigmoid(gb_ref[...].astype(F32)) * pb)
    y = jnp.dot(merged.astype(BF16), wo_ref[...], preferred_element_type=F32)

    r = alpha * x_ref[...] + g1_ref[0] * y
    x1 = _ln_rows(r) * ln1g_ref[...] + ln1b_ref[...]
    x1_ref[...] = x1
    h2 = _ln_rows(x1) * (1.0 + sc2_ref[0]) + sh2_ref[0]
    h2_ref[...] = h2.astype(BF16)

    logits = jnp.dot(h2, wr_ref[...], preferred_element_type=F32,
                     precision=lax.Precision.HIGHEST) + br_ref[...]
    lane = lax.broadcasted_iota(I32, logits.shape, 1)
    gl = jnp.where(lane < N_GROUPS, logits, NEG_BIG)
    gmax = jnp.max(gl, axis=1, keepdims=True)
    g_sel = jnp.min(jnp.where(gl == gmax, lane, LANES), axis=1, keepdims=True)
    p_group = 1.0 / jnp.sum(jnp.exp(gl - gmax), axis=1, keepdims=True)
    lo_lane = EXPERT_LANE0 + g_sel * EXPERTS_PER_GROUP
    in_group = (lane >= lo_lane) & (lane < lo_lane + EXPERTS_PER_GROUP)
    el = jnp.where(in_group, logits, NEG_BIG)
    m1 = jnp.max(el, axis=1, keepdims=True)
    i1 = jnp.min(jnp.where(el == m1, lane, LANES), axis=1, keepdims=True)
    el2 = jnp.where(lane == i1, NEG_BIG, el)
    m2 = jnp.max(el2, axis=1, keepdims=True)
    i2 = jnp.min(jnp.where(el2 == m2, lane, LANES), axis=1, keepdims=True)
    e21 = jnp.exp(m2 - m1)
    w1 = p_group / (1.0 + e21)
    w2 = p_group * e21 / (1.0 + e21)
    route = jnp.where(lane == 0, w1, 0.0)
    route = jnp.where(lane == 1, w2, route)
    route = jnp.where(lane == 2, (i1 - EXPERT_LANE0).astype(F32), route)
    route = jnp.where(lane == 3, (i2 - EXPERT_LANE0).astype(F32), route)
    route_ref[...] = route
    oh1_ref[...] = jnp.where(lane == i1, 1.0, 0.0).astype(BF16)
    oh2_ref[...] = jnp.where(lane == i2, 1.0, 0.0).astype(BF16)


def _mix(out_a, proj, x2, mod3, sgu_w, sgu_bias_full, sgu_g, sgu_b, wpa, wpb, wo,
         ln1_g, ln1_b, wr, br, seq, alpha, tm=256):
    n, d = x2.shape
    tm = min(tm, seq)
    def pcol(start, width):
        assert start % width == 0
        return start // width
    su_c = pcol(_COL_SU(d), SGU_WIDTH)
    sv_c = pcol(_COL_SV(d), SGU_WIDTH)
    ga_c = pcol(_COL_GA, d)
    gb_c = pcol(_COL_GB(d), d)
    bidx = lambda i: (i * tm) // seq
    row = lambda i: (i, 0)
    const2 = lambda i: (0, 0)
    const3 = lambda i: (0, 0, 0)
    return pl.pallas_call(
        functools.partial(_mix_kernel, tm=tm, alpha=alpha),
        grid=(n // tm,),
        in_specs=[pl.BlockSpec((tm, SB_WIDTH), row),
                  pl.BlockSpec((tm, SGU_WIDTH), lambda i: (i, su_c)),
                  pl.BlockSpec((tm, SGU_WIDTH), lambda i: (i, sv_c)),
                  pl.BlockSpec((tm, d), lambda i: (i, ga_c)),
                  pl.BlockSpec((tm, d), lambda i: (i, gb_c)),
                  pl.BlockSpec((tm, d), row),
                  pl.BlockSpec((SGU_GROUPS, SGU_BLOCK, SGU_BLOCK), const3),
                  pl.BlockSpec((SGU_BLOCK, SGU_WIDTH), const2),
                  pl.BlockSpec((1, SGU_WIDTH), const2),
                  pl.BlockSpec((1, SGU_WIDTH), const2),
                  pl.BlockSpec((SB_WIDTH, d), const2),
                  pl.BlockSpec((SGU_WIDTH, d), const2),
                  pl.BlockSpec((d, d), const2),
                  pl.BlockSpec((1, 1, d), lambda i: (bidx(i) * 6 + 2, 0, 0)),
                  pl.BlockSpec((1, d), const2),
                  pl.BlockSpec((1, d), const2),
                  pl.BlockSpec((1, 1, d), lambda i: (bidx(i) * 6 + 3, 0, 0)),
                  pl.BlockSpec((1, 1, d), lambda i: (bidx(i) * 6 + 4, 0, 0)),
                  pl.BlockSpec((d, LANES), const2),
                  pl.BlockSpec((1, LANES), const2)],
        out_specs=[pl.BlockSpec((tm, d), row),
                   pl.BlockSpec((tm, d), row),
                   pl.BlockSpec((tm, LANES), row),
                   pl.BlockSpec((tm, LANES), row),
                   pl.BlockSpec((tm, LANES), row)],
        out_shape=[jax.ShapeDtypeStruct((n, d), F32),
                   jax.ShapeDtypeStruct((n, d), BF16),
                   jax.ShapeDtypeStruct((n, LANES), F32),
                   jax.ShapeDtypeStruct((n, LANES), BF16),
                   jax.ShapeDtypeStruct((n, LANES), BF16)],
        compiler_params=_cparams("arbitrary"),
        name="mix",
    )(out_a, proj, proj, proj, proj, x2, sgu_w, sgu_bias_full, sgu_g, sgu_b,
      wpa, wpb, wo, mod3, ln1_g, ln1_b, mod3, mod3, wr, br)


def _plan_kernel(oh1_ref, oh2_ref, dest_ref, blk_ref, cnt_ref, start_ref, *, tm, bm, nblk_pad):
    phase = pl.program_id(0)
    i = pl.program_id(1)
    oh1 = oh1_ref[...]
    oh2 = oh2_ref[...]
    both = oh1 + oh2
    lane = lax.broadcasted_iota(I32, (SUBLANES, LANES), 1)

    @pl.when((phase == 0) & (i == 0))
    def _():
        cnt_ref[...] = jnp.zeros_like(cnt_ref)

    @pl.when(phase == 0)
    def _():
        ones = jnp.ones((SUBLANES, tm), BF16)
        cnt_ref[...] += jnp.dot(ones, both, preferred_element_type=F32)

    @pl.when((phase == 1) & (i == 0))
    def _():
        cnt = cnt_ref[...]
        padded = jnp.floor((cnt + (bm - 1)) * (1.0 / bm)) * bm
        r = lax.broadcasted_iota(I32, (LANES, LANES), 0)
        c = lax.broadcasted_iota(I32, (LANES, LANES), 1)
        upper = (r < c).astype(F32)
        starts = jnp.dot(padded, upper, preferred_element_type=F32,
                         precision=lax.Precision.HIGHEST)
        start_ref[...] = starts
        cnt_ref[...] = jnp.zeros_like(cnt_ref)
        ends = starts + padded
        is_exp = (lane[0:1] >= EXPERT_LANE0) & (lane[0:1] < EXPERT_LANE0 + N_EXPERTS)
        bstart = (lax.broadcasted_iota(I32, (nblk_pad, LANES), 0) * bm).astype(F32)
        done = jnp.where(is_exp & (ends[0:1] <= bstart), 1.0, 0.0)
        bexp = jnp.minimum(jnp.sum(done, axis=1, keepdims=True), N_EXPERTS - 1.0)
        total = jnp.sum(jnp.where(is_exp, padded[0:1], 0.0), axis=1, keepdims=True)
        blane = lax.broadcasted_iota(I32, (nblk_pad, LANES), 1)
        blk = jnp.where(blane == 0, bexp, jnp.where(blane == 1, total * (1.0 / bm), 0.0))
        blk_ref[...] = blk.astype(I32)

    @pl.when(phase == 1)
    def _():
        r = lax.broadcasted_iota(I32, (tm, tm), 0)
        c = lax.broadcasted_iota(I32, (tm, tm), 1)
        lower = (c < r).astype(BF16)
        before = jnp.dot(lower, both, preferred_element_type=F32) + cnt_ref[0:1]
        pos = before + start_ref[0:1]
        d1 = jnp.sum(pos * oh1.astype(F32), axis=1, keepdims=True)
        d2 = jnp.sum(pos * oh2.astype(F32), axis=1, keepdims=True)
        dlane = lax.broadcasted_iota(I32, (tm, LANES), 1)
        dest = jnp.where(dlane == 0, d1, jnp.where(dlane == 1, d2, 0.0))
        dest_ref[...] = dest.astype(I32)
        ones = jnp.ones((SUBLANES, tm), BF16)
        cnt_ref[...] += jnp.dot(ones, both, preferred_element_type=F32)


def _plan(oh1, oh2, bm, nblk, tm=512):
    n = oh1.shape[0]
    tm = min(tm, n)
    nblk_pad = -(-nblk // SUBLANES) * SUBLANES
    dest, blk = pl.pallas_call(
        functools.partial(_plan_kernel, tm=tm, bm=bm, nblk_pad=nblk_pad),
        grid=(2, n // tm),
        in_specs=[pl.BlockSpec((tm, LANES), lambda p, i: (i, 0)),
                  pl.BlockSpec((tm, LANES), lambda p, i: (i, 0))],
        out_specs=[pl.BlockSpec((tm, LANES), lambda p, i: (i * p, 0)),
                   pl.BlockSpec((nblk_pad, LANES), lambda p, i: (0, 0))],
        out_shape=[jax.ShapeDtypeStruct((n, LANES), I32),
                   jax.ShapeDtypeStruct((nblk_pad, LANES), I32)],
        scratch_shapes=[pltpu.VMEM((SUBLANES, LANES), F32),
                        pltpu.VMEM((SUBLANES, LANES), F32)],
        compiler_params=_cparams("arbitrary", "arbitrary"),
        name="plan",
    )(oh1, oh2)
    return dest, blk


def _dispatch_kernel(d1_ref, d2_ref, h_ref, xs_in_ref, xs_ref, sem, *, tm):
    del xs_in_ref
    base = pl.program_id(0) * tm

    def row_copy(t, dref):
        return pltpu.make_async_copy(h_ref.at[t], xs_ref.at[dref[base + t]], sem)

    def issue(t, carry):
        row_copy(t, d1_ref).start()
        row_copy(t, d2_ref).start()
        return carry

    def drain(t, carry):
        row_copy(t, d1_ref).wait()
        row_copy(t, d2_ref).wait()
        return carry

    lax.fori_loop(0, tm, issue, 0)
    lax.fori_loop(0, tm, drain, 0)


def _dispatch(dest1, dest2, h2r, n_rows, tm=256):
    n, sub, lanes = h2r.shape
    tm = min(tm, n)
    xs0 = jnp.zeros((n_rows, sub, lanes), h2r.dtype)
    return pl.pallas_call(
        functools.partial(_dispatch_kernel, tm=tm),
        grid_spec=pltpu.PrefetchScalarGridSpec(
            num_scalar_prefetch=2,
            grid=(n // tm,),
            in_specs=[pl.BlockSpec((tm, sub, lanes), lambda i, d1, d2: (i, 0, 0)),
                      pl.BlockSpec(memory_space=pl.ANY)],
            out_specs=pl.BlockSpec(memory_space=pl.ANY),
            scratch_shapes=[pltpu.SemaphoreType.DMA(())],
        ),
        out_shape=jax.ShapeDtypeStruct((n_rows, sub, lanes), h2r.dtype),
        input_output_aliases={3: 0},
        compiler_params=_cparams("arbitrary"),
        name="dispatch",
    )(dest1, dest2, h2r, xs0)


def _experts_kernel(be_ref, na_ref, x_ref, wg_ref, wu_ref, wd_ref, o_ref, wgb, wub, wdb):
    i = pl.program_id(0)

    @pl.when(i < na_ref[0])
    def _():
        prev = be_ref[jnp.maximum(i - 1, 0)]

        @pl.when((i == 0) | (be_ref[i] != prev))
        def _():
            wgb[...] = wg_ref[...].astype(BF16)
            wub[...] = wu_ref[...].astype(BF16)
            wdb[...] = wd_ref[...].astype(BF16)

        x = x_ref[...]
        g = jnp.dot(x, wgb[...], preferred_element_type=F32)
        u = jnp.dot(x, wub[...], preferred_element_type=F32)
        hb = (g * _sigmoid(g) * u).astype(BF16)
        o_ref[...] = jnp.dot(hb, wdb[...], preferred_element_type=F32).astype(o_ref.dtype)

    @pl.when(i >= na_ref[0])
    def _():
        o_ref[...] = jnp.zeros_like(o_ref)


def _experts(block_expert, n_active, xs, w_gate, w_up, w_down, bm):
    n_rows, d = xs.shape
    nblk = n_rows // bm
    de = w_gate.shape[2]

    def blk(i, be, na):
        return jnp.maximum(jnp.minimum(i, na[0] - 1), 0)

    return pl.pallas_call(
        _experts_kernel,
        grid_spec=pltpu.PrefetchScalarGridSpec(
            num_scalar_prefetch=2,
            grid=(nblk,),
            in_specs=[pl.BlockSpec((bm, d), lambda i, be, na: (blk(i, be, na), 0)),
                      pl.BlockSpec((None, d, de), lambda i, be, na: (be[blk(i, be, na)], 0, 0)),
                      pl.BlockSpec((None, d, de), lambda i, be, na: (be[blk(i, be, na)], 0, 0)),
                      pl.BlockSpec((None, de, d), lambda i, be, na: (be[blk(i, be, na)], 0, 0))],
            out_specs=pl.BlockSpec((bm, d), lambda i, be, na: (i, 0)),
            scratch_shapes=[pltpu.VMEM((d, de), BF16),
                            pltpu.VMEM((d, de), BF16),
                            pltpu.VMEM((de, d), BF16)],
        ),
        out_shape=jax.ShapeDtypeStruct((n_rows, d), BF16),
        compiler_params=_cparams("arbitrary"),
        name="experts",
    )(block_expert, n_active, xs, w_gate, w_up, w_down)


def _combine_kernel(d1_ref, d2_ref, ys_ref, x1_ref, w1_ref, w2_ref, g2_ref, lg_ref, lb_ref,
                    o_ref, y1buf, y2buf, sem, *, tm, alpha):
    base = pl.program_id(0) * tm

    def row_copy(t, dref, buf):
        return pltpu.make_async_copy(ys_ref.at[dref[base + t]], buf.at[t], sem)

    def issue(t, carry):
        row_copy(t, d1_ref, y1buf).start()
        row_copy(t, d2_ref, y2buf).start()
        return carry

    def drain(t, carry):
        row_copy(t, d1_ref, y1buf).wait()
        row_copy(t, d2_ref, y2buf).wait()
        return carry

    lax.fori_loop(0, tm, issue, 0)
    lax.fori_loop(0, tm, drain, 0)

    y = w1_ref[...] * y1buf[...].astype(F32) + w2_ref[...] * y2buf[...].astype(F32)
    r = alpha * x1_ref[...] + g2_ref[...] * y
    inv_d = 1.0 / (r.shape[1] * r.shape[2])
    mu = jnp.sum(jnp.sum(r, axis=2, keepdims=True), axis=1, keepdims=True) * inv_d
    rc = r - mu
    var = jnp.sum(jnp.sum(rc * rc, axis=2, keepdims=True), axis=1, keepdims=True) * inv_d
    o_ref[...] = rc * lax.rsqrt(var + LN_EPS) * lg_ref[...] + lb_ref[...]


def _combine(dest1, dest2, ys3, x1r, w1b, w2b, gate2r, ln2_g, ln2_b, seq, alpha, tm=256):
    n, sub, lanes = x1r.shape
    tm = min(tm, seq)
    tok = lambda i, d1, d2: (i, 0, 0)
    const = lambda i, d1, d2: (0, 0, 0)
    return pl.pallas_call(
        functools.partial(_combine_kernel, tm=tm, alpha=alpha),
        grid_spec=pltpu.PrefetchScalarGridSpec(
            num_scalar_prefetch=2,
            grid=(n // tm,),
            in_specs=[pl.BlockSpec(memory_space=pl.ANY),
                      pl.BlockSpec((tm, sub, lanes), tok),
                      pl.BlockSpec((tm, 1, lanes), tok),
                      pl.BlockSpec((tm, 1, lanes), tok),
                      pl.BlockSpec((1, sub, lanes), lambda i, d1, d2: ((i * tm) // seq, 0, 0)),
                      pl.BlockSpec((1, sub, lanes), const),
                      pl.BlockSpec((1, sub, lanes), const)],
            out_specs=pl.BlockSpec((tm, sub, lanes), tok),
            scratch_shapes=[pltpu.VMEM((tm, sub, lanes), ys3.dtype),
                            pltpu.VMEM((tm, sub, lanes), ys3.dtype),
                            pltpu.SemaphoreType.DMA(())],
        ),
        out_shape=jax.ShapeDtypeStruct((n, sub, lanes), F32),
        compiler_params=_cparams("arbitrary"),
        name="combine",
    )(dest1, dest2, ys3, x1r, w1b, w2b, gate2r, ln2_g, ln2_b)


EXPERT_BLOCK_ROWS = 256


def kernel(x, c, w_ada, b_ada, w_in, sgu_w, sgu_b, sgu_ln_g, sgu_ln_b, w_proj_a, w_proj_b,
           w_out, ln1_g, ln1_b, w_group, b_group, w_router, b_router, w_gate, w_up, w_down,
           ln2_g, ln2_b):
    b, s, d = x.shape
    n = b * s
    depth = w_ada.shape[0]
    alpha = (2.0 * depth) ** 0.25
    sub = d // LANES
    bm = EXPERT_BLOCK_ROWS
    n_assign = 2 * n
    n_rows = n_assign + N_EXPERTS * bm
    nblk = n_rows // bm

    in_cols = w_in.shape[2]
    colscale = jnp.ones((1, in_cols), F32).at[:, _COL_Q(d):_COL_K(d)].set(
        SB_HEAD_DIM ** -0.5 * LOG2E)
    n_mix_cols = 3 * SB_WIDTH + 2 * SGU_WIDTH

    for l in range(depth):
        mod = _ada(c, w_ada[l], b_ada[l])
        mod3 = mod.reshape(b * 6, 1, d)

        x2 = x.reshape(n, d)
        w_in_l = jnp.concatenate([w_in[l][:, n_mix_cols:], w_in[l][:, :n_mix_cols]],
                                 axis=1).astype(BF16)
        proj = _inproj(x2, mod3, w_in_l, colscale, s)
        out_a = _attention(proj.reshape(b, s, in_cols), d).reshape(n, SB_WIDTH)

        wr = jnp.zeros((d, LANES), F32)
        wr = wr.at[:, GROUP_LANE0:GROUP_LANE0 + N_GROUPS].set(w_group[l])
        wr = wr.at[:, EXPERT_LANE0:EXPERT_LANE0 + N_EXPERTS].set(w_router[l])
        br = jnp.zeros((1, LANES), F32)
        br = br.at[0, GROUP_LANE0:GROUP_LANE0 + N_GROUPS].set(b_group[l])
        br = br.at[0, EXPERT_LANE0:EXPERT_LANE0 + N_EXPERTS].set(b_router[l])
        sgu_bias_full = jnp.repeat(sgu_b[l].T, SGU_GROUP_DIM, axis=1)

        x1, h2, route, oh1, oh2 = _mix(
            out_a, proj, x2, mod3, sgu_w[l], sgu_bias_full,
            sgu_ln_g[l].reshape(1, -1), sgu_ln_b[l].reshape(1, -1),
            w_proj_a[l].astype(BF16), w_proj_b[l].astype(BF16), w_out[l].astype(BF16),
            ln1_g[l].reshape(1, d), ln1_b[l].reshape(1, d), wr, br, s, alpha)

        dest, blk = _plan(oh1, oh2, bm, nblk)
        dest1 = dest[:, 0]
        dest2 = dest[:, 1]
        block_expert = blk[:nblk, 0]
        n_active = blk[0:1, 1]

        xs = _dispatch(dest1, dest2, h2.reshape(n, sub, LANES), n_rows)
        ys = _experts(block_expert, n_active, xs.reshape(n_rows, d),
                      w_gate[l], w_up[l], w_down[l], bm)

        w1b = jnp.broadcast_to(route[:, 0][:, None, None], (n, 1, LANES))
        w2b = jnp.broadcast_to(route[:, 1][:, None, None], (n, 1, LANES))
        gate2r = mod[:, 5 * d:6 * d].reshape(b, sub, LANES)
        out = _combine(dest1, dest2, ys.reshape(n_rows, sub, LANES), x1.reshape(n, sub, LANES),
                       w1b, w2b, gate2r, ln2_g[l].reshape(1, sub, LANES),
                       ln2_b[l].reshape(1, sub, LANES), s, alpha)
        x = out.reshape(b, s, d)
    return x
```

```python
import functools
import math

import jax
import jax.numpy as jnp
from jax import lax
from jax.experimental import pallas as pl
from jax.experimental.pallas import tpu as pltpu

F32 = jnp.float32
BF16 = jnp.bfloat16
I32 = jnp.int32

CHUNK = 64
SB_HEADS = 8
SB_HEAD_DIM = 128
SB_WIDTH = SB_HEADS * SB_HEAD_DIM
SGU_GROUPS = 8
SGU_GROUP_DIM = 128
SGU_WIDTH = SGU_GROUPS * SGU_GROUP_DIM
SGU_BLOCK = 128
N_GROUPS = 4
EXPERTS_PER_GROUP = 8
N_EXPERTS = N_GROUPS * EXPERTS_PER_GROUP
D_EXPERT = 512
LN_EPS = 1e-5

LANES = 128
SUBLANES = 8
VMEM_LIMIT_BYTES = 60000 * 1024

GROUP_LANE0 = 0
EXPERT_LANE0 = N_GROUPS

LOG2E = 1.4426950408889634
NEG_BIG = -1e30

_COL_GA = 0


def _COL_GB(d):
    return d


def _COL_Q(d):
    return 2 * d


def _COL_K(d):
    return 2 * d + SB_WIDTH


def _COL_V(d):
    return 2 * d + 2 * SB_WIDTH


def _COL_SU(d):
    return 2 * d + 3 * SB_WIDTH


def _COL_SV(d):
    return 2 * d + 3 * SB_WIDTH + SGU_WIDTH


def _cparams(*sem):
    return pltpu.CompilerParams(dimension_semantics=sem, vmem_limit_bytes=VMEM_LIMIT_BYTES)


def _ln_rows(x):
    mu = jnp.mean(x, axis=-1, keepdims=True)
    xc = x - mu
    var = jnp.mean(xc * xc, axis=-1, keepdims=True)
    return xc * lax.rsqrt(var + LN_EPS)


def _sigmoid(x):
    return 1.0 / (1.0 + jnp.exp(-x))


def _ada_kernel(c_ref, w_ref, b_ref, o_ref):
    c = c_ref[...]
    ca = c * _sigmoid(c)
    o_ref[...] = jnp.dot(ca, w_ref[...], preferred_element_type=F32,
                         precision=lax.Precision.HIGHEST) + b_ref[...]


def _ada(c, w_ada, b_ada, tn=1024):
    b, d = c.shape
    n_out = w_ada.shape[1]
    cp = jnp.zeros((SUBLANES, d), F32).at[:b].set(c)
    out = pl.pallas_call(
        _ada_kernel,
        grid=(n_out // tn,),
        in_specs=[pl.BlockSpec((SUBLANES, d), lambda j: (0, 0)),
                  pl.BlockSpec((d, tn), lambda j: (0, j)),
                  pl.BlockSpec((1, tn), lambda j: (0, j))],
        out_specs=pl.BlockSpec((SUBLANES, tn), lambda j: (0, j)),
        out_shape=jax.ShapeDtypeStruct((SUBLANES, n_out), F32),
        compiler_params=_cparams("arbitrary"),
        name="ada",
    )(cp, w_ada, b_ada.reshape(1, n_out))
    return out[:b]


def _inproj_kernel(x_ref, sh_ref, sc_ref, w_ref, cs_ref, o_ref, h_ref):
    @pl.when(pl.program_id(1) == 0)
    def _():
        h = _ln_rows(x_ref[...]) * (1.0 + sc_ref[0]) + sh_ref[0]
        h_ref[...] = h.astype(BF16)

    acc = jnp.dot(h_ref[...], w_ref[...], preferred_element_type=F32)
    o_ref[...] = (acc * cs_ref[...]).astype(BF16)


def _inproj(x2, mod3, w_in_bf, colscale, seq, tm=1024, tn=1024):
    n, d = x2.shape
    n_cols = w_in_bf.shape[1]
    tm = min(tm, seq)
    return pl.pallas_call(
        _inproj_kernel,
        grid=(n // tm, n_cols // tn),
        in_specs=[pl.BlockSpec((tm, d), lambda i, j: (i, 0)),
                  pl.BlockSpec((1, 1, d), lambda i, j: (((i * tm) // seq) * 6 + 0, 0, 0)),
                  pl.BlockSpec((1, 1, d), lambda i, j: (((i * tm) // seq) * 6 + 1, 0, 0)),
                  pl.BlockSpec((d, tn), lambda i, j: (0, j)),
                  pl.BlockSpec((1, tn), lambda i, j: (0, j))],
        out_specs=pl.BlockSpec((tm, tn), lambda i, j: (i, j)),
        out_shape=jax.ShapeDtypeStruct((n, n_cols), BF16),
        scratch_shapes=[pltpu.VMEM((tm, d), BF16)],
        compiler_params=_cparams("arbitrary", "arbitrary"),
        name="inproj",
    )(x2, mod3, mod3, w_in_bf, colscale)


Z2_MAX = 100.0
STICK_EXIT_LOG2 = 150.0


def _attn_kernel(q_ref, k_ref, v_ref, o_ref, *, tq, tk, heads):
    qi = pl.program_id(2)
    nd = tq // tk
    jd = qi * nd
    trow = lax.broadcasted_iota(I32, (tk, tk), 0)
    tcol = lax.broadcasted_iota(I32, (tk, tk), 1)
    tri = (trow > tcol).astype(BF16)
    hd = SB_HEAD_DIM
    qs = [q_ref[0, :, h * hd:(h + 1) * hd] for h in range(heads)]

    def block(j, runs, accs, causal):
        off = pl.multiple_of(j * tk, tk)
        hs = range(heads)
        z2s = [lax.dot_general(qs[h], k_ref[0, pl.ds(off, tk), h * hd:(h + 1) * hd],
                               (((1,), (1,)), ((), ())), preferred_element_type=F32) for h in hs]
        z2s = [lax.clamp(-Z2_MAX, z2, Z2_MAX) for z2 in z2s]
        sps = [jnp.log(1.0 + jnp.exp2(z2)) for z2 in z2s]
        if causal is not None:
            sps = [jnp.where(causal, sp, 0.0) for sp in sps]
        laters = [jnp.dot(sp.astype(BF16), tri, preferred_element_type=F32) for sp in sps]
        ws = [jnp.exp2(z2s[h] - LOG2E * (sps[h] + (laters[h] + runs[h]))) for h in hs]
        if causal is not None:
            ws = [jnp.where(causal, w, 0.0) for w in ws]
        new_accs = [accs[h] + jnp.dot(ws[h].astype(BF16),
                                      v_ref[0, pl.ds(off, tk), h * hd:(h + 1) * hd],
                                      preferred_element_type=F32) for h in hs]
        new_runs = [runs[h] + jnp.sum(sps[h], axis=1, keepdims=True) for h in hs]
        return tuple(new_runs), tuple(new_accs)

    def mass_left(runs):
        low = runs[0]
        for r in runs[1:]:
            low = jnp.minimum(low, r)
        return (jnp.min(low) * LOG2E < STICK_EXIT_LOG2).astype(I32)

    def diag_body(t, carry):
        c = nd - 1 - t
        row = lax.broadcasted_iota(I32, (tq, tk), 0)
        col = lax.broadcasted_iota(I32, (tq, tk), 1)
        causal = col + c * tk < row
        return block(jd + c, carry[0], carry[1], causal)

    def more(carry):
        t, go, _, _ = carry
        return (t < jd) & (go > 0)

    def body(carry):
        t, _, runs, accs = carry
        runs, accs = block(jd - 1 - t, runs, accs, None)
        return t + 1, mass_left(runs), runs, accs

    runs = tuple(jnp.zeros((tq, 1), F32) for _ in range(heads))
    accs = tuple(jnp.zeros((tq, hd), F32) for _ in range(heads))
    runs, accs = lax.fori_loop(0, nd, diag_body, (runs, accs))
    _, _, runs, accs = lax.while_loop(more, body, (jnp.int32(0), mass_left(runs), runs, accs))
    for h in range(heads):
        o_ref[0, :, h * hd:(h + 1) * hd] = accs[h].astype(BF16)


def _attention(proj3, d, tq=256, tk=256, heads=4):
    b, s, _ = proj3.shape
    tk = min(tk, s)
    tq = min(tq, s)
    assert tq % tk == 0 and s % tq == 0
    width = heads * SB_HEAD_DIM
    qc = _COL_Q(d) // width
    kc = _COL_K(d) // width
    vc = _COL_V(d) // width
    return pl.pallas_call(
        functools.partial(_attn_kernel, tq=tq, tk=tk, heads=heads),
        grid=(b, SB_HEADS // heads, s // tq),
        in_specs=[pl.BlockSpec((1, tq, width), lambda bi, g, qi: (bi, qi, qc + g)),
                  pl.BlockSpec((1, s, width), lambda bi, g, qi: (bi, 0, kc + g)),
                  pl.BlockSpec((1, s, width), lambda bi, g, qi: (bi, 0, vc + g))],
        out_specs=pl.BlockSpec((1, tq, width), lambda bi, g, qi: (bi, qi, g)),
        out_shape=jax.ShapeDtypeStruct((b, s, SB_WIDTH), BF16),
        compiler_params=_cparams("arbitrary", "arbitrary", "arbitrary"),
        name="attn",
    )(proj3, proj3, proj3)


def _erf(x):
    return lax.erf(x)


def _gelu(x):
    return 0.5 * x * (1.0 + _erf(x * (1.0 / math.sqrt(2.0))))


def _mix_kernel(oa_ref, su_ref, sv_ref, ga_ref, gb_ref, x_ref,
                sw_ref, sbias_ref, sg_ref, sb_ref,
                wpa_ref, wpb_ref, wo_ref,
                g1_ref, ln1g_ref, ln1b_ref, sh2_ref, sc2_ref,
                wr_ref, br_ref,
                x1_ref, h2_ref, route_ref, oh1_ref, oh2_ref, *, tm, alpha):
    u = _gelu(su_ref[...].astype(F32))
    v = _gelu(sv_ref[...].astype(F32))
    v = _ln_rows(v) * sg_ref[...] + sb_ref[...]
    vb = v.astype(BF16)
    pos_t = lax.broadcasted_iota(I32, (SGU_BLOCK, SGU_BLOCK), 0)
    pos_s = lax.broadcasted_iota(I32, (SGU_BLOCK, SGU_BLOCK), 1)
    chunk_causal = (pos_s // CHUNK) <= (pos_t // CHUNK)
    n_pos_blocks = tm // SGU_BLOCK
    per_group = []
    for g in range(SGU_GROUPS):
        c0 = g * SGU_GROUP_DIM
        wg = jnp.where(chunk_causal, sw_ref[g], 0.0).astype(BF16)
        rhs = jnp.concatenate(
            [vb[nb * SGU_BLOCK:(nb + 1) * SGU_BLOCK, c0:c0 + SGU_GROUP_DIM]
             for nb in range(n_pos_blocks)], axis=1)
        per_group.append(jnp.dot(wg, rhs, preferred_element_type=F32))
    mixed = jnp.concatenate(
        [jnp.concatenate([per_group[g][:, nb * SGU_GROUP_DIM:(nb + 1) * SGU_GROUP_DIM]
                          for g in range(SGU_GROUPS)], axis=1) + sbias_ref[...]
         for nb in range(n_pos_blocks)], axis=0)
    out_b = (u * mixed).astype(BF16)

    pa = jnp.dot(oa_ref[...], wpa_ref[...], preferred_element_type=F32)
    pb = jnp.dot(out_b, wpb_ref[...], preferred_element_type=F32)
    merged = (_sigmoid(ga_ref[...].astype(F32)) * pa
              + _sigmoid(gb_ref[...].astype(F32)) * pb)
    y = jnp.dot(merged.astype(BF16), wo_ref[...], preferred_element_type=F32)

    r = alpha * x_ref[...] + g1_ref[0] * y
    x1 = _ln_rows(r) * ln1g_ref[...] + ln1b_ref[...]
    x1_ref[...] = x1
    h2 = _ln_rows(x1) * (1.0 + sc2_ref[0]) + sh2_ref[0]
    chunks = h2.shape[1] // LANES
    for c in range(chunks):
        h2_ref[pl.ds(c, tm, stride=chunks), :] = h2[:, c * LANES:(c + 1) * LANES]

    h_hi = h2.astype(BF16)
    h_lo = (h2 - h_hi.astype(F32)).astype(BF16)
    hi_w = jnp.dot(h_hi, wr_ref[...], preferred_element_type=F32)
    lo_w = jnp.dot(h_lo, wr_ref[:, :LANES], preferred_element_type=F32)
    logits = (hi_w[:, :LANES] + hi_w[:, LANES:]) + lo_w + br_ref[...]
    lane = lax.broadcasted_iota(I32, logits.shape, 1)
    gl = jnp.where(lane < N_GROUPS, logits, NEG_BIG)
    gmax = jnp.max(gl, axis=1, keepdims=True)
    g_sel = jnp.min(jnp.where(gl == gmax, lane, LANES), axis=1, keepdims=True)
    p_group = 1.0 / jnp.sum(jnp.exp(gl - gmax), axis=1, keepdims=True)
    lo_lane = EXPERT_LANE0 + g_sel * EXPERTS_PER_GROUP
    in_group = (lane >= lo_lane) & (lane < lo_lane + EXPERTS_PER_GROUP)
    el = jnp.where(in_group, logits, NEG_BIG)
    m1 = jnp.max(el, axis=1, keepdims=True)
    i1 = jnp.min(jnp.where(el == m1, lane, LANES), axis=1, keepdims=True)
    el2 = jnp.where(lane == i1, NEG_BIG, el)
    m2 = jnp.max(el2, axis=1, keepdims=True)
    i2 = jnp.min(jnp.where(el2 == m2, lane, LANES), axis=1, keepdims=True)
    e21 = jnp.exp(m2 - m1)
    w1 = p_group / (1.0 + e21)
    w2 = p_group * e21 / (1.0 + e21)
    route = jnp.where(lane == 0, w1, 0.0)
    route = jnp.where(lane == 1, w2, route)
    route = jnp.where(lane == 2, (i1 - EXPERT_LANE0).astype(F32), route)
    route = jnp.where(lane == 3, (i2 - EXPERT_LANE0).astype(F32), route)
    route_ref[...] = route
    oh1_ref[...] = jnp.where(lane == i1, 1.0, 0.0).astype(BF16)
    oh2_ref[...] = jnp.where(lane == i2, 1.0, 0.0).astype(BF16)


def _mix(out_a, proj, x2, mod3, sgu_w, sgu_bias_full, sgu_g, sgu_b, wpa, wpb, wo,
         ln1_g, ln1_b, wr, br, seq, alpha, tm=256):
    n, d = x2.shape
    tm = min(tm, seq)
    def pcol(start, width):
        assert start % width == 0
        return start // width
    su_c = pcol(_COL_SU(d), SGU_WIDTH)
    sv_c = pcol(_COL_SV(d), SGU_WIDTH)
    ga_c = pcol(_COL_GA, d)
    gb_c = pcol(_COL_GB(d), d)
    bidx = lambda i: (i * tm) // seq
    row = lambda i: (i, 0)
    const2 = lambda i: (0, 0)
    const3 = lambda i: (0, 0, 0)
    return pl.pallas_call(
        functools.partial(_mix_kernel, tm=tm, alpha=alpha),
        grid=(n // tm,),
        in_specs=[pl.BlockSpec((tm, SB_WIDTH), row),
                  pl.BlockSpec((tm, SGU_WIDTH), lambda i: (i, su_c)),
                  pl.BlockSpec((tm, SGU_WIDTH), lambda i: (i, sv_c)),
                  pl.BlockSpec((tm, d), lambda i: (i, ga_c)),
                  pl.BlockSpec((tm, d), lambda i: (i, gb_c)),
                  pl.BlockSpec((tm, d), row),
                  pl.BlockSpec((SGU_GROUPS, SGU_BLOCK, SGU_BLOCK), const3),
                  pl.BlockSpec((SGU_BLOCK, SGU_WIDTH), const2),
                  pl.BlockSpec((1, SGU_WIDTH), const2),
                  pl.BlockSpec((1, SGU_WIDTH), const2),
                  pl.BlockSpec((SB_WIDTH, d), const2),
                  pl.BlockSpec((SGU_WIDTH, d), const2),
                  pl.BlockSpec((d, d), const2),
                  pl.BlockSpec((1, 1, d), lambda i: (bidx(i) * 6 + 2, 0, 0)),
                  pl.BlockSpec((1, d), const2),
                  pl.BlockSpec((1, d), const2),
                  pl.BlockSpec((1, 1, d), lambda i: (bidx(i) * 6 + 3, 0, 0)),
                  pl.BlockSpec((1, 1, d), lambda i: (bidx(i) * 6 + 4, 0, 0)),
                  pl.BlockSpec((d, 2 * LANES), const2),
                  pl.BlockSpec((1, LANES), const2)],
        out_specs=[pl.BlockSpec((tm, d), row),
                   pl.BlockSpec((tm * (d // LANES), LANES), row),
                   pl.BlockSpec((tm, LANES), row),
                   pl.BlockSpec((tm, LANES), row),
                   pl.BlockSpec((tm, LANES), row)],
        out_shape=[jax.ShapeDtypeStruct((n, d), F32),
                   jax.ShapeDtypeStruct((n * (d // LANES), LANES), F32),
                   jax.ShapeDtypeStruct((n, LANES), F32),
                   jax.ShapeDtypeStruct((n, LANES), BF16),
                   jax.ShapeDtypeStruct((n, LANES), BF16)],
        compiler_params=_cparams("arbitrary"),
        name="mix",
    )(out_a, proj, proj, proj, proj, x2, sgu_w, sgu_bias_full, sgu_g, sgu_b,
      wpa, wpb, wo, mod3, ln1_g, ln1_b, mod3, mod3, wr, br)


def _plan_kernel(oh1_ref, oh2_ref, dest_ref, blk_ref, cnt_ref, start_ref, *, tm, bm, nblk_pad):
    phase = pl.program_id(0)
    i = pl.program_id(1)
    oh1 = oh1_ref[...]
    oh2 = oh2_ref[...]
    both = oh1 + oh2
    lane = lax.broadcasted_iota(I32, (SUBLANES, LANES), 1)

    @pl.when((phase == 0) & (i == 0))
    def _():
        cnt_ref[...] = jnp.zeros_like(cnt_ref)

    @pl.when(phase == 0)
    def _():
        ones = jnp.ones((SUBLANES, tm), BF16)
        cnt_ref[...] += jnp.dot(ones, both, preferred_element_type=F32)

    @pl.when((phase == 1) & (i == 0))
    def _():
        cnt = cnt_ref[...]
        padded = jnp.floor((cnt + (bm - 1)) * (1.0 / bm)) * bm
        r = lax.broadcasted_iota(I32, (LANES, LANES), 0)
        c = lax.broadcasted_iota(I32, (LANES, LANES), 1)
        upper = (r < c).astype(F32)
        starts = jnp.dot(padded, upper, preferred_element_type=F32,
                         precision=lax.Precision.HIGHEST)
        start_ref[...] = starts
        cnt_ref[...] = jnp.zeros_like(cnt_ref)
        ends = starts + padded
        is_exp = (lane[0:1] >= EXPERT_LANE0) & (lane[0:1] < EXPERT_LANE0 + N_EXPERTS)
        bstart = (lax.broadcasted_iota(I32, (nblk_pad, LANES), 0) * bm).astype(F32)
        done = jnp.where(is_exp & (ends[0:1] <= bstart), 1.0, 0.0)
        bexp = jnp.minimum(jnp.sum(done, axis=1, keepdims=True), N_EXPERTS - 1.0)
        total = jnp.sum(jnp.where(is_exp, padded[0:1], 0.0), axis=1, keepdims=True)
        blane = lax.broadcasted_iota(I32, (nblk_pad, LANES), 1)
        blk = jnp.where(blane == 0, bexp, jnp.where(blane == 1, total * (1.0 / bm), 0.0))
        blk_ref[...] = blk.astype(I32)

    @pl.when(phase == 1)
    def _():
        r = lax.broadcasted_iota(I32, (tm, tm), 0)
        c = lax.broadcasted_iota(I32, (tm, tm), 1)
        lower = (c < r).astype(BF16)
        before = jnp.dot(lower, both, preferred_element_type=F32) + cnt_ref[0:1]
        pos = before + start_ref[0:1]
        d1 = jnp.sum(pos * oh1.astype(F32), axis=1, keepdims=True)
        d2 = jnp.sum(pos * oh2.astype(F32), axis=1, keepdims=True)
        dlane = lax.broadcasted_iota(I32, (tm, LANES), 1)
        dest = jnp.where(dlane == 0, d1, jnp.where(dlane == 1, d2, 0.0))
        dest_ref[...] = dest.astype(I32)
        ones = jnp.ones((SUBLANES, tm), BF16)
        cnt_ref[...] += jnp.dot(ones, both, preferred_element_type=F32)


def _plan(oh1, oh2, bm, nblk, tm=512):
    n = oh1.shape[0]
    tm = min(tm, n)
    nblk_pad = -(-nblk // SUBLANES) * SUBLANES
    dest, blk = pl.pallas_call(
        functools.partial(_plan_kernel, tm=tm, bm=bm, nblk_pad=nblk_pad),
        grid=(2, n // tm),
        in_specs=[pl.BlockSpec((tm, LANES), lambda p, i: (i, 0)),
                  pl.BlockSpec((tm, LANES), lambda p, i: (i, 0))],
        out_specs=[pl.BlockSpec((tm, LANES), lambda p, i: (i * p, 0)),
                   pl.BlockSpec((nblk_pad, LANES), lambda p, i: (0, 0))],
        out_shape=[jax.ShapeDtypeStruct((n, LANES), I32),
                   jax.ShapeDtypeStruct((nblk_pad, LANES), I32)],
        scratch_shapes=[pltpu.VMEM((SUBLANES, LANES), F32),
                        pltpu.VMEM((SUBLANES, LANES), F32)],
        compiler_params=_cparams("arbitrary", "arbitrary"),
        name="plan",
    )(oh1, oh2)
    return dest, blk


ROW_UNROLL = 8


def _invert_kernel(d1_ref, d2_ref, rt_ref, *, n, n_rows):
    def clear(i, carry):
        for u in range(ROW_UNROLL):
            rt_ref[i * ROW_UNROLL + u] = 0
        return carry

    def put(i, carry):
        for u in range(ROW_UNROLL):
            t = i * ROW_UNROLL + u
            rt_ref[d1_ref[t]] = t
            rt_ref[d2_ref[t]] = t
        return carry

    lax.fori_loop(0, n_rows // ROW_UNROLL, clear, 0)
    lax.fori_loop(0, n // ROW_UNROLL, put, 0)


def _invert(dest1, dest2, n_rows):
    n = dest1.shape[0]
    assert n % ROW_UNROLL == 0 and n_rows % ROW_UNROLL == 0
    return pl.pallas_call(
        functools.partial(_invert_kernel, n=n, n_rows=n_rows),
        grid_spec=pltpu.PrefetchScalarGridSpec(
            num_scalar_prefetch=2,
            grid=(1,),
            in_specs=[],
            out_specs=pl.BlockSpec(memory_space=pltpu.SMEM),
        ),
        out_shape=jax.ShapeDtypeStruct((n_rows,), I32),
        compiler_params=_cparams("arbitrary"),
        name="invert",
    )(dest1, dest2)


def _slab(idx, chunks):
    return pl.ds(pl.multiple_of(idx * chunks, chunks), chunks)


def _rows_from_slabs(buf_ref, slot, rows, chunks):
    return jnp.concatenate(
        [buf_ref[slot, pl.ds(c, rows, stride=chunks), :] for c in range(chunks)], axis=1)


def _experts_kernel(be_ref, na_ref, rt_ref, h_ref, wg_ref, wu_ref, wd_ref, o_ref,
                    xbuf, sem, wgb, wub, wdb, *, bm, chunks):
    i = pl.program_id(0)
    n_active = na_ref[0]

    def row_copy(blk, r, slot):
        return pltpu.make_async_copy(h_ref.at[_slab(rt_ref[blk * bm + r], chunks), :],
                                     xbuf.at[slot, _slab(r, chunks), :], sem.at[slot])

    def gather(blk, slot, start):
        def trip(r8, carry):
            for u in range(ROW_UNROLL):
                cp = row_copy(blk, r8 * ROW_UNROLL + u, slot)
                cp.start() if start else cp.wait()
            return carry
        lax.fori_loop(0, bm // ROW_UNROLL, trip, 0)

    @pl.when(i == 0)
    def _():
        gather(0, 0, True)

    @pl.when(i < n_active)
    def _():
        slot = lax.rem(i, 2)
        gather(i, slot, False)

        @pl.when(i + 1 < n_active)
        def _():
            gather(i + 1, 1 - slot, True)

        @pl.when((i == 0) | (be_ref[i] != be_ref[jnp.maximum(i - 1, 0)]))
        def _():
            wgb[...] = wg_ref[...].astype(BF16)
            wub[...] = wu_ref[...].astype(BF16)
            wdb[...] = wd_ref[...].astype(BF16)

        x = _rows_from_slabs(xbuf, slot, bm, chunks).astype(BF16)
        g = jnp.dot(x, wgb[...], preferred_element_type=F32)
        u = jnp.dot(x, wub[...], preferred_element_type=F32)
        hb = (g * _sigmoid(g) * u).astype(BF16)
        y = jnp.dot(hb, wdb[...], preferred_element_type=F32)
        for c in range(chunks):
            o_ref[pl.ds(c, bm, stride=chunks), :] = y[:, c * LANES:(c + 1) * LANES]

    @pl.when(i >= n_active)
    def _():
        o_ref[...] = jnp.zeros_like(o_ref)


def _experts(block_expert, n_active, row_tok, h2s, w_gate, w_up, w_down, bm):
    n_rows = row_tok.shape[0]
    nblk = n_rows // bm
    d, de = w_gate.shape[1], w_gate.shape[2]
    chunks = d // LANES
    assert bm % ROW_UNROLL == 0

    def blk(i, be, na, rt):
        return jnp.maximum(jnp.minimum(i, na[0] - 1), 0)

    return pl.pallas_call(
        functools.partial(_experts_kernel, bm=bm, chunks=chunks),
        grid_spec=pltpu.PrefetchScalarGridSpec(
            num_scalar_prefetch=3,
            grid=(nblk,),
            in_specs=[pl.BlockSpec(memory_space=pl.ANY),
                      pl.BlockSpec((None, d, de), lambda i, be, na, rt: (be[blk(i, be, na, rt)], 0, 0)),
                      pl.BlockSpec((None, d, de), lambda i, be, na, rt: (be[blk(i, be, na, rt)], 0, 0)),
                      pl.BlockSpec((None, de, d), lambda i, be, na, rt: (be[blk(i, be, na, rt)], 0, 0))],
            out_specs=pl.BlockSpec((bm * chunks, LANES), lambda i, be, na, rt: (i, 0)),
            scratch_shapes=[pltpu.VMEM((2, bm * chunks, LANES), F32),
                            pltpu.SemaphoreType.DMA((2,)),
                            pltpu.VMEM((d, de), BF16),
                            pltpu.VMEM((d, de), BF16),
                            pltpu.VMEM((de, d), BF16)],
        ),
        out_shape=jax.ShapeDtypeStruct((n_rows * chunks, LANES), F32),
        compiler_params=_cparams("arbitrary"),
        name="experts",
    )(block_expert, n_active, row_tok, h2s, w_gate, w_up, w_down)


def _combine_kernel(d1_ref, d2_ref, ys_ref, x1_ref, route_ref, g2_ref, lg_ref, lb_ref,
                    o_ref, y1buf, y2buf, sem, *, tm, chunks, alpha):
    i = pl.program_id(0)
    n_tiles = pl.num_programs(0)

    def row_copies(tile, t, slot):
        tok = tile * tm + t
        return (pltpu.make_async_copy(ys_ref.at[_slab(d1_ref[tok], chunks), :],
                                      y1buf.at[slot, _slab(t, chunks), :], sem.at[slot]),
                pltpu.make_async_copy(ys_ref.at[_slab(d2_ref[tok], chunks), :],
                                      y2buf.at[slot, _slab(t, chunks), :], sem.at[slot]))

    def gather(tile, slot, start):
        def trip(t8, carry):
            for u in range(ROW_UNROLL):
                for cp in row_copies(tile, t8 * ROW_UNROLL + u, slot):
                    cp.start() if start else cp.wait()
            return carry
        lax.fori_loop(0, tm // ROW_UNROLL, trip, 0)

    @pl.when(i == 0)
    def _():
        gather(0, 0, True)

    slot = lax.rem(i, 2)
    gather(i, slot, False)

    @pl.when(i + 1 < n_tiles)
    def _():
        gather(i + 1, 1 - slot, True)

    route = route_ref[...]
    y = (route[:, 0:1] * _rows_from_slabs(y1buf, slot, tm, chunks)
         + route[:, 1:2] * _rows_from_slabs(y2buf, slot, tm, chunks))
    r = alpha * x1_ref[...] + g2_ref[0] * y
    o_ref[...] = _ln_rows(r) * lg_ref[...] + lb_ref[...]


def _combine(dest1, dest2, ys, x1, route, mod3, ln2_g, ln2_b, seq, alpha, tm=256):
    n, d = x1.shape
    tm = min(tm, seq)
    chunks = d // LANES
    assert tm % ROW_UNROLL == 0
    tok = lambda i, d1, d2: (i, 0)
    const = lambda i, d1, d2: (0, 0)
    return pl.pallas_call(
        functools.partial(_combine_kernel, tm=tm, chunks=chunks, alpha=alpha),
        grid_spec=pltpu.PrefetchScalarGridSpec(
            num_scalar_prefetch=2,
            grid=(n // tm,),
            in_specs=[pl.BlockSpec(memory_space=pl.ANY),
                      pl.BlockSpec((tm, d), tok),
                      pl.BlockSpec((tm, LANES), tok),
                      pl.BlockSpec((1, 1, d), lambda i, d1, d2: (((i * tm) // seq) * 6 + 5, 0, 0)),
                      pl.BlockSpec((1, d), const),
                      pl.BlockSpec((1, d), const)],
            out_specs=pl.BlockSpec((tm, d), tok),
            scratch_shapes=[pltpu.VMEM((2, tm * chunks, LANES), F32),
                            pltpu.VMEM((2, tm * chunks, LANES), F32),
                            pltpu.SemaphoreType.DMA((2,))],
        ),
        out_shape=jax.ShapeDtypeStruct((n, d), F32),
        compiler_params=_cparams("arbitrary"),
        name="combine",
    )(dest1, dest2, ys, x1, route, mod3, ln2_g, ln2_b)


EXPERT_BLOCK_ROWS = 256


def kernel(x, c, w_ada, b_ada, w_in, sgu_w, sgu_b, sgu_ln_g, sgu_ln_b, w_proj_a, w_proj_b,
           w_out, ln1_g, ln1_b, w_group, b_group, w_router, b_router, w_gate, w_up, w_down,
           ln2_g, ln2_b):
    b, s, d = x.shape
    n = b * s
    depth = w_ada.shape[0]
    alpha = (2.0 * depth) ** 0.25
    sub = d // LANES
    bm = EXPERT_BLOCK_ROWS
    n_assign = 2 * n
    n_rows = n_assign + N_EXPERTS * bm
    nblk = n_rows // bm

    in_cols = w_in.shape[2]
    colscale = jnp.ones((1, in_cols), F32).at[:, _COL_Q(d):_COL_K(d)].set(
        SB_HEAD_DIM ** -0.5 * LOG2E)
    n_mix_cols = 3 * SB_WIDTH + 2 * SGU_WIDTH

    for l in range(depth):
        mod = _ada(c, w_ada[l], b_ada[l])
        mod3 = mod.reshape(b * 6, 1, d)

        x2 = x.reshape(n, d)
        w_in_l = jnp.concatenate([w_in[l][:, n_mix_cols:], w_in[l][:, :n_mix_cols]],
                                 axis=1).astype(BF16)
        proj = _inproj(x2, mod3, w_in_l, colscale, s)
        out_a = _attention(proj.reshape(b, s, in_cols), d).reshape(n, SB_WIDTH)

        wr = jnp.zeros((d, LANES), F32)
        wr = wr.at[:, GROUP_LANE0:GROUP_LANE0 + N_GROUPS].set(w_group[l])
        wr = wr.at[:, EXPERT_LANE0:EXPERT_LANE0 + N_EXPERTS].set(w_router[l])
        wr_hi = wr.astype(BF16)
        wr_lo = (wr - wr_hi.astype(F32)).astype(BF16)
        wr = jnp.concatenate([wr_hi, wr_lo], axis=1)
        br = jnp.zeros((1, LANES), F32)
        br = br.at[0, GROUP_LANE0:GROUP_LANE0 + N_GROUPS].set(b_group[l])
        br = br.at[0, EXPERT_LANE0:EXPERT_LANE0 + N_EXPERTS].set(b_router[l])
        sgu_bias_full = jnp.repeat(sgu_b[l].T, SGU_GROUP_DIM, axis=1)

        x1, h2s, route, oh1, oh2 = _mix(
            out_a, proj, x2, mod3, sgu_w[l], sgu_bias_full,
            sgu_ln_g[l].reshape(1, -1), sgu_ln_b[l].reshape(1, -1),
            w_proj_a[l].astype(BF16), w_proj_b[l].astype(BF16), w_out[l].astype(BF16),
            ln1_g[l].reshape(1, d), ln1_b[l].reshape(1, d), wr, br, s, alpha)

        dest, blk = _plan(oh1, oh2, bm, nblk)
        dest1 = dest[:, 0]
        dest2 = dest[:, 1]
        block_expert = blk[:nblk, 0]
        n_active = blk[0:1, 1]

        row_tok = _invert(dest1, dest2, n_rows)
        ys = _experts(block_expert, n_active, row_tok, h2s, w_gate[l], w_up[l], w_down[l], bm)
        out = _combine(dest1, dest2, ys, x1, route, mod3, ln2_g[l].reshape(1, d),
                       ln2_b[l].reshape(1, d), s, alpha)
        x = out.reshape(b, s, d)
    return x
```

```python
import functools
import math

import jax
import jax.numpy as jnp
from jax import lax
from jax.experimental import pallas as pl
from jax.experimental.pallas import tpu as pltpu

F32 = jnp.float32
BF16 = jnp.bfloat16
I32 = jnp.int32

CHUNK = 64
SB_HEADS = 8
SB_HEAD_DIM = 128
SB_WIDTH = SB_HEADS * SB_HEAD_DIM
SGU_GROUPS = 8
SGU_GROUP_DIM = 128
SGU_WIDTH = SGU_GROUPS * SGU_GROUP_DIM
SGU_BLOCK = 128
N_GROUPS = 4
EXPERTS_PER_GROUP = 8
N_EXPERTS = N_GROUPS * EXPERTS_PER_GROUP
D_EXPERT = 512
LN_EPS = 1e-5

LANES = 128
SUBLANES = 8
VMEM_LIMIT_BYTES = 60000 * 1024

GROUP_LANE0 = 0
EXPERT_LANE0 = N_GROUPS

LOG2E = 1.4426950408889634
NEG_BIG = -1e30

_COL_Q = 0
_COL_K = SB_WIDTH
_COL_V = 2 * SB_WIDTH
_COL_SU = 3 * SB_WIDTH
_COL_SV = 3 * SB_WIDTH + SGU_WIDTH
_COL_GA = 3 * SB_WIDTH + 2 * SGU_WIDTH


def _COL_GB(d):
    return _COL_GA + d


def _cparams(*sem):
    return pltpu.CompilerParams(dimension_semantics=sem, vmem_limit_bytes=VMEM_LIMIT_BYTES)


def _ln_rows(x):
    mu = jnp.mean(x, axis=-1, keepdims=True)
    xc = x - mu
    var = jnp.mean(xc * xc, axis=-1, keepdims=True)
    return xc * lax.rsqrt(var + LN_EPS)


def _sigmoid(x):
    return 1.0 / (1.0 + jnp.exp(-x))


def _ada_kernel(c_ref, w_ref, b_ref, o_ref):
    c = c_ref[...]
    ca = c * _sigmoid(c)
    o_ref[...] = jnp.dot(ca, w_ref[...], preferred_element_type=F32,
                         precision=lax.Precision.HIGHEST) + b_ref[...]


def _ada(c, w_ada, b_ada, tn=2048):
    b, d = c.shape
    n_out = w_ada.shape[1]
    cp = jnp.zeros((SUBLANES, d), F32).at[:b].set(c)
    out = pl.pallas_call(
        _ada_kernel,
        grid=(n_out // tn,),
        in_specs=[pl.BlockSpec((SUBLANES, d), lambda j: (0, 0)),
                  pl.BlockSpec((d, tn), lambda j: (0, j)),
                  pl.BlockSpec((1, tn), lambda j: (0, j))],
        out_specs=pl.BlockSpec((SUBLANES, tn), lambda j: (0, j)),
        out_shape=jax.ShapeDtypeStruct((SUBLANES, n_out), F32),
        compiler_params=_cparams("arbitrary"),
        name="ada",
    )(cp, w_ada, b_ada.reshape(1, n_out))
    return out[:b]


def _inproj_kernel(x_ref, sh_ref, sc_ref, w_ref, cs_ref, o_ref, h_ref):
    @pl.when(pl.program_id(1) == 0)
    def _():
        h = _ln_rows(x_ref[...]) * (1.0 + sc_ref[0]) + sh_ref[0]
        h_ref[...] = h.astype(BF16)

    acc = jnp.dot(h_ref[...], w_ref[...], preferred_element_type=F32)
    o_ref[...] = (acc * cs_ref[...]).astype(BF16)


def _inproj(x2, mod3, w_in_bf, colscale, seq, tm=1024, tn=1024):
    n, d = x2.shape
    n_cols = w_in_bf.shape[1]
    tm = min(tm, seq)
    return pl.pallas_call(
        _inproj_kernel,
        grid=(n // tm, n_cols // tn),
        in_specs=[pl.BlockSpec((tm, d), lambda i, j: (i, 0)),
                  pl.BlockSpec((1, 1, d), lambda i, j: (((i * tm) // seq) * 6 + 0, 0, 0)),
                  pl.BlockSpec((1, 1, d), lambda i, j: (((i * tm) // seq) * 6 + 1, 0, 0)),
                  pl.BlockSpec((d, tn), lambda i, j: (0, j)),
                  pl.BlockSpec((1, tn), lambda i, j: (0, j))],
        out_specs=pl.BlockSpec((tm, tn), lambda i, j: (i, j)),
        out_shape=jax.ShapeDtypeStruct((n, n_cols), BF16),
        scratch_shapes=[pltpu.VMEM((tm, d), BF16)],
        compiler_params=_cparams("arbitrary", "arbitrary"),
        name="inproj",
    )(x2, mod3, mod3, w_in_bf, colscale)


Z2_MAX = 100.0
STICK_EXIT_LOG2 = 150.0


def _attn_kernel(q_ref, k_ref, v_ref, o_ref, *, tq, tk, heads):
    qi = pl.program_id(2)
    nd = tq // tk
    jd = qi * nd
    trow = lax.broadcasted_iota(I32, (tk, tk), 0)
    tcol = lax.broadcasted_iota(I32, (tk, tk), 1)
    tri = (trow > tcol).astype(BF16)
    hd = SB_HEAD_DIM
    qs = [q_ref[0, :, h * hd:(h + 1) * hd] for h in range(heads)]

    def block(j, runs, accs, causal):
        off = pl.multiple_of(j * tk, tk)
        hs = range(heads)
        z2s = [lax.dot_general(qs[h], k_ref[0, pl.ds(off, tk), h * hd:(h + 1) * hd],
                               (((1,), (1,)), ((), ())), preferred_element_type=F32) for h in hs]
        z2s = [lax.clamp(-Z2_MAX, z2, Z2_MAX) for z2 in z2s]
        sps = [jnp.log(1.0 + jnp.exp2(z2)) for z2 in z2s]
        if causal is not None:
            sps = [jnp.where(causal, sp, 0.0) for sp in sps]
        laters = [jnp.dot(sp.astype(BF16), tri, preferred_element_type=F32) for sp in sps]
        ws = [jnp.exp2(z2s[h] - LOG2E * (sps[h] + (laters[h] + runs[h]))) for h in hs]
        if causal is not None:
            ws = [jnp.where(causal, w, 0.0) for w in ws]
        new_accs = [accs[h] + jnp.dot(ws[h].astype(BF16),
                                      v_ref[0, pl.ds(off, tk), h * hd:(h + 1) * hd],
                                      preferred_element_type=F32) for h in hs]
        new_runs = [runs[h] + jnp.sum(sps[h], axis=1, keepdims=True) for h in hs]
        return tuple(new_runs), tuple(new_accs)

    def mass_left(runs):
        low = runs[0]
        for r in runs[1:]:
            low = jnp.minimum(low, r)
        return (jnp.min(low) * LOG2E < STICK_EXIT_LOG2).astype(I32)

    def diag_body(t, carry):
        c = nd - 1 - t
        row = lax.broadcasted_iota(I32, (tq, tk), 0)
        col = lax.broadcasted_iota(I32, (tq, tk), 1)
        causal = col + c * tk < row
        return block(jd + c, carry[0], carry[1], causal)

    def more(carry):
        t, go, _, _ = carry
        return (t < jd) & (go > 0)

    def body(carry):
        t, _, runs, accs = carry
        runs, accs = block(jd - 1 - t, runs, accs, None)
        return t + 1, mass_left(runs), runs, accs

    runs = tuple(jnp.zeros((tq, 1), F32) for _ in range(heads))
    accs = tuple(jnp.zeros((tq, hd), F32) for _ in range(heads))
    runs, accs = lax.fori_loop(0, nd, diag_body, (runs, accs))
    _, _, runs, accs = lax.while_loop(more, body, (jnp.int32(0), mass_left(runs), runs, accs))
    for h in range(heads):
        o_ref[0, :, h * hd:(h + 1) * hd] = accs[h].astype(BF16)


def _attention(proj3, d, tq=256, tk=256, heads=4):
    b, s, _ = proj3.shape
    tk = min(tk, s)
    tq = min(tq, s)
    assert tq % tk == 0 and s % tq == 0
    width = heads * SB_HEAD_DIM
    qc = _COL_Q // width
    kc = _COL_K // width
    vc = _COL_V // width
    return pl.pallas_call(
        functools.partial(_attn_kernel, tq=tq, tk=tk, heads=heads),
        grid=(b, SB_HEADS // heads, s // tq),
        in_specs=[pl.BlockSpec((1, tq, width), lambda bi, g, qi: (bi, qi, qc + g)),
                  pl.BlockSpec((1, s, width), lambda bi, g, qi: (bi, 0, kc + g)),
                  pl.BlockSpec((1, s, width), lambda bi, g, qi: (bi, 0, vc + g))],
        out_specs=pl.BlockSpec((1, tq, width), lambda bi, g, qi: (bi, qi, g)),
        out_shape=jax.ShapeDtypeStruct((b, s, SB_WIDTH), BF16),
        compiler_params=_cparams("arbitrary", "arbitrary", "arbitrary"),
        name="attn",
    )(proj3, proj3, proj3)


def _erf(x):
    return lax.erf(x)


def _gelu(x):
    return 0.5 * x * (1.0 + _erf(x * (1.0 / math.sqrt(2.0))))


def _mix_kernel(oa_ref, su_ref, sv_ref, ga0_ref, ga1_ref, gb0_ref, gb1_ref, x_ref,
                sw_ref, sbias_ref, sg_ref, sb_ref,
                wpa_ref, wpb_ref, wo_ref,
                g1_ref, ln1g_ref, ln1b_ref, sh2_ref, sc2_ref,
                wr_ref, br_ref,
                x1_ref, h2_ref, route_ref, oh1_ref, oh2_ref, *, tm, alpha):
    u = _gelu(su_ref[...].astype(F32))
    v = _gelu(sv_ref[...].astype(F32))
    v = _ln_rows(v) * sg_ref[...] + sb_ref[...]
    vb = v.astype(BF16)
    pos_t = lax.broadcasted_iota(I32, (SGU_BLOCK, SGU_BLOCK), 0)
    pos_s = lax.broadcasted_iota(I32, (SGU_BLOCK, SGU_BLOCK), 1)
    chunk_causal = (pos_s // CHUNK) <= (pos_t // CHUNK)
    n_pos_blocks = tm // SGU_BLOCK
    per_group = []
    for g in range(SGU_GROUPS):
        c0 = g * SGU_GROUP_DIM
        wg = jnp.where(chunk_causal, sw_ref[g], 0.0).astype(BF16)
        rhs = jnp.concatenate(
            [vb[nb * SGU_BLOCK:(nb + 1) * SGU_BLOCK, c0:c0 + SGU_GROUP_DIM]
             for nb in range(n_pos_blocks)], axis=1)
        per_group.append(jnp.dot(wg, rhs, preferred_element_type=F32))
    mixed = jnp.concatenate(
        [jnp.concatenate([per_group[g][:, nb * SGU_GROUP_DIM:(nb + 1) * SGU_GROUP_DIM]
                          for g in range(SGU_GROUPS)], axis=1) + sbias_ref[...]
         for nb in range(n_pos_blocks)], axis=0)
    out_b = (u * mixed).astype(BF16)

    pa = jnp.dot(oa_ref[...], wpa_ref[...], preferred_element_type=F32)
    pb = jnp.dot(out_b, wpb_ref[...], preferred_element_type=F32)
    ga = jnp.concatenate([ga0_ref[...], ga1_ref[...]], axis=1).astype(F32)
    gb = jnp.concatenate([gb0_ref[...], gb1_ref[...]], axis=1).astype(F32)
    merged = _sigmoid(ga) * pa + _sigmoid(gb) * pb
    y = jnp.dot(merged.astype(BF16), wo_ref[...], preferred_element_type=F32)

    r = alpha * x_ref[...] + g1_ref[0] * y
    x1 = _ln_rows(r) * ln1g_ref[...] + ln1b_ref[...]
    x1_ref[...] = x1
    h2 = _ln_rows(x1) * (1.0 + sc2_ref[0]) + sh2_ref[0]
    chunks = h2.shape[1] // LANES
    for c in range(chunks):
        h2_ref[pl.ds(c, tm, stride=chunks), :] = h2[:, c * LANES:(c + 1) * LANES]

    h_hi = h2.astype(BF16)
    h_lo = (h2 - h_hi.astype(F32)).astype(BF16)
    hi_w = jnp.dot(h_hi, wr_ref[...], preferred_element_type=F32)
    lo_w = jnp.dot(h_lo, wr_ref[:, :LANES], preferred_element_type=F32)
    logits = (hi_w[:, :LANES] + hi_w[:, LANES:]) + lo_w + br_ref[...]
    lane = lax.broadcasted_iota(I32, logits.shape, 1)
    gl = jnp.where(lane < N_GROUPS, logits, NEG_BIG)
    gmax = jnp.max(gl, axis=1, keepdims=True)
    g_sel = jnp.min(jnp.where(gl == gmax, lane, LANES), axis=1, keepdims=True)
    p_group = 1.0 / jnp.sum(jnp.exp(gl - gmax), axis=1, keepdims=True)
    lo_lane = EXPERT_LANE0 + g_sel * EXPERTS_PER_GROUP
    in_group = (lane >= lo_lane) & (lane < lo_lane + EXPERTS_PER_GROUP)
    el = jnp.where(in_group, logits, NEG_BIG)
    m1 = jnp.max(el, axis=1, keepdims=True)
    i1 = jnp.min(jnp.where(el == m1, lane, LANES), axis=1, keepdims=True)
    el2 = jnp.where(lane == i1, NEG_BIG, el)
    m2 = jnp.max(el2, axis=1, keepdims=True)
    i2 = jnp.min(jnp.where(el2 == m2, lane, LANES), axis=1, keepdims=True)
    e21 = jnp.exp(m2 - m1)
    w1 = p_group / (1.0 + e21)
    w2 = p_group * e21 / (1.0 + e21)
    route = jnp.where(lane == 0, w1, 0.0)
    route = jnp.where(lane == 1, w2, route)
    route = jnp.where(lane == 2, (i1 - EXPERT_LANE0).astype(F32), route)
    route = jnp.where(lane == 3, (i2 - EXPERT_LANE0).astype(F32), route)
    route_ref[...] = route
    oh1_ref[...] = jnp.where(lane == i1, 1.0, 0.0).astype(BF16)
    oh2_ref[...] = jnp.where(lane == i2, 1.0, 0.0).astype(BF16)


def _mix(out_a, proj, x2, mod3, sgu_w, sgu_bias_full, sgu_g, sgu_b, wpa, wpb, wo,
         ln1_g, ln1_b, wr, br, seq, alpha, tm=256):
    n, d = x2.shape
    tm = min(tm, seq)
    def pcol(start, width):
        assert start % width == 0
        return start // width
    su_c = pcol(_COL_SU, SGU_WIDTH)
    sv_c = pcol(_COL_SV, SGU_WIDTH)
    half = d // 2
    ga_c = pcol(_COL_GA, half)
    gb_c = pcol(_COL_GB(d), half)
    bidx = lambda i: (i * tm) // seq
    row = lambda i: (i, 0)
    const2 = lambda i: (0, 0)
    const3 = lambda i: (0, 0, 0)
    return pl.pallas_call(
        functools.partial(_mix_kernel, tm=tm, alpha=alpha),
        grid=(n // tm,),
        in_specs=[pl.BlockSpec((tm, SB_WIDTH), row),
                  pl.BlockSpec((tm, SGU_WIDTH), lambda i: (i, su_c)),
                  pl.BlockSpec((tm, SGU_WIDTH), lambda i: (i, sv_c)),
                  pl.BlockSpec((tm, half), lambda i: (i, ga_c)),
                  pl.BlockSpec((tm, half), lambda i: (i, ga_c + 1)),
                  pl.BlockSpec((tm, half), lambda i: (i, gb_c)),
                  pl.BlockSpec((tm, half), lambda i: (i, gb_c + 1)),
                  pl.BlockSpec((tm, d), row),
                  pl.BlockSpec((SGU_GROUPS, SGU_BLOCK, SGU_BLOCK), const3),
                  pl.BlockSpec((SGU_BLOCK, SGU_WIDTH), const2),
                  pl.BlockSpec((1, SGU_WIDTH), const2),
                  pl.BlockSpec((1, SGU_WIDTH), const2),
                  pl.BlockSpec((SB_WIDTH, d), const2),
                  pl.BlockSpec((SGU_WIDTH, d), const2),
                  pl.BlockSpec((d, d), const2),
                  pl.BlockSpec((1, 1, d), lambda i: (bidx(i) * 6 + 2, 0, 0)),
                  pl.BlockSpec((1, d), const2),
                  pl.BlockSpec((1, d), const2),
                  pl.BlockSpec((1, 1, d), lambda i: (bidx(i) * 6 + 3, 0, 0)),
                  pl.BlockSpec((1, 1, d), lambda i: (bidx(i) * 6 + 4, 0, 0)),
                  pl.BlockSpec((d, 2 * LANES), const2),
                  pl.BlockSpec((1, LANES), const2)],
        out_specs=[pl.BlockSpec((tm, d), row),
                   pl.BlockSpec((tm * (d // LANES), LANES), row),
                   pl.BlockSpec((tm, LANES), row),
                   pl.BlockSpec((tm, LANES), row),
                   pl.BlockSpec((tm, LANES), row)],
        out_shape=[jax.ShapeDtypeStruct((n, d), F32),
                   jax.ShapeDtypeStruct((n * (d // LANES), LANES), F32),
                   jax.ShapeDtypeStruct((n, LANES), F32),
                   jax.ShapeDtypeStruct((n, LANES), BF16),
                   jax.ShapeDtypeStruct((n, LANES), BF16)],
        compiler_params=_cparams("arbitrary"),
        name="mix",
    )(out_a, proj, proj, proj, proj, proj, proj, x2, sgu_w, sgu_bias_full, sgu_g, sgu_b,
      wpa, wpb, wo, mod3, ln1_g, ln1_b, mod3, mod3, wr, br)


def _plan_kernel(oh1_ref, oh2_ref, dest_ref, blk_ref, cnt_ref, start_ref, *, tm, bm, nblk_pad):
    phase = pl.program_id(0)
    i = pl.program_id(1)
    oh1 = oh1_ref[...]
    oh2 = oh2_ref[...]
    both = oh1 + oh2
    lane = lax.broadcasted_iota(I32, (SUBLANES, LANES), 1)

    @pl.when((phase == 0) & (i == 0))
    def _():
        cnt_ref[...] = jnp.zeros_like(cnt_ref)

    @pl.when(phase == 0)
    def _():
        ones = jnp.ones((SUBLANES, tm), BF16)
        cnt_ref[...] += jnp.dot(ones, both, preferred_element_type=F32)

    @pl.when((phase == 1) & (i == 0))
    def _():
        cnt = cnt_ref[...]
        padded = jnp.floor((cnt + (bm - 1)) * (1.0 / bm)) * bm
        r = lax.broadcasted_iota(I32, (LANES, LANES), 0)
        c = lax.broadcasted_iota(I32, (LANES, LANES), 1)
        upper = (r < c).astype(F32)
        starts = jnp.dot(padded, upper, preferred_element_type=F32,
                         precision=lax.Precision.HIGHEST)
        start_ref[...] = starts
        cnt_ref[...] = jnp.zeros_like(cnt_ref)
        ends = starts + padded
        is_exp = (lane[0:1] >= EXPERT_LANE0) & (lane[0:1] < EXPERT_LANE0 + N_EXPERTS)
        bstart = (lax.broadcasted_iota(I32, (nblk_pad, LANES), 0) * bm).astype(F32)
        done = jnp.where(is_exp & (ends[0:1] <= bstart), 1.0, 0.0)
        bexp = jnp.minimum(jnp.sum(done, axis=1, keepdims=True), N_EXPERTS - 1.0)
        total = jnp.sum(jnp.where(is_exp, padded[0:1], 0.0), axis=1, keepdims=True)
        blane = lax.broadcasted_iota(I32, (nblk_pad, LANES), 1)
        blk = jnp.where(blane == 0, bexp, jnp.where(blane == 1, total * (1.0 / bm), 0.0))
        blk_ref[...] = blk.astype(I32)

    @pl.when(phase == 1)
    def _():
        r = lax.broadcasted_iota(I32, (tm, tm), 0)
        c = lax.broadcasted_iota(I32, (tm, tm), 1)
        lower = (c < r).astype(BF16)
        before = jnp.dot(lower, both, preferred_element_type=F32) + cnt_ref[0:1]
        pos = before + start_ref[0:1]
        d1 = jnp.sum(pos * oh1.astype(F32), axis=1, keepdims=True)
        d2 = jnp.sum(pos * oh2.astype(F32), axis=1, keepdims=True)
        dlane = lax.broadcasted_iota(I32, (tm, LANES), 1)
        dest = jnp.where(dlane == 0, d1, jnp.where(dlane == 1, d2, 0.0))
        dest_ref[...] = dest.astype(I32)
        ones = jnp.ones((SUBLANES, tm), BF16)
        cnt_ref[...] += jnp.dot(ones, both, preferred_element_type=F32)


def _plan(oh1, oh2, bm, nblk, tm=512):
    n = oh1.shape[0]
    tm = min(tm, n)
    nblk_pad = -(-nblk // SUBLANES) * SUBLANES
    dest, blk = pl.pallas_call(
        functools.partial(_plan_kernel, tm=tm, bm=bm, nblk_pad=nblk_pad),
        grid=(2, n // tm),
        in_specs=[pl.BlockSpec((tm, LANES), lambda p, i: (i, 0)),
                  pl.BlockSpec((tm, LANES), lambda p, i: (i, 0))],
        out_specs=[pl.BlockSpec((tm, LANES), lambda p, i: (i * p, 0)),
                   pl.BlockSpec((nblk_pad, LANES), lambda p, i: (0, 0))],
        out_shape=[jax.ShapeDtypeStruct((n, LANES), I32),
                   jax.ShapeDtypeStruct((nblk_pad, LANES), I32)],
        scratch_shapes=[pltpu.VMEM((SUBLANES, LANES), F32),
                        pltpu.VMEM((SUBLANES, LANES), F32)],
        compiler_params=_cparams("arbitrary", "arbitrary"),
        name="plan",
    )(oh1, oh2)
    return dest, blk


ROW_UNROLL = 8


def _invert_kernel(d1_ref, d2_ref, rt_ref, *, n, n_rows):
    def clear(i, carry):
        for u in range(ROW_UNROLL):
            rt_ref[i * ROW_UNROLL + u] = 0
        return carry

    def put(i, carry):
        for u in range(ROW_UNROLL):
            t = i * ROW_UNROLL + u
            rt_ref[d1_ref[t]] = t
            rt_ref[d2_ref[t]] = t
        return carry

    lax.fori_loop(0, n_rows // ROW_UNROLL, clear, 0)
    lax.fori_loop(0, n // ROW_UNROLL, put, 0)


def _invert(dest1, dest2, n_rows):
    n = dest1.shape[0]
    assert n % ROW_UNROLL == 0 and n_rows % ROW_UNROLL == 0
    return pl.pallas_call(
        functools.partial(_invert_kernel, n=n, n_rows=n_rows),
        grid_spec=pltpu.PrefetchScalarGridSpec(
            num_scalar_prefetch=2,
            grid=(1,),
            in_specs=[],
            out_specs=pl.BlockSpec(memory_space=pltpu.SMEM),
        ),
        out_shape=jax.ShapeDtypeStruct((n_rows,), I32),
        compiler_params=_cparams("arbitrary"),
        name="invert",
    )(dest1, dest2)


def _slab(idx, chunks):
    return pl.ds(pl.multiple_of(idx * chunks, chunks), chunks)


def _rows_from_slabs(buf_ref, slot, rows, chunks):
    return jnp.concatenate(
        [buf_ref[slot, pl.ds(c, rows, stride=chunks), :] for c in range(chunks)], axis=1)


def _experts_kernel(be_ref, na_ref, rt_ref, h_ref, wg_ref, wu_ref, wd_ref, o_ref,
                    xbuf, sem, wgb, wub, wdb, *, bm, chunks):
    i = pl.program_id(0)
    n_active = na_ref[0]

    def row_copy(blk, r, slot):
        return pltpu.make_async_copy(h_ref.at[_slab(rt_ref[blk * bm + r], chunks), :],
                                     xbuf.at[slot, _slab(r, chunks), :], sem.at[slot])

    def gather(blk, slot, start):
        def trip(r8, carry):
            for u in range(ROW_UNROLL):
                cp = row_copy(blk, r8 * ROW_UNROLL + u, slot)
                cp.start(priority=u % 2) if start else cp.wait()
            return carry
        lax.fori_loop(0, bm // ROW_UNROLL, trip, 0)

    @pl.when(i == 0)
    def _():
        gather(0, 0, True)

    @pl.when(i < n_active)
    def _():
        slot = lax.rem(i, 2)
        gather(i, slot, False)

        @pl.when(i + 1 < n_active)
        def _():
            gather(i + 1, 1 - slot, True)

        @pl.when((i == 0) | (be_ref[i] != be_ref[jnp.maximum(i - 1, 0)]))
        def _():
            wgb[...] = wg_ref[...].astype(BF16)
            wub[...] = wu_ref[...].astype(BF16)
            wdb[...] = wd_ref[...].astype(BF16)

        x = _rows_from_slabs(xbuf, slot, bm, chunks).astype(BF16)
        g = jnp.dot(x, wgb[...], preferred_element_type=F32)
        u = jnp.dot(x, wub[...], preferred_element_type=F32)
        hb = (g * _sigmoid(g) * u).astype(BF16)
        y = jnp.dot(hb, wdb[...], preferred_element_type=F32)
        for c in range(chunks):
            o_ref[pl.ds(c, bm, stride=chunks), :] = y[:, c * LANES:(c + 1) * LANES]

    @pl.when(i >= n_active)
    def _():
        o_ref[...] = jnp.zeros_like(o_ref)


def _experts(block_expert, n_active, row_tok, h2s, w_gate, w_up, w_down, bm):
    n_rows = row_tok.shape[0]
    nblk = n_rows // bm
    d, de = w_gate.shape[1], w_gate.shape[2]
    chunks = d // LANES
    assert bm % ROW_UNROLL == 0

    def blk(i, be, na, rt):
        return jnp.maximum(jnp.minimum(i, na[0] - 1), 0)

    return pl.pallas_call(
        functools.partial(_experts_kernel, bm=bm, chunks=chunks),
        grid_spec=pltpu.PrefetchScalarGridSpec(
            num_scalar_prefetch=3,
            grid=(nblk,),
            in_specs=[pl.BlockSpec(memory_space=pl.ANY),
                      pl.BlockSpec((None, d, de), lambda i, be, na, rt: (be[blk(i, be, na, rt)], 0, 0)),
                      pl.BlockSpec((None, d, de), lambda i, be, na, rt: (be[blk(i, be, na, rt)], 0, 0)),
                      pl.BlockSpec((None, de, d), lambda i, be, na, rt: (be[blk(i, be, na, rt)], 0, 0))],
            out_specs=pl.BlockSpec((bm * chunks, LANES), lambda i, be, na, rt: (i, 0)),
            scratch_shapes=[pltpu.VMEM((2, bm * chunks, LANES), F32),
                            pltpu.SemaphoreType.DMA((2,)),
                            pltpu.VMEM((d, de), BF16),
                            pltpu.VMEM((d, de), BF16),
                            pltpu.VMEM((de, d), BF16)],
        ),
        out_shape=jax.ShapeDtypeStruct((n_rows * chunks, LANES), F32),
        compiler_params=_cparams("arbitrary"),
        name="experts",
    )(block_expert, n_active, row_tok, h2s, w_gate, w_up, w_down)


def _combine_kernel(d1_ref, d2_ref, ys_ref, x1_ref, route_ref, g2_ref, lg_ref, lb_ref,
                    o_ref, y1buf, y2buf, sem, *, tm, chunks, alpha):
    i = pl.program_id(0)
    n_tiles = pl.num_programs(0)

    def row_copies(tile, t, slot):
        tok = tile * tm + t
        return (pltpu.make_async_copy(ys_ref.at[_slab(d1_ref[tok], chunks), :],
                                      y1buf.at[slot, _slab(t, chunks), :], sem.at[slot]),
                pltpu.make_async_copy(ys_ref.at[_slab(d2_ref[tok], chunks), :],
                                      y2buf.at[slot, _slab(t, chunks), :], sem.at[slot]))

    def gather(tile, slot, start):
        def trip(t8, carry):
            for u in range(ROW_UNROLL):
                for k, cp in enumerate(row_copies(tile, t8 * ROW_UNROLL + u, slot)):
                    cp.start(priority=k) if start else cp.wait()
            return carry
        lax.fori_loop(0, tm // ROW_UNROLL, trip, 0)

    @pl.when(i == 0)
    def _():
        gather(0, 0, True)

    slot = lax.rem(i, 2)
    gather(i, slot, False)

    @pl.when(i + 1 < n_tiles)
    def _():
        gather(i + 1, 1 - slot, True)

    route = route_ref[...]
    y = (route[:, 0:1] * _rows_from_slabs(y1buf, slot, tm, chunks)
         + route[:, 1:2] * _rows_from_slabs(y2buf, slot, tm, chunks))
    r = alpha * x1_ref[...] + g2_ref[0] * y
    o_ref[...] = _ln_rows(r) * lg_ref[...] + lb_ref[...]


def _combine(dest1, dest2, ys, x1, route, mod3, ln2_g, ln2_b, seq, alpha, tm=256):
    n, d = x1.shape
    tm = min(tm, seq)
    chunks = d // LANES
    assert tm % ROW_UNROLL == 0
    tok = lambda i, d1, d2: (i, 0)
    const = lambda i, d1, d2: (0, 0)
    return pl.pallas_call(
        functools.partial(_combine_kernel, tm=tm, chunks=chunks, alpha=alpha),
        grid_spec=pltpu.PrefetchScalarGridSpec(
            num_scalar_prefetch=2,
            grid=(n // tm,),
            in_specs=[pl.BlockSpec(memory_space=pl.ANY),
                      pl.BlockSpec((tm, d), tok),
                      pl.BlockSpec((tm, LANES), tok),
                      pl.BlockSpec((1, 1, d), lambda i, d1, d2: (((i * tm) // seq) * 6 + 5, 0, 0)),
                      pl.BlockSpec((1, d), const),
                      pl.BlockSpec((1, d), const)],
            out_specs=pl.BlockSpec((tm, d), tok),
            scratch_shapes=[pltpu.VMEM((2, tm * chunks, LANES), F32),
                            pltpu.VMEM((2, tm * chunks, LANES), F32),
                            pltpu.SemaphoreType.DMA((2,))],
        ),
        out_shape=jax.ShapeDtypeStruct((n, d), F32),
        compiler_params=_cparams("arbitrary"),
        name="combine",
    )(dest1, dest2, ys, x1, route, mod3, ln2_g, ln2_b)


EXPERT_BLOCK_ROWS = 256


def kernel(x, c, w_ada, b_ada, w_in, sgu_w, sgu_b, sgu_ln_g, sgu_ln_b, w_proj_a, w_proj_b,
           w_out, ln1_g, ln1_b, w_group, b_group, w_router, b_router, w_gate, w_up, w_down,
           ln2_g, ln2_b):
    b, s, d = x.shape
    n = b * s
    depth = w_ada.shape[0]
    alpha = (2.0 * depth) ** 0.25
    sub = d // LANES
    bm = EXPERT_BLOCK_ROWS
    n_assign = 2 * n
    n_rows = n_assign + N_EXPERTS * bm
    nblk = n_rows // bm

    in_cols = w_in.shape[2]
    colscale = jnp.ones((1, in_cols), F32).at[:, _COL_Q:_COL_K].set(
        SB_HEAD_DIM ** -0.5 * LOG2E)

    for l in range(depth):
        mod = _ada(c, w_ada[l], b_ada[l])
        mod3 = mod.reshape(b * 6, 1, d)

        x2 = x.reshape(n, d)
        proj = _inproj(x2, mod3, w_in[l].astype(BF16), colscale, s)
        out_a = _attention(proj.reshape(b, s, in_cols), d).reshape(n, SB_WIDTH)

        wr = jnp.zeros((d, LANES), F32)
        wr = wr.at[:, GROUP_LANE0:GROUP_LANE0 + N_GROUPS].set(w_group[l])
        wr = wr.at[:, EXPERT_LANE0:EXPERT_LANE0 + N_EXPERTS].set(w_router[l])
        wr_hi = wr.astype(BF16)
        wr_lo = (wr - wr_hi.astype(F32)).astype(BF16)
        wr = jnp.concatenate([wr_hi, wr_lo], axis=1)
        br = jnp.zeros((1, LANES), F32)
        br = br.at[0, GROUP_LANE0:GROUP_LANE0 + N_GROUPS].set(b_group[l])
        br = br.at[0, EXPERT_LANE0:EXPERT_LANE0 + N_EXPERTS].set(b_router[l])
        sgu_bias_full = jnp.repeat(sgu_b[l].T, SGU_GROUP_DIM, axis=1)

        x1, h2s, route, oh1, oh2 = _mix(
            out_a, proj, x2, mod3, sgu_w[l], sgu_bias_full,
            sgu_ln_g[l].reshape(1, -1), sgu_ln_b[l].reshape(1, -1),
            w_proj_a[l].astype(BF16), w_proj_b[l].astype(BF16), w_out[l].astype(BF16),
            ln1_g[l].reshape(1, d), ln1_b[l].reshape(1, d), wr, br, s, alpha)

        dest, blk = _plan(oh1, oh2, bm, nblk)
        dest1 = dest[:, 0]
        dest2 = dest[:, 1]
        block_expert = blk[:nblk, 0]
        n_active = blk[0:1, 1]

        row_tok = _invert(dest1, dest2, n_rows)
        ys = _experts(block_expert, n_active, row_tok, h2s, w_gate[l], w_up[l], w_down[l], bm)
        out = _combine(dest1, dest2, ys, x1, route, mod3, ln2_g[l].reshape(1, d),
                       ln2_b[l].reshape(1, d), s, alpha)
        x = out.reshape(b, s, d)
    return x
```

```python
import functools
import math

import jax
import jax.numpy as jnp
from jax import lax
from jax.experimental import pallas as pl
from jax.experimental.pallas import tpu as pltpu

F32 = jnp.float32
BF16 = jnp.bfloat16
I32 = jnp.int32

CHUNK = 64
SB_HEADS = 8
SB_HEAD_DIM = 128
SB_WIDTH = SB_HEADS * SB_HEAD_DIM
SGU_GROUPS = 8
SGU_GROUP_DIM = 128
SGU_WIDTH = SGU_GROUPS * SGU_GROUP_DIM
SGU_BLOCK = 128
N_GROUPS = 4
EXPERTS_PER_GROUP = 8
N_EXPERTS = N_GROUPS * EXPERTS_PER_GROUP
D_EXPERT = 512
LN_EPS = 1e-5

LANES = 128
SUBLANES = 8
VMEM_LIMIT_BYTES = 60000 * 1024

GROUP_LANE0 = 0
EXPERT_LANE0 = N_GROUPS

LOG2E = 1.4426950408889634
NEG_BIG = -1e30

_COL_Q = 0
_COL_K = SB_WIDTH
_COL_V = 2 * SB_WIDTH
_COL_SU = 3 * SB_WIDTH
_COL_SV = 3 * SB_WIDTH + SGU_WIDTH
_COL_GA = 3 * SB_WIDTH + 2 * SGU_WIDTH


def _COL_GB(d):
    return _COL_GA + d


def _cparams(*sem):
    return pltpu.CompilerParams(dimension_semantics=sem, vmem_limit_bytes=VMEM_LIMIT_BYTES)


def _ln_rows(x):
    mu = jnp.mean(x, axis=-1, keepdims=True)
    xc = x - mu
    var = jnp.mean(xc * xc, axis=-1, keepdims=True)
    return xc * lax.rsqrt(var + LN_EPS)


def _sigmoid(x):
    return 1.0 / (1.0 + jnp.exp(-x))


def _ada_kernel(c_ref, w_ref, b_ref, o_ref):
    c = c_ref[...]
    ca = c * _sigmoid(c)
    o_ref[...] = jnp.dot(ca, w_ref[...], preferred_element_type=F32,
                         precision=lax.Precision.HIGHEST) + b_ref[...]


def _ada(c, w_ada, b_ada, tn=1024):
    b, d = c.shape
    n_out = w_ada.shape[1]
    cp = jnp.zeros((SUBLANES, d), F32).at[:b].set(c)
    out = pl.pallas_call(
        _ada_kernel,
        grid=(n_out // tn,),
        in_specs=[pl.BlockSpec((SUBLANES, d), lambda j: (0, 0)),
                  pl.BlockSpec((d, tn), lambda j: (0, j)),
                  pl.BlockSpec((1, tn), lambda j: (0, j))],
        out_specs=pl.BlockSpec((SUBLANES, tn), lambda j: (0, j)),
        out_shape=jax.ShapeDtypeStruct((SUBLANES, n_out), F32),
        compiler_params=_cparams("arbitrary"),
        name="ada",
    )(cp, w_ada, b_ada.reshape(1, n_out))
    return out[:b]


def _inproj_kernel(x_ref, sh_ref, sc_ref, w_ref, cs_ref, o_ref, h_ref):
    @pl.when(pl.program_id(1) == 0)
    def _():
        h = _ln_rows(x_ref[...]) * (1.0 + sc_ref[0]) + sh_ref[0]
        h_ref[...] = h.astype(BF16)

    acc = jnp.dot(h_ref[...], w_ref[...], preferred_element_type=F32)
    o_ref[...] = (acc * cs_ref[...]).astype(BF16)


def _inproj(x2, mod3, w_in_bf, colscale, seq, tm=1024, tn=1024):
    n, d = x2.shape
    n_cols = w_in_bf.shape[1]
    tm = min(tm, seq)
    return pl.pallas_call(
        _inproj_kernel,
        grid=(n // tm, n_cols // tn),
        in_specs=[pl.BlockSpec((tm, d), lambda i, j: (i, 0)),
                  pl.BlockSpec((1, 1, d), lambda i, j: (((i * tm) // seq) * 6 + 0, 0, 0)),
                  pl.BlockSpec((1, 1, d), lambda i, j: (((i * tm) // seq) * 6 + 1, 0, 0)),
                  pl.BlockSpec((d, tn), lambda i, j: (0, j)),
                  pl.BlockSpec((1, tn), lambda i, j: (0, j))],
        out_specs=pl.BlockSpec((tm, tn), lambda i, j: (i, j)),
        out_shape=jax.ShapeDtypeStruct((n, n_cols), BF16),
        scratch_shapes=[pltpu.VMEM((tm, d), BF16)],
        compiler_params=_cparams("arbitrary", "arbitrary"),
        name="inproj",
    )(x2, mod3, mod3, w_in_bf, colscale)


Z2_MAX = 100.0
STICK_EXIT_LOG2 = 150.0


def _attn_kernel(q_ref, k_ref, v_ref, o_ref, *, tq, tk, heads):
    qi = pl.program_id(2)
    nd = tq // tk
    jd = qi * nd
    trow = lax.broadcasted_iota(I32, (tk, tk), 0)
    tcol = lax.broadcasted_iota(I32, (tk, tk), 1)
    tri = (trow > tcol).astype(BF16)
    hd = SB_HEAD_DIM
    qs = [q_ref[0, :, h * hd:(h + 1) * hd] for h in range(heads)]

    def block(j, runs, accs, causal):
        off = pl.multiple_of(j * tk, tk)
        hs = range(heads)
        z2s = [lax.dot_general(qs[h], k_ref[0, pl.ds(off, tk), h * hd:(h + 1) * hd],
                               (((1,), (1,)), ((), ())), preferred_element_type=F32) for h in hs]
        z2s = [lax.clamp(-Z2_MAX, z2, Z2_MAX) for z2 in z2s]
        sps = [jnp.log(1.0 + jnp.exp2(z2)) for z2 in z2s]
        if causal is not None:
            sps = [jnp.where(causal, sp, 0.0) for sp in sps]
        laters = [jnp.dot(sp.astype(BF16), tri, preferred_element_type=F32) for sp in sps]
        ws = [jnp.exp2(z2s[h] - LOG2E * (sps[h] + (laters[h] + runs[h]))) for h in hs]
        if causal is not None:
            ws = [jnp.where(causal, w, 0.0) for w in ws]
        new_accs = [accs[h] + jnp.dot(ws[h].astype(BF16),
                                      v_ref[0, pl.ds(off, tk), h * hd:(h + 1) * hd],
                                      preferred_element_type=F32) for h in hs]
        new_runs = [runs[h] + jnp.sum(sps[h], axis=1, keepdims=True) for h in hs]
        return tuple(new_runs), tuple(new_accs)

    def mass_left(runs):
        low = runs[0]
        for r in runs[1:]:
            low = jnp.minimum(low, r)
        return (jnp.min(low) * LOG2E < STICK_EXIT_LOG2).astype(I32)

    def diag_body(t, carry):
        c = nd - 1 - t
        row = lax.broadcasted_iota(I32, (tq, tk), 0)
        col = lax.broadcasted_iota(I32, (tq, tk), 1)
        causal = col + c * tk < row
        return block(jd + c, carry[0], carry[1], causal)

    def more(carry):
        t, go, _, _ = carry
        return (t < jd) & (go > 0)

    def body(carry):
        t, _, runs, accs = carry
        runs, accs = block(jd - 1 - t, runs, accs, None)
        return t + 1, mass_left(runs), runs, accs

    runs = tuple(jnp.zeros((tq, 1), F32) for _ in range(heads))
    accs = tuple(jnp.zeros((tq, hd), F32) for _ in range(heads))
    runs, accs = lax.fori_loop(0, nd, diag_body, (runs, accs))
    _, _, runs, accs = lax.while_loop(more, body, (jnp.int32(0), mass_left(runs), runs, accs))
    for h in range(heads):
        o_ref[0, :, h * hd:(h + 1) * hd] = accs[h].astype(BF16)


def _attention(proj3, d, tq=256, tk=256, heads=4):
    b, s, _ = proj3.shape
    tk = min(tk, s)
    tq = min(tq, s)
    assert tq % tk == 0 and s % tq == 0
    width = heads * SB_HEAD_DIM
    qc = _COL_Q // width
    kc = _COL_K // width
    vc = _COL_V // width
    return pl.pallas_call(
        functools.partial(_attn_kernel, tq=tq, tk=tk, heads=heads),
        grid=(b, SB_HEADS // heads, s // tq),
        in_specs=[pl.BlockSpec((1, tq, width), lambda bi, g, qi: (bi, qi, qc + g)),
                  pl.BlockSpec((1, s, width), lambda bi, g, qi: (bi, 0, kc + g)),
                  pl.BlockSpec((1, s, width), lambda bi, g, qi: (bi, 0, vc + g))],
        out_specs=pl.BlockSpec((1, tq, width), lambda bi, g, qi: (bi, qi, g)),
        out_shape=jax.ShapeDtypeStruct((b, s, SB_WIDTH), BF16),
        compiler_params=_cparams("arbitrary", "arbitrary", "arbitrary"),
        name="attn",
    )(proj3, proj3, proj3)


def _erf(x):
    return lax.erf(x)


def _gelu(x):
    return 0.5 * x * (1.0 + _erf(x * (1.0 / math.sqrt(2.0))))


def _mix_kernel(oa_ref, su_ref, sv_ref, ga0_ref, ga1_ref, gb0_ref, gb1_ref, x_ref,
                sw_ref, sbias_ref, sg_ref, sb_ref,
                wpa_ref, wpb_ref, wo_ref,
                g1_ref, ln1g_ref, ln1b_ref, sh2_ref, sc2_ref,
                wr_ref, br_ref,
                x1_ref, h2_ref, route_ref, oh1_ref, oh2_ref, *, tm, alpha):
    u = _gelu(su_ref[...].astype(F32))
    v = _gelu(sv_ref[...].astype(F32))
    v = _ln_rows(v) * sg_ref[...] + sb_ref[...]
    vb = v.astype(BF16)
    pos_t = lax.broadcasted_iota(I32, (SGU_BLOCK, SGU_BLOCK), 0)
    pos_s = lax.broadcasted_iota(I32, (SGU_BLOCK, SGU_BLOCK), 1)
    chunk_causal = (pos_s // CHUNK) <= (pos_t // CHUNK)
    n_pos_blocks = tm // SGU_BLOCK
    per_group = []
    for g in range(SGU_GROUPS):
        c0 = g * SGU_GROUP_DIM
        wg = jnp.where(chunk_causal, sw_ref[g], 0.0).astype(BF16)
        rhs = jnp.concatenate(
            [vb[nb * SGU_BLOCK:(nb + 1) * SGU_BLOCK, c0:c0 + SGU_GROUP_DIM]
             for nb in range(n_pos_blocks)], axis=1)
        per_group.append(jnp.dot(wg, rhs, preferred_element_type=F32))
    mixed = jnp.concatenate(
        [jnp.concatenate([per_group[g][:, nb * SGU_GROUP_DIM:(nb + 1) * SGU_GROUP_DIM]
                          for g in range(SGU_GROUPS)], axis=1) + sbias_ref[...]
         for nb in range(n_pos_blocks)], axis=0)
    out_b = (u * mixed).astype(BF16)

    pa = jnp.dot(oa_ref[...], wpa_ref[...], preferred_element_type=F32)
    pb = jnp.dot(out_b, wpb_ref[...], preferred_element_type=F32)
    ga = jnp.concatenate([ga0_ref[...], ga1_ref[...]], axis=1).astype(F32)
    gb = jnp.concatenate([gb0_ref[...], gb1_ref[...]], axis=1).astype(F32)
    merged = _sigmoid(ga) * pa + _sigmoid(gb) * pb
    y = jnp.dot(merged.astype(BF16), wo_ref[...], preferred_element_type=F32)

    r = alpha * x_ref[...] + g1_ref[0] * y
    x1 = _ln_rows(r) * ln1g_ref[...] + ln1b_ref[...]
    x1_ref[...] = x1
    h2 = _ln_rows(x1) * (1.0 + sc2_ref[0]) + sh2_ref[0]
    chunks = h2.shape[1] // LANES
    for c in range(chunks):
        h2_ref[pl.ds(c, tm, stride=chunks), :] = h2[:, c * LANES:(c + 1) * LANES]

    h_hi = h2.astype(BF16)
    h_lo = (h2 - h_hi.astype(F32)).astype(BF16)
    hi_w = jnp.dot(h_hi, wr_ref[...], preferred_element_type=F32)
    lo_w = jnp.dot(h_lo, wr_ref[:, :LANES], preferred_element_type=F32)
    logits = (hi_w[:, :LANES] + hi_w[:, LANES:]) + lo_w + br_ref[...]
    lane = lax.broadcasted_iota(I32, logits.shape, 1)
    gl = jnp.where(lane < N_GROUPS, logits, NEG_BIG)
    gmax = jnp.max(gl, axis=1, keepdims=True)
    g_sel = jnp.min(jnp.where(gl == gmax, lane, LANES), axis=1, keepdims=True)
    p_group = 1.0 / jnp.sum(jnp.exp(gl - gmax), axis=1, keepdims=True)
    lo_lane = EXPERT_LANE0 + g_sel * EXPERTS_PER_GROUP
    in_group = (lane >= lo_lane) & (lane < lo_lane + EXPERTS_PER_GROUP)
    el = jnp.where(in_group, logits, NEG_BIG)
    m1 = jnp.max(el, axis=1, keepdims=True)
    i1 = jnp.min(jnp.where(el == m1, lane, LANES), axis=1, keepdims=True)
    el2 = jnp.where(lane == i1, NEG_BIG, el)
    m2 = jnp.max(el2, axis=1, keepdims=True)
    i2 = jnp.min(jnp.where(el2 == m2, lane, LANES), axis=1, keepdims=True)
    e21 = jnp.exp(m2 - m1)
    w1 = p_group / (1.0 + e21)
    w2 = p_group * e21 / (1.0 + e21)
    route = jnp.where(lane == 0, w1, 0.0)
    route = jnp.where(lane == 1, w2, route)
    route = jnp.where(lane == 2, (i1 - EXPERT_LANE0).astype(F32), route)
    route = jnp.where(lane == 3, (i2 - EXPERT_LANE0).astype(F32), route)
    route_ref[...] = route
    oh1_ref[...] = jnp.where(lane == i1, 1.0, 0.0).astype(BF16)
    oh2_ref[...] = jnp.where(lane == i2, 1.0, 0.0).astype(BF16)


def _mix(out_a, proj, x2, mod3, sgu_w, sgu_bias_full, sgu_g, sgu_b, wpa, wpb, wo,
         ln1_g, ln1_b, wr, br, seq, alpha, tm=256):
    n, d = x2.shape
    tm = min(tm, seq)
    def pcol(start, width):
        assert start % width == 0
        return start // width
    su_c = pcol(_COL_SU, SGU_WIDTH)
    sv_c = pcol(_COL_SV, SGU_WIDTH)
    half = d // 2
    ga_c = pcol(_COL_GA, half)
    gb_c = pcol(_COL_GB(d), half)
    bidx = lambda i: (i * tm) // seq
    row = lambda i: (i, 0)
    const2 = lambda i: (0, 0)
    const3 = lambda i: (0, 0, 0)
    return pl.pallas_call(
        functools.partial(_mix_kernel, tm=tm, alpha=alpha),
        grid=(n // tm,),
        in_specs=[pl.BlockSpec((tm, SB_WIDTH), row),
                  pl.BlockSpec((tm, SGU_WIDTH), lambda i: (i, su_c)),
                  pl.BlockSpec((tm, SGU_WIDTH), lambda i: (i, sv_c)),
                  pl.BlockSpec((tm, half), lambda i: (i, ga_c)),
                  pl.BlockSpec((tm, half), lambda i: (i, ga_c + 1)),
                  pl.BlockSpec((tm, half), lambda i: (i, gb_c)),
                  pl.BlockSpec((tm, half), lambda i: (i, gb_c + 1)),
                  pl.BlockSpec((tm, d), row),
                  pl.BlockSpec((SGU_GROUPS, SGU_BLOCK, SGU_BLOCK), const3),
                  pl.BlockSpec((SGU_BLOCK, SGU_WIDTH), const2),
                  pl.BlockSpec((1, SGU_WIDTH), const2),
                  pl.BlockSpec((1, SGU_WIDTH), const2),
                  pl.BlockSpec((SB_WIDTH, d), const2),
                  pl.BlockSpec((SGU_WIDTH, d), const2),
                  pl.BlockSpec((d, d), const2),
                  pl.BlockSpec((1, 1, d), lambda i: (bidx(i) * 6 + 2, 0, 0)),
                  pl.BlockSpec((1, d), const2),
                  pl.BlockSpec((1, d), const2),
                  pl.BlockSpec((1, 1, d), lambda i: (bidx(i) * 6 + 3, 0, 0)),
                  pl.BlockSpec((1, 1, d), lambda i: (bidx(i) * 6 + 4, 0, 0)),
                  pl.BlockSpec((d, 2 * LANES), const2),
                  pl.BlockSpec((1, LANES), const2)],
        out_specs=[pl.BlockSpec((tm, d), row),
                   pl.BlockSpec((tm * (d // LANES), LANES), row),
                   pl.BlockSpec((tm, LANES), row),
                   pl.BlockSpec((tm, LANES), row),
                   pl.BlockSpec((tm, LANES), row)],
        out_shape=[jax.ShapeDtypeStruct((n, d), F32),
                   jax.ShapeDtypeStruct((n * (d // LANES), LANES), F32),
                   jax.ShapeDtypeStruct((n, LANES), F32),
                   jax.ShapeDtypeStruct((n, LANES), BF16),
                   jax.ShapeDtypeStruct((n, LANES), BF16)],
        compiler_params=_cparams("arbitrary"),
        name="mix",
    )(out_a, proj, proj, proj, proj, proj, proj, x2, sgu_w, sgu_bias_full, sgu_g, sgu_b,
      wpa, wpb, wo, mod3, ln1_g, ln1_b, mod3, mod3, wr, br)


def _plan_kernel(oh1_ref, oh2_ref, dest_ref, blk_ref, cnt_ref, start_ref, *, tm, bm, nblk_pad):
    phase = pl.program_id(0)
    i = pl.program_id(1)
    oh1 = oh1_ref[...]
    oh2 = oh2_ref[...]
    both = oh1 + oh2
    lane = lax.broadcasted_iota(I32, (SUBLANES, LANES), 1)

    @pl.when((phase == 0) & (i == 0))
    def _():
        cnt_ref[...] = jnp.zeros_like(cnt_ref)

    @pl.when(phase == 0)
    def _():
        ones = jnp.ones((SUBLANES, tm), BF16)
        cnt_ref[...] += jnp.dot(ones, both, preferred_element_type=F32)

    @pl.when((phase == 1) & (i == 0))
    def _():
        cnt = cnt_ref[...]
        padded = jnp.floor((cnt + (bm - 1)) * (1.0 / bm)) * bm
        r = lax.broadcasted_iota(I32, (LANES, LANES), 0)
        c = lax.broadcasted_iota(I32, (LANES, LANES), 1)
        upper = (r < c).astype(F32)
        starts = jnp.dot(padded, upper, preferred_element_type=F32,
                         precision=lax.Precision.HIGHEST)
        start_ref[...] = starts
        cnt_ref[...] = jnp.zeros_like(cnt_ref)
        ends = starts + padded
        is_exp = (lane[0:1] >= EXPERT_LANE0) & (lane[0:1] < EXPERT_LANE0 + N_EXPERTS)
        bstart = (lax.broadcasted_iota(I32, (nblk_pad, LANES), 0) * bm).astype(F32)
        done = jnp.where(is_exp & (ends[0:1] <= bstart), 1.0, 0.0)
        bexp = jnp.minimum(jnp.sum(done, axis=1, keepdims=True), N_EXPERTS - 1.0)
        total = jnp.sum(jnp.where(is_exp, padded[0:1], 0.0), axis=1, keepdims=True)
        blane = lax.broadcasted_iota(I32, (nblk_pad, LANES), 1)
        blk = jnp.where(blane == 0, bexp, jnp.where(blane == 1, total * (1.0 / bm), 0.0))
        blk_ref[...] = blk.astype(I32)

    @pl.when(phase == 1)
    def _():
        r = lax.broadcasted_iota(I32, (tm, tm), 0)
        c = lax.broadcasted_iota(I32, (tm, tm), 1)
        lower = (c < r).astype(BF16)
        before = jnp.dot(lower, both, preferred_element_type=F32) + cnt_ref[0:1]
        pos = before + start_ref[0:1]
        d1 = jnp.sum(pos * oh1.astype(F32), axis=1, keepdims=True)
        d2 = jnp.sum(pos * oh2.astype(F32), axis=1, keepdims=True)
        dlane = lax.broadcasted_iota(I32, (tm, LANES), 1)
        dest = jnp.where(dlane == 0, d1, jnp.where(dlane == 1, d2, 0.0))
        dest_ref[...] = dest.astype(I32)
        ones = jnp.ones((SUBLANES, tm), BF16)
        cnt_ref[...] += jnp.dot(ones, both, preferred_element_type=F32)


def _plan(oh1, oh2, bm, nblk, tm=512):
    n = oh1.shape[0]
    tm = min(tm, n)
    nblk_pad = -(-nblk // SUBLANES) * SUBLANES
    dest, blk = pl.pallas_call(
        functools.partial(_plan_kernel, tm=tm, bm=bm, nblk_pad=nblk_pad),
        grid=(2, n // tm),
        in_specs=[pl.BlockSpec((tm, LANES), lambda p, i: (i, 0)),
                  pl.BlockSpec((tm, LANES), lambda p, i: (i, 0))],
        out_specs=[pl.BlockSpec((tm, LANES), lambda p, i: (i * p, 0)),
                   pl.BlockSpec((nblk_pad, LANES), lambda p, i: (0, 0))],
        out_shape=[jax.ShapeDtypeStruct((n, LANES), I32),
                   jax.ShapeDtypeStruct((nblk_pad, LANES), I32)],
        scratch_shapes=[pltpu.VMEM((SUBLANES, LANES), F32),
                        pltpu.VMEM((SUBLANES, LANES), F32)],
        compiler_params=_cparams("arbitrary", "arbitrary"),
        name="plan",
    )(oh1, oh2)
    return dest, blk


ROW_UNROLL = 8


def _invert_kernel(d1_ref, d2_ref, zero_ref, rt_ref, sem, *, n):
    clear = pltpu.make_async_copy(zero_ref, rt_ref, sem)
    clear.start()
    clear.wait()

    def put(i, carry):
        for u in range(ROW_UNROLL):
            t = i * ROW_UNROLL + u
            rt_ref[d1_ref[t]] = t
            rt_ref[d2_ref[t]] = t
        return carry

    lax.fori_loop(0, n // ROW_UNROLL, put, 0)


def _invert(dest1, dest2, n_rows):
    n = dest1.shape[0]
    assert n % ROW_UNROLL == 0
    return pl.pallas_call(
        functools.partial(_invert_kernel, n=n),
        grid_spec=pltpu.PrefetchScalarGridSpec(
            num_scalar_prefetch=2,
            grid=(1,),
            in_specs=[pl.BlockSpec(memory_space=pl.ANY)],
            out_specs=pl.BlockSpec(memory_space=pltpu.SMEM),
            scratch_shapes=[pltpu.SemaphoreType.DMA(())],
        ),
        out_shape=jax.ShapeDtypeStruct((n_rows,), I32),
        compiler_params=_cparams("arbitrary"),
        name="invert",
    )(dest1, dest2, jnp.zeros((n_rows,), I32))


def _slab(idx, chunks):
    return pl.ds(pl.multiple_of(idx * chunks, chunks), chunks)


def _rows_from_slabs(buf_ref, slot, rows, chunks):
    return jnp.concatenate(
        [buf_ref[slot, pl.ds(c, rows, stride=chunks), :] for c in range(chunks)], axis=1)


def _experts_kernel(be_ref, na_ref, rt_ref, h_ref, wg_ref, wu_ref, wd_ref, o_ref,
                    xbuf, sem, wgb, wub, wdb, *, bm, chunks):
    i = pl.program_id(0)
    n_active = na_ref[0]

    def row_copy(blk, r, slot):
        return pltpu.make_async_copy(h_ref.at[_slab(rt_ref[blk * bm + r], chunks), :],
                                     xbuf.at[slot, _slab(r, chunks), :], sem.at[slot])

    def gather(blk, slot):
        def trip(r8, carry):
            for u in range(ROW_UNROLL):
                row_copy(blk, r8 * ROW_UNROLL + u, slot).start(priority=u % 2)
            return carry
        lax.fori_loop(0, bm // ROW_UNROLL, trip, 0)

    def gather_wait(slot):
        pltpu.make_async_copy(h_ref.at[pl.ds(0, bm * chunks), :], xbuf.at[slot],
                              sem.at[slot]).wait()

    @pl.when(i == 0)
    def _():
        gather(0, 0)

    @pl.when(i < n_active)
    def _():
        slot = lax.rem(i, 2)
        gather_wait(slot)

        @pl.when(i + 1 < n_active)
        def _():
            gather(i + 1, 1 - slot)

        @pl.when((i == 0) | (be_ref[i] != be_ref[jnp.maximum(i - 1, 0)]))
        def _():
            wgb[...] = wg_ref[...].astype(BF16)
            wub[...] = wu_ref[...].astype(BF16)
            wdb[...] = wd_ref[...].astype(BF16)

        x = _rows_from_slabs(xbuf, slot, bm, chunks).astype(BF16)
        g = jnp.dot(x, wgb[...], preferred_element_type=F32)
        u = jnp.dot(x, wub[...], preferred_element_type=F32)
        hb = (g * _sigmoid(g) * u).astype(BF16)
        y = jnp.dot(hb, wdb[...], preferred_element_type=F32)
        for c in range(chunks):
            o_ref[pl.ds(c, bm, stride=chunks), :] = y[:, c * LANES:(c + 1) * LANES]

    @pl.when(i >= n_active)
    def _():
        o_ref[...] = jnp.zeros_like(o_ref)


def _experts(block_expert, n_active, row_tok, h2s, w_gate, w_up, w_down, bm):
    n_rows = row_tok.shape[0]
    nblk = n_rows // bm
    d, de = w_gate.shape[1], w_gate.shape[2]
    chunks = d // LANES
    assert bm % ROW_UNROLL == 0

    def blk(i, be, na, rt):
        return jnp.maximum(jnp.minimum(i, na[0] - 1), 0)

    return pl.pallas_call(
        functools.partial(_experts_kernel, bm=bm, chunks=chunks),
        grid_spec=pltpu.PrefetchScalarGridSpec(
            num_scalar_prefetch=3,
            grid=(nblk,),
            in_specs=[pl.BlockSpec(memory_space=pl.ANY),
                      pl.BlockSpec((None, d, de), lambda i, be, na, rt: (be[blk(i, be, na, rt)], 0, 0)),
                      pl.BlockSpec((None, d, de), lambda i, be, na, rt: (be[blk(i, be, na, rt)], 0, 0)),
                      pl.BlockSpec((None, de, d), lambda i, be, na, rt: (be[blk(i, be, na, rt)], 0, 0))],
            out_specs=pl.BlockSpec((bm * chunks, LANES), lambda i, be, na, rt: (i, 0)),
            scratch_shapes=[pltpu.VMEM((2, bm * chunks, LANES), F32),
                            pltpu.SemaphoreType.DMA((2,)),
                            pltpu.VMEM((d, de), BF16),
                            pltpu.VMEM((d, de), BF16),
                            pltpu.VMEM((de, d), BF16)],
        ),
        out_shape=jax.ShapeDtypeStruct((n_rows * chunks, LANES), F32),
        compiler_params=_cparams("arbitrary"),
        name="experts",
    )(block_expert, n_active, row_tok, h2s, w_gate, w_up, w_down)


def _combine_kernel(d1_ref, d2_ref, ys_ref, x1_ref, route_ref, g2_ref, lg_ref, lb_ref,
                    o_ref, y1buf, y2buf, sem, *, tm, chunks, alpha):
    i = pl.program_id(0)
    n_tiles = pl.num_programs(0)

    def row_copies(tile, t, slot):
        tok = tile * tm + t
        return (pltpu.make_async_copy(ys_ref.at[_slab(d1_ref[tok], chunks), :],
                                      y1buf.at[slot, _slab(t, chunks), :], sem.at[slot]),
                pltpu.make_async_copy(ys_ref.at[_slab(d2_ref[tok], chunks), :],
                                      y2buf.at[slot, _slab(t, chunks), :], sem.at[slot]))

    def gather(tile, slot):
        def trip(t8, carry):
            for u in range(ROW_UNROLL):
                for k, cp in enumerate(row_copies(tile, t8 * ROW_UNROLL + u, slot)):
                    cp.start(priority=k)
            return carry
        lax.fori_loop(0, tm // ROW_UNROLL, trip, 0)

    def gather_wait(slot):
        for buf in (y1buf, y2buf):
            pltpu.make_async_copy(ys_ref.at[pl.ds(0, tm * chunks), :], buf.at[slot],
                                  sem.at[slot]).wait()

    @pl.when(i == 0)
    def _():
        gather(0, 0)

    slot = lax.rem(i, 2)
    gather_wait(slot)

    @pl.when(i + 1 < n_tiles)
    def _():
        gather(i + 1, 1 - slot)

    route = route_ref[...]
    y = (route[:, 0:1] * _rows_from_slabs(y1buf, slot, tm, chunks)
         + route[:, 1:2] * _rows_from_slabs(y2buf, slot, tm, chunks))
    r = alpha * x1_ref[...] + g2_ref[0] * y
    o_ref[...] = _ln_rows(r) * lg_ref[...] + lb_ref[...]


def _combine(dest1, dest2, ys, x1, route, mod3, ln2_g, ln2_b, seq, alpha, tm=256):
    n, d = x1.shape
    tm = min(tm, seq)
    chunks = d // LANES
    assert tm % ROW_UNROLL == 0
    tok = lambda i, d1, d2: (i, 0)
    const = lambda i, d1, d2: (0, 0)
    return pl.pallas_call(
        functools.partial(_combine_kernel, tm=tm, chunks=chunks, alpha=alpha),
        grid_spec=pltpu.PrefetchScalarGridSpec(
            num_scalar_prefetch=2,
            grid=(n // tm,),
            in_specs=[pl.BlockSpec(memory_space=pl.ANY),
                      pl.BlockSpec((tm, d), tok),
                      pl.BlockSpec((tm, LANES), tok),
                      pl.BlockSpec((1, 1, d), lambda i, d1, d2: (((i * tm) // seq) * 6 + 5, 0, 0)),
                      pl.BlockSpec((1, d), const),
                      pl.BlockSpec((1, d), const)],
            out_specs=pl.BlockSpec((tm, d), tok),
            scratch_shapes=[pltpu.VMEM((2, tm * chunks, LANES), F32),
                            pltpu.VMEM((2, tm * chunks, LANES), F32),
                            pltpu.SemaphoreType.DMA((2,))],
        ),
        out_shape=jax.ShapeDtypeStruct((n, d), F32),
        compiler_params=_cparams("arbitrary"),
        name="combine",
    )(dest1, dest2, ys, x1, route, mod3, ln2_g, ln2_b)


EXPERT_BLOCK_ROWS = 256


def kernel(x, c, w_ada, b_ada, w_in, sgu_w, sgu_b, sgu_ln_g, sgu_ln_b, w_proj_a, w_proj_b,
           w_out, ln1_g, ln1_b, w_group, b_group, w_router, b_router, w_gate, w_up, w_down,
           ln2_g, ln2_b):
    b, s, d = x.shape
    n = b * s
    depth = w_ada.shape[0]
    alpha = (2.0 * depth) ** 0.25
    sub = d // LANES
    bm = EXPERT_BLOCK_ROWS
    n_assign = 2 * n
    n_rows = n_assign + N_EXPERTS * bm
    nblk = n_rows // bm

    in_cols = w_in.shape[2]
    colscale = jnp.ones((1, in_cols), F32).at[:, _COL_Q:_COL_K].set(
        SB_HEAD_DIM ** -0.5 * LOG2E)

    for l in range(depth):
        mod = _ada(c, w_ada[l], b_ada[l])
        mod3 = mod.reshape(b * 6, 1, d)

        x2 = x.reshape(n, d)
        proj = _inproj(x2, mod3, w_in[l].astype(BF16), colscale, s)
        out_a = _attention(proj.reshape(b, s, in_cols), d).reshape(n, SB_WIDTH)

        wr = jnp.zeros((d, LANES), F32)
        wr = wr.at[:, GROUP_LANE0:GROUP_LANE0 + N_GROUPS].set(w_group[l])
        wr = wr.at[:, EXPERT_LANE0:EXPERT_LANE0 + N_EXPERTS].set(w_router[l])
        wr_hi = wr.astype(BF16)
        wr_lo = (wr - wr_hi.astype(F32)).astype(BF16)
        wr = jnp.concatenate([wr_hi, wr_lo], axis=1)
        br = jnp.zeros((1, LANES), F32)
        br = br.at[0, GROUP_LANE0:GROUP_LANE0 + N_GROUPS].set(b_group[l])
        br = br.at[0, EXPERT_LANE0:EXPERT_LANE0 + N_EXPERTS].set(b_router[l])
        sgu_bias_full = jnp.repeat(sgu_b[l].T, SGU_GROUP_DIM, axis=1)

        x1, h2s, route, oh1, oh2 = _mix(
            out_a, proj, x2, mod3, sgu_w[l], sgu_bias_full,
            sgu_ln_g[l].reshape(1, -1), sgu_ln_b[l].reshape(1, -1),
            w_proj_a[l].astype(BF16), w_proj_b[l].astype(BF16), w_out[l].astype(BF16),
            ln1_g[l].reshape(1, d), ln1_b[l].reshape(1, d), wr, br, s, alpha)

        dest, blk = _plan(oh1, oh2, bm, nblk)
        dest1 = dest[:, 0]
        dest2 = dest[:, 1]
        block_expert = blk[:nblk, 0]
        n_active = blk[0:1, 1]

        row_tok = _invert(dest1, dest2, n_rows)
        ys = _experts(block_expert, n_active, row_tok, h2s, w_gate[l], w_up[l], w_down[l], bm)
        out = _combine(dest1, dest2, ys, x1, route, mod3, ln2_g[l].reshape(1, d),
                       ln2_b[l].reshape(1, d), s, alpha)
        x = out.reshape(b, s, d)
    return x
```

```python
import functools
import math

import jax
import jax.numpy as jnp
from jax import lax
from jax.experimental import pallas as pl
from jax.experimental.pallas import tpu as pltpu

F32 = jnp.float32
BF16 = jnp.bfloat16
I32 = jnp.int32

CHUNK = 64
SB_HEADS = 8
SB_HEAD_DIM = 128
SB_WIDTH = SB_HEADS * SB_HEAD_DIM
SGU_GROUPS = 8
SGU_GROUP_DIM = 128
SGU_WIDTH = SGU_GROUPS * SGU_GROUP_DIM
SGU_BLOCK = 128
N_GROUPS = 4
EXPERTS_PER_GROUP = 8
N_EXPERTS = N_GROUPS * EXPERTS_PER_GROUP
D_EXPERT = 512
LN_EPS = 1e-5

LANES = 128
SUBLANES = 8
VMEM_LIMIT_BYTES = 60000 * 1024

GROUP_LANE0 = 0
EXPERT_LANE0 = N_GROUPS

LOG2E = 1.4426950408889634
NEG_BIG = -1e30

_COL_Q = 0
_COL_K = SB_WIDTH
_COL_V = 2 * SB_WIDTH
_COL_SU = 3 * SB_WIDTH
_COL_SV = 3 * SB_WIDTH + SGU_WIDTH
_COL_GA = 3 * SB_WIDTH + 2 * SGU_WIDTH


def _COL_GB(d):
    return _COL_GA + d


def _cparams(*sem):
    return pltpu.CompilerParams(dimension_semantics=sem, vmem_limit_bytes=VMEM_LIMIT_BYTES)


def _ln_rows(x):
    mu = jnp.mean(x, axis=-1, keepdims=True)
    xc = x - mu
    var = jnp.mean(xc * xc, axis=-1, keepdims=True)
    return xc * lax.rsqrt(var + LN_EPS)


def _sigmoid(x):
    return 1.0 / (1.0 + jnp.exp(-x))


def _ada_kernel(c_ref, w_ref, b_ref, o_ref):
    c = c_ref[...]
    ca = c * _sigmoid(c)
    o_ref[...] = jnp.dot(ca, w_ref[...], preferred_element_type=F32,
                         precision=lax.Precision.HIGHEST) + b_ref[...]


def _ada(c, w_ada, b_ada, tn=1024):
    b, d = c.shape
    n_out = w_ada.shape[1]
    cp = jnp.zeros((SUBLANES, d), F32).at[:b].set(c)
    out = pl.pallas_call(
        _ada_kernel,
        grid=(n_out // tn,),
        in_specs=[pl.BlockSpec((SUBLANES, d), lambda j: (0, 0)),
                  pl.BlockSpec((d, tn), lambda j: (0, j)),
                  pl.BlockSpec((1, tn), lambda j: (0, j))],
        out_specs=pl.BlockSpec((SUBLANES, tn), lambda j: (0, j)),
        out_shape=jax.ShapeDtypeStruct((SUBLANES, n_out), F32),
        compiler_params=_cparams("arbitrary"),
        name="ada",
    )(cp, w_ada, b_ada.reshape(1, n_out))
    return out[:b]


def _inproj_kernel(x_ref, sh_ref, sc_ref, w_ref, cs_ref, o_ref, h_ref):
    @pl.when(pl.program_id(1) == 0)
    def _():
        h = _ln_rows(x_ref[...]) * (1.0 + sc_ref[0]) + sh_ref[0]
        h_ref[...] = h.astype(BF16)

    acc = jnp.dot(h_ref[...], w_ref[...], preferred_element_type=F32)
    o_ref[...] = (acc * cs_ref[...]).astype(BF16)


def _inproj(x2, mod3, w_in_bf, colscale, seq, tm=1024, tn=1024):
    n, d = x2.shape
    n_cols = w_in_bf.shape[1]
    tm = min(tm, seq)
    return pl.pallas_call(
        _inproj_kernel,
        grid=(n // tm, n_cols // tn),
        in_specs=[pl.BlockSpec((tm, d), lambda i, j: (i, 0)),
                  pl.BlockSpec((1, 1, d), lambda i, j: (((i * tm) // seq) * 6 + 0, 0, 0)),
                  pl.BlockSpec((1, 1, d), lambda i, j: (((i * tm) // seq) * 6 + 1, 0, 0)),
                  pl.BlockSpec((d, tn), lambda i, j: (0, j)),
                  pl.BlockSpec((1, tn), lambda i, j: (0, j))],
        out_specs=pl.BlockSpec((tm, tn), lambda i, j: (i, j)),
        out_shape=jax.ShapeDtypeStruct((n, n_cols), BF16),
        scratch_shapes=[pltpu.VMEM((tm, d), BF16)],
        compiler_params=_cparams("arbitrary", "arbitrary"),
        name="inproj",
    )(x2, mod3, mod3, w_in_bf, colscale)


Z2_MAX = 100.0
STICK_EXIT_LOG2 = 150.0


def _attn_kernel(q_ref, k_ref, v_ref, o_ref, *, tq, tk, heads):
    qi = pl.program_id(2)
    nd = tq // tk
    jd = qi * nd
    trow = lax.broadcasted_iota(I32, (tk, tk), 0)
    tcol = lax.broadcasted_iota(I32, (tk, tk), 1)
    tri = (trow > tcol).astype(BF16)
    hd = SB_HEAD_DIM
    qs = [q_ref[0, :, h * hd:(h + 1) * hd] for h in range(heads)]

    def block(j, runs, accs, causal):
        off = pl.multiple_of(j * tk, tk)
        hs = range(heads)
        z2s = [lax.dot_general(qs[h], k_ref[0, pl.ds(off, tk), h * hd:(h + 1) * hd],
                               (((1,), (1,)), ((), ())), preferred_element_type=F32) for h in hs]
        z2s = [lax.clamp(-Z2_MAX, z2, Z2_MAX) for z2 in z2s]
        sps = [jnp.log(1.0 + jnp.exp2(z2)) for z2 in z2s]
        if causal is not None:
            sps = [jnp.where(causal, sp, 0.0) for sp in sps]
        laters = [jnp.dot(sp.astype(BF16), tri, preferred_element_type=F32) for sp in sps]
        ws = [jnp.exp2(z2s[h] - LOG2E * (sps[h] + (laters[h] + runs[h]))) for h in hs]
        if causal is not None:
            ws = [jnp.where(causal, w, 0.0) for w in ws]
        new_accs = [accs[h] + jnp.dot(ws[h].astype(BF16),
                                      v_ref[0, pl.ds(off, tk), h * hd:(h + 1) * hd],
                                      preferred_element_type=F32) for h in hs]
        new_runs = [runs[h] + jnp.sum(sps[h], axis=1, keepdims=True) for h in hs]
        return tuple(new_runs), tuple(new_accs)

    def mass_left(runs):
        low = runs[0]
        for r in runs[1:]:
            low = jnp.minimum(low, r)
        return (jnp.min(low) * LOG2E < STICK_EXIT_LOG2).astype(I32)

    def diag_body(t, carry):
        c = nd - 1 - t
        row = lax.broadcasted_iota(I32, (tq, tk), 0)
        col = lax.broadcasted_iota(I32, (tq, tk), 1)
        causal = col + c * tk < row
        return block(jd + c, carry[0], carry[1], causal)

    def more(carry):
        t, go, _, _ = carry
        return (t < jd) & (go > 0)

    def body(carry):
        t, _, runs, accs = carry
        runs, accs = block(jd - 1 - t, runs, accs, None)
        return t + 1, mass_left(runs), runs, accs

    runs = tuple(jnp.zeros((tq, 1), F32) for _ in range(heads))
    accs = tuple(jnp.zeros((tq, hd), F32) for _ in range(heads))
    runs, accs = lax.fori_loop(0, nd, diag_body, (runs, accs))
    _, _, runs, accs = lax.while_loop(more, body, (jnp.int32(0), mass_left(runs), runs, accs))
    for h in range(heads):
        o_ref[0, :, h * hd:(h + 1) * hd] = accs[h].astype(BF16)


def _attention(proj3, d, tq=256, tk=256, heads=4):
    b, s, _ = proj3.shape
    tk = min(tk, s)
    tq = min(tq, s)
    assert tq % tk == 0 and s % tq == 0
    width = heads * SB_HEAD_DIM
    qc = _COL_Q // width
    kc = _COL_K // width
    vc = _COL_V // width
    return pl.pallas_call(
        functools.partial(_attn_kernel, tq=tq, tk=tk, heads=heads),
        grid=(b, SB_HEADS // heads, s // tq),
        in_specs=[pl.BlockSpec((1, tq, width), lambda bi, g, qi: (bi, qi, qc + g)),
                  pl.BlockSpec((1, s, width), lambda bi, g, qi: (bi, 0, kc + g)),
                  pl.BlockSpec((1, s, width), lambda bi, g, qi: (bi, 0, vc + g))],
        out_specs=pl.BlockSpec((1, tq, width), lambda bi, g, qi: (bi, qi, g)),
        out_shape=jax.ShapeDtypeStruct((b, s, SB_WIDTH), BF16),
        compiler_params=_cparams("arbitrary", "arbitrary", "arbitrary"),
        name="attn",
    )(proj3, proj3, proj3)


def _erf(x):
    return lax.erf(x)


def _gelu(x):
    return 0.5 * x * (1.0 + _erf(x * (1.0 / math.sqrt(2.0))))


def _mix_kernel(oa_ref, su_ref, sv_ref, ga0_ref, ga1_ref, gb0_ref, gb1_ref, x_ref,
                sw_ref, sbias_ref, sg_ref, sb_ref,
                wpa_ref, wpb_ref, wo_ref,
                g1_ref, ln1g_ref, ln1b_ref, sh2_ref, sc2_ref,
                wr_ref, br_ref,
                x1_ref, h2_ref, route_ref, oh1_ref, oh2_ref, *, tm, alpha):
    u = _gelu(su_ref[...].astype(F32))
    v = _gelu(sv_ref[...].astype(F32))
    v = _ln_rows(v) * sg_ref[...] + sb_ref[...]
    vb = v.astype(BF16)
    pos_t = lax.broadcasted_iota(I32, (SGU_BLOCK, SGU_BLOCK), 0)
    pos_s = lax.broadcasted_iota(I32, (SGU_BLOCK, SGU_BLOCK), 1)
    chunk_causal = (pos_s // CHUNK) <= (pos_t // CHUNK)
    n_pos_blocks = tm // SGU_BLOCK
    per_group = []
    for g in range(SGU_GROUPS):
        c0 = g * SGU_GROUP_DIM
        wg = jnp.where(chunk_causal, sw_ref[g], 0.0).astype(BF16)
        rhs = jnp.concatenate(
            [vb[nb * SGU_BLOCK:(nb + 1) * SGU_BLOCK, c0:c0 + SGU_GROUP_DIM]
             for nb in range(n_pos_blocks)], axis=1)
        per_group.append(jnp.dot(wg, rhs, preferred_element_type=F32))
    mixed = jnp.concatenate(
        [jnp.concatenate([per_group[g][:, nb * SGU_GROUP_DIM:(nb + 1) * SGU_GROUP_DIM]
                          for g in range(SGU_GROUPS)], axis=1) + sbias_ref[...]
         for nb in range(n_pos_blocks)], axis=0)
    out_b = (u * mixed).astype(BF16)

    pa = jnp.dot(oa_ref[...], wpa_ref[...], preferred_element_type=F32)
    pb = jnp.dot(out_b, wpb_ref[...], preferred_element_type=F32)
    ga = jnp.concatenate([ga0_ref[...], ga1_ref[...]], axis=1).astype(F32)
    gb = jnp.concatenate([gb0_ref[...], gb1_ref[...]], axis=1).astype(F32)
    merged = _sigmoid(ga) * pa + _sigmoid(gb) * pb
    y = jnp.dot(merged.astype(BF16), wo_ref[...], preferred_element_type=F32)

    r = alpha * x_ref[...] + g1_ref[0] * y
    x1 = _ln_rows(r) * ln1g_ref[...] + ln1b_ref[...]
    x1_ref[...] = x1
    h2 = _ln_rows(x1) * (1.0 + sc2_ref[0]) + sh2_ref[0]
    chunks = h2.shape[1] // LANES
    for c in range(chunks):
        h2_ref[pl.ds(c, tm, stride=chunks), :] = h2[:, c * LANES:(c + 1) * LANES]

    h_hi = h2.astype(BF16)
    h_lo = (h2 - h_hi.astype(F32)).astype(BF16)
    hi_w = jnp.dot(h_hi, wr_ref[...], preferred_element_type=F32)
    lo_w = jnp.dot(h_lo, wr_ref[:, :LANES], preferred_element_type=F32)
    logits = (hi_w[:, :LANES] + hi_w[:, LANES:]) + lo_w + br_ref[...]
    lane = lax.broadcasted_iota(I32, logits.shape, 1)
    gl = jnp.where(lane < N_GROUPS, logits, NEG_BIG)
    gmax = jnp.max(gl, axis=1, keepdims=True)
    g_sel = jnp.min(jnp.where(gl == gmax, lane, LANES), axis=1, keepdims=True)
    p_group = 1.0 / jnp.sum(jnp.exp(gl - gmax), axis=1, keepdims=True)
    lo_lane = EXPERT_LANE0 + g_sel * EXPERTS_PER_GROUP
    in_group = (lane >= lo_lane) & (lane < lo_lane + EXPERTS_PER_GROUP)
    el = jnp.where(in_group, logits, NEG_BIG)
    m1 = jnp.max(el, axis=1, keepdims=True)
    i1 = jnp.min(jnp.where(el == m1, lane, LANES), axis=1, keepdims=True)
    el2 = jnp.where(lane == i1, NEG_BIG, el)
    m2 = jnp.max(el2, axis=1, keepdims=True)
    i2 = jnp.min(jnp.where(el2 == m2, lane, LANES), axis=1, keepdims=True)
    e21 = jnp.exp(m2 - m1)
    w1 = p_group / (1.0 + e21)
    w2 = p_group * e21 / (1.0 + e21)
    route = jnp.where(lane == 0, w1, 0.0)
    route = jnp.where(lane == 1, w2, route)
    route = jnp.where(lane == 2, (i1 - EXPERT_LANE0).astype(F32), route)
    route = jnp.where(lane == 3, (i2 - EXPERT_LANE0).astype(F32), route)
    route_ref[...] = route
    oh1_ref[...] = jnp.where(lane == i1, 1.0, 0.0).astype(BF16)
    oh2_ref[...] = jnp.where(lane == i2, 1.0, 0.0).astype(BF16)


def _mix(out_a, proj, x2, mod3, sgu_w, sgu_bias_full, sgu_g, sgu_b, wpa, wpb, wo,
         ln1_g, ln1_b, wr, br, seq, alpha, tm=256):
    n, d = x2.shape
    tm = min(tm, seq)
    def pcol(start, width):
        assert start % width == 0
        return start // width
    su_c = pcol(_COL_SU, SGU_WIDTH)
    sv_c = pcol(_COL_SV, SGU_WIDTH)
    half = d // 2
    ga_c = pcol(_COL_GA, half)
    gb_c = pcol(_COL_GB(d), half)
    bidx = lambda i: (i * tm) // seq
    row = lambda i: (i, 0)
    const2 = lambda i: (0, 0)
    const3 = lambda i: (0, 0, 0)
    return pl.pallas_call(
        functools.partial(_mix_kernel, tm=tm, alpha=alpha),
        grid=(n // tm,),
        in_specs=[pl.BlockSpec((tm, SB_WIDTH), row),
                  pl.BlockSpec((tm, SGU_WIDTH), lambda i: (i, su_c)),
                  pl.BlockSpec((tm, SGU_WIDTH), lambda i: (i, sv_c)),
                  pl.BlockSpec((tm, half), lambda i: (i, ga_c)),
                  pl.BlockSpec((tm, half), lambda i: (i, ga_c + 1)),
                  pl.BlockSpec((tm, half), lambda i: (i, gb_c)),
                  pl.BlockSpec((tm, half), lambda i: (i, gb_c + 1)),
                  pl.BlockSpec((tm, d), row),
                  pl.BlockSpec((SGU_GROUPS, SGU_BLOCK, SGU_BLOCK), const3),
                  pl.BlockSpec((SGU_BLOCK, SGU_WIDTH), const2),
                  pl.BlockSpec((1, SGU_WIDTH), const2),
                  pl.BlockSpec((1, SGU_WIDTH), const2),
                  pl.BlockSpec((SB_WIDTH, d), const2),
                  pl.BlockSpec((SGU_WIDTH, d), const2),
                  pl.BlockSpec((d, d), const2),
                  pl.BlockSpec((1, 1, d), lambda i: (bidx(i) * 6 + 2, 0, 0)),
                  pl.BlockSpec((1, d), const2),
                  pl.BlockSpec((1, d), const2),
                  pl.BlockSpec((1, 1, d), lambda i: (bidx(i) * 6 + 3, 0, 0)),
                  pl.BlockSpec((1, 1, d), lambda i: (bidx(i) * 6 + 4, 0, 0)),
                  pl.BlockSpec((d, 2 * LANES), const2),
                  pl.BlockSpec((1, LANES), const2)],
        out_specs=[pl.BlockSpec((tm, d), row),
                   pl.BlockSpec((tm * (d // LANES), LANES), row),
                   pl.BlockSpec((tm, LANES), row),
                   pl.BlockSpec((tm, LANES), row),
                   pl.BlockSpec((tm, LANES), row)],
        out_shape=[jax.ShapeDtypeStruct((n, d), F32),
                   jax.ShapeDtypeStruct((n * (d // LANES), LANES), F32),
                   jax.ShapeDtypeStruct((n, LANES), F32),
                   jax.ShapeDtypeStruct((n, LANES), BF16),
                   jax.ShapeDtypeStruct((n, LANES), BF16)],
        compiler_params=_cparams("arbitrary"),
        name="mix",
    )(out_a, proj, proj, proj, proj, proj, proj, x2, sgu_w, sgu_bias_full, sgu_g, sgu_b,
      wpa, wpb, wo, mod3, ln1_g, ln1_b, mod3, mod3, wr, br)


def _plan_kernel(oh1_ref, oh2_ref, dest_ref, blk_ref, cnt_ref, start_ref, *, tm, bm, nblk_pad):
    phase = pl.program_id(0)
    i = pl.program_id(1)
    oh1 = oh1_ref[...]
    oh2 = oh2_ref[...]
    both = oh1 + oh2
    lane = lax.broadcasted_iota(I32, (SUBLANES, LANES), 1)

    @pl.when((phase == 0) & (i == 0))
    def _():
        cnt_ref[...] = jnp.zeros_like(cnt_ref)

    @pl.when(phase == 0)
    def _():
        ones = jnp.ones((SUBLANES, tm), BF16)
        cnt_ref[...] += jnp.dot(ones, both, preferred_element_type=F32)

    @pl.when((phase == 1) & (i == 0))
    def _():
        cnt = cnt_ref[...]
        padded = jnp.floor((cnt + (bm - 1)) * (1.0 / bm)) * bm
        r = lax.broadcasted_iota(I32, (LANES, LANES), 0)
        c = lax.broadcasted_iota(I32, (LANES, LANES), 1)
        upper = (r < c).astype(F32)
        starts = jnp.dot(padded, upper, preferred_element_type=F32,
                         precision=lax.Precision.HIGHEST)
        start_ref[...] = starts
        cnt_ref[...] = jnp.zeros_like(cnt_ref)
        ends = starts + padded
        is_exp = (lane[0:1] >= EXPERT_LANE0) & (lane[0:1] < EXPERT_LANE0 + N_EXPERTS)
        bstart = (lax.broadcasted_iota(I32, (nblk_pad, LANES), 0) * bm).astype(F32)
        done = jnp.where(is_exp & (ends[0:1] <= bstart), 1.0, 0.0)
        bexp = jnp.minimum(jnp.sum(done, axis=1, keepdims=True), N_EXPERTS - 1.0)
        total = jnp.sum(jnp.where(is_exp, padded[0:1], 0.0), axis=1, keepdims=True)
        blane = lax.broadcasted_iota(I32, (nblk_pad, LANES), 1)
        blk = jnp.where(blane == 0, bexp, jnp.where(blane == 1, total * (1.0 / bm), 0.0))
        blk_ref[...] = blk.astype(I32)

    @pl.when(phase == 1)
    def _():
        r = lax.broadcasted_iota(I32, (tm, tm), 0)
        c = lax.broadcasted_iota(I32, (tm, tm), 1)
        lower = (c < r).astype(BF16)
        before = jnp.dot(lower, both, preferred_element_type=F32) + cnt_ref[0:1]
        pos = before + start_ref[0:1]
        d1 = jnp.sum(pos * oh1.astype(F32), axis=1, keepdims=True)
        d2 = jnp.sum(pos * oh2.astype(F32), axis=1, keepdims=True)
        dlane = lax.broadcasted_iota(I32, (tm, LANES), 1)
        dest = jnp.where(dlane == 0, d1, jnp.where(dlane == 1, d2, 0.0))
        dest_ref[...] = dest.astype(I32)
        ones = jnp.ones((SUBLANES, tm), BF16)
        cnt_ref[...] += jnp.dot(ones, both, preferred_element_type=F32)


def _plan(oh1, oh2, bm, nblk, tm=512):
    n = oh1.shape[0]
    tm = min(tm, n)
    nblk_pad = -(-nblk // SUBLANES) * SUBLANES
    dest, blk = pl.pallas_call(
        functools.partial(_plan_kernel, tm=tm, bm=bm, nblk_pad=nblk_pad),
        grid=(2, n // tm),
        in_specs=[pl.BlockSpec((tm, LANES), lambda p, i: (i, 0)),
                  pl.BlockSpec((tm, LANES), lambda p, i: (i, 0))],
        out_specs=[pl.BlockSpec((tm, LANES), lambda p, i: (i * p, 0)),
                   pl.BlockSpec((nblk_pad, LANES), lambda p, i: (0, 0))],
        out_shape=[jax.ShapeDtypeStruct((n, LANES), I32),
                   jax.ShapeDtypeStruct((nblk_pad, LANES), I32)],
        scratch_shapes=[pltpu.VMEM((SUBLANES, LANES), F32),
                        pltpu.VMEM((SUBLANES, LANES), F32)],
        compiler_params=_cparams("arbitrary", "arbitrary"),
        name="plan",
    )(oh1, oh2)
    return dest, blk


ROW_UNROLL = 8


def _invert_kernel(d1_ref, d2_ref, zero_ref, rt_ref, sem, *, n):
    clear = pltpu.make_async_copy(zero_ref, rt_ref, sem)
    clear.start()
    clear.wait()

    def put(i, carry):
        for u in range(ROW_UNROLL):
            t = i * ROW_UNROLL + u
            rt_ref[d1_ref[t]] = t
            rt_ref[d2_ref[t]] = t
        return carry

    lax.fori_loop(0, n // ROW_UNROLL, put, 0)


def _invert(dest1, dest2, n_rows):
    n = dest1.shape[0]
    assert n % ROW_UNROLL == 0
    return pl.pallas_call(
        functools.partial(_invert_kernel, n=n),
        grid_spec=pltpu.PrefetchScalarGridSpec(
            num_scalar_prefetch=2,
            grid=(1,),
            in_specs=[pl.BlockSpec(memory_space=pl.ANY)],
            out_specs=pl.BlockSpec(memory_space=pltpu.SMEM),
            scratch_shapes=[pltpu.SemaphoreType.DMA(())],
        ),
        out_shape=jax.ShapeDtypeStruct((n_rows,), I32),
        compiler_params=_cparams("arbitrary"),
        name="invert",
    )(dest1, dest2, jnp.zeros((n_rows,), I32))


def _slab(idx, chunks):
    return pl.ds(pl.multiple_of(idx * chunks, chunks), chunks)


def _rows_from_slabs(buf_ref, slot, rows, chunks):
    return jnp.concatenate(
        [buf_ref[slot, pl.ds(c, rows, stride=chunks), :] for c in range(chunks)], axis=1)


def _experts_kernel(be_ref, na_ref, rt_ref, first_ref, wslot_ref, nxt_ref,
                    h_ref, wg_hbm, wu_hbm, wd_hbm, o_ref,
                    xbuf, sem, wgf, wuf, wdf, wsem, wgb, wub, wdb, *, bm, chunks):
    i = pl.program_id(0)
    n_active = na_ref[0]

    def row_copy(blk, r, slot):
        return pltpu.make_async_copy(h_ref.at[_slab(rt_ref[blk * bm + r], chunks), :],
                                     xbuf.at[slot, _slab(r, chunks), :], sem.at[slot])

    def gather(blk, slot):
        def trip(r8, carry):
            for u in range(ROW_UNROLL):
                row_copy(blk, r8 * ROW_UNROLL + u, slot).start()
            return carry
        lax.fori_loop(0, bm // ROW_UNROLL, trip, 0)

    def weight_copies(e, ws):
        return (pltpu.make_async_copy(wg_hbm.at[e], wgf.at[ws], wsem.at[ws]),
                pltpu.make_async_copy(wu_hbm.at[e], wuf.at[ws], wsem.at[ws]),
                pltpu.make_async_copy(wd_hbm.at[e], wdf.at[ws], wsem.at[ws]))

    def fetch_weights(e, ws):
        for cp in weight_copies(e, ws):
            cp.start(priority=1)

    def gather_wait(slot):
        pltpu.make_async_copy(h_ref.at[pl.ds(0, bm * chunks), :], xbuf.at[slot],
                              sem.at[slot]).wait()

    @pl.when(i == 0)
    def _():
        gather(0, 0)
        fetch_weights(be_ref[0], wslot_ref[0])

    @pl.when(i < n_active)
    def _():
        slot = lax.rem(i, 2)
        gather_wait(slot)

        @pl.when(i + 1 < n_active)
        def _():
            gather(i + 1, 1 - slot)

        @pl.when(first_ref[i] == 1)
        def _():
            ws = wslot_ref[i]
            for cp in weight_copies(be_ref[i], ws):
                cp.wait()

            @pl.when(nxt_ref[i] >= 0)
            def _():
                fetch_weights(nxt_ref[i], 1 - ws)

            wgb[...] = wgf[ws].astype(BF16)
            wub[...] = wuf[ws].astype(BF16)
            wdb[...] = wdf[ws].astype(BF16)

        x = _rows_from_slabs(xbuf, slot, bm, chunks).astype(BF16)
        g = jnp.dot(x, wgb[...], preferred_element_type=F32)
        u = jnp.dot(x, wub[...], preferred_element_type=F32)
        hb = (g * _sigmoid(g) * u).astype(BF16)
        y = jnp.dot(hb, wdb[...], preferred_element_type=F32)
        for c in range(chunks):
            o_ref[pl.ds(c, bm, stride=chunks), :] = y[:, c * LANES:(c + 1) * LANES]

    @pl.when(i >= n_active)
    def _():
        o_ref[...] = jnp.zeros_like(o_ref)


def _experts(block_expert, n_active, row_tok, h2s, w_gate, w_up, w_down, bm):
    n_rows = row_tok.shape[0]
    nblk = n_rows // bm
    d, de = w_gate.shape[1], w_gate.shape[2]
    chunks = d // LANES
    assert bm % ROW_UNROLL == 0

    idx = jnp.arange(nblk, dtype=I32)
    prev = jnp.concatenate([block_expert[:1] - 1, block_expert[:-1]])
    first = ((block_expert != prev) & (idx < n_active[0])).astype(I32)
    wslot = ((jnp.cumsum(first) - 1) % 2).astype(I32)
    run_start = lax.cummin(jnp.where(first > 0, idx, nblk), axis=0, reverse=True)
    next_start = jnp.concatenate([run_start[1:], jnp.full((1,), nblk, I32)])
    nxt = jnp.where(next_start < nblk,
                    block_expert[jnp.minimum(next_start, nblk - 1)], -1).astype(I32)

    return pl.pallas_call(
        functools.partial(_experts_kernel, bm=bm, chunks=chunks),
        grid_spec=pltpu.PrefetchScalarGridSpec(
            num_scalar_prefetch=6,
            grid=(nblk,),
            in_specs=[pl.BlockSpec(memory_space=pl.ANY)] * 4,
            out_specs=pl.BlockSpec((bm * chunks, LANES), lambda i, *_: (i, 0)),
            scratch_shapes=[pltpu.VMEM((2, bm * chunks, LANES), F32),
                            pltpu.SemaphoreType.DMA((2,)),
                            pltpu.VMEM((2, d, de), F32),
                            pltpu.VMEM((2, d, de), F32),
                            pltpu.VMEM((2, de, d), F32),
                            pltpu.SemaphoreType.DMA((2,)),
                            pltpu.VMEM((d, de), BF16),
                            pltpu.VMEM((d, de), BF16),
                            pltpu.VMEM((de, d), BF16)],
        ),
        out_shape=jax.ShapeDtypeStruct((n_rows * chunks, LANES), F32),
        compiler_params=_cparams("arbitrary"),
        name="experts",
    )(block_expert, n_active, row_tok, first, wslot, nxt, h2s, w_gate, w_up, w_down)


def _combine_kernel(d1_ref, d2_ref, ys_ref, x1_ref, route_ref, g2_ref, lg_ref, lb_ref,
                    o_ref, y1buf, y2buf, sem, *, tm, chunks, alpha):
    i = pl.program_id(0)
    n_tiles = pl.num_programs(0)

    def row_copies(tile, t, slot):
        tok = tile * tm + t
        return (pltpu.make_async_copy(ys_ref.at[_slab(d1_ref[tok], chunks), :],
                                      y1buf.at[slot, _slab(t, chunks), :], sem.at[slot]),
                pltpu.make_async_copy(ys_ref.at[_slab(d2_ref[tok], chunks), :],
                                      y2buf.at[slot, _slab(t, chunks), :], sem.at[slot]))

    def gather(tile, slot):
        def trip(t8, carry):
            for u in range(ROW_UNROLL):
                for k, cp in enumerate(row_copies(tile, t8 * ROW_UNROLL + u, slot)):
                    cp.start(priority=k)
            return carry
        lax.fori_loop(0, tm // ROW_UNROLL, trip, 0)

    def gather_wait(slot):
        for buf in (y1buf, y2buf):
            pltpu.make_async_copy(ys_ref.at[pl.ds(0, tm * chunks), :], buf.at[slot],
                                  sem.at[slot]).wait()

    @pl.when(i == 0)
    def _():
        gather(0, 0)

    slot = lax.rem(i, 2)
    gather_wait(slot)

    @pl.when(i + 1 < n_tiles)
    def _():
        gather(i + 1, 1 - slot)

    route = route_ref[...]
    y = (route[:, 0:1] * _rows_from_slabs(y1buf, slot, tm, chunks)
         + route[:, 1:2] * _rows_from_slabs(y2buf, slot, tm, chunks))
    r = alpha * x1_ref[...] + g2_ref[0] * y
    o_ref[...] = _ln_rows(r) * lg_ref[...] + lb_ref[...]


def _combine(dest1, dest2, ys, x1, route, mod3, ln2_g, ln2_b, seq, alpha, tm=256):
    n, d = x1.shape
    tm = min(tm, seq)
    chunks = d // LANES
    assert tm % ROW_UNROLL == 0
    tok = lambda i, d1, d2: (i, 0)
    const = lambda i, d1, d2: (0, 0)
    return pl.pallas_call(
        functools.partial(_combine_kernel, tm=tm, chunks=chunks, alpha=alpha),
        grid_spec=pltpu.PrefetchScalarGridSpec(
            num_scalar_prefetch=2,
            grid=(n // tm,),
            in_specs=[pl.BlockSpec(memory_space=pl.ANY),
                      pl.BlockSpec((tm, d), tok),
                      pl.BlockSpec((tm, LANES), tok),
                      pl.BlockSpec((1, 1, d), lambda i, d1, d2: (((i * tm) // seq) * 6 + 5, 0, 0)),
                      pl.BlockSpec((1, d), const),
                      pl.BlockSpec((1, d), const)],
            out_specs=pl.BlockSpec((tm, d), tok),
            scratch_shapes=[pltpu.VMEM((2, tm * chunks, LANES), F32),
                            pltpu.VMEM((2, tm * chunks, LANES), F32),
                            pltpu.SemaphoreType.DMA((2,))],
        ),
        out_shape=jax.ShapeDtypeStruct((n, d), F32),
        compiler_params=_cparams("arbitrary"),
        name="combine",
    )(dest1, dest2, ys, x1, route, mod3, ln2_g, ln2_b)


EXPERT_BLOCK_ROWS = 256


def kernel(x, c, w_ada, b_ada, w_in, sgu_w, sgu_b, sgu_ln_g, sgu_ln_b, w_proj_a, w_proj_b,
           w_out, ln1_g, ln1_b, w_group, b_group, w_router, b_router, w_gate, w_up, w_down,
           ln2_g, ln2_b):
    b, s, d = x.shape
    n = b * s
    depth = w_ada.shape[0]
    alpha = (2.0 * depth) ** 0.25
    sub = d // LANES
    bm = EXPERT_BLOCK_ROWS
    n_assign = 2 * n
    n_rows = n_assign + N_EXPERTS * bm
    nblk = n_rows // bm

    in_cols = w_in.shape[2]
    colscale = jnp.ones((1, in_cols), F32).at[:, _COL_Q:_COL_K].set(
        SB_HEAD_DIM ** -0.5 * LOG2E)

    for l in range(depth):
        mod = _ada(c, w_ada[l], b_ada[l])
        mod3 = mod.reshape(b * 6, 1, d)

        x2 = x.reshape(n, d)
        proj = _inproj(x2, mod3, w_in[l].astype(BF16), colscale, s)
        out_a = _attention(proj.reshape(b, s, in_cols), d).reshape(n, SB_WIDTH)

        wr = jnp.zeros((d, LANES), F32)
        wr = wr.at[:, GROUP_LANE0:GROUP_LANE0 + N_GROUPS].set(w_group[l])
        wr = wr.at[:, EXPERT_LANE0:EXPERT_LANE0 + N_EXPERTS].set(w_router[l])
        wr_hi = wr.astype(BF16)
        wr_lo = (wr - wr_hi.astype(F32)).astype(BF16)
        wr = jnp.concatenate([wr_hi, wr_lo], axis=1)
        br = jnp.zeros((1, LANES), F32)
        br = br.at[0, GROUP_LANE0:GROUP_LANE0 + N_GROUPS].set(b_group[l])
        br = br.at[0, EXPERT_LANE0:EXPERT_LANE0 + N_EXPERTS].set(b_router[l])
        sgu_bias_full = jnp.repeat(sgu_b[l].T, SGU_GROUP_DIM, axis=1)

        x1, h2s, route, oh1, oh2 = _mix(
            out_a, proj, x2, mod3, sgu_w[l], sgu_bias_full,
            sgu_ln_g[l].reshape(1, -1), sgu_ln_b[l].reshape(1, -1),
            w_proj_a[l].astype(BF16), w_proj_b[l].astype(BF16), w_out[l].astype(BF16),
            ln1_g[l].reshape(1, d), ln1_b[l].reshape(1, d), wr, br, s, alpha)

        dest, blk = _plan(oh1, oh2, bm, nblk)
        dest1 = dest[:, 0]
        dest2 = dest[:, 1]
        block_expert = blk[:nblk, 0]
        n_active = blk[0:1, 1]

        row_tok = _invert(dest1, dest2, n_rows)
        ys = _experts(block_expert, n_active, row_tok, h2s, w_gate[l], w_up[l], w_down[l], bm)
        out = _combine(dest1, dest2, ys, x1, route, mod3, ln2_g[l].reshape(1, d),
                       ln2_b[l].reshape(1, d), s, alpha)
        x = out.reshape(b, s, d)
    return x
```

```python
import functools
import math

import jax
import jax.numpy as jnp
from jax import lax
from jax.experimental import pallas as pl
from jax.experimental.pallas import tpu as pltpu

F32 = jnp.float32
BF16 = jnp.bfloat16
I32 = jnp.int32

CHUNK = 64
SB_HEADS = 8
SB_HEAD_DIM = 128
SB_WIDTH = SB_HEADS * SB_HEAD_DIM
SGU_GROUPS = 8
SGU_GROUP_DIM = 128
SGU_WIDTH = SGU_GROUPS * SGU_GROUP_DIM
SGU_BLOCK = 128
N_GROUPS = 4
EXPERTS_PER_GROUP = 8
N_EXPERTS = N_GROUPS * EXPERTS_PER_GROUP
D_EXPERT = 512
LN_EPS = 1e-5

LANES = 128
SUBLANES = 8
VMEM_LIMIT_BYTES = 60000 * 1024

GROUP_LANE0 = 0
EXPERT_LANE0 = N_GROUPS

LOG2E = 1.4426950408889634
NEG_BIG = -1e30

_COL_Q = 0
_COL_K = SB_WIDTH
_COL_V = 2 * SB_WIDTH
_COL_SU = 3 * SB_WIDTH
_COL_SV = 3 * SB_WIDTH + SGU_WIDTH
_COL_GA = 3 * SB_WIDTH + 2 * SGU_WIDTH


def _COL_GB(d):
    return _COL_GA + d


def _cparams(*sem):
    return pltpu.CompilerParams(dimension_semantics=sem, vmem_limit_bytes=VMEM_LIMIT_BYTES)


def _ln_rows(x):
    mu = jnp.mean(x, axis=-1, keepdims=True)
    xc = x - mu
    var = jnp.mean(xc * xc, axis=-1, keepdims=True)
    return xc * lax.rsqrt(var + LN_EPS)


def _sigmoid(x):
    return 1.0 / (1.0 + jnp.exp(-x))


def _ada_kernel(c_ref, w_ref, b_ref, o_ref):
    c = c_ref[...]
    ca = c * _sigmoid(c)
    o_ref[...] = jnp.dot(ca, w_ref[...], preferred_element_type=F32,
                         precision=lax.Precision.HIGHEST) + b_ref[...]


def _ada(c, w_ada, b_ada, tn=1024):
    b, d = c.shape
    n_out = w_ada.shape[1]
    cp = jnp.zeros((SUBLANES, d), F32).at[:b].set(c)
    out = pl.pallas_call(
        _ada_kernel,
        grid=(n_out // tn,),
        in_specs=[pl.BlockSpec((SUBLANES, d), lambda j: (0, 0)),
                  pl.BlockSpec((d, tn), lambda j: (0, j)),
                  pl.BlockSpec((1, tn), lambda j: (0, j))],
        out_specs=pl.BlockSpec((SUBLANES, tn), lambda j: (0, j)),
        out_shape=jax.ShapeDtypeStruct((SUBLANES, n_out), F32),
        compiler_params=_cparams("arbitrary"),
        name="ada",
    )(cp, w_ada, b_ada.reshape(1, n_out))
    return out[:b]


def _inproj_kernel(x_ref, sh_ref, sc_ref, w_ref, cs_ref, o_ref, h_ref):
    @pl.when(pl.program_id(1) == 0)
    def _():
        h = _ln_rows(x_ref[...]) * (1.0 + sc_ref[0]) + sh_ref[0]
        h_ref[...] = h.astype(BF16)

    acc = jnp.dot(h_ref[...], w_ref[...], preferred_element_type=F32)
    o_ref[...] = (acc * cs_ref[...]).astype(BF16)


def _inproj(x2, mod3, w_in_bf, colscale, seq, tm=1024, tn=1024):
    n, d = x2.shape
    n_cols = w_in_bf.shape[1]
    tm = min(tm, seq)
    return pl.pallas_call(
        _inproj_kernel,
        grid=(n // tm, n_cols // tn),
        in_specs=[pl.BlockSpec((tm, d), lambda i, j: (i, 0)),
                  pl.BlockSpec((1, 1, d), lambda i, j: (((i * tm) // seq) * 6 + 0, 0, 0)),
                  pl.BlockSpec((1, 1, d), lambda i, j: (((i * tm) // seq) * 6 + 1, 0, 0)),
                  pl.BlockSpec((d, tn), lambda i, j: (0, j)),
                  pl.BlockSpec((1, tn), lambda i, j: (0, j))],
        out_specs=pl.BlockSpec((tm, tn), lambda i, j: (i, j)),
        out_shape=jax.ShapeDtypeStruct((n, n_cols), BF16),
        scratch_shapes=[pltpu.VMEM((tm, d), BF16)],
        compiler_params=_cparams("arbitrary", "arbitrary"),
        name="inproj",
    )(x2, mod3, mod3, w_in_bf, colscale)


Z2_MAX = 100.0
STICK_EXIT_LOG2 = 150.0


def _attn_kernel(q_ref, k_ref, v_ref, o_ref, *, tq, tk, heads):
    qi = pl.program_id(2)
    nd = tq // tk
    jd = qi * nd
    trow = lax.broadcasted_iota(I32, (tk, tk), 0)
    tcol = lax.broadcasted_iota(I32, (tk, tk), 1)
    tri = (trow > tcol).astype(BF16)
    hd = SB_HEAD_DIM
    qs = [q_ref[0, :, h * hd:(h + 1) * hd] for h in range(heads)]

    def block(j, runs, accs, causal):
        off = pl.multiple_of(j * tk, tk)
        hs = range(heads)
        z2s = [lax.dot_general(qs[h], k_ref[0, pl.ds(off, tk), h * hd:(h + 1) * hd],
                               (((1,), (1,)), ((), ())), preferred_element_type=F32) for h in hs]
        z2s = [lax.clamp(-Z2_MAX, z2, Z2_MAX) for z2 in z2s]
        sps = [jnp.log(1.0 + jnp.exp2(z2)) for z2 in z2s]
        if causal is not None:
            sps = [jnp.where(causal, sp, 0.0) for sp in sps]
        laters = [jnp.dot(sp.astype(BF16), tri, preferred_element_type=F32) for sp in sps]
        ws = [jnp.exp2(z2s[h] - LOG2E * (sps[h] + (laters[h] + runs[h]))) for h in hs]
        if causal is not None:
            ws = [jnp.where(causal, w, 0.0) for w in ws]
        new_accs = [accs[h] + jnp.dot(ws[h].astype(BF16),
                                      v_ref[0, pl.ds(off, tk), h * hd:(h + 1) * hd],
                                      preferred_element_type=F32) for h in hs]
        new_runs = [runs[h] + jnp.sum(sps[h], axis=1, keepdims=True) for h in hs]
        return tuple(new_runs), tuple(new_accs)

    def mass_left(runs):
        low = runs[0]
        for r in runs[1:]:
            low = jnp.minimum(low, r)
        return (jnp.min(low) * LOG2E < STICK_EXIT_LOG2).astype(I32)

    def diag_body(t, carry):
        c = nd - 1 - t
        row = lax.broadcasted_iota(I32, (tq, tk), 0)
        col = lax.broadcasted_iota(I32, (tq, tk), 1)
        causal = col + c * tk < row
        return block(jd + c, carry[0], carry[1], causal)

    def more(carry):
        t, go, _, _ = carry
        return (t < jd) & (go > 0)

    def body(carry):
        t, _, runs, accs = carry
        runs, accs = block(jd - 1 - t, runs, accs, None)
        return t + 1, mass_left(runs), runs, accs

    runs = tuple(jnp.zeros((tq, 1), F32) for _ in range(heads))
    accs = tuple(jnp.zeros((tq, hd), F32) for _ in range(heads))
    runs, accs = lax.fori_loop(0, nd, diag_body, (runs, accs))
    _, _, runs, accs = lax.while_loop(more, body, (jnp.int32(0), mass_left(runs), runs, accs))
    for h in range(heads):
        o_ref[0, :, h * hd:(h + 1) * hd] = accs[h].astype(BF16)


def _attention(proj3, d, tq=256, tk=256, heads=4):
    b, s, _ = proj3.shape
    tk = min(tk, s)
    tq = min(tq, s)
    assert tq % tk == 0 and s % tq == 0
    width = heads * SB_HEAD_DIM
    qc = _COL_Q // width
    kc = _COL_K // width
    vc = _COL_V // width
    return pl.pallas_call(
        functools.partial(_attn_kernel, tq=tq, tk=tk, heads=heads),
        grid=(b, SB_HEADS // heads, s // tq),
        in_specs=[pl.BlockSpec((1, tq, width), lambda bi, g, qi: (bi, qi, qc + g)),
                  pl.BlockSpec((1, s, width), lambda bi, g, qi: (bi, 0, kc + g)),
                  pl.BlockSpec((1, s, width), lambda bi, g, qi: (bi, 0, vc + g))],
        out_specs=pl.BlockSpec((1, tq, width), lambda bi, g, qi: (bi, qi, g)),
        out_shape=jax.ShapeDtypeStruct((b, s, SB_WIDTH), BF16),
        compiler_params=_cparams("arbitrary", "arbitrary", "arbitrary"),
        name="attn",
    )(proj3, proj3, proj3)


def _erf(x):
    return lax.erf(x)


def _gelu(x):
    return 0.5 * x * (1.0 + _erf(x * (1.0 / math.sqrt(2.0))))


def _mix_kernel(oa_ref, su_ref, sv_ref, ga0_ref, ga1_ref, gb0_ref, gb1_ref, x_ref,
                sw_ref, sbias_ref, sg_ref, sb_ref,
                wpa_ref, wpb_ref, wo_ref,
                g1_ref, ln1g_ref, ln1b_ref, sh2_ref, sc2_ref,
                wr_ref, br_ref,
                x1_ref, h2_ref, route_ref, oh1_ref, oh2_ref, *, tm, alpha):
    u = _gelu(su_ref[...].astype(F32))
    v = _gelu(sv_ref[...].astype(F32))
    v = _ln_rows(v) * sg_ref[...] + sb_ref[...]
    vb = v.astype(BF16)
    pos_t = lax.broadcasted_iota(I32, (SGU_BLOCK, SGU_BLOCK), 0)
    pos_s = lax.broadcasted_iota(I32, (SGU_BLOCK, SGU_BLOCK), 1)
    chunk_causal = (pos_s // CHUNK) <= (pos_t // CHUNK)
    n_pos_blocks = tm // SGU_BLOCK
    per_group = []
    for g in range(SGU_GROUPS):
        c0 = g * SGU_GROUP_DIM
        wg = jnp.where(chunk_causal, sw_ref[g], 0.0).astype(BF16)
        rhs = jnp.concatenate(
            [vb[nb * SGU_BLOCK:(nb + 1) * SGU_BLOCK, c0:c0 + SGU_GROUP_DIM]
             for nb in range(n_pos_blocks)], axis=1)
        per_group.append(jnp.dot(wg, rhs, preferred_element_type=F32))
    mixed = jnp.concatenate(
        [jnp.concatenate([per_group[g][:, nb * SGU_GROUP_DIM:(nb + 1) * SGU_GROUP_DIM]
                          for g in range(SGU_GROUPS)], axis=1) + sbias_ref[...]
         for nb in range(n_pos_blocks)], axis=0)
    out_b = (u * mixed).astype(BF16)

    pa = jnp.dot(oa_ref[...], wpa_ref[...], preferred_element_type=F32)
    pb = jnp.dot(out_b, wpb_ref[...], preferred_element_type=F32)
    ga = jnp.concatenate([ga0_ref[...], ga1_ref[...]], axis=1).astype(F32)
    gb = jnp.concatenate([gb0_ref[...], gb1_ref[...]], axis=1).astype(F32)
    merged = _sigmoid(ga) * pa + _sigmoid(gb) * pb
    y = jnp.dot(merged.astype(BF16), wo_ref[...], preferred_element_type=F32)

    r = alpha * x_ref[...] + g1_ref[0] * y
    x1 = _ln_rows(r) * ln1g_ref[...] + ln1b_ref[...]
    x1_ref[...] = x1
    h2 = _ln_rows(x1) * (1.0 + sc2_ref[0]) + sh2_ref[0]
    chunks = h2.shape[1] // LANES
    for c in range(chunks):
        h2_ref[pl.ds(c, tm, stride=chunks), :] = h2[:, c * LANES:(c + 1) * LANES]

    h_hi = h2.astype(BF16)
    h_lo = (h2 - h_hi.astype(F32)).astype(BF16)
    hi_w = jnp.dot(h_hi, wr_ref[...], preferred_element_type=F32)
    lo_w = jnp.dot(h_lo, wr_ref[:, :LANES], preferred_element_type=F32)
    logits = (hi_w[:, :LANES] + hi_w[:, LANES:]) + lo_w + br_ref[...]
    lane = lax.broadcasted_iota(I32, logits.shape, 1)
    gl = jnp.where(lane < N_GROUPS, logits, NEG_BIG)
    gmax = jnp.max(gl, axis=1, keepdims=True)
    g_sel = jnp.min(jnp.where(gl == gmax, lane, LANES), axis=1, keepdims=True)
    p_group = 1.0 / jnp.sum(jnp.exp(gl - gmax), axis=1, keepdims=True)
    lo_lane = EXPERT_LANE0 + g_sel * EXPERTS_PER_GROUP
    in_group = (lane >= lo_lane) & (lane < lo_lane + EXPERTS_PER_GROUP)
    el = jnp.where(in_group, logits, NEG_BIG)
    m1 = jnp.max(el, axis=1, keepdims=True)
    i1 = jnp.min(jnp.where(el == m1, lane, LANES), axis=1, keepdims=True)
    el2 = jnp.where(lane == i1, NEG_BIG, el)
    m2 = jnp.max(el2, axis=1, keepdims=True)
    i2 = jnp.min(jnp.where(el2 == m2, lane, LANES), axis=1, keepdims=True)
    e21 = jnp.exp(m2 - m1)
    w1 = p_group / (1.0 + e21)
    w2 = p_group * e21 / (1.0 + e21)
    route = jnp.where(lane == 0, w1, 0.0)
    route = jnp.where(lane == 1, w2, route)
    route = jnp.where(lane == 2, (i1 - EXPERT_LANE0).astype(F32), route)
    route = jnp.where(lane == 3, (i2 - EXPERT_LANE0).astype(F32), route)
    route_ref[...] = route
    oh1_ref[...] = jnp.where(lane == i1, 1.0, 0.0).astype(BF16)
    oh2_ref[...] = jnp.where(lane == i2, 1.0, 0.0).astype(BF16)


def _mix(out_a, proj, x2, mod3, sgu_w, sgu_bias_full, sgu_g, sgu_b, wpa, wpb, wo,
         ln1_g, ln1_b, wr, br, seq, alpha, tm=256):
    n, d = x2.shape
    tm = min(tm, seq)
    def pcol(start, width):
        assert start % width == 0
        return start // width
    su_c = pcol(_COL_SU, SGU_WIDTH)
    sv_c = pcol(_COL_SV, SGU_WIDTH)
    half = d // 2
    ga_c = pcol(_COL_GA, half)
    gb_c = pcol(_COL_GB(d), half)
    bidx = lambda i: (i * tm) // seq
    row = lambda i: (i, 0)
    const2 = lambda i: (0, 0)
    const3 = lambda i: (0, 0, 0)
    return pl.pallas_call(
        functools.partial(_mix_kernel, tm=tm, alpha=alpha),
        grid=(n // tm,),
        in_specs=[pl.BlockSpec((tm, SB_WIDTH), row),
                  pl.BlockSpec((tm, SGU_WIDTH), lambda i: (i, su_c)),
                  pl.BlockSpec((tm, SGU_WIDTH), lambda i: (i, sv_c)),
                  pl.BlockSpec((tm, half), lambda i: (i, ga_c)),
                  pl.BlockSpec((tm, half), lambda i: (i, ga_c + 1)),
                  pl.BlockSpec((tm, half), lambda i: (i, gb_c)),
                  pl.BlockSpec((tm, half), lambda i: (i, gb_c + 1)),
                  pl.BlockSpec((tm, d), row),
                  pl.BlockSpec((SGU_GROUPS, SGU_BLOCK, SGU_BLOCK), const3),
                  pl.BlockSpec((SGU_BLOCK, SGU_WIDTH), const2),
                  pl.BlockSpec((1, SGU_WIDTH), const2),
                  pl.BlockSpec((1, SGU_WIDTH), const2),
                  pl.BlockSpec((SB_WIDTH, d), const2),
                  pl.BlockSpec((SGU_WIDTH, d), const2),
                  pl.BlockSpec((d, d), const2),
                  pl.BlockSpec((1, 1, d), lambda i: (bidx(i) * 6 + 2, 0, 0)),
                  pl.BlockSpec((1, d), const2),
                  pl.BlockSpec((1, d), const2),
                  pl.BlockSpec((1, 1, d), lambda i: (bidx(i) * 6 + 3, 0, 0)),
                  pl.BlockSpec((1, 1, d), lambda i: (bidx(i) * 6 + 4, 0, 0)),
                  pl.BlockSpec((d, 2 * LANES), const2),
                  pl.BlockSpec((1, LANES), const2)],
        out_specs=[pl.BlockSpec((tm, d), row),
                   pl.BlockSpec((tm * (d // LANES), LANES), row),
                   pl.BlockSpec((tm, LANES), row),
                   pl.BlockSpec((tm, LANES), row),
                   pl.BlockSpec((tm, LANES), row)],
        out_shape=[jax.ShapeDtypeStruct((n, d), F32),
                   jax.ShapeDtypeStruct((n * (d // LANES), LANES), F32),
                   jax.ShapeDtypeStruct((n, LANES), F32),
                   jax.ShapeDtypeStruct((n, LANES), BF16),
                   jax.ShapeDtypeStruct((n, LANES), BF16)],
        compiler_params=_cparams("arbitrary"),
        name="mix",
    )(out_a, proj, proj, proj, proj, proj, proj, x2, sgu_w, sgu_bias_full, sgu_g, sgu_b,
      wpa, wpb, wo, mod3, ln1_g, ln1_b, mod3, mod3, wr, br)


def _plan_kernel(oh1_ref, oh2_ref, dest_ref, blk_ref, cnt_ref, start_ref, *, tm, bm, nblk_pad):
    phase = pl.program_id(0)
    i = pl.program_id(1)
    oh1 = oh1_ref[...]
    oh2 = oh2_ref[...]
    both = oh1 + oh2
    lane = lax.broadcasted_iota(I32, (SUBLANES, LANES), 1)

    @pl.when((phase == 0) & (i == 0))
    def _():
        cnt_ref[...] = jnp.zeros_like(cnt_ref)

    @pl.when(phase == 0)
    def _():
        ones = jnp.ones((SUBLANES, tm), BF16)
        cnt_ref[...] += jnp.dot(ones, both, preferred_element_type=F32)

    @pl.when((phase == 1) & (i == 0))
    def _():
        cnt = cnt_ref[...]
        padded = jnp.floor((cnt + (bm - 1)) * (1.0 / bm)) * bm
        r = lax.broadcasted_iota(I32, (LANES, LANES), 0)
        c = lax.broadcasted_iota(I32, (LANES, LANES), 1)
        upper = (r < c).astype(F32)
        starts = jnp.dot(padded, upper, preferred_element_type=F32,
                         precision=lax.Precision.HIGHEST)
        start_ref[...] = starts
        cnt_ref[...] = jnp.zeros_like(cnt_ref)
        ends = starts + padded
        is_exp = (lane[0:1] >= EXPERT_LANE0) & (lane[0:1] < EXPERT_LANE0 + N_EXPERTS)
        bstart = (lax.broadcasted_iota(I32, (nblk_pad, LANES), 0) * bm).astype(F32)
        done = jnp.where(is_exp & (ends[0:1] <= bstart), 1.0, 0.0)
        bexp = jnp.minimum(jnp.sum(done, axis=1, keepdims=True), N_EXPERTS - 1.0)
        total = jnp.sum(jnp.where(is_exp, padded[0:1], 0.0), axis=1, keepdims=True)
        blane = lax.broadcasted_iota(I32, (nblk_pad, LANES), 1)
        blk = jnp.where(blane == 0, bexp, jnp.where(blane == 1, total * (1.0 / bm), 0.0))
        blk_ref[...] = blk.astype(I32)

    @pl.when(phase == 1)
    def _():
        r = lax.broadcasted_iota(I32, (tm, tm), 0)
        c = lax.broadcasted_iota(I32, (tm, tm), 1)
        lower = (c < r).astype(BF16)
        before = jnp.dot(lower, both, preferred_element_type=F32) + cnt_ref[0:1]
        pos = before + start_ref[0:1]
        d1 = jnp.sum(pos * oh1.astype(F32), axis=1, keepdims=True)
        d2 = jnp.sum(pos * oh2.astype(F32), axis=1, keepdims=True)
        dlane = lax.broadcasted_iota(I32, (tm, LANES), 1)
        dest = jnp.where(dlane == 0, d1, jnp.where(dlane == 1, d2, 0.0))
        dest_ref[...] = dest.astype(I32)
        ones = jnp.ones((SUBLANES, tm), BF16)
        cnt_ref[...] += jnp.dot(ones, both, preferred_element_type=F32)


def _plan(oh1, oh2, bm, nblk, tm=512):
    n = oh1.shape[0]
    tm = min(tm, n)
    nblk_pad = -(-nblk // SUBLANES) * SUBLANES
    dest, blk = pl.pallas_call(
        functools.partial(_plan_kernel, tm=tm, bm=bm, nblk_pad=nblk_pad),
        grid=(2, n // tm),
        in_specs=[pl.BlockSpec((tm, LANES), lambda p, i: (i, 0)),
                  pl.BlockSpec((tm, LANES), lambda p, i: (i, 0))],
        out_specs=[pl.BlockSpec((tm, LANES), lambda p, i: (i * p, 0)),
                   pl.BlockSpec((nblk_pad, LANES), lambda p, i: (0, 0))],
        out_shape=[jax.ShapeDtypeStruct((n, LANES), I32),
                   jax.ShapeDtypeStruct((nblk_pad, LANES), I32)],
        scratch_shapes=[pltpu.VMEM((SUBLANES, LANES), F32),
                        pltpu.VMEM((SUBLANES, LANES), F32)],
        compiler_params=_cparams("arbitrary", "arbitrary"),
        name="plan",
    )(oh1, oh2)
    return dest, blk


ROW_UNROLL = 8


def _invert_kernel(d1_ref, d2_ref, zero_ref, rt_ref, sem, *, tb):
    i = pl.program_id(0)

    @pl.when(i == 0)
    def _():
        clear = pltpu.make_async_copy(zero_ref, rt_ref, sem)
        clear.start()
        clear.wait()

    def put(j, carry):
        for u in range(ROW_UNROLL):
            t = j * ROW_UNROLL + u
            rt_ref[d1_ref[0, 0, t]] = i * tb + t
            rt_ref[d2_ref[0, 0, t]] = i * tb + t
        return carry

    lax.fori_loop(0, tb // ROW_UNROLL, put, 0)


def _invert(dest1, dest2, n_rows, tb=2048):
    n = dest1.shape[0]
    tb = min(tb, n)
    assert n % tb == 0 and tb % ROW_UNROLL == 0
    blocks = lambda i: (i, 0, 0)
    return pl.pallas_call(
        functools.partial(_invert_kernel, tb=tb),
        grid=(n // tb,),
        in_specs=[pl.BlockSpec((1, 1, tb), blocks, memory_space=pltpu.SMEM),
                  pl.BlockSpec((1, 1, tb), blocks, memory_space=pltpu.SMEM),
                  pl.BlockSpec(memory_space=pl.ANY)],
        out_specs=pl.BlockSpec(memory_space=pltpu.SMEM),
        out_shape=jax.ShapeDtypeStruct((n_rows,), I32),
        scratch_shapes=[pltpu.SemaphoreType.DMA(())],
        compiler_params=_cparams("arbitrary"),
        name="invert",
    )(dest1.reshape(n // tb, 1, tb), dest2.reshape(n // tb, 1, tb), jnp.zeros((n_rows,), I32))


def _slab(idx, chunks):
    return pl.ds(pl.multiple_of(idx * chunks, chunks), chunks)


def _rows_from_slabs(buf_ref, slot, rows, chunks):
    return jnp.concatenate(
        [buf_ref[slot, pl.ds(c, rows, stride=chunks), :] for c in range(chunks)], axis=1)


def _experts_kernel(be_ref, na_ref, first_ref, wslot_ref, nxt_ref,
                    rt0_ref, rtn_ref, h_ref, wg_hbm, wu_hbm, wd_hbm, o_ref,
                    xbuf, sem, wgf, wuf, wdf, wsem, wgb, wub, wdb, *, bm, chunks):
    i = pl.program_id(0)
    n_active = na_ref[0]

    def gather(tok_ref, slot):
        def trip(r8, carry):
            for u in range(ROW_UNROLL):
                r = r8 * ROW_UNROLL + u
                pltpu.make_async_copy(h_ref.at[_slab(tok_ref[0, 0, r], chunks), :],
                                      xbuf.at[slot, _slab(r, chunks), :], sem.at[slot]).start()
            return carry
        lax.fori_loop(0, bm // ROW_UNROLL, trip, 0)

    def weight_copies(e, ws):
        return (pltpu.make_async_copy(wg_hbm.at[e], wgf.at[ws], wsem.at[ws]),
                pltpu.make_async_copy(wu_hbm.at[e], wuf.at[ws], wsem.at[ws]),
                pltpu.make_async_copy(wd_hbm.at[e], wdf.at[ws], wsem.at[ws]))

    def fetch_weights(e, ws):
        for cp in weight_copies(e, ws):
            cp.start(priority=1)

    def gather_wait(slot):
        pltpu.make_async_copy(h_ref.at[pl.ds(0, bm * chunks), :], xbuf.at[slot],
                              sem.at[slot]).wait()

    @pl.when(i == 0)
    def _():
        gather(rt0_ref, 0)
        fetch_weights(be_ref[0], wslot_ref[0])

    @pl.when(i < n_active)
    def _():
        slot = lax.rem(i, 2)
        gather_wait(slot)

        @pl.when(i + 1 < n_active)
        def _():
            gather(rtn_ref, 1 - slot)

        @pl.when(first_ref[i] == 1)
        def _():
            ws = wslot_ref[i]
            for cp in weight_copies(be_ref[i], ws):
                cp.wait()

            @pl.when(nxt_ref[i] >= 0)
            def _():
                fetch_weights(nxt_ref[i], 1 - ws)

            wgb[...] = wgf[ws].astype(BF16)
            wub[...] = wuf[ws].astype(BF16)
            wdb[...] = wdf[ws].astype(BF16)

        x = _rows_from_slabs(xbuf, slot, bm, chunks).astype(BF16)
        g = jnp.dot(x, wgb[...], preferred_element_type=F32)
        u = jnp.dot(x, wub[...], preferred_element_type=F32)
        hb = (g * _sigmoid(g) * u).astype(BF16)
        y = jnp.dot(hb, wdb[...], preferred_element_type=F32)
        for c in range(chunks):
            o_ref[pl.ds(c, bm, stride=chunks), :] = y[:, c * LANES:(c + 1) * LANES]

    @pl.when(i >= n_active)
    def _():
        o_ref[...] = jnp.zeros_like(o_ref)


def _experts(block_expert, n_active, row_tok, h2s, w_gate, w_up, w_down, bm):
    n_rows = row_tok.shape[0]
    nblk = n_rows // bm
    d, de = w_gate.shape[1], w_gate.shape[2]
    chunks = d // LANES
    assert bm % ROW_UNROLL == 0
    row_blocks = row_tok.reshape(nblk, 1, bm)

    idx = jnp.arange(nblk, dtype=I32)
    prev = jnp.concatenate([block_expert[:1] - 1, block_expert[:-1]])
    first = ((block_expert != prev) & (idx < n_active[0])).astype(I32)
    wslot = ((jnp.cumsum(first) - 1) % 2).astype(I32)
    run_start = lax.cummin(jnp.where(first > 0, idx, nblk), axis=0, reverse=True)
    next_start = jnp.concatenate([run_start[1:], jnp.full((1,), nblk, I32)])
    nxt = jnp.where(next_start < nblk,
                    block_expert[jnp.minimum(next_start, nblk - 1)], -1).astype(I32)

    return pl.pallas_call(
        functools.partial(_experts_kernel, bm=bm, chunks=chunks),
        grid_spec=pltpu.PrefetchScalarGridSpec(
            num_scalar_prefetch=5,
            grid=(nblk,),
            in_specs=[pl.BlockSpec((1, 1, bm), lambda i, *_: (0, 0, 0),
                                   memory_space=pltpu.SMEM),
                      pl.BlockSpec((1, 1, bm), lambda i, *_: (jnp.minimum(i + 1, nblk - 1), 0, 0),
                                   memory_space=pltpu.SMEM)]
                     + [pl.BlockSpec(memory_space=pl.ANY)] * 4,
            out_specs=pl.BlockSpec((bm * chunks, LANES), lambda i, *_: (i, 0)),
            scratch_shapes=[pltpu.VMEM((2, bm * chunks, LANES), F32),
                            pltpu.SemaphoreType.DMA((2,)),
                            pltpu.VMEM((2, d, de), F32),
                            pltpu.VMEM((2, d, de), F32),
                            pltpu.VMEM((2, de, d), F32),
                            pltpu.SemaphoreType.DMA((2,)),
                            pltpu.VMEM((d, de), BF16),
                            pltpu.VMEM((d, de), BF16),
                            pltpu.VMEM((de, d), BF16)],
        ),
        out_shape=jax.ShapeDtypeStruct((n_rows * chunks, LANES), F32),
        compiler_params=_cparams("arbitrary"),
        name="experts",
    )(block_expert, n_active, first, wslot, nxt, row_blocks, row_blocks,
      h2s, w_gate, w_up, w_down)


def _combine_kernel(d1_ref, d2_ref, ys_ref, x1_ref, route_ref, g2_ref, lg_ref, lb_ref,
                    o_ref, y1buf, y2buf, sem, *, tm, chunks, alpha):
    i = pl.program_id(0)
    n_tiles = pl.num_programs(0)

    def row_copies(tile, t, slot):
        tok = tile * tm + t
        return (pltpu.make_async_copy(ys_ref.at[_slab(d1_ref[tok], chunks), :],
                                      y1buf.at[slot, _slab(t, chunks), :], sem.at[slot]),
                pltpu.make_async_copy(ys_ref.at[_slab(d2_ref[tok], chunks), :],
                                      y2buf.at[slot, _slab(t, chunks), :], sem.at[slot]))

    def gather(tile, slot):
        def trip(t8, carry):
            for u in range(ROW_UNROLL):
                for k, cp in enumerate(row_copies(tile, t8 * ROW_UNROLL + u, slot)):
                    cp.start(priority=k)
            return carry
        lax.fori_loop(0, tm // ROW_UNROLL, trip, 0)

    def gather_wait(slot):
        for buf in (y1buf, y2buf):
            pltpu.make_async_copy(ys_ref.at[pl.ds(0, tm * chunks), :], buf.at[slot],
                                  sem.at[slot]).wait()

    @pl.when(i == 0)
    def _():
        gather(0, 0)

    slot = lax.rem(i, 2)
    gather_wait(slot)

    @pl.when(i + 1 < n_tiles)
    def _():
        gather(i + 1, 1 - slot)

    route = route_ref[...]
    y = (route[:, 0:1] * _rows_from_slabs(y1buf, slot, tm, chunks)
         + route[:, 1:2] * _rows_from_slabs(y2buf, slot, tm, chunks))
    r = alpha * x1_ref[...] + g2_ref[0] * y
    o_ref[...] = _ln_rows(r) * lg_ref[...] + lb_ref[...]


def _combine(dest1, dest2, ys, x1, route, mod3, ln2_g, ln2_b, seq, alpha, tm=256):
    n, d = x1.shape
    tm = min(tm, seq)
    chunks = d // LANES
    assert tm % ROW_UNROLL == 0
    tok = lambda i, d1, d2: (i, 0)
    const = lambda i, d1, d2: (0, 0)
    return pl.pallas_call(
        functools.partial(_combine_kernel, tm=tm, chunks=chunks, alpha=alpha),
        grid_spec=pltpu.PrefetchScalarGridSpec(
            num_scalar_prefetch=2,
            grid=(n // tm,),
            in_specs=[pl.BlockSpec(memory_space=pl.ANY),
                      pl.BlockSpec((tm, d), tok),
                      pl.BlockSpec((tm, LANES), tok),
                      pl.BlockSpec((1, 1, d), lambda i, d1, d2: (((i * tm) // seq) * 6 + 5, 0, 0)),
                      pl.BlockSpec((1, d), const),
                      pl.BlockSpec((1, d), const)],
            out_specs=pl.BlockSpec((tm, d), tok),
            scratch_shapes=[pltpu.VMEM((2, tm * chunks, LANES), F32),
                            pltpu.VMEM((2, tm * chunks, LANES), F32),
                            pltpu.SemaphoreType.DMA((2,))],
        ),
        out_shape=jax.ShapeDtypeStruct((n, d), F32),
        compiler_params=_cparams("arbitrary"),
        name="combine",
    )(dest1, dest2, ys, x1, route, mod3, ln2_g, ln2_b)


EXPERT_BLOCK_ROWS = 256


def kernel(x, c, w_ada, b_ada, w_in, sgu_w, sgu_b, sgu_ln_g, sgu_ln_b, w_proj_a, w_proj_b,
           w_out, ln1_g, ln1_b, w_group, b_group, w_router, b_router, w_gate, w_up, w_down,
           ln2_g, ln2_b):
    b, s, d = x.shape
    n = b * s
    depth = w_ada.shape[0]
    alpha = (2.0 * depth) ** 0.25
    sub = d // LANES
    bm = EXPERT_BLOCK_ROWS
    n_assign = 2 * n
    n_rows = n_assign + N_EXPERTS * bm
    nblk = n_rows // bm

    in_cols = w_in.shape[2]
    colscale = jnp.ones((1, in_cols), F32).at[:, _COL_Q:_COL_K].set(
        SB_HEAD_DIM ** -0.5 * LOG2E)

    for l in range(depth):
        mod = _ada(c, w_ada[l], b_ada[l])
        mod3 = mod.reshape(b * 6, 1, d)

        x2 = x.reshape(n, d)
        proj = _inproj(x2, mod3, w_in[l].astype(BF16), colscale, s)
        out_a = _attention(proj.reshape(b, s, in_cols), d).reshape(n, SB_WIDTH)

        wr = jnp.zeros((d, LANES), F32)
        wr = wr.at[:, GROUP_LANE0:GROUP_LANE0 + N_GROUPS].set(w_group[l])
        wr = wr.at[:, EXPERT_LANE0:EXPERT_LANE0 + N_EXPERTS].set(w_router[l])
        wr_hi = wr.astype(BF16)
        wr_lo = (wr - wr_hi.astype(F32)).astype(BF16)
        wr = jnp.concatenate([wr_hi, wr_lo], axis=1)
        br = jnp.zeros((1, LANES), F32)
        br = br.at[0, GROUP_LANE0:GROUP_LANE0 + N_GROUPS].set(b_group[l])
        br = br.at[0, EXPERT_LANE0:EXPERT_LANE0 + N_EXPERTS].set(b_router[l])
        sgu_bias_full = jnp.repeat(sgu_b[l].T, SGU_GROUP_DIM, axis=1)

        x1, h2s, route, oh1, oh2 = _mix(
            out_a, proj, x2, mod3, sgu_w[l], sgu_bias_full,
            sgu_ln_g[l].reshape(1, -1), sgu_ln_b[l].reshape(1, -1),
            w_proj_a[l].astype(BF16), w_proj_b[l].astype(BF16), w_out[l].astype(BF16),
            ln1_g[l].reshape(1, d), ln1_b[l].reshape(1, d), wr, br, s, alpha)

        dest, blk = _plan(oh1, oh2, bm, nblk)
        dest1 = dest[:, 0]
        dest2 = dest[:, 1]
        block_expert = blk[:nblk, 0]
        n_active = blk[0:1, 1]

        row_tok = _invert(dest1, dest2, n_rows)
        ys = _experts(block_expert, n_active, row_tok, h2s, w_gate[l], w_up[l], w_down[l], bm)
        out = _combine(dest1, dest2, ys, x1, route, mod3, ln2_g[l].reshape(1, d),
                       ln2_b[l].reshape(1, d), s, alpha)
        x = out.reshape(b, s, d)
    return x
```

```python
import functools
import math

import jax
import jax.numpy as jnp
from jax import lax
from jax.experimental import pallas as pl
from jax.experimental.pallas import tpu as pltpu

F32 = jnp.float32
BF16 = jnp.bfloat16
I32 = jnp.int32

CHUNK = 64
SB_HEADS = 8
SB_HEAD_DIM = 128
SB_WIDTH = SB_HEADS * SB_HEAD_DIM
SGU_GROUPS = 8
SGU_GROUP_DIM = 128
SGU_WIDTH = SGU_GROUPS * SGU_GROUP_DIM
SGU_BLOCK = 128
N_GROUPS = 4
EXPERTS_PER_GROUP = 8
N_EXPERTS = N_GROUPS * EXPERTS_PER_GROUP
D_EXPERT = 512
LN_EPS = 1e-5

LANES = 128
SUBLANES = 8
VMEM_LIMIT_BYTES = 60000 * 1024

GROUP_LANE0 = 0
EXPERT_LANE0 = N_GROUPS

LOG2E = 1.4426950408889634
NEG_BIG = -1e30

_COL_Q = 0
_COL_K = SB_WIDTH
_COL_V = 2 * SB_WIDTH
_COL_SU = 3 * SB_WIDTH
_COL_SV = 3 * SB_WIDTH + SGU_WIDTH
_COL_GA = 3 * SB_WIDTH + 2 * SGU_WIDTH


def _COL_GB(d):
    return _COL_GA + d


def _cparams(*sem):
    return pltpu.CompilerParams(dimension_semantics=sem, vmem_limit_bytes=VMEM_LIMIT_BYTES)


def _ln_rows(x):
    mu = jnp.mean(x, axis=-1, keepdims=True)
    xc = x - mu
    var = jnp.mean(xc * xc, axis=-1, keepdims=True)
    return xc * lax.rsqrt(var + LN_EPS)


def _sigmoid(x):
    return 1.0 / (1.0 + jnp.exp(-x))


def _ada_kernel(c_ref, w_ref, b_ref, o_ref):
    c = c_ref[...]
    ca = c * _sigmoid(c)
    o_ref[...] = jnp.dot(ca, w_ref[...], preferred_element_type=F32,
                         precision=lax.Precision.HIGHEST) + b_ref[...]


def _ada(c, w_ada, b_ada, tn=1024):
    b, d = c.shape
    n_out = w_ada.shape[1]
    cp = jnp.zeros((SUBLANES, d), F32).at[:b].set(c)
    out = pl.pallas_call(
        _ada_kernel,
        grid=(n_out // tn,),
        in_specs=[pl.BlockSpec((SUBLANES, d), lambda j: (0, 0)),
                  pl.BlockSpec((d, tn), lambda j: (0, j)),
                  pl.BlockSpec((1, tn), lambda j: (0, j))],
        out_specs=pl.BlockSpec((SUBLANES, tn), lambda j: (0, j)),
        out_shape=jax.ShapeDtypeStruct((SUBLANES, n_out), F32),
        compiler_params=_cparams("arbitrary"),
        name="ada",
    )(cp, w_ada, b_ada.reshape(1, n_out))
    return out[:b]


def _inproj_kernel(x_ref, sh_ref, sc_ref, w_ref, cs_ref, o_ref, h_ref):
    @pl.when(pl.program_id(1) == 0)
    def _():
        h = _ln_rows(x_ref[...]) * (1.0 + sc_ref[0]) + sh_ref[0]
        h_ref[...] = h.astype(BF16)

    acc = jnp.dot(h_ref[...], w_ref[...], preferred_element_type=F32)
    o_ref[...] = (acc * cs_ref[...]).astype(BF16)


def _inproj(x2, mod3, w_in_bf, colscale, seq, tm=1024, tn=1024):
    n, d = x2.shape
    n_cols = w_in_bf.shape[1]
    tm = min(tm, seq)
    return pl.pallas_call(
        _inproj_kernel,
        grid=(n // tm, n_cols // tn),
        in_specs=[pl.BlockSpec((tm, d), lambda i, j: (i, 0)),
                  pl.BlockSpec((1, 1, d), lambda i, j: (((i * tm) // seq) * 6 + 0, 0, 0)),
                  pl.BlockSpec((1, 1, d), lambda i, j: (((i * tm) // seq) * 6 + 1, 0, 0)),
                  pl.BlockSpec((d, tn), lambda i, j: (0, j)),
                  pl.BlockSpec((1, tn), lambda i, j: (0, j))],
        out_specs=pl.BlockSpec((tm, tn), lambda i, j: (i, j)),
        out_shape=jax.ShapeDtypeStruct((n, n_cols), BF16),
        scratch_shapes=[pltpu.VMEM((tm, d), BF16)],
        compiler_params=_cparams("arbitrary", "arbitrary"),
        name="inproj",
    )(x2, mod3, mod3, w_in_bf, colscale)


Z2_MAX = 100.0
STICK_EXIT_LOG2 = 150.0


def _attn_kernel(q_ref, k_ref, v_ref, o_ref, *, tq, tk, heads):
    qi = pl.program_id(2)
    nd = tq // tk
    jd = qi * nd
    trow = lax.broadcasted_iota(I32, (tk, tk), 0)
    tcol = lax.broadcasted_iota(I32, (tk, tk), 1)
    tri = (trow > tcol).astype(BF16)
    hd = SB_HEAD_DIM
    qs = [q_ref[0, :, h * hd:(h + 1) * hd] for h in range(heads)]

    def block(j, runs, accs, causal):
        off = pl.multiple_of(j * tk, tk)
        hs = range(heads)
        z2s = [lax.dot_general(qs[h], k_ref[0, pl.ds(off, tk), h * hd:(h + 1) * hd],
                               (((1,), (1,)), ((), ())), preferred_element_type=F32) for h in hs]
        z2s = [lax.clamp(-Z2_MAX, z2, Z2_MAX) for z2 in z2s]
        sps = [jnp.log(1.0 + jnp.exp2(z2)) for z2 in z2s]
        if causal is not None:
            sps = [jnp.where(causal, sp, 0.0) for sp in sps]
        laters = [jnp.dot(sp.astype(BF16), tri, preferred_element_type=F32) for sp in sps]
        ws = [jnp.exp2(z2s[h] - LOG2E * (sps[h] + (laters[h] + runs[h]))) for h in hs]
        if causal is not None:
            ws = [jnp.where(causal, w, 0.0) for w in ws]
        new_accs = [accs[h] + jnp.dot(ws[h].astype(BF16),
                                      v_ref[0, pl.ds(off, tk), h * hd:(h + 1) * hd],
                                      preferred_element_type=F32) for h in hs]
        new_runs = [runs[h] + jnp.sum(sps[h], axis=1, keepdims=True) for h in hs]
        return tuple(new_runs), tuple(new_accs)

    def mass_left(runs):
        low = runs[0]
        for r in runs[1:]:
            low = jnp.minimum(low, r)
        return (jnp.min(low) * LOG2E < STICK_EXIT_LOG2).astype(I32)

    def diag_body(t, carry):
        c = nd - 1 - t
        row = lax.broadcasted_iota(I32, (tq, tk), 0)
        col = lax.broadcasted_iota(I32, (tq, tk), 1)
        causal = col + c * tk < row
        return block(jd + c, carry[0], carry[1], causal)

    def more(carry):
        t, go, _, _ = carry
        return (t < jd) & (go > 0)

    def body(carry):
        t, _, runs, accs = carry
        runs, accs = block(jd - 1 - t, runs, accs, None)
        return t + 1, mass_left(runs), runs, accs

    runs = tuple(jnp.zeros((tq, 1), F32) for _ in range(heads))
    accs = tuple(jnp.zeros((tq, hd), F32) for _ in range(heads))
    runs, accs = lax.fori_loop(0, nd, diag_body, (runs, accs))
    _, _, runs, accs = lax.while_loop(more, body, (jnp.int32(0), mass_left(runs), runs, accs))
    for h in range(heads):
        o_ref[0, :, h * hd:(h + 1) * hd] = accs[h].astype(BF16)


def _attention(proj3, d, tq=256, tk=256, heads=4):
    b, s, _ = proj3.shape
    tk = min(tk, s)
    tq = min(tq, s)
    assert tq % tk == 0 and s % tq == 0
    width = heads * SB_HEAD_DIM
    qc = _COL_Q // width
    kc = _COL_K // width
    vc = _COL_V // width
    return pl.pallas_call(
        functools.partial(_attn_kernel, tq=tq, tk=tk, heads=heads),
        grid=(b, SB_HEADS // heads, s // tq),
        in_specs=[pl.BlockSpec((1, tq, width), lambda bi, g, qi: (bi, qi, qc + g)),
                  pl.BlockSpec((1, s, width), lambda bi, g, qi: (bi, 0, kc + g)),
                  pl.BlockSpec((1, s, width), lambda bi, g, qi: (bi, 0, vc + g))],
        out_specs=pl.BlockSpec((1, tq, width), lambda bi, g, qi: (bi, qi, g)),
        out_shape=jax.ShapeDtypeStruct((b, s, SB_WIDTH), BF16),
        compiler_params=_cparams("arbitrary", "arbitrary", "arbitrary"),
        name="attn",
    )(proj3, proj3, proj3)


def _erf(x):
    return lax.erf(x)


def _gelu(x):
    return 0.5 * x * (1.0 + _erf(x * (1.0 / math.sqrt(2.0))))


def _mix_kernel(oa_ref, su_ref, sv_ref, ga0_ref, ga1_ref, gb0_ref, gb1_ref, x_ref,
                sw_ref, sbias_ref, sg_ref, sb_ref,
                wpa_ref, wpb_ref, wo_ref,
                g1_ref, ln1g_ref, ln1b_ref, sh2_ref, sc2_ref,
                wr_ref, br_ref,
                x1_ref, h2_ref, route_ref, oh1_ref, oh2_ref, *, tm, alpha):
    u = _gelu(su_ref[...].astype(F32))
    v = _gelu(sv_ref[...].astype(F32))
    v = _ln_rows(v) * sg_ref[...] + sb_ref[...]
    vb = v.astype(BF16)
    pos_t = lax.broadcasted_iota(I32, (SGU_BLOCK, SGU_BLOCK), 0)
    pos_s = lax.broadcasted_iota(I32, (SGU_BLOCK, SGU_BLOCK), 1)
    chunk_causal = (pos_s // CHUNK) <= (pos_t // CHUNK)
    n_pos_blocks = tm // SGU_BLOCK
    per_group = []
    for g in range(SGU_GROUPS):
        c0 = g * SGU_GROUP_DIM
        wg = jnp.where(chunk_causal, sw_ref[g], 0.0).astype(BF16)
        rhs = jnp.concatenate(
            [vb[nb * SGU_BLOCK:(nb + 1) * SGU_BLOCK, c0:c0 + SGU_GROUP_DIM]
             for nb in range(n_pos_blocks)], axis=1)
        per_group.append(jnp.dot(wg, rhs, preferred_element_type=F32))
    mixed = jnp.concatenate(
        [jnp.concatenate([per_group[g][:, nb * SGU_GROUP_DIM:(nb + 1) * SGU_GROUP_DIM]
                          for g in range(SGU_GROUPS)], axis=1) + sbias_ref[...]
         for nb in range(n_pos_blocks)], axis=0)
    out_b = (u * mixed).astype(BF16)

    pa = jnp.dot(oa_ref[...], wpa_ref[...], preferred_element_type=F32)
    pb = jnp.dot(out_b, wpb_ref[...], preferred_element_type=F32)
    ga = jnp.concatenate([ga0_ref[...], ga1_ref[...]], axis=1).astype(F32)
    gb = jnp.concatenate([gb0_ref[...], gb1_ref[...]], axis=1).astype(F32)
    merged = _sigmoid(ga) * pa + _sigmoid(gb) * pb
    y = jnp.dot(merged.astype(BF16), wo_ref[...], preferred_element_type=F32)

    r = alpha * x_ref[...] + g1_ref[0] * y
    x1 = _ln_rows(r) * ln1g_ref[...] + ln1b_ref[...]
    x1_ref[...] = x1
    h2 = _ln_rows(x1) * (1.0 + sc2_ref[0]) + sh2_ref[0]
    chunks = h2.shape[1] // LANES
    for c in range(chunks):
        h2_ref[pl.ds(c, tm, stride=chunks), :] = h2[:, c * LANES:(c + 1) * LANES]

    h_hi = h2.astype(BF16)
    h_lo = (h2 - h_hi.astype(F32)).astype(BF16)
    hi_w = jnp.dot(h_hi, wr_ref[...], preferred_element_type=F32)
    lo_w = jnp.dot(h_lo, wr_ref[:, :LANES], preferred_element_type=F32)
    logits = (hi_w[:, :LANES] + hi_w[:, LANES:]) + lo_w + br_ref[...]
    lane = lax.broadcasted_iota(I32, logits.shape, 1)
    gl = jnp.where(lane < N_GROUPS, logits, NEG_BIG)
    gmax = jnp.max(gl, axis=1, keepdims=True)
    g_sel = jnp.min(jnp.where(gl == gmax, lane, LANES), axis=1, keepdims=True)
    p_group = 1.0 / jnp.sum(jnp.exp(gl - gmax), axis=1, keepdims=True)
    lo_lane = EXPERT_LANE0 + g_sel * EXPERTS_PER_GROUP
    in_group = (lane >= lo_lane) & (lane < lo_lane + EXPERTS_PER_GROUP)
    el = jnp.where(in_group, logits, NEG_BIG)
    m1 = jnp.max(el, axis=1, keepdims=True)
    i1 = jnp.min(jnp.where(el == m1, lane, LANES), axis=1, keepdims=True)
    el2 = jnp.where(lane == i1, NEG_BIG, el)
    m2 = jnp.max(el2, axis=1, keepdims=True)
    i2 = jnp.min(jnp.where(el2 == m2, lane, LANES), axis=1, keepdims=True)
    e21 = jnp.exp(m2 - m1)
    w1 = p_group / (1.0 + e21)
    w2 = p_group * e21 / (1.0 + e21)
    route = jnp.where(lane == 0, w1, 0.0)
    route = jnp.where(lane == 1, w2, route)
    route = jnp.where(lane == 2, (i1 - EXPERT_LANE0).astype(F32), route)
    route = jnp.where(lane == 3, (i2 - EXPERT_LANE0).astype(F32), route)
    route_ref[...] = route
    oh1_ref[...] = jnp.where(lane == i1, 1.0, 0.0).astype(BF16)
    oh2_ref[...] = jnp.where(lane == i2, 1.0, 0.0).astype(BF16)


def _mix(out_a, proj, x2, mod3, sgu_w, sgu_bias_full, sgu_g, sgu_b, wpa, wpb, wo,
         ln1_g, ln1_b, wr, br, seq, alpha, tm=256):
    n, d = x2.shape
    tm = min(tm, seq)
    def pcol(start, width):
        assert start % width == 0
        return start // width
    su_c = pcol(_COL_SU, SGU_WIDTH)
    sv_c = pcol(_COL_SV, SGU_WIDTH)
    half = d // 2
    ga_c = pcol(_COL_GA, half)
    gb_c = pcol(_COL_GB(d), half)
    bidx = lambda i: (i * tm) // seq
    row = lambda i: (i, 0)
    const2 = lambda i: (0, 0)
    const3 = lambda i: (0, 0, 0)
    return pl.pallas_call(
        functools.partial(_mix_kernel, tm=tm, alpha=alpha),
        grid=(n // tm,),
        in_specs=[pl.BlockSpec((tm, SB_WIDTH), row),
                  pl.BlockSpec((tm, SGU_WIDTH), lambda i: (i, su_c)),
                  pl.BlockSpec((tm, SGU_WIDTH), lambda i: (i, sv_c)),
                  pl.BlockSpec((tm, half), lambda i: (i, ga_c)),
                  pl.BlockSpec((tm, half), lambda i: (i, ga_c + 1)),
                  pl.BlockSpec((tm, half), lambda i: (i, gb_c)),
                  pl.BlockSpec((tm, half), lambda i: (i, gb_c + 1)),
                  pl.BlockSpec((tm, d), row),
                  pl.BlockSpec((SGU_GROUPS, SGU_BLOCK, SGU_BLOCK), const3),
                  pl.BlockSpec((SGU_BLOCK, SGU_WIDTH), const2),
                  pl.BlockSpec((1, SGU_WIDTH), const2),
                  pl.BlockSpec((1, SGU_WIDTH), const2),
                  pl.BlockSpec((SB_WIDTH, d), const2),
                  pl.BlockSpec((SGU_WIDTH, d), const2),
                  pl.BlockSpec((d, d), const2),
                  pl.BlockSpec((1, 1, d), lambda i: (bidx(i) * 6 + 2, 0, 0)),
                  pl.BlockSpec((1, d), const2),
                  pl.BlockSpec((1, d), const2),
                  pl.BlockSpec((1, 1, d), lambda i: (bidx(i) * 6 + 3, 0, 0)),
                  pl.BlockSpec((1, 1, d), lambda i: (bidx(i) * 6 + 4, 0, 0)),
                  pl.BlockSpec((d, 2 * LANES), const2),
                  pl.BlockSpec((1, LANES), const2)],
        out_specs=[pl.BlockSpec((tm, d), row),
                   pl.BlockSpec((tm * (d // LANES), LANES), row),
                   pl.BlockSpec((tm, LANES), row),
                   pl.BlockSpec((tm, LANES), row),
                   pl.BlockSpec((tm, LANES), row)],
        out_shape=[jax.ShapeDtypeStruct((n, d), F32),
                   jax.ShapeDtypeStruct((n * (d // LANES), LANES), F32),
                   jax.ShapeDtypeStruct((n, LANES), F32),
                   jax.ShapeDtypeStruct((n, LANES), BF16),
                   jax.ShapeDtypeStruct((n, LANES), BF16)],
        compiler_params=_cparams("arbitrary"),
        name="mix",
    )(out_a, proj, proj, proj, proj, proj, proj, x2, sgu_w, sgu_bias_full, sgu_g, sgu_b,
      wpa, wpb, wo, mod3, ln1_g, ln1_b, mod3, mod3, wr, br)


def _plan_kernel(oh1_ref, oh2_ref, dest_ref, blk_ref, cnt_ref, start_ref, *, tm, bm, nblk_pad):
    phase = pl.program_id(0)
    i = pl.program_id(1)
    oh1 = oh1_ref[...]
    oh2 = oh2_ref[...]
    both = oh1 + oh2
    lane = lax.broadcasted_iota(I32, (SUBLANES, LANES), 1)

    @pl.when((phase == 0) & (i == 0))
    def _():
        cnt_ref[...] = jnp.zeros_like(cnt_ref)

    @pl.when(phase == 0)
    def _():
        ones = jnp.ones((SUBLANES, tm), BF16)
        cnt_ref[...] += jnp.dot(ones, both, preferred_element_type=F32)

    @pl.when((phase == 1) & (i == 0))
    def _():
        cnt = cnt_ref[...]
        padded = jnp.floor((cnt + (bm - 1)) * (1.0 / bm)) * bm
        r = lax.broadcasted_iota(I32, (LANES, LANES), 0)
        c = lax.broadcasted_iota(I32, (LANES, LANES), 1)
        upper = (r < c).astype(F32)
        starts = jnp.dot(padded, upper, preferred_element_type=F32,
                         precision=lax.Precision.HIGHEST)
        start_ref[...] = starts
        cnt_ref[...] = jnp.zeros_like(cnt_ref)
        ends = starts + padded
        is_exp = (lane[0:1] >= EXPERT_LANE0) & (lane[0:1] < EXPERT_LANE0 + N_EXPERTS)
        bstart = (lax.broadcasted_iota(I32, (nblk_pad, LANES), 0) * bm).astype(F32)
        done = jnp.where(is_exp & (ends[0:1] <= bstart), 1.0, 0.0)
        bexp = jnp.minimum(jnp.sum(done, axis=1, keepdims=True), N_EXPERTS - 1.0)
        total = jnp.sum(jnp.where(is_exp, padded[0:1], 0.0), axis=1, keepdims=True)
        blane = lax.broadcasted_iota(I32, (nblk_pad, LANES), 1)
        blk = jnp.where(blane == 0, bexp, jnp.where(blane == 1, total * (1.0 / bm), 0.0))
        blk_ref[...] = blk.astype(I32)

    @pl.when(phase == 1)
    def _():
        r = lax.broadcasted_iota(I32, (tm, tm), 0)
        c = lax.broadcasted_iota(I32, (tm, tm), 1)
        lower = (c < r).astype(BF16)
        before = jnp.dot(lower, both, preferred_element_type=F32) + cnt_ref[0:1]
        pos = before + start_ref[0:1]
        d1 = jnp.sum(pos * oh1.astype(F32), axis=1, keepdims=True)
        d2 = jnp.sum(pos * oh2.astype(F32), axis=1, keepdims=True)
        dlane = lax.broadcasted_iota(I32, (tm, LANES), 1)
        dest = jnp.where(dlane == 0, d1, jnp.where(dlane == 1, d2, 0.0))
        dest_ref[...] = dest.astype(I32)
        ones = jnp.ones((SUBLANES, tm), BF16)
        cnt_ref[...] += jnp.dot(ones, both, preferred_element_type=F32)


def _plan(oh1, oh2, bm, nblk, tm=512):
    n = oh1.shape[0]
    tm = min(tm, n)
    nblk_pad = -(-nblk // SUBLANES) * SUBLANES
    dest, blk = pl.pallas_call(
        functools.partial(_plan_kernel, tm=tm, bm=bm, nblk_pad=nblk_pad),
        grid=(2, n // tm),
        in_specs=[pl.BlockSpec((tm, LANES), lambda p, i: (i, 0)),
                  pl.BlockSpec((tm, LANES), lambda p, i: (i, 0))],
        out_specs=[pl.BlockSpec((tm, LANES), lambda p, i: (i * p, 0)),
                   pl.BlockSpec((nblk_pad, LANES), lambda p, i: (0, 0))],
        out_shape=[jax.ShapeDtypeStruct((n, LANES), I32),
                   jax.ShapeDtypeStruct((nblk_pad, LANES), I32)],
        scratch_shapes=[pltpu.VMEM((SUBLANES, LANES), F32),
                        pltpu.VMEM((SUBLANES, LANES), F32)],
        compiler_params=_cparams("arbitrary", "arbitrary"),
        name="plan",
    )(oh1, oh2)
    return dest, blk


ROW_UNROLL = 8


def _invert_kernel(d1_ref, d2_ref, zero_ref, rt_ref, sem, *, tb):
    i = pl.program_id(0)

    @pl.when(i == 0)
    def _():
        clear = pltpu.make_async_copy(zero_ref, rt_ref, sem)
        clear.start()
        clear.wait()

    def put(j, carry):
        for u in range(ROW_UNROLL):
            t = j * ROW_UNROLL + u
            rt_ref[d1_ref[0, 0, t]] = i * tb + t
            rt_ref[d2_ref[0, 0, t]] = i * tb + t
        return carry

    lax.fori_loop(0, tb // ROW_UNROLL, put, 0)


def _invert(dest1, dest2, n_rows, tb=2048):
    n = dest1.shape[0]
    tb = min(tb, n)
    assert n % tb == 0 and tb % ROW_UNROLL == 0
    blocks = lambda i: (i, 0, 0)
    return pl.pallas_call(
        functools.partial(_invert_kernel, tb=tb),
        grid=(n // tb,),
        in_specs=[pl.BlockSpec((1, 1, tb), blocks, memory_space=pltpu.SMEM),
                  pl.BlockSpec((1, 1, tb), blocks, memory_space=pltpu.SMEM),
                  pl.BlockSpec(memory_space=pl.ANY)],
        out_specs=pl.BlockSpec(memory_space=pltpu.SMEM),
        out_shape=jax.ShapeDtypeStruct((n_rows,), I32),
        scratch_shapes=[pltpu.SemaphoreType.DMA(())],
        compiler_params=_cparams("arbitrary"),
        name="invert",
    )(dest1.reshape(n // tb, 1, tb), dest2.reshape(n // tb, 1, tb), jnp.zeros((n_rows,), I32))


def _slab(idx, chunks):
    return pl.ds(pl.multiple_of(idx * chunks, chunks), chunks)


def _rows_from_slabs(buf_ref, slot, rows, chunks):
    return jnp.concatenate(
        [buf_ref[slot, pl.ds(c, rows, stride=chunks), :] for c in range(chunks)], axis=1)


GATHER_AHEAD = 2


def _experts_kernel(be_ref, na_ref, first_ref, wslot_ref, nxt_ref,
                    rt0_ref, rt1_ref, rtn_ref, h_ref, wg_hbm, wu_hbm, wd_hbm, o_ref,
                    xbuf, sem, wgf, wuf, wdf, wsem, wgb, wub, wdb, *, bm, chunks):
    i = pl.program_id(0)
    n_active = na_ref[0]

    def gather(tok_ref, slot):
        def trip(r8, carry):
            for u in range(ROW_UNROLL):
                r = r8 * ROW_UNROLL + u
                pltpu.make_async_copy(h_ref.at[_slab(tok_ref[0, 0, r], chunks), :],
                                      xbuf.at[slot, _slab(r, chunks), :], sem.at[slot]).start()
            return carry
        lax.fori_loop(0, bm // ROW_UNROLL, trip, 0)

    def weight_copies(e, ws):
        return (pltpu.make_async_copy(wg_hbm.at[e], wgf.at[ws], wsem.at[ws]),
                pltpu.make_async_copy(wu_hbm.at[e], wuf.at[ws], wsem.at[ws]),
                pltpu.make_async_copy(wd_hbm.at[e], wdf.at[ws], wsem.at[ws]))

    def fetch_weights(e, ws):
        for cp in weight_copies(e, ws):
            cp.start(priority=1)

    def gather_wait(slot):
        pltpu.make_async_copy(h_ref.at[pl.ds(0, bm * chunks), :], xbuf.at[slot],
                              sem.at[slot]).wait()

    @pl.when(i == 0)
    def _():
        gather(rt0_ref, 0)
        fetch_weights(be_ref[0], wslot_ref[0])

        @pl.when(1 < n_active)
        def _():
            gather(rt1_ref, 1)

    @pl.when(i < n_active)
    def _():
        slot = lax.rem(i, GATHER_AHEAD + 1)
        gather_wait(slot)

        @pl.when(i + GATHER_AHEAD < n_active)
        def _():
            gather(rtn_ref, lax.rem(i + GATHER_AHEAD, GATHER_AHEAD + 1))

        @pl.when(first_ref[i] == 1)
        def _():
            ws = wslot_ref[i]
            for cp in weight_copies(be_ref[i], ws):
                cp.wait()

            @pl.when(nxt_ref[i] >= 0)
            def _():
                fetch_weights(nxt_ref[i], 1 - ws)

            wgb[...] = wgf[ws].astype(BF16)
            wub[...] = wuf[ws].astype(BF16)
            wdb[...] = wdf[ws].astype(BF16)

        x = _rows_from_slabs(xbuf, slot, bm, chunks).astype(BF16)
        g = jnp.dot(x, wgb[...], preferred_element_type=F32)
        u = jnp.dot(x, wub[...], preferred_element_type=F32)
        hb = (g * _sigmoid(g) * u).astype(BF16)
        y = jnp.dot(hb, wdb[...], preferred_element_type=F32)
        for c in range(chunks):
            o_ref[pl.ds(c, bm, stride=chunks), :] = y[:, c * LANES:(c + 1) * LANES]

    @pl.when(i >= n_active)
    def _():
        o_ref[...] = jnp.zeros_like(o_ref)


def _experts(block_expert, n_active, row_tok, h2s, w_gate, w_up, w_down, bm):
    n_rows = row_tok.shape[0]
    nblk = n_rows // bm
    d, de = w_gate.shape[1], w_gate.shape[2]
    chunks = d // LANES
    assert bm % ROW_UNROLL == 0
    assert GATHER_AHEAD == 2 and nblk > GATHER_AHEAD
    row_blocks = row_tok.reshape(nblk, 1, bm)

    idx = jnp.arange(nblk, dtype=I32)
    prev = jnp.concatenate([block_expert[:1] - 1, block_expert[:-1]])
    first = ((block_expert != prev) & (idx < n_active[0])).astype(I32)
    wslot = ((jnp.cumsum(first) - 1) % 2).astype(I32)
    run_start = lax.cummin(jnp.where(first > 0, idx, nblk), axis=0, reverse=True)
    next_start = jnp.concatenate([run_start[1:], jnp.full((1,), nblk, I32)])
    nxt = jnp.where(next_start < nblk,
                    block_expert[jnp.minimum(next_start, nblk - 1)], -1).astype(I32)

    return pl.pallas_call(
        functools.partial(_experts_kernel, bm=bm, chunks=chunks),
        grid_spec=pltpu.PrefetchScalarGridSpec(
            num_scalar_prefetch=5,
            grid=(nblk,),
            in_specs=[pl.BlockSpec((1, 1, bm), lambda i, *_: (0, 0, 0),
                                   memory_space=pltpu.SMEM),
                      pl.BlockSpec((1, 1, bm), lambda i, *_: (1, 0, 0),
                                   memory_space=pltpu.SMEM),
                      pl.BlockSpec((1, 1, bm),
                                   lambda i, *_: (jnp.minimum(i + GATHER_AHEAD, nblk - 1), 0, 0),
                                   memory_space=pltpu.SMEM)]
                     + [pl.BlockSpec(memory_space=pl.ANY)] * 4,
            out_specs=pl.BlockSpec((bm * chunks, LANES), lambda i, *_: (i, 0)),
            scratch_shapes=[pltpu.VMEM((GATHER_AHEAD + 1, bm * chunks, LANES), F32),
                            pltpu.SemaphoreType.DMA((GATHER_AHEAD + 1,)),
                            pltpu.VMEM((2, d, de), F32),
                            pltpu.VMEM((2, d, de), F32),
                            pltpu.VMEM((2, de, d), F32),
                            pltpu.SemaphoreType.DMA((2,)),
                            pltpu.VMEM((d, de), BF16),
                            pltpu.VMEM((d, de), BF16),
                            pltpu.VMEM((de, d), BF16)],
        ),
        out_shape=jax.ShapeDtypeStruct((n_rows * chunks, LANES), F32),
        compiler_params=_cparams("arbitrary"),
        name="experts",
    )(block_expert, n_active, first, wslot, nxt, row_blocks, row_blocks, row_blocks,
      h2s, w_gate, w_up, w_down)


def _combine_kernel(d1_ref, d2_ref, ys_ref, x1_ref, route_ref, g2_ref, lg_ref, lb_ref,
                    o_ref, y1buf, y2buf, sem, *, tm, chunks, alpha):
    i = pl.program_id(0)
    n_tiles = pl.num_programs(0)

    def row_copies(tile, t, slot):
        tok = tile * tm + t
        return (pltpu.make_async_copy(ys_ref.at[_slab(d1_ref[tok], chunks), :],
                                      y1buf.at[slot, _slab(t, chunks), :], sem.at[slot]),
                pltpu.make_async_copy(ys_ref.at[_slab(d2_ref[tok], chunks), :],
                                      y2buf.at[slot, _slab(t, chunks), :], sem.at[slot]))

    def gather(tile, slot):
        def trip(t8, carry):
            for u in range(ROW_UNROLL):
                for k, cp in enumerate(row_copies(tile, t8 * ROW_UNROLL + u, slot)):
                    cp.start(priority=k)
            return carry
        lax.fori_loop(0, tm // ROW_UNROLL, trip, 0)

    def gather_wait(slot):
        for buf in (y1buf, y2buf):
            pltpu.make_async_copy(ys_ref.at[pl.ds(0, tm * chunks), :], buf.at[slot],
                                  sem.at[slot]).wait()

    @pl.when(i == 0)
    def _():
        gather(0, 0)

    slot = lax.rem(i, 2)
    gather_wait(slot)

    @pl.when(i + 1 < n_tiles)
    def _():
        gather(i + 1, 1 - slot)

    route = route_ref[...]
    y = (route[:, 0:1] * _rows_from_slabs(y1buf, slot, tm, chunks)
         + route[:, 1:2] * _rows_from_slabs(y2buf, slot, tm, chunks))
    r = alpha * x1_ref[...] + g2_ref[0] * y
    o_ref[...] = _ln_rows(r) * lg_ref[...] + lb_ref[...]


def _combine(dest1, dest2, ys, x1, route, mod3, ln2_g, ln2_b, seq, alpha, tm=256):
    n, d = x1.shape
    tm = min(tm, seq)
    chunks = d // LANES
    assert tm % ROW_UNROLL == 0
    tok = lambda i, d1, d2: (i, 0)
    const = lambda i, d1, d2: (0, 0)
    return pl.pallas_call(
        functools.partial(_combine_kernel, tm=tm, chunks=chunks, alpha=alpha),
        grid_spec=pltpu.PrefetchScalarGridSpec(
            num_scalar_prefetch=2,
            grid=(n // tm,),
            in_specs=[pl.BlockSpec(memory_space=pl.ANY),
                      pl.BlockSpec((tm, d), tok),
                      pl.BlockSpec((tm, LANES), tok),
                      pl.BlockSpec((1, 1, d), lambda i, d1, d2: (((i * tm) // seq) * 6 + 5, 0, 0)),
                      pl.BlockSpec((1, d), const),
                      pl.BlockSpec((1, d), const)],
            out_specs=pl.BlockSpec((tm, d), tok),
            scratch_shapes=[pltpu.VMEM((2, tm * chunks, LANES), F32),
                            pltpu.VMEM((2, tm * chunks, LANES), F32),
                            pltpu.SemaphoreType.DMA((2,))],
        ),
        out_shape=jax.ShapeDtypeStruct((n, d), F32),
        compiler_params=_cparams("arbitrary"),
        name="combine",
    )(dest1, dest2, ys, x1, route, mod3, ln2_g, ln2_b)


EXPERT_BLOCK_ROWS = 256


def kernel(x, c, w_ada, b_ada, w_in, sgu_w, sgu_b, sgu_ln_g, sgu_ln_b, w_proj_a, w_proj_b,
           w_out, ln1_g, ln1_b, w_group, b_group, w_router, b_router, w_gate, w_up, w_down,
           ln2_g, ln2_b):
    b, s, d = x.shape
    n = b * s
    depth = w_ada.shape[0]
    alpha = (2.0 * depth) ** 0.25
    sub = d // LANES
    bm = EXPERT_BLOCK_ROWS
    n_assign = 2 * n
    n_rows = n_assign + N_EXPERTS * bm
    nblk = n_rows // bm

    in_cols = w_in.shape[2]
    colscale = jnp.ones((1, in_cols), F32).at[:, _COL_Q:_COL_K].set(
        SB_HEAD_DIM ** -0.5 * LOG2E)

    for l in range(depth):
        mod = _ada(c, w_ada[l], b_ada[l])
        mod3 = mod.reshape(b * 6, 1, d)

        x2 = x.reshape(n, d)
        proj = _inproj(x2, mod3, w_in[l].astype(BF16), colscale, s)
        out_a = _attention(proj.reshape(b, s, in_cols), d).reshape(n, SB_WIDTH)

        wr = jnp.zeros((d, LANES), F32)
        wr = wr.at[:, GROUP_LANE0:GROUP_LANE0 + N_GROUPS].set(w_group[l])
        wr = wr.at[:, EXPERT_LANE0:EXPERT_LANE0 + N_EXPERTS].set(w_router[l])
        wr_hi = wr.astype(BF16)
        wr_lo = (wr - wr_hi.astype(F32)).astype(BF16)
        wr = jnp.concatenate([wr_hi, wr_lo], axis=1)
        br = jnp.zeros((1, LANES), F32)
        br = br.at[0, GROUP_LANE0:GROUP_LANE0 + N_GROUPS].set(b_group[l])
        br = br.at[0, EXPERT_LANE0:EXPERT_LANE0 + N_EXPERTS].set(b_router[l])
        sgu_bias_full = jnp.repeat(sgu_b[l].T, SGU_GROUP_DIM, axis=1)

        x1, h2s, route, oh1, oh2 = _mix(
            out_a, proj, x2, mod3, sgu_w[l], sgu_bias_full,
            sgu_ln_g[l].reshape(1, -1), sgu_ln_b[l].reshape(1, -1),
            w_proj_a[l].astype(BF16), w_proj_b[l].astype(BF16), w_out[l].astype(BF16),
            ln1_g[l].reshape(1, d), ln1_b[l].reshape(1, d), wr, br, s, alpha)

        dest, blk = _plan(oh1, oh2, bm, nblk)
        dest1 = dest[:, 0]
        dest2 = dest[:, 1]
        block_expert = blk[:nblk, 0]
        n_active = blk[0:1, 1]

        row_tok = _invert(dest1, dest2, n_rows)
        ys = _experts(block_expert, n_active, row_tok, h2s, w_gate[l], w_up[l], w_down[l], bm)
        out = _combine(dest1, dest2, ys, x1, route, mod3, ln2_g[l].reshape(1, d),
                       ln2_b[l].reshape(1, d), s, alpha)
        x = out.reshape(b, s, d)
    return x
```

```python
import functools
import math

import jax
import jax.numpy as jnp
from jax import lax
from jax.experimental import pallas as pl
from jax.experimental.pallas import tpu as pltpu

F32 = jnp.float32
BF16 = jnp.bfloat16
I32 = jnp.int32

CHUNK = 64
SB_HEADS = 8
SB_HEAD_DIM = 128
SB_WIDTH = SB_HEADS * SB_HEAD_DIM
SGU_GROUPS = 8
SGU_GROUP_DIM = 128
SGU_WIDTH = SGU_GROUPS * SGU_GROUP_DIM
SGU_BLOCK = 128
N_GROUPS = 4
EXPERTS_PER_GROUP = 8
N_EXPERTS = N_GROUPS * EXPERTS_PER_GROUP
D_EXPERT = 512
LN_EPS = 1e-5

LANES = 128
SUBLANES = 8
VMEM_LIMIT_BYTES = 60000 * 1024

GROUP_LANE0 = 0
EXPERT_LANE0 = N_GROUPS

LOG2E = 1.4426950408889634
NEG_BIG = -1e30

_COL_Q = 0
_COL_K = SB_WIDTH
_COL_V = 2 * SB_WIDTH
_COL_SU = 3 * SB_WIDTH
_COL_SV = 3 * SB_WIDTH + SGU_WIDTH
_COL_GA = 3 * SB_WIDTH + 2 * SGU_WIDTH


def _COL_GB(d):
    return _COL_GA + d


def _cparams(*sem):
    return pltpu.CompilerParams(dimension_semantics=sem, vmem_limit_bytes=VMEM_LIMIT_BYTES)


def _ln_rows(x):
    mu = jnp.mean(x, axis=-1, keepdims=True)
    xc = x - mu
    var = jnp.mean(xc * xc, axis=-1, keepdims=True)
    return xc * lax.rsqrt(var + LN_EPS)


def _sigmoid(x):
    return 1.0 / (1.0 + jnp.exp(-x))


def _ada_kernel(c_ref, w_ref, b_ref, o_ref):
    c = c_ref[...]
    ca = c * _sigmoid(c)
    o_ref[...] = jnp.dot(ca, w_ref[...], preferred_element_type=F32,
                         precision=lax.Precision.HIGHEST) + b_ref[...]


def _ada(c, w_ada, b_ada, tn=1024):
    b, d = c.shape
    n_out = w_ada.shape[1]
    cp = jnp.zeros((SUBLANES, d), F32).at[:b].set(c)
    out = pl.pallas_call(
        _ada_kernel,
        grid=(n_out // tn,),
        in_specs=[pl.BlockSpec((SUBLANES, d), lambda j: (0, 0)),
                  pl.BlockSpec((d, tn), lambda j: (0, j)),
                  pl.BlockSpec((1, tn), lambda j: (0, j))],
        out_specs=pl.BlockSpec((SUBLANES, tn), lambda j: (0, j)),
        out_shape=jax.ShapeDtypeStruct((SUBLANES, n_out), F32),
        compiler_params=_cparams("arbitrary"),
        name="ada",
    )(cp, w_ada, b_ada.reshape(1, n_out))
    return out[:b]


def _inproj_kernel(x_ref, sh_ref, sc_ref, w_ref, cs_ref, o_ref, h_ref):
    @pl.when(pl.program_id(1) == 0)
    def _():
        h = _ln_rows(x_ref[...]) * (1.0 + sc_ref[0]) + sh_ref[0]
        h_ref[...] = h.astype(BF16)

    acc = jnp.dot(h_ref[...], w_ref[...], preferred_element_type=F32)
    o_ref[...] = (acc * cs_ref[...]).astype(BF16)


def _inproj(x2, mod3, w_in_bf, colscale, seq, tm=1024, tn=1024):
    n, d = x2.shape
    n_cols = w_in_bf.shape[1]
    tm = min(tm, seq)
    return pl.pallas_call(
        _inproj_kernel,
        grid=(n // tm, n_cols // tn),
        in_specs=[pl.BlockSpec((tm, d), lambda i, j: (i, 0)),
                  pl.BlockSpec((1, 1, d), lambda i, j: (((i * tm) // seq) * 6 + 0, 0, 0)),
                  pl.BlockSpec((1, 1, d), lambda i, j: (((i * tm) // seq) * 6 + 1, 0, 0)),
                  pl.BlockSpec((d, tn), lambda i, j: (0, j)),
                  pl.BlockSpec((1, tn), lambda i, j: (0, j))],
        out_specs=pl.BlockSpec((tm, tn), lambda i, j: (i, j)),
        out_shape=jax.ShapeDtypeStruct((n, n_cols), BF16),
        scratch_shapes=[pltpu.VMEM((tm, d), BF16)],
        compiler_params=_cparams("arbitrary", "arbitrary"),
        name="inproj",
    )(x2, mod3, mod3, w_in_bf, colscale)


Z2_MAX = 100.0
STICK_EXIT_LOG2 = 150.0


def _attn_kernel(q_ref, k_ref, v_ref, o_ref, *, tq, tk, heads):
    qi = pl.program_id(2)
    nd = tq // tk
    jd = qi * nd
    trow = lax.broadcasted_iota(I32, (tk, tk), 0)
    tcol = lax.broadcasted_iota(I32, (tk, tk), 1)
    tri = (trow > tcol).astype(BF16)
    hd = SB_HEAD_DIM
    qs = [q_ref[0, :, h * hd:(h + 1) * hd] for h in range(heads)]

    def block(j, runs, accs, causal):
        off = pl.multiple_of(j * tk, tk)
        hs = range(heads)
        z2s = [lax.dot_general(qs[h], k_ref[0, pl.ds(off, tk), h * hd:(h + 1) * hd],
                               (((1,), (1,)), ((), ())), preferred_element_type=F32) for h in hs]
        z2s = [lax.clamp(-Z2_MAX, z2, Z2_MAX) for z2 in z2s]
        sps = [jnp.log(1.0 + jnp.exp2(z2)) for z2 in z2s]
        if causal is not None:
            sps = [jnp.where(causal, sp, 0.0) for sp in sps]
        laters = [jnp.dot(sp.astype(BF16), tri, preferred_element_type=F32) for sp in sps]
        ws = [jnp.exp2(z2s[h] - LOG2E * (sps[h] + (laters[h] + runs[h]))) for h in hs]
        if causal is not None:
            ws = [jnp.where(causal, w, 0.0) for w in ws]
        new_accs = [accs[h] + jnp.dot(ws[h].astype(BF16),
                                      v_ref[0, pl.ds(off, tk), h * hd:(h + 1) * hd],
                                      preferred_element_type=F32) for h in hs]
        new_runs = [runs[h] + jnp.sum(sps[h], axis=1, keepdims=True) for h in hs]
        return tuple(new_runs), tuple(new_accs)

    def mass_left(runs):
        low = runs[0]
        for r in runs[1:]:
            low = jnp.minimum(low, r)
        return (jnp.min(low) * LOG2E < STICK_EXIT_LOG2).astype(I32)

    def diag_body(t, carry):
        c = nd - 1 - t
        row = lax.broadcasted_iota(I32, (tq, tk), 0)
        col = lax.broadcasted_iota(I32, (tq, tk), 1)
        causal = col + c * tk < row
        return block(jd + c, carry[0], carry[1], causal)

    def more(carry):
        t, go, _, _ = carry
        return (t < jd) & (go > 0)

    def body(carry):
        t, _, runs, accs = carry
        runs, accs = block(jd - 1 - t, runs, accs, None)
        return t + 1, mass_left(runs), runs, accs

    runs = tuple(jnp.zeros((tq, 1), F32) for _ in range(heads))
    accs = tuple(jnp.zeros((tq, hd), F32) for _ in range(heads))
    runs, accs = lax.fori_loop(0, nd, diag_body, (runs, accs))
    _, _, runs, accs = lax.while_loop(more, body, (jnp.int32(0), mass_left(runs), runs, accs))
    for h in range(heads):
        o_ref[0, :, h * hd:(h + 1) * hd] = accs[h].astype(BF16)


def _attention(proj3, d, tq=256, tk=256, heads=4):
    b, s, _ = proj3.shape
    tk = min(tk, s)
    tq = min(tq, s)
    assert tq % tk == 0 and s % tq == 0
    width = heads * SB_HEAD_DIM
    qc = _COL_Q // width
    kc = _COL_K // width
    vc = _COL_V // width
    return pl.pallas_call(
        functools.partial(_attn_kernel, tq=tq, tk=tk, heads=heads),
        grid=(b, SB_HEADS // heads, s // tq),
        in_specs=[pl.BlockSpec((1, tq, width), lambda bi, g, qi: (bi, qi, qc + g)),
                  pl.BlockSpec((1, s, width), lambda bi, g, qi: (bi, 0, kc + g)),
                  pl.BlockSpec((1, s, width), lambda bi, g, qi: (bi, 0, vc + g))],
        out_specs=pl.BlockSpec((1, tq, width), lambda bi, g, qi: (bi, qi, g)),
        out_shape=jax.ShapeDtypeStruct((b, s, SB_WIDTH), BF16),
        compiler_params=_cparams("arbitrary", "arbitrary", "arbitrary"),
        name="attn",
    )(proj3, proj3, proj3)


def _erf(x):
    return lax.erf(x)


def _gelu(x):
    return 0.5 * x * (1.0 + _erf(x * (1.0 / math.sqrt(2.0))))


def _mix_kernel(oa_ref, su_ref, sv_ref, ga0_ref, ga1_ref, gb0_ref, gb1_ref, x_ref,
                sw_ref, sbias_ref, sg_ref, sb_ref,
                wpa_ref, wpb_ref, wo_ref,
                g1_ref, ln1g_ref, ln1b_ref, sh2_ref, sc2_ref,
                wr_ref, br_ref,
                x1_ref, h2_ref, route_ref, oh1_ref, oh2_ref, *, tm, alpha):
    u = _gelu(su_ref[...].astype(F32))
    v = _gelu(sv_ref[...].astype(F32))
    v = _ln_rows(v) * sg_ref[...] + sb_ref[...]
    vb = v.astype(BF16)
    pos_t = lax.broadcasted_iota(I32, (SGU_BLOCK, SGU_BLOCK), 0)
    pos_s = lax.broadcasted_iota(I32, (SGU_BLOCK, SGU_BLOCK), 1)
    chunk_causal = (pos_s // CHUNK) <= (pos_t // CHUNK)
    n_pos_blocks = tm // SGU_BLOCK
    per_group = []
    for g in range(SGU_GROUPS):
        c0 = g * SGU_GROUP_DIM
        wg = jnp.where(chunk_causal, sw_ref[g], 0.0).astype(BF16)
        rhs = jnp.concatenate(
            [vb[nb * SGU_BLOCK:(nb + 1) * SGU_BLOCK, c0:c0 + SGU_GROUP_DIM]
             for nb in range(n_pos_blocks)], axis=1)
        per_group.append(jnp.dot(wg, rhs, preferred_element_type=F32))
    mixed = jnp.concatenate(
        [jnp.concatenate([per_group[g][:, nb * SGU_GROUP_DIM:(nb + 1) * SGU_GROUP_DIM]
                          for g in range(SGU_GROUPS)], axis=1) + sbias_ref[...]
         for nb in range(n_pos_blocks)], axis=0)
    out_b = (u * mixed).astype(BF16)

    pa = jnp.dot(oa_ref[...], wpa_ref[...], preferred_element_type=F32)
    pb = jnp.dot(out_b, wpb_ref[...], preferred_element_type=F32)
    ga = jnp.concatenate([ga0_ref[...], ga1_ref[...]], axis=1).astype(F32)
    gb = jnp.concatenate([gb0_ref[...], gb1_ref[...]], axis=1).astype(F32)
    merged = _sigmoid(ga) * pa + _sigmoid(gb) * pb
    y = jnp.dot(merged.astype(BF16), wo_ref[...], preferred_element_type=F32)

    r = alpha * x_ref[...] + g1_ref[0] * y
    x1 = _ln_rows(r) * ln1g_ref[...] + ln1b_ref[...]
    x1_ref[...] = x1
    h2 = _ln_rows(x1) * (1.0 + sc2_ref[0]) + sh2_ref[0]
    chunks = h2.shape[1] // LANES
    for c in range(chunks):
        h2_ref[pl.ds(c, tm, stride=chunks), :] = h2[:, c * LANES:(c + 1) * LANES]

    h_hi = h2.astype(BF16)
    h_lo = (h2 - h_hi.astype(F32)).astype(BF16)
    hi_w = jnp.dot(h_hi, wr_ref[...], preferred_element_type=F32)
    lo_w = jnp.dot(h_lo, wr_ref[:, :LANES], preferred_element_type=F32)
    logits = (hi_w[:, :LANES] + hi_w[:, LANES:]) + lo_w + br_ref[...]
    lane = lax.broadcasted_iota(I32, logits.shape, 1)
    gl = jnp.where(lane < N_GROUPS, logits, NEG_BIG)
    gmax = jnp.max(gl, axis=1, keepdims=True)
    g_sel = jnp.min(jnp.where(gl == gmax, lane, LANES), axis=1, keepdims=True)
    p_group = 1.0 / jnp.sum(jnp.exp(gl - gmax), axis=1, keepdims=True)
    lo_lane = EXPERT_LANE0 + g_sel * EXPERTS_PER_GROUP
    in_group = (lane >= lo_lane) & (lane < lo_lane + EXPERTS_PER_GROUP)
    el = jnp.where(in_group, logits, NEG_BIG)
    m1 = jnp.max(el, axis=1, keepdims=True)
    i1 = jnp.min(jnp.where(el == m1, lane, LANES), axis=1, keepdims=True)
    el2 = jnp.where(lane == i1, NEG_BIG, el)
    m2 = jnp.max(el2, axis=1, keepdims=True)
    i2 = jnp.min(jnp.where(el2 == m2, lane, LANES), axis=1, keepdims=True)
    e21 = jnp.exp(m2 - m1)
    w1 = p_group / (1.0 + e21)
    w2 = p_group * e21 / (1.0 + e21)
    route = jnp.where(lane == 0, w1, 0.0)
    route = jnp.where(lane == 1, w2, route)
    route = jnp.where(lane == 2, (i1 - EXPERT_LANE0).astype(F32), route)
    route = jnp.where(lane == 3, (i2 - EXPERT_LANE0).astype(F32), route)
    route_ref[...] = route
    oh1_ref[...] = jnp.where(lane == i1, 1.0, 0.0).astype(BF16)
    oh2_ref[...] = jnp.where(lane == i2, 1.0, 0.0).astype(BF16)


def _mix(out_a, proj, x2, mod3, sgu_w, sgu_bias_full, sgu_g, sgu_b, wpa, wpb, wo,
         ln1_g, ln1_b, wr, br, seq, alpha, tm=256):
    n, d = x2.shape
    tm = min(tm, seq)
    def pcol(start, width):
        assert start % width == 0
        return start // width
    su_c = pcol(_COL_SU, SGU_WIDTH)
    sv_c = pcol(_COL_SV, SGU_WIDTH)
    half = d // 2
    ga_c = pcol(_COL_GA, half)
    gb_c = pcol(_COL_GB(d), half)
    bidx = lambda i: (i * tm) // seq
    row = lambda i: (i, 0)
    const2 = lambda i: (0, 0)
    const3 = lambda i: (0, 0, 0)
    return pl.pallas_call(
        functools.partial(_mix_kernel, tm=tm, alpha=alpha),
        grid=(n // tm,),
        in_specs=[pl.BlockSpec((tm, SB_WIDTH), row),
                  pl.BlockSpec((tm, SGU_WIDTH), lambda i: (i, su_c)),
                  pl.BlockSpec((tm, SGU_WIDTH), lambda i: (i, sv_c)),
                  pl.BlockSpec((tm, half), lambda i: (i, ga_c)),
                  pl.BlockSpec((tm, half), lambda i: (i, ga_c + 1)),
                  pl.BlockSpec((tm, half), lambda i: (i, gb_c)),
                  pl.BlockSpec((tm, half), lambda i: (i, gb_c + 1)),
                  pl.BlockSpec((tm, d), row),
                  pl.BlockSpec((SGU_GROUPS, SGU_BLOCK, SGU_BLOCK), const3),
                  pl.BlockSpec((SGU_BLOCK, SGU_WIDTH), const2),
                  pl.BlockSpec((1, SGU_WIDTH), const2),
                  pl.BlockSpec((1, SGU_WIDTH), const2),
                  pl.BlockSpec((SB_WIDTH, d), const2),
                  pl.BlockSpec((SGU_WIDTH, d), const2),
                  pl.BlockSpec((d, d), const2),
                  pl.BlockSpec((1, 1, d), lambda i: (bidx(i) * 6 + 2, 0, 0)),
                  pl.BlockSpec((1, d), const2),
                  pl.BlockSpec((1, d), const2),
                  pl.BlockSpec((1, 1, d), lambda i: (bidx(i) * 6 + 3, 0, 0)),
                  pl.BlockSpec((1, 1, d), lambda i: (bidx(i) * 6 + 4, 0, 0)),
                  pl.BlockSpec((d, 2 * LANES), const2),
                  pl.BlockSpec((1, LANES), const2)],
        out_specs=[pl.BlockSpec((tm, d), row),
                   pl.BlockSpec((tm * (d // LANES), LANES), row),
                   pl.BlockSpec((tm, LANES), row),
                   pl.BlockSpec((tm, LANES), row),
                   pl.BlockSpec((tm, LANES), row)],
        out_shape=[jax.ShapeDtypeStruct((n, d), F32),
                   jax.ShapeDtypeStruct((n * (d // LANES), LANES), F32),
                   jax.ShapeDtypeStruct((n, LANES), F32),
                   jax.ShapeDtypeStruct((n, LANES), BF16),
                   jax.ShapeDtypeStruct((n, LANES), BF16)],
        compiler_params=_cparams("arbitrary"),
        name="mix",
    )(out_a, proj, proj, proj, proj, proj, proj, x2, sgu_w, sgu_bias_full, sgu_g, sgu_b,
      wpa, wpb, wo, mod3, ln1_g, ln1_b, mod3, mod3, wr, br)


def _plan_kernel(oh1_ref, oh2_ref, dest_ref, blk_ref, cnt_ref, start_ref, *, tm, bm, nblk_pad):
    phase = pl.program_id(0)
    i = pl.program_id(1)
    oh1 = oh1_ref[...]
    oh2 = oh2_ref[...]
    both = oh1 + oh2
    lane = lax.broadcasted_iota(I32, (SUBLANES, LANES), 1)

    @pl.when((phase == 0) & (i == 0))
    def _():
        cnt_ref[...] = jnp.zeros_like(cnt_ref)

    @pl.when(phase == 0)
    def _():
        ones = jnp.ones((SUBLANES, tm), BF16)
        cnt_ref[...] += jnp.dot(ones, both, preferred_element_type=F32)

    @pl.when((phase == 1) & (i == 0))
    def _():
        cnt = cnt_ref[...]
        padded = jnp.floor((cnt + (bm - 1)) * (1.0 / bm)) * bm
        r = lax.broadcasted_iota(I32, (LANES, LANES), 0)
        c = lax.broadcasted_iota(I32, (LANES, LANES), 1)
        upper = (r < c).astype(F32)
        starts = jnp.dot(padded, upper, preferred_element_type=F32,
                         precision=lax.Precision.HIGHEST)
        start_ref[...] = starts
        cnt_ref[...] = jnp.zeros_like(cnt_ref)
        ends = starts + padded
        is_exp = (lane[0:1] >= EXPERT_LANE0) & (lane[0:1] < EXPERT_LANE0 + N_EXPERTS)
        bstart = (lax.broadcasted_iota(I32, (nblk_pad, LANES), 0) * bm).astype(F32)
        done = jnp.where(is_exp & (ends[0:1] <= bstart), 1.0, 0.0)
        bexp = jnp.minimum(jnp.sum(done, axis=1, keepdims=True), N_EXPERTS - 1.0)
        total = jnp.sum(jnp.where(is_exp, padded[0:1], 0.0), axis=1, keepdims=True)
        blane = lax.broadcasted_iota(I32, (nblk_pad, LANES), 1)
        blk = jnp.where(blane == 0, bexp, jnp.where(blane == 1, total * (1.0 / bm), 0.0))
        blk_ref[...] = blk.astype(I32)

    @pl.when(phase == 1)
    def _():
        r = lax.broadcasted_iota(I32, (tm, tm), 0)
        c = lax.broadcasted_iota(I32, (tm, tm), 1)
        lower = (c < r).astype(BF16)
        before = jnp.dot(lower, both, preferred_element_type=F32) + cnt_ref[0:1]
        pos = before + start_ref[0:1]
        d1 = jnp.sum(pos * oh1.astype(F32), axis=1, keepdims=True)
        d2 = jnp.sum(pos * oh2.astype(F32), axis=1, keepdims=True)
        dlane = lax.broadcasted_iota(I32, (tm, LANES), 1)
        dest = jnp.where(dlane == 0, d1, jnp.where(dlane == 1, d2, 0.0))
        dest_ref[...] = dest.astype(I32)
        ones = jnp.ones((SUBLANES, tm), BF16)
        cnt_ref[...] += jnp.dot(ones, both, preferred_element_type=F32)


def _plan(oh1, oh2, bm, nblk, tm=512):
    n = oh1.shape[0]
    tm = min(tm, n)
    nblk_pad = -(-nblk // SUBLANES) * SUBLANES
    dest, blk = pl.pallas_call(
        functools.partial(_plan_kernel, tm=tm, bm=bm, nblk_pad=nblk_pad),
        grid=(2, n // tm),
        in_specs=[pl.BlockSpec((tm, LANES), lambda p, i: (i, 0)),
                  pl.BlockSpec((tm, LANES), lambda p, i: (i, 0))],
        out_specs=[pl.BlockSpec((tm, LANES), lambda p, i: (i * p, 0)),
                   pl.BlockSpec((nblk_pad, LANES), lambda p, i: (0, 0))],
        out_shape=[jax.ShapeDtypeStruct((n, LANES), I32),
                   jax.ShapeDtypeStruct((nblk_pad, LANES), I32)],
        scratch_shapes=[pltpu.VMEM((SUBLANES, LANES), F32),
                        pltpu.VMEM((SUBLANES, LANES), F32)],
        compiler_params=_cparams("arbitrary", "arbitrary"),
        name="plan",
    )(oh1, oh2)
    return dest, blk


ROW_UNROLL = 8


def _invert_kernel(d1_ref, d2_ref, zero_ref, rt_ref, sem, *, tb):
    i = pl.program_id(0)

    @pl.when(i == 0)
    def _():
        clear = pltpu.make_async_copy(zero_ref, rt_ref, sem)
        clear.start()
        clear.wait()

    def put(j, carry):
        for u in range(ROW_UNROLL):
            t = j * ROW_UNROLL + u
            rt_ref[d1_ref[0, 0, t]] = i * tb + t
            rt_ref[d2_ref[0, 0, t]] = i * tb + t
        return carry

    lax.fori_loop(0, tb // ROW_UNROLL, put, 0)


def _invert(dest1, dest2, n_rows, tb=2048):
    n = dest1.shape[0]
    tb = min(tb, n)
    assert n % tb == 0 and tb % ROW_UNROLL == 0
    blocks = lambda i: (i, 0, 0)
    return pl.pallas_call(
        functools.partial(_invert_kernel, tb=tb),
        grid=(n // tb,),
        in_specs=[pl.BlockSpec((1, 1, tb), blocks, memory_space=pltpu.SMEM),
                  pl.BlockSpec((1, 1, tb), blocks, memory_space=pltpu.SMEM),
                  pl.BlockSpec(memory_space=pl.ANY)],
        out_specs=pl.BlockSpec(memory_space=pltpu.SMEM),
        out_shape=jax.ShapeDtypeStruct((n_rows,), I32),
        scratch_shapes=[pltpu.SemaphoreType.DMA(())],
        compiler_params=_cparams("arbitrary"),
        name="invert",
    )(dest1.reshape(n // tb, 1, tb), dest2.reshape(n // tb, 1, tb), jnp.zeros((n_rows,), I32))


def _slab(idx, chunks):
    return pl.ds(pl.multiple_of(idx * chunks, chunks), chunks)


def _rows_from_slabs(buf_ref, slot, rows, chunks):
    return jnp.concatenate(
        [buf_ref[slot, pl.ds(c, rows, stride=chunks), :] for c in range(chunks)], axis=1)


GATHER_AHEAD = 3


def _experts_kernel(be_ref, na_ref, first_ref, wslot_ref, nxt_ref, *refs, bm, chunks):
    head_refs = refs[:GATHER_AHEAD]
    (rtn_ref, h_ref, wg_hbm, wu_hbm, wd_hbm, o_ref,
     xbuf, sem, wgf, wuf, wdf, wsem, wgb, wub, wdb) = refs[GATHER_AHEAD:]
    i = pl.program_id(0)
    n_active = na_ref[0]

    def gather(tok_ref, slot):
        def trip(r8, carry):
            for u in range(ROW_UNROLL):
                r = r8 * ROW_UNROLL + u
                pltpu.make_async_copy(h_ref.at[_slab(tok_ref[0, 0, r], chunks), :],
                                      xbuf.at[slot, _slab(r, chunks), :], sem.at[slot]).start()
            return carry
        lax.fori_loop(0, bm // ROW_UNROLL, trip, 0)

    def weight_copies(e, ws):
        return (pltpu.make_async_copy(wg_hbm.at[e], wgf.at[ws], wsem.at[ws]),
                pltpu.make_async_copy(wu_hbm.at[e], wuf.at[ws], wsem.at[ws]),
                pltpu.make_async_copy(wd_hbm.at[e], wdf.at[ws], wsem.at[ws]))

    def fetch_weights(e, ws):
        for cp in weight_copies(e, ws):
            cp.start(priority=1)

    def gather_wait(slot):
        pltpu.make_async_copy(h_ref.at[pl.ds(0, bm * chunks), :], xbuf.at[slot],
                              sem.at[slot]).wait()

    @pl.when(i == 0)
    def _():
        fetch_weights(be_ref[0], wslot_ref[0])
        gather(head_refs[0], 0)
        for k in range(1, GATHER_AHEAD):
            pl.when(k < n_active)(functools.partial(gather, head_refs[k], k))

    @pl.when(i < n_active)
    def _():
        slot = lax.rem(i, GATHER_AHEAD + 1)
        gather_wait(slot)

        @pl.when(i + GATHER_AHEAD < n_active)
        def _():
            gather(rtn_ref, lax.rem(i + GATHER_AHEAD, GATHER_AHEAD + 1))

        @pl.when(first_ref[i] == 1)
        def _():
            ws = wslot_ref[i]
            for cp in weight_copies(be_ref[i], ws):
                cp.wait()

            @pl.when(nxt_ref[i] >= 0)
            def _():
                fetch_weights(nxt_ref[i], 1 - ws)

            wgb[...] = wgf[ws].astype(BF16)
            wub[...] = wuf[ws].astype(BF16)
            wdb[...] = wdf[ws].astype(BF16)

        x = _rows_from_slabs(xbuf, slot, bm, chunks).astype(BF16)
        g = jnp.dot(x, wgb[...], preferred_element_type=F32)
        u = jnp.dot(x, wub[...], preferred_element_type=F32)
        hb = (g * _sigmoid(g) * u).astype(BF16)
        y = jnp.dot(hb, wdb[...], preferred_element_type=F32)
        for c in range(chunks):
            o_ref[pl.ds(c, bm, stride=chunks), :] = y[:, c * LANES:(c + 1) * LANES]

    @pl.when(i >= n_active)
    def _():
        o_ref[...] = jnp.zeros_like(o_ref)


def _experts(block_expert, n_active, row_tok, h2s, w_gate, w_up, w_down, bm):
    n_rows = row_tok.shape[0]
    nblk = n_rows // bm
    d, de = w_gate.shape[1], w_gate.shape[2]
    chunks = d // LANES
    assert bm % ROW_UNROLL == 0
    assert nblk > GATHER_AHEAD
    row_blocks = row_tok.reshape(nblk, 1, bm)

    idx = jnp.arange(nblk, dtype=I32)
    prev = jnp.concatenate([block_expert[:1] - 1, block_expert[:-1]])
    first = ((block_expert != prev) & (idx < n_active[0])).astype(I32)
    wslot = ((jnp.cumsum(first) - 1) % 2).astype(I32)
    run_start = lax.cummin(jnp.where(first > 0, idx, nblk), axis=0, reverse=True)
    next_start = jnp.concatenate([run_start[1:], jnp.full((1,), nblk, I32)])
    nxt = jnp.where(next_start < nblk,
                    block_expert[jnp.minimum(next_start, nblk - 1)], -1).astype(I32)

    return pl.pallas_call(
        functools.partial(_experts_kernel, bm=bm, chunks=chunks),
        grid_spec=pltpu.PrefetchScalarGridSpec(
            num_scalar_prefetch=5,
            grid=(nblk,),
            in_specs=[pl.BlockSpec((1, 1, bm), functools.partial(lambda k, i, *_: (k, 0, 0), k),
                                   memory_space=pltpu.SMEM) for k in range(GATHER_AHEAD)]
                     + [pl.BlockSpec((1, 1, bm),
                                     lambda i, *_: (jnp.minimum(i + GATHER_AHEAD, nblk - 1), 0, 0),
                                     memory_space=pltpu.SMEM)]
                     + [pl.BlockSpec(memory_space=pl.ANY)] * 4,
            out_specs=pl.BlockSpec((bm * chunks, LANES), lambda i, *_: (i, 0)),
            scratch_shapes=[pltpu.VMEM((GATHER_AHEAD + 1, bm * chunks, LANES), F32),
                            pltpu.SemaphoreType.DMA((GATHER_AHEAD + 1,)),
                            pltpu.VMEM((2, d, de), F32),
                            pltpu.VMEM((2, d, de), F32),
                            pltpu.VMEM((2, de, d), F32),
                            pltpu.SemaphoreType.DMA((2,)),
                            pltpu.VMEM((d, de), BF16),
                            pltpu.VMEM((d, de), BF16),
                            pltpu.VMEM((de, d), BF16)],
        ),
        out_shape=jax.ShapeDtypeStruct((n_rows * chunks, LANES), F32),
        compiler_params=_cparams("arbitrary"),
        name="experts",
    )(block_expert, n_active, first, wslot, nxt, *([row_blocks] * (GATHER_AHEAD + 1)),
      h2s, w_gate, w_up, w_down)


def _combine_kernel(d1_ref, d2_ref, ys_ref, x1_ref, route_ref, g2_ref, lg_ref, lb_ref,
                    o_ref, y1buf, y2buf, sem, *, tm, chunks, alpha):
    i = pl.program_id(0)
    n_tiles = pl.num_programs(0)

    def row_copies(tile, t, slot):
        tok = tile * tm + t
        return (pltpu.make_async_copy(ys_ref.at[_slab(d1_ref[tok], chunks), :],
                                      y1buf.at[slot, _slab(t, chunks), :], sem.at[slot]),
                pltpu.make_async_copy(ys_ref.at[_slab(d2_ref[tok], chunks), :],
                                      y2buf.at[slot, _slab(t, chunks), :], sem.at[slot]))

    def gather(tile, slot):
        def trip(t8, carry):
            for u in range(ROW_UNROLL):
                for k, cp in enumerate(row_copies(tile, t8 * ROW_UNROLL + u, slot)):
                    cp.start(priority=k)
            return carry
        lax.fori_loop(0, tm // ROW_UNROLL, trip, 0)

    def gather_wait(slot):
        for buf in (y1buf, y2buf):
            pltpu.make_async_copy(ys_ref.at[pl.ds(0, tm * chunks), :], buf.at[slot],
                                  sem.at[slot]).wait()

    @pl.when(i == 0)
    def _():
        gather(0, 0)

    slot = lax.rem(i, 2)
    gather_wait(slot)

    @pl.when(i + 1 < n_tiles)
    def _():
        gather(i + 1, 1 - slot)

    route = route_ref[...]
    y = (route[:, 0:1] * _rows_from_slabs(y1buf, slot, tm, chunks)
         + route[:, 1:2] * _rows_from_slabs(y2buf, slot, tm, chunks))
    r = alpha * x1_ref[...] + g2_ref[0] * y
    o_ref[...] = _ln_rows(r) * lg_ref[...] + lb_ref[...]


def _combine(dest1, dest2, ys, x1, route, mod3, ln2_g, ln2_b, seq, alpha, tm=256):
    n, d = x1.shape
    tm = min(tm, seq)
    chunks = d // LANES
    assert tm % ROW_UNROLL == 0
    tok = lambda i, d1, d2: (i, 0)
    const = lambda i, d1, d2: (0, 0)
    return pl.pallas_call(
        functools.partial(_combine_kernel, tm=tm, chunks=chunks, alpha=alpha),
        grid_spec=pltpu.PrefetchScalarGridSpec(
            num_scalar_prefetch=2,
            grid=(n // tm,),
            in_specs=[pl.BlockSpec(memory_space=pl.ANY),
                      pl.BlockSpec((tm, d), tok),
                      pl.BlockSpec((tm, LANES), tok),
                      pl.BlockSpec((1, 1, d), lambda i, d1, d2: (((i * tm) // seq) * 6 + 5, 0, 0)),
                      pl.BlockSpec((1, d), const),
                      pl.BlockSpec((1, d), const)],
            out_specs=pl.BlockSpec((tm, d), tok),
            scratch_shapes=[pltpu.VMEM((2, tm * chunks, LANES), F32),
                            pltpu.VMEM((2, tm * chunks, LANES), F32),
                            pltpu.SemaphoreType.DMA((2,))],
        ),
        out_shape=jax.ShapeDtypeStruct((n, d), F32),
        compiler_params=_cparams("arbitrary"),
        name="combine",
    )(dest1, dest2, ys, x1, route, mod3, ln2_g, ln2_b)


EXPERT_BLOCK_ROWS = 256


def kernel(x, c, w_ada, b_ada, w_in, sgu_w, sgu_b, sgu_ln_g, sgu_ln_b, w_proj_a, w_proj_b,
           w_out, ln1_g, ln1_b, w_group, b_group, w_router, b_router, w_gate, w_up, w_down,
           ln2_g, ln2_b):
    b, s, d = x.shape
    n = b * s
    depth = w_ada.shape[0]
    alpha = (2.0 * depth) ** 0.25
    sub = d // LANES
    bm = EXPERT_BLOCK_ROWS
    n_assign = 2 * n
    n_rows = n_assign + N_EXPERTS * bm
    nblk = n_rows // bm

    in_cols = w_in.shape[2]
    colscale = jnp.ones((1, in_cols), F32).at[:, _COL_Q:_COL_K].set(
        SB_HEAD_DIM ** -0.5 * LOG2E)

    for l in range(depth):
        mod = _ada(c, w_ada[l], b_ada[l])
        mod3 = mod.reshape(b * 6, 1, d)

        x2 = x.reshape(n, d)
        proj = _inproj(x2, mod3, w_in[l].astype(BF16), colscale, s)
        out_a = _attention(proj.reshape(b, s, in_cols), d).reshape(n, SB_WIDTH)

        wr = jnp.zeros((d, LANES), F32)
        wr = wr.at[:, GROUP_LANE0:GROUP_LANE0 + N_GROUPS].set(w_group[l])
        wr = wr.at[:, EXPERT_LANE0:EXPERT_LANE0 + N_EXPERTS].set(w_router[l])
        wr_hi = wr.astype(BF16)
        wr_lo = (wr - wr_hi.astype(F32)).astype(BF16)
        wr = jnp.concatenate([wr_hi, wr_lo], axis=1)
        br = jnp.zeros((1, LANES), F32)
        br = br.at[0, GROUP_LANE0:GROUP_LANE0 + N_GROUPS].set(b_group[l])
        br = br.at[0, EXPERT_LANE0:EXPERT_LANE0 + N_EXPERTS].set(b_router[l])
        sgu_bias_full = jnp.repeat(sgu_b[l].T, SGU_GROUP_DIM, axis=1)

        x1, h2s, route, oh1, oh2 = _mix(
            out_a, proj, x2, mod3, sgu_w[l], sgu_bias_full,
            sgu_ln_g[l].reshape(1, -1), sgu_ln_b[l].reshape(1, -1),
            w_proj_a[l].astype(BF16), w_proj_b[l].astype(BF16), w_out[l].astype(BF16),
            ln1_g[l].reshape(1, d), ln1_b[l].reshape(1, d), wr, br, s, alpha)

        dest, blk = _plan(oh1, oh2, bm, nblk)
        dest1 = dest[:, 0]
        dest2 = dest[:, 1]
        block_expert = blk[:nblk, 0]
        n_active = blk[0:1, 1]

        row_tok = _invert(dest1, dest2, n_rows)
        ys = _experts(block_expert, n_active, row_tok, h2s, w_gate[l], w_up[l], w_down[l], bm)
        out = _combine(dest1, dest2, ys, x1, route, mod3, ln2_g[l].reshape(1, d),
                       ln2_b[l].reshape(1, d), s, alpha)
        x = out.reshape(b, s, d)
    return x
```

```python
import functools
import math

import jax
import jax.numpy as jnp
from jax import lax
from jax.experimental import pallas as pl
from jax.experimental.pallas import tpu as pltpu

F32 = jnp.float32
BF16 = jnp.bfloat16
I32 = jnp.int32

CHUNK = 64
SB_HEADS = 8
SB_HEAD_DIM = 128
SB_WIDTH = SB_HEADS * SB_HEAD_DIM
SGU_GROUPS = 8
SGU_GROUP_DIM = 128
SGU_WIDTH = SGU_GROUPS * SGU_GROUP_DIM
SGU_BLOCK = 128
N_GROUPS = 4
EXPERTS_PER_GROUP = 8
N_EXPERTS = N_GROUPS * EXPERTS_PER_GROUP
D_EXPERT = 512
LN_EPS = 1e-5

LANES = 128
SUBLANES = 8
VMEM_LIMIT_BYTES = 60000 * 1024

GROUP_LANE0 = 0
EXPERT_LANE0 = N_GROUPS

LOG2E = 1.4426950408889634
NEG_BIG = -1e30

_COL_Q = 0
_COL_K = SB_WIDTH
_COL_V = 2 * SB_WIDTH
_COL_SU = 3 * SB_WIDTH
_COL_SV = 3 * SB_WIDTH + SGU_WIDTH
_COL_GA = 3 * SB_WIDTH + 2 * SGU_WIDTH


def _COL_GB(d):
    return _COL_GA + d


def _cparams(*sem):
    return pltpu.CompilerParams(dimension_semantics=sem, vmem_limit_bytes=VMEM_LIMIT_BYTES)


def _ln_rows(x):
    mu = jnp.mean(x, axis=-1, keepdims=True)
    xc = x - mu
    var = jnp.mean(xc * xc, axis=-1, keepdims=True)
    return xc * lax.rsqrt(var + LN_EPS)


def _sigmoid(x):
    return 1.0 / (1.0 + jnp.exp(-x))


def _ada_kernel(c_ref, w_ref, b_ref, o_ref):
    c = c_ref[...]
    ca = c * _sigmoid(c)
    o_ref[...] = jnp.dot(ca, w_ref[...], preferred_element_type=F32,
                         precision=lax.Precision.HIGHEST) + b_ref[...]


def _ada(c, w_ada, b_ada, tn=1024):
    b, d = c.shape
    n_out = w_ada.shape[1]
    cp = jnp.zeros((SUBLANES, d), F32).at[:b].set(c)
    out = pl.pallas_call(
        _ada_kernel,
        grid=(n_out // tn,),
        in_specs=[pl.BlockSpec((SUBLANES, d), lambda j: (0, 0)),
                  pl.BlockSpec((d, tn), lambda j: (0, j)),
                  pl.BlockSpec((1, tn), lambda j: (0, j))],
        out_specs=pl.BlockSpec((SUBLANES, tn), lambda j: (0, j)),
        out_shape=jax.ShapeDtypeStruct((SUBLANES, n_out), F32),
        compiler_params=_cparams("arbitrary"),
        name="ada",
    )(cp, w_ada, b_ada.reshape(1, n_out))
    return out[:b]


def _inproj_kernel(x_ref, sh_ref, sc_ref, w_ref, cs_ref, o_ref, h_ref):
    @pl.when(pl.program_id(1) == 0)
    def _():
        h = _ln_rows(x_ref[...]) * (1.0 + sc_ref[0]) + sh_ref[0]
        h_ref[...] = h.astype(BF16)

    acc = jnp.dot(h_ref[...], w_ref[...], preferred_element_type=F32)
    o_ref[...] = (acc * cs_ref[...]).astype(BF16)


def _inproj(x2, mod3, w_in_bf, colscale, seq, tm=1024, tn=1024):
    n, d = x2.shape
    n_cols = w_in_bf.shape[1]
    tm = min(tm, seq)
    return pl.pallas_call(
        _inproj_kernel,
        grid=(n // tm, n_cols // tn),
        in_specs=[pl.BlockSpec((tm, d), lambda i, j: (i, 0)),
                  pl.BlockSpec((1, 1, d), lambda i, j: (((i * tm) // seq) * 6 + 0, 0, 0)),
                  pl.BlockSpec((1, 1, d), lambda i, j: (((i * tm) // seq) * 6 + 1, 0, 0)),
                  pl.BlockSpec((d, tn), lambda i, j: (0, j)),
                  pl.BlockSpec((1, tn), lambda i, j: (0, j))],
        out_specs=pl.BlockSpec((tm, tn), lambda i, j: (i, j)),
        out_shape=jax.ShapeDtypeStruct((n, n_cols), BF16),
        scratch_shapes=[pltpu.VMEM((tm, d), BF16)],
        compiler_params=_cparams("arbitrary", "arbitrary"),
        name="inproj",
    )(x2, mod3, mod3, w_in_bf, colscale)


Z2_MAX = 100.0
STICK_EXIT_LOG2 = 150.0


def _attn_kernel(q_ref, k_ref, v_ref, o_ref, *, tq, tk, heads):
    qi = pl.program_id(2)
    nd = tq // tk
    jd = qi * nd
    trow = lax.broadcasted_iota(I32, (tk, tk), 0)
    tcol = lax.broadcasted_iota(I32, (tk, tk), 1)
    tri = (trow > tcol).astype(BF16)
    hd = SB_HEAD_DIM
    qs = [q_ref[0, :, h * hd:(h + 1) * hd] for h in range(heads)]

    def block(j, runs, accs, causal):
        off = pl.multiple_of(j * tk, tk)
        hs = range(heads)
        z2s = [lax.dot_general(qs[h], k_ref[0, pl.ds(off, tk), h * hd:(h + 1) * hd],
                               (((1,), (1,)), ((), ())), preferred_element_type=F32) for h in hs]
        z2s = [lax.clamp(-Z2_MAX, z2, Z2_MAX) for z2 in z2s]
        sps = [jnp.log(1.0 + jnp.exp2(z2)) for z2 in z2s]
        if causal is not None:
            sps = [jnp.where(causal, sp, 0.0) for sp in sps]
        laters = [jnp.dot(sp.astype(BF16), tri, preferred_element_type=F32) for sp in sps]
        ws = [jnp.exp2(z2s[h] - LOG2E * (sps[h] + (laters[h] + runs[h]))) for h in hs]
        if causal is not None:
            ws = [jnp.where(causal, w, 0.0) for w in ws]
        new_accs = [accs[h] + jnp.dot(ws[h].astype(BF16),
                                      v_ref[0, pl.ds(off, tk), h * hd:(h + 1) * hd],
                                      preferred_element_type=F32) for h in hs]
        new_runs = [runs[h] + jnp.sum(sps[h], axis=1, keepdims=True) for h in hs]
        return tuple(new_runs), tuple(new_accs)

    def mass_left(runs):
        low = runs[0]
        for r in runs[1:]:
            low = jnp.minimum(low, r)
        return (jnp.min(low) * LOG2E < STICK_EXIT_LOG2).astype(I32)

    def diag_body(t, carry):
        c = nd - 1 - t
        row = lax.broadcasted_iota(I32, (tq, tk), 0)
        col = lax.broadcasted_iota(I32, (tq, tk), 1)
        causal = col + c * tk < row
        return block(jd + c, carry[0], carry[1], causal)

    def more(carry):
        t, go, _, _ = carry
        return (t < jd) & (go > 0)

    def body(carry):
        t, _, runs, accs = carry
        runs, accs = block(jd - 1 - t, runs, accs, None)
        return t + 1, mass_left(runs), runs, accs

    runs = tuple(jnp.zeros((tq, 1), F32) for _ in range(heads))
    accs = tuple(jnp.zeros((tq, hd), F32) for _ in range(heads))
    runs, accs = lax.fori_loop(0, nd, diag_body, (runs, accs))
    _, _, runs, accs = lax.while_loop(more, body, (jnp.int32(0), mass_left(runs), runs, accs))
    for h in range(heads):
        o_ref[0, :, h * hd:(h + 1) * hd] = accs[h].astype(BF16)


def _attention(proj3, d, tq=256, tk=256, heads=4):
    b, s, _ = proj3.shape
    tk = min(tk, s)
    tq = min(tq, s)
    assert tq % tk == 0 and s % tq == 0
    width = heads * SB_HEAD_DIM
    qc = _COL_Q // width
    kc = _COL_K // width
    vc = _COL_V // width
    return pl.pallas_call(
        functools.partial(_attn_kernel, tq=tq, tk=tk, heads=heads),
        grid=(b, SB_HEADS // heads, s // tq),
        in_specs=[pl.BlockSpec((1, tq, width), lambda bi, g, qi: (bi, qi, qc + g)),
                  pl.BlockSpec((1, s, width), lambda bi, g, qi: (bi, 0, kc + g)),
                  pl.BlockSpec((1, s, width), lambda bi, g, qi: (bi, 0, vc + g))],
        out_specs=pl.BlockSpec((1, tq, width), lambda bi, g, qi: (bi, qi, g)),
        out_shape=jax.ShapeDtypeStruct((b, s, SB_WIDTH), BF16),
        compiler_params=_cparams("arbitrary", "arbitrary", "arbitrary"),
        name="attn",
    )(proj3, proj3, proj3)


def _erf(x):
    return lax.erf(x)


def _gelu(x):
    return 0.5 * x * (1.0 + _erf(x * (1.0 / math.sqrt(2.0))))


def _mix_kernel(oa_ref, su_ref, sv_ref, ga0_ref, ga1_ref, gb0_ref, gb1_ref, x_ref,
                sw_ref, sbias_ref, sg_ref, sb_ref,
                wpa_ref, wpb_ref, wo_ref,
                g1_ref, ln1g_ref, ln1b_ref, sh2_ref, sc2_ref,
                wr_ref, br_ref,
                x1_ref, h2_ref, route_ref, oh1_ref, oh2_ref, *, tm, alpha):
    u = _gelu(su_ref[...].astype(F32))
    v = _gelu(sv_ref[...].astype(F32))
    v = _ln_rows(v) * sg_ref[...] + sb_ref[...]
    vb = v.astype(BF16)
    pos_t = lax.broadcasted_iota(I32, (SGU_BLOCK, SGU_BLOCK), 0)
    pos_s = lax.broadcasted_iota(I32, (SGU_BLOCK, SGU_BLOCK), 1)
    chunk_causal = (pos_s // CHUNK) <= (pos_t // CHUNK)
    n_pos_blocks = tm // SGU_BLOCK
    per_group = []
    for g in range(SGU_GROUPS):
        c0 = g * SGU_GROUP_DIM
        wg = jnp.where(chunk_causal, sw_ref[g], 0.0).astype(BF16)
        rhs = jnp.concatenate(
            [vb[nb * SGU_BLOCK:(nb + 1) * SGU_BLOCK, c0:c0 + SGU_GROUP_DIM]
             for nb in range(n_pos_blocks)], axis=1)
        per_group.append(jnp.dot(wg, rhs, preferred_element_type=F32))
    mixed = jnp.concatenate(
        [jnp.concatenate([per_group[g][:, nb * SGU_GROUP_DIM:(nb + 1) * SGU_GROUP_DIM]
                          for g in range(SGU_GROUPS)], axis=1) + sbias_ref[...]
         for nb in range(n_pos_blocks)], axis=0)
    out_b = (u * mixed).astype(BF16)

    pa = jnp.dot(oa_ref[...], wpa_ref[...], preferred_element_type=F32)
    pb = jnp.dot(out_b, wpb_ref[...], preferred_element_type=F32)
    ga = jnp.concatenate([ga0_ref[...], ga1_ref[...]], axis=1).astype(F32)
    gb = jnp.concatenate([gb0_ref[...], gb1_ref[...]], axis=1).astype(F32)
    merged = _sigmoid(ga) * pa + _sigmoid(gb) * pb
    y = jnp.dot(merged.astype(BF16), wo_ref[...], preferred_element_type=F32)

    r = alpha * x_ref[...] + g1_ref[0] * y
    x1 = _ln_rows(r) * ln1g_ref[...] + ln1b_ref[...]
    x1_ref[...] = x1
    h2 = _ln_rows(x1) * (1.0 + sc2_ref[0]) + sh2_ref[0]
    chunks = h2.shape[1] // LANES
    for c in range(chunks):
        h2_ref[pl.ds(c, tm, stride=chunks), :] = h2[:, c * LANES:(c + 1) * LANES]

    h_hi = h2.astype(BF16)
    h_lo = (h2 - h_hi.astype(F32)).astype(BF16)
    hi_w = jnp.dot(h_hi, wr_ref[...], preferred_element_type=F32)
    lo_w = jnp.dot(h_lo, wr_ref[:, :LANES], preferred_element_type=F32)
    logits = (hi_w[:, :LANES] + hi_w[:, LANES:]) + lo_w + br_ref[...]
    lane = lax.broadcasted_iota(I32, logits.shape, 1)
    gl = jnp.where(lane < N_GROUPS, logits, NEG_BIG)
    gmax = jnp.max(gl, axis=1, keepdims=True)
    g_sel = jnp.min(jnp.where(gl == gmax, lane, LANES), axis=1, keepdims=True)
    p_group = 1.0 / jnp.sum(jnp.exp(gl - gmax), axis=1, keepdims=True)
    lo_lane = EXPERT_LANE0 + g_sel * EXPERTS_PER_GROUP
    in_group = (lane >= lo_lane) & (lane < lo_lane + EXPERTS_PER_GROUP)
    el = jnp.where(in_group, logits, NEG_BIG)
    m1 = jnp.max(el, axis=1, keepdims=True)
    i1 = jnp.min(jnp.where(el == m1, lane, LANES), axis=1, keepdims=True)
    el2 = jnp.where(lane == i1, NEG_BIG, el)
    m2 = jnp.max(el2, axis=1, keepdims=True)
    i2 = jnp.min(jnp.where(el2 == m2, lane, LANES), axis=1, keepdims=True)
    e21 = jnp.exp(m2 - m1)
    w1 = p_group / (1.0 + e21)
    w2 = p_group * e21 / (1.0 + e21)
    route = jnp.where(lane == 0, w1, 0.0)
    route = jnp.where(lane == 1, w2, route)
    route = jnp.where(lane == 2, (i1 - EXPERT_LANE0).astype(F32), route)
    route = jnp.where(lane == 3, (i2 - EXPERT_LANE0).astype(F32), route)
    route_ref[...] = route
    oh1_ref[...] = jnp.where(lane == i1, 1.0, 0.0).astype(BF16)
    oh2_ref[...] = jnp.where(lane == i2, 1.0, 0.0).astype(BF16)


def _mix(out_a, proj, x2, mod3, sgu_w, sgu_bias_full, sgu_g, sgu_b, wpa, wpb, wo,
         ln1_g, ln1_b, wr, br, seq, alpha, tm=256):
    n, d = x2.shape
    tm = min(tm, seq)
    def pcol(start, width):
        assert start % width == 0
        return start // width
    su_c = pcol(_COL_SU, SGU_WIDTH)
    sv_c = pcol(_COL_SV, SGU_WIDTH)
    half = d // 2
    ga_c = pcol(_COL_GA, half)
    gb_c = pcol(_COL_GB(d), half)
    bidx = lambda i: (i * tm) // seq
    row = lambda i: (i, 0)
    const2 = lambda i: (0, 0)
    const3 = lambda i: (0, 0, 0)
    return pl.pallas_call(
        functools.partial(_mix_kernel, tm=tm, alpha=alpha),
        grid=(n // tm,),
        in_specs=[pl.BlockSpec((tm, SB_WIDTH), row),
                  pl.BlockSpec((tm, SGU_WIDTH), lambda i: (i, su_c)),
                  pl.BlockSpec((tm, SGU_WIDTH), lambda i: (i, sv_c)),
                  pl.BlockSpec((tm, half), lambda i: (i, ga_c)),
                  pl.BlockSpec((tm, half), lambda i: (i, ga_c + 1)),
                  pl.BlockSpec((tm, half), lambda i: (i, gb_c)),
                  pl.BlockSpec((tm, half), lambda i: (i, gb_c + 1)),
                  pl.BlockSpec((tm, d), row),
                  pl.BlockSpec((SGU_GROUPS, SGU_BLOCK, SGU_BLOCK), const3),
                  pl.BlockSpec((SGU_BLOCK, SGU_WIDTH), const2),
                  pl.BlockSpec((1, SGU_WIDTH), const2),
                  pl.BlockSpec((1, SGU_WIDTH), const2),
                  pl.BlockSpec((SB_WIDTH, d), const2),
                  pl.BlockSpec((SGU_WIDTH, d), const2),
                  pl.BlockSpec((d, d), const2),
                  pl.BlockSpec((1, 1, d), lambda i: (bidx(i) * 6 + 2, 0, 0)),
                  pl.BlockSpec((1, d), const2),
                  pl.BlockSpec((1, d), const2),
                  pl.BlockSpec((1, 1, d), lambda i: (bidx(i) * 6 + 3, 0, 0)),
                  pl.BlockSpec((1, 1, d), lambda i: (bidx(i) * 6 + 4, 0, 0)),
                  pl.BlockSpec((d, 2 * LANES), const2),
                  pl.BlockSpec((1, LANES), const2)],
        out_specs=[pl.BlockSpec((tm, d), row),
                   pl.BlockSpec((tm * (d // LANES), LANES), row),
                   pl.BlockSpec((tm, LANES), row),
                   pl.BlockSpec((tm, LANES), row),
                   pl.BlockSpec((tm, LANES), row)],
        out_shape=[jax.ShapeDtypeStruct((n, d), F32),
                   jax.ShapeDtypeStruct((n * (d // LANES), LANES), F32),
                   jax.ShapeDtypeStruct((n, LANES), F32),
                   jax.ShapeDtypeStruct((n, LANES), BF16),
                   jax.ShapeDtypeStruct((n, LANES), BF16)],
        compiler_params=_cparams("arbitrary"),
        name="mix",
    )(out_a, proj, proj, proj, proj, proj, proj, x2, sgu_w, sgu_bias_full, sgu_g, sgu_b,
      wpa, wpb, wo, mod3, ln1_g, ln1_b, mod3, mod3, wr, br)


def _plan_kernel(oh1_ref, oh2_ref, dest_ref, blk_ref, cnt_ref, start_ref, *, tm, bm, nblk_pad):
    phase = pl.program_id(0)
    i = pl.program_id(1)
    oh1 = oh1_ref[...]
    oh2 = oh2_ref[...]
    both = oh1 + oh2
    lane = lax.broadcasted_iota(I32, (SUBLANES, LANES), 1)

    @pl.when((phase == 0) & (i == 0))
    def _():
        cnt_ref[...] = jnp.zeros_like(cnt_ref)

    @pl.when(phase == 0)
    def _():
        ones = jnp.ones((SUBLANES, tm), BF16)
        cnt_ref[...] += jnp.dot(ones, both, preferred_element_type=F32)

    @pl.when((phase == 1) & (i == 0))
    def _():
        cnt = cnt_ref[...]
        padded = jnp.floor((cnt + (bm - 1)) * (1.0 / bm)) * bm
        r = lax.broadcasted_iota(I32, (LANES, LANES), 0)
        c = lax.broadcasted_iota(I32, (LANES, LANES), 1)
        upper = (r < c).astype(F32)
        starts = jnp.dot(padded, upper, preferred_element_type=F32,
                         precision=lax.Precision.HIGHEST)
        start_ref[...] = starts
        cnt_ref[...] = jnp.zeros_like(cnt_ref)
        ends = starts + padded
        is_exp = (lane[0:1] >= EXPERT_LANE0) & (lane[0:1] < EXPERT_LANE0 + N_EXPERTS)
        bstart = (lax.broadcasted_iota(I32, (nblk_pad, LANES), 0) * bm).astype(F32)
        done = jnp.where(is_exp & (ends[0:1] <= bstart), 1.0, 0.0)
        bexp = jnp.minimum(jnp.sum(done, axis=1, keepdims=True), N_EXPERTS - 1.0)
        total = jnp.sum(jnp.where(is_exp, padded[0:1], 0.0), axis=1, keepdims=True)
        blane = lax.broadcasted_iota(I32, (nblk_pad, LANES), 1)
        blk = jnp.where(blane == 0, bexp, jnp.where(blane == 1, total * (1.0 / bm), 0.0))
        blk_ref[...] = blk.astype(I32)

    @pl.when(phase == 1)
    def _():
        r = lax.broadcasted_iota(I32, (tm, tm), 0)
        c = lax.broadcasted_iota(I32, (tm, tm), 1)
        lower = (c < r).astype(BF16)
        before = jnp.dot(lower, both, preferred_element_type=F32) + cnt_ref[0:1]
        pos = before + start_ref[0:1]
        d1 = jnp.sum(pos * oh1.astype(F32), axis=1, keepdims=True)
        d2 = jnp.sum(pos * oh2.astype(F32), axis=1, keepdims=True)
        dlane = lax.broadcasted_iota(I32, (tm, LANES), 1)
        dest = jnp.where(dlane == 0, d1, jnp.where(dlane == 1, d2, 0.0))
        dest_ref[...] = dest.astype(I32)
        ones = jnp.ones((SUBLANES, tm), BF16)
        cnt_ref[...] += jnp.dot(ones, both, preferred_element_type=F32)


def _plan(oh1, oh2, bm, nblk, tm=512):
    n = oh1.shape[0]
    tm = min(tm, n)
    nblk_pad = -(-nblk // SUBLANES) * SUBLANES
    dest, blk = pl.pallas_call(
        functools.partial(_plan_kernel, tm=tm, bm=bm, nblk_pad=nblk_pad),
        grid=(2, n // tm),
        in_specs=[pl.BlockSpec((tm, LANES), lambda p, i: (i, 0)),
                  pl.BlockSpec((tm, LANES), lambda p, i: (i, 0))],
        out_specs=[pl.BlockSpec((tm, LANES), lambda p, i: (i * p, 0)),
                   pl.BlockSpec((nblk_pad, LANES), lambda p, i: (0, 0))],
        out_shape=[jax.ShapeDtypeStruct((n, LANES), I32),
                   jax.ShapeDtypeStruct((nblk_pad, LANES), I32)],
        scratch_shapes=[pltpu.VMEM((SUBLANES, LANES), F32),
                        pltpu.VMEM((SUBLANES, LANES), F32)],
        compiler_params=_cparams("arbitrary", "arbitrary"),
        name="plan",
    )(oh1, oh2)
    return dest, blk


ROW_UNROLL = 8


def _invert_kernel(d1_ref, d2_ref, zero_ref, rt_ref, sem, *, tb):
    i = pl.program_id(0)

    @pl.when(i == 0)
    def _():
        clear = pltpu.make_async_copy(zero_ref, rt_ref, sem)
        clear.start()
        clear.wait()

    def put(j, carry):
        for u in range(ROW_UNROLL):
            t = j * ROW_UNROLL + u
            rt_ref[d1_ref[0, 0, t]] = i * tb + t
            rt_ref[d2_ref[0, 0, t]] = i * tb + t
        return carry

    lax.fori_loop(0, tb // ROW_UNROLL, put, 0)


def _invert(dest1, dest2, n_rows, tb=2048):
    n = dest1.shape[0]
    tb = min(tb, n)
    assert n % tb == 0 and tb % ROW_UNROLL == 0
    blocks = lambda i: (i, 0, 0)
    return pl.pallas_call(
        functools.partial(_invert_kernel, tb=tb),
        grid=(n // tb,),
        in_specs=[pl.BlockSpec((1, 1, tb), blocks, memory_space=pltpu.SMEM),
                  pl.BlockSpec((1, 1, tb), blocks, memory_space=pltpu.SMEM),
                  pl.BlockSpec(memory_space=pl.ANY)],
        out_specs=pl.BlockSpec(memory_space=pltpu.SMEM),
        out_shape=jax.ShapeDtypeStruct((n_rows,), I32),
        scratch_shapes=[pltpu.SemaphoreType.DMA(())],
        compiler_params=_cparams("arbitrary"),
        name="invert",
    )(dest1.reshape(n // tb, 1, tb), dest2.reshape(n // tb, 1, tb), jnp.zeros((n_rows,), I32))


def _slab(idx, chunks):
    return pl.ds(pl.multiple_of(idx * chunks, chunks), chunks)


def _rows_from_slabs(buf_ref, slot, rows, chunks):
    return jnp.concatenate(
        [buf_ref[slot, pl.ds(c, rows, stride=chunks), :] for c in range(chunks)], axis=1)


GATHER_AHEAD = 6


def _experts_kernel(be_ref, na_ref, first_ref, wslot_ref, nxt_ref, *refs, bm, chunks):
    head_refs = refs[:GATHER_AHEAD]
    (rtn_ref, h_ref, wg_hbm, wu_hbm, wd_hbm, o_ref,
     xbuf, sem, wgf, wuf, wdf, wsem, wgb, wub, wdb) = refs[GATHER_AHEAD:]
    i = pl.program_id(0)
    n_active = na_ref[0]

    def gather(tok_ref, slot):
        def trip(r8, carry):
            for u in range(ROW_UNROLL):
                r = r8 * ROW_UNROLL + u
                pltpu.make_async_copy(h_ref.at[_slab(tok_ref[0, 0, r], chunks), :],
                                      xbuf.at[slot, _slab(r, chunks), :], sem.at[slot]).start()
            return carry
        lax.fori_loop(0, bm // ROW_UNROLL, trip, 0)

    def weight_copies(e, ws):
        return (pltpu.make_async_copy(wg_hbm.at[e], wgf.at[ws], wsem.at[ws]),
                pltpu.make_async_copy(wu_hbm.at[e], wuf.at[ws], wsem.at[ws]),
                pltpu.make_async_copy(wd_hbm.at[e], wdf.at[ws], wsem.at[ws]))

    def fetch_weights(e, ws):
        for cp in weight_copies(e, ws):
            cp.start(priority=1)

    def gather_wait(slot):
        pltpu.make_async_copy(h_ref.at[pl.ds(0, bm * chunks), :], xbuf.at[slot],
                              sem.at[slot]).wait()

    @pl.when(i == 0)
    def _():
        fetch_weights(be_ref[0], wslot_ref[0])
        gather(head_refs[0], 0)
        for k in range(1, GATHER_AHEAD):
            pl.when(k < n_active)(functools.partial(gather, head_refs[k], k))

    @pl.when(i < n_active)
    def _():
        slot = lax.rem(i, GATHER_AHEAD + 1)
        gather_wait(slot)

        @pl.when(i + GATHER_AHEAD < n_active)
        def _():
            gather(rtn_ref, lax.rem(i + GATHER_AHEAD, GATHER_AHEAD + 1))

        @pl.when(first_ref[i] == 1)
        def _():
            ws = wslot_ref[i]
            for cp in weight_copies(be_ref[i], ws):
                cp.wait()

            @pl.when(nxt_ref[i] >= 0)
            def _():
                fetch_weights(nxt_ref[i], 1 - ws)

            wgb[...] = wgf[ws].astype(BF16)
            wub[...] = wuf[ws].astype(BF16)
            wdb[...] = wdf[ws].astype(BF16)

        x = _rows_from_slabs(xbuf, slot, bm, chunks).astype(BF16)
        g = jnp.dot(x, wgb[...], preferred_element_type=F32)
        u = jnp.dot(x, wub[...], preferred_element_type=F32)
        hb = (g * _sigmoid(g) * u).astype(BF16)
        y = jnp.dot(hb, wdb[...], preferred_element_type=F32)
        for c in range(chunks):
            o_ref[pl.ds(c, bm, stride=chunks), :] = y[:, c * LANES:(c + 1) * LANES]

    @pl.when(i >= n_active)
    def _():
        o_ref[...] = jnp.zeros_like(o_ref)


def _experts(block_expert, n_active, row_tok, h2s, w_gate, w_up, w_down, bm):
    n_rows = row_tok.shape[0]
    nblk = n_rows // bm
    d, de = w_gate.shape[1], w_gate.shape[2]
    chunks = d // LANES
    assert bm % ROW_UNROLL == 0
    assert nblk > GATHER_AHEAD
    row_blocks = row_tok.reshape(nblk, 1, bm)

    idx = jnp.arange(nblk, dtype=I32)
    prev = jnp.concatenate([block_expert[:1] - 1, block_expert[:-1]])
    first = ((block_expert != prev) & (idx < n_active[0])).astype(I32)
    wslot = ((jnp.cumsum(first) - 1) % 2).astype(I32)
    run_start = lax.cummin(jnp.where(first > 0, idx, nblk), axis=0, reverse=True)
    next_start = jnp.concatenate([run_start[1:], jnp.full((1,), nblk, I32)])
    nxt = jnp.where(next_start < nblk,
                    block_expert[jnp.minimum(next_start, nblk - 1)], -1).astype(I32)

    return pl.pallas_call(
        functools.partial(_experts_kernel, bm=bm, chunks=chunks),
        grid_spec=pltpu.PrefetchScalarGridSpec(
            num_scalar_prefetch=5,
            grid=(nblk,),
            in_specs=[pl.BlockSpec((1, 1, bm), functools.partial(lambda k, i, *_: (k, 0, 0), k),
                                   memory_space=pltpu.SMEM) for k in range(GATHER_AHEAD)]
                     + [pl.BlockSpec((1, 1, bm),
                                     lambda i, *_: (jnp.minimum(i + GATHER_AHEAD, nblk - 1), 0, 0),
                                     memory_space=pltpu.SMEM)]
                     + [pl.BlockSpec(memory_space=pl.ANY)] * 4,
            out_specs=pl.BlockSpec((bm * chunks, LANES), lambda i, *_: (i, 0)),
            scratch_shapes=[pltpu.VMEM((GATHER_AHEAD + 1, bm * chunks, LANES), F32),
                            pltpu.SemaphoreType.DMA((GATHER_AHEAD + 1,)),
                            pltpu.VMEM((2, d, de), F32),
                            pltpu.VMEM((2, d, de), F32),
                            pltpu.VMEM((2, de, d), F32),
                            pltpu.SemaphoreType.DMA((2,)),
                            pltpu.VMEM((d, de), BF16),
                            pltpu.VMEM((d, de), BF16),
                            pltpu.VMEM((de, d), BF16)],
        ),
        out_shape=jax.ShapeDtypeStruct((n_rows * chunks, LANES), F32),
        compiler_params=_cparams("arbitrary"),
        name="experts",
    )(block_expert, n_active, first, wslot, nxt, *([row_blocks] * (GATHER_AHEAD + 1)),
      h2s, w_gate, w_up, w_down)


def _combine_kernel(d1_ref, d2_ref, ys_ref, x1_ref, route_ref, g2_ref, lg_ref, lb_ref,
                    o_ref, y1buf, y2buf, sem, *, tm, chunks, alpha):
    i = pl.program_id(0)
    n_tiles = pl.num_programs(0)

    def row_copies(tile, t, slot):
        tok = tile * tm + t
        return (pltpu.make_async_copy(ys_ref.at[_slab(d1_ref[tok], chunks), :],
                                      y1buf.at[slot, _slab(t, chunks), :], sem.at[slot]),
                pltpu.make_async_copy(ys_ref.at[_slab(d2_ref[tok], chunks), :],
                                      y2buf.at[slot, _slab(t, chunks), :], sem.at[slot]))

    def gather(tile, slot):
        def trip(t8, carry):
            for u in range(ROW_UNROLL):
                for k, cp in enumerate(row_copies(tile, t8 * ROW_UNROLL + u, slot)):
                    cp.start(priority=k)
            return carry
        lax.fori_loop(0, tm // ROW_UNROLL, trip, 0)

    def gather_wait(slot):
        for buf in (y1buf, y2buf):
            pltpu.make_async_copy(ys_ref.at[pl.ds(0, tm * chunks), :], buf.at[slot],
                                  sem.at[slot]).wait()

    @pl.when(i == 0)
    def _():
        gather(0, 0)

    slot = lax.rem(i, 2)
    gather_wait(slot)

    @pl.when(i + 1 < n_tiles)
    def _():
        gather(i + 1, 1 - slot)

    route = route_ref[...]
    y = (route[:, 0:1] * _rows_from_slabs(y1buf, slot, tm, chunks)
         + route[:, 1:2] * _rows_from_slabs(y2buf, slot, tm, chunks))
    r = alpha * x1_ref[...] + g2_ref[0] * y
    o_ref[...] = _ln_rows(r) * lg_ref[...] + lb_ref[...]


def _combine(dest1, dest2, ys, x1, route, mod3, ln2_g, ln2_b, seq, alpha, tm=256):
    n, d = x1.shape
    tm = min(tm, seq)
    chunks = d // LANES
    assert tm % ROW_UNROLL == 0
    tok = lambda i, d1, d2: (i, 0)
    const = lambda i, d1, d2: (0, 0)
    return pl.pallas_call(
        functools.partial(_combine_kernel, tm=tm, chunks=chunks, alpha=alpha),
        grid_spec=pltpu.PrefetchScalarGridSpec(
            num_scalar_prefetch=2,
            grid=(n // tm,),
            in_specs=[pl.BlockSpec(memory_space=pl.ANY),
                      pl.BlockSpec((tm, d), tok),
                      pl.BlockSpec((tm, LANES), tok),
                      pl.BlockSpec((1, 1, d), lambda i, d1, d2: (((i * tm) // seq) * 6 + 5, 0, 0)),
                      pl.BlockSpec((1, d), const),
                      pl.BlockSpec((1, d), const)],
            out_specs=pl.BlockSpec((tm, d), tok),
            scratch_shapes=[pltpu.VMEM((2, tm * chunks, LANES), F32),
                            pltpu.VMEM((2, tm * chunks, LANES), F32),
                            pltpu.SemaphoreType.DMA((2,))],
        ),
        out_shape=jax.ShapeDtypeStruct((n, d), F32),
        compiler_params=_cparams("arbitrary"),
        name="combine",
    )(dest1, dest2, ys, x1, route, mod3, ln2_g, ln2_b)


EXPERT_BLOCK_ROWS = 256


def kernel(x, c, w_ada, b_ada, w_in, sgu_w, sgu_b, sgu_ln_g, sgu_ln_b, w_proj_a, w_proj_b,
           w_out, ln1_g, ln1_b, w_group, b_group, w_router, b_router, w_gate, w_up, w_down,
           ln2_g, ln2_b):
    b, s, d = x.shape
    n = b * s
    depth = w_ada.shape[0]
    alpha = (2.0 * depth) ** 0.25
    sub = d // LANES
    bm = EXPERT_BLOCK_ROWS
    n_assign = 2 * n
    n_rows = n_assign + N_EXPERTS * bm
    nblk = n_rows // bm

    in_cols = w_in.shape[2]
    colscale = jnp.ones((1, in_cols), F32).at[:, _COL_Q:_COL_K].set(
        SB_HEAD_DIM ** -0.5 * LOG2E)

    for l in range(depth):
        mod = _ada(c, w_ada[l], b_ada[l])
        mod3 = mod.reshape(b * 6, 1, d)

        x2 = x.reshape(n, d)
        proj = _inproj(x2, mod3, w_in[l].astype(BF16), colscale, s)
        out_a = _attention(proj.reshape(b, s, in_cols), d).reshape(n, SB_WIDTH)

        wr = jnp.zeros((d, LANES), F32)
        wr = wr.at[:, GROUP_LANE0:GROUP_LANE0 + N_GROUPS].set(w_group[l])
        wr = wr.at[:, EXPERT_LANE0:EXPERT_LANE0 + N_EXPERTS].set(w_router[l])
        wr_hi = wr.astype(BF16)
        wr_lo = (wr - wr_hi.astype(F32)).astype(BF16)
        wr = jnp.concatenate([wr_hi, wr_lo], axis=1)
        br = jnp.zeros((1, LANES), F32)
        br = br.at[0, GROUP_LANE0:GROUP_LANE0 + N_GROUPS].set(b_group[l])
        br = br.at[0, EXPERT_LANE0:EXPERT_LANE0 + N_EXPERTS].set(b_router[l])
        sgu_bias_full = jnp.repeat(sgu_b[l].T, SGU_GROUP_DIM, axis=1)

        x1, h2s, route, oh1, oh2 = _mix(
            out_a, proj, x2, mod3, sgu_w[l], sgu_bias_full,
            sgu_ln_g[l].reshape(1, -1), sgu_ln_b[l].reshape(1, -1),
            w_proj_a[l].astype(BF16), w_proj_b[l].astype(BF16), w_out[l].astype(BF16),
            ln1_g[l].reshape(1, d), ln1_b[l].reshape(1, d), wr, br, s, alpha)

        dest, blk = _plan(oh1, oh2, bm, nblk)
        dest1 = dest[:, 0]
        dest2 = dest[:, 1]
        block_expert = blk[:nblk, 0]
        n_active = blk[0:1, 1]

        row_tok = _invert(dest1, dest2, n_rows)
        ys = _experts(block_expert, n_active, row_tok, h2s, w_gate[l], w_up[l], w_down[l], bm)
        out = _combine(dest1, dest2, ys, x1, route, mod3, ln2_g[l].reshape(1, d),
                       ln2_b[l].reshape(1, d), s, alpha)
        x = out.reshape(b, s, d)
    return x
```

```python
import functools
import math

import jax
import jax.numpy as jnp
from jax import lax
from jax.experimental import pallas as pl
from jax.experimental.pallas import tpu as pltpu

F32 = jnp.float32
BF16 = jnp.bfloat16
I32 = jnp.int32

CHUNK = 64
SB_HEADS = 8
SB_HEAD_DIM = 128
SB_WIDTH = SB_HEADS * SB_HEAD_DIM
SGU_GROUPS = 8
SGU_GROUP_DIM = 128
SGU_WIDTH = SGU_GROUPS * SGU_GROUP_DIM
SGU_BLOCK = 128
N_GROUPS = 4
EXPERTS_PER_GROUP = 8
N_EXPERTS = N_GROUPS * EXPERTS_PER_GROUP
D_EXPERT = 512
LN_EPS = 1e-5

LANES = 128
SUBLANES = 8
VMEM_LIMIT_BYTES = 60000 * 1024

GROUP_LANE0 = 0
EXPERT_LANE0 = N_GROUPS

LOG2E = 1.4426950408889634
NEG_BIG = -1e30

_COL_Q = 0
_COL_K = SB_WIDTH
_COL_V = 2 * SB_WIDTH
_COL_SU = 3 * SB_WIDTH
_COL_SV = 3 * SB_WIDTH + SGU_WIDTH
_COL_GA = 3 * SB_WIDTH + 2 * SGU_WIDTH


def _COL_GB(d):
    return _COL_GA + d


def _cparams(*sem):
    return pltpu.CompilerParams(dimension_semantics=sem, vmem_limit_bytes=VMEM_LIMIT_BYTES)


def _ln_rows(x):
    mu = jnp.mean(x, axis=-1, keepdims=True)
    xc = x - mu
    var = jnp.mean(xc * xc, axis=-1, keepdims=True)
    return xc * lax.rsqrt(var + LN_EPS)


def _sigmoid(x):
    return 1.0 / (1.0 + jnp.exp(-x))


def _ada_kernel(c_ref, w_ref, b_ref, o_ref):
    c = c_ref[...]
    ca = c * _sigmoid(c)
    o_ref[...] = jnp.dot(ca, w_ref[...], preferred_element_type=F32,
                         precision=lax.Precision.HIGHEST) + b_ref[...]


def _ada(c, w_ada, b_ada, tn=1024):
    b, d = c.shape
    n_out = w_ada.shape[1]
    cp = jnp.zeros((SUBLANES, d), F32).at[:b].set(c)
    out = pl.pallas_call(
        _ada_kernel,
        grid=(n_out // tn,),
        in_specs=[pl.BlockSpec((SUBLANES, d), lambda j: (0, 0)),
                  pl.BlockSpec((d, tn), lambda j: (0, j)),
                  pl.BlockSpec((1, tn), lambda j: (0, j))],
        out_specs=pl.BlockSpec((SUBLANES, tn), lambda j: (0, j)),
        out_shape=jax.ShapeDtypeStruct((SUBLANES, n_out), F32),
        compiler_params=_cparams("arbitrary"),
        name="ada",
    )(cp, w_ada, b_ada.reshape(1, n_out))
    return out[:b]


def _inproj_kernel(x_ref, sh_ref, sc_ref, w_ref, cs_ref, o_ref, h_ref):
    @pl.when(pl.program_id(1) == 0)
    def _():
        h = _ln_rows(x_ref[...]) * (1.0 + sc_ref[0]) + sh_ref[0]
        h_ref[...] = h.astype(BF16)

    acc = jnp.dot(h_ref[...], w_ref[...], preferred_element_type=F32)
    o_ref[...] = (acc * cs_ref[...]).astype(BF16)


def _inproj(x2, mod3, w_in_bf, colscale, seq, tm=1024, tn=1024):
    n, d = x2.shape
    n_cols = w_in_bf.shape[1]
    tm = min(tm, seq)
    return pl.pallas_call(
        _inproj_kernel,
        grid=(n // tm, n_cols // tn),
        in_specs=[pl.BlockSpec((tm, d), lambda i, j: (i, 0)),
                  pl.BlockSpec((1, 1, d), lambda i, j: (((i * tm) // seq) * 6 + 0, 0, 0)),
                  pl.BlockSpec((1, 1, d), lambda i, j: (((i * tm) // seq) * 6 + 1, 0, 0)),
                  pl.BlockSpec((d, tn), lambda i, j: (0, j)),
                  pl.BlockSpec((1, tn), lambda i, j: (0, j))],
        out_specs=pl.BlockSpec((tm, tn), lambda i, j: (i, j)),
        out_shape=jax.ShapeDtypeStruct((n, n_cols), BF16),
        scratch_shapes=[pltpu.VMEM((tm, d), BF16)],
        compiler_params=_cparams("arbitrary", "arbitrary"),
        name="inproj",
    )(x2, mod3, mod3, w_in_bf, colscale)


Z2_MAX = 100.0
STICK_EXIT_LOG2 = 150.0


def _attn_kernel(q_ref, k_ref, v_ref, o_ref, *, tq, tk, heads):
    qi = pl.program_id(2)
    nd = tq // tk
    jd = qi * nd
    trow = lax.broadcasted_iota(I32, (tk, tk), 0)
    tcol = lax.broadcasted_iota(I32, (tk, tk), 1)
    tri = (trow > tcol).astype(BF16)
    hd = SB_HEAD_DIM
    qs = [q_ref[0, :, h * hd:(h + 1) * hd] for h in range(heads)]

    def block(j, runs, accs, causal):
        off = pl.multiple_of(j * tk, tk)
        hs = range(heads)
        z2s = [lax.dot_general(qs[h], k_ref[0, pl.ds(off, tk), h * hd:(h + 1) * hd],
                               (((1,), (1,)), ((), ())), preferred_element_type=F32) for h in hs]
        z2s = [lax.clamp(-Z2_MAX, z2, Z2_MAX) for z2 in z2s]
        sps = [jnp.log(1.0 + jnp.exp2(z2)) for z2 in z2s]
        if causal is not None:
            sps = [jnp.where(causal, sp, 0.0) for sp in sps]
        laters = [jnp.dot(sp.astype(BF16), tri, preferred_element_type=F32) for sp in sps]
        ws = [jnp.exp2(z2s[h] - LOG2E * (sps[h] + (laters[h] + runs[h]))) for h in hs]
        if causal is not None:
            ws = [jnp.where(causal, w, 0.0) for w in ws]
        new_accs = [accs[h] + jnp.dot(ws[h].astype(BF16),
                                      v_ref[0, pl.ds(off, tk), h * hd:(h + 1) * hd],
                                      preferred_element_type=F32) for h in hs]
        new_runs = [runs[h] + jnp.sum(sps[h], axis=1, keepdims=True) for h in hs]
        return tuple(new_runs), tuple(new_accs)

    def mass_left(runs):
        low = runs[0]
        for r in runs[1:]:
            low = jnp.minimum(low, r)
        return (jnp.min(low) * LOG2E < STICK_EXIT_LOG2).astype(I32)

    def diag_body(t, carry):
        c = nd - 1 - t
        row = lax.broadcasted_iota(I32, (tq, tk), 0)
        col = lax.broadcasted_iota(I32, (tq, tk), 1)
        causal = col + c * tk < row
        return block(jd + c, carry[0], carry[1], causal)

    def more(carry):
        t, go, _, _ = carry
        return (t < jd) & (go > 0)

    def body(carry):
        t, _, runs, accs = carry
        runs, accs = block(jd - 1 - t, runs, accs, None)
        return t + 1, mass_left(runs), runs, accs

    zero = (tuple(jnp.zeros((tq, 1), F32) for _ in range(heads)),
            tuple(jnp.zeros((tq, hd), F32) for _ in range(heads)))

    def diagonal():
        return lax.fori_loop(0, nd, diag_body, zero)

    def diagonal_and_next():
        runs, accs = diagonal()
        return block(jd - 1, runs, accs, None)

    runs, accs = lax.cond(qi > 0, diagonal_and_next, diagonal)
    done = jnp.minimum(qi, 1).astype(I32)
    _, _, runs, accs = lax.while_loop(more, body, (done, mass_left(runs), runs, accs))
    for h in range(heads):
        o_ref[0, :, h * hd:(h + 1) * hd] = accs[h].astype(BF16)


def _attention(proj3, d, tq=256, tk=256, heads=4):
    b, s, _ = proj3.shape
    tk = min(tk, s)
    tq = min(tq, s)
    assert tq % tk == 0 and s % tq == 0
    width = heads * SB_HEAD_DIM
    qc = _COL_Q // width
    kc = _COL_K // width
    vc = _COL_V // width
    return pl.pallas_call(
        functools.partial(_attn_kernel, tq=tq, tk=tk, heads=heads),
        grid=(b, SB_HEADS // heads, s // tq),
        in_specs=[pl.BlockSpec((1, tq, width), lambda bi, g, qi: (bi, qi, qc + g)),
                  pl.BlockSpec((1, s, width), lambda bi, g, qi: (bi, 0, kc + g)),
                  pl.BlockSpec((1, s, width), lambda bi, g, qi: (bi, 0, vc + g))],
        out_specs=pl.BlockSpec((1, tq, width), lambda bi, g, qi: (bi, qi, g)),
        out_shape=jax.ShapeDtypeStruct((b, s, SB_WIDTH), BF16),
        compiler_params=_cparams("arbitrary", "arbitrary", "arbitrary"),
        name="attn",
    )(proj3, proj3, proj3)


def _erf(x):
    return lax.erf(x)


def _gelu(x):
    return 0.5 * x * (1.0 + _erf(x * (1.0 / math.sqrt(2.0))))


def _mix_kernel(oa_ref, su_ref, sv_ref, ga0_ref, ga1_ref, gb0_ref, gb1_ref, x_ref,
                sw_ref, sbias_ref, sg_ref, sb_ref,
                wpa_ref, wpb_ref, wo_ref,
                g1_ref, ln1g_ref, ln1b_ref, sh2_ref, sc2_ref,
                wr_ref, br_ref,
                x1_ref, h2_ref, route_ref, oh1_ref, oh2_ref, *, tm, alpha):
    u = _gelu(su_ref[...].astype(F32))
    v = _gelu(sv_ref[...].astype(F32))
    v = _ln_rows(v) * sg_ref[...] + sb_ref[...]
    vb = v.astype(BF16)
    pos_t = lax.broadcasted_iota(I32, (SGU_BLOCK, SGU_BLOCK), 0)
    pos_s = lax.broadcasted_iota(I32, (SGU_BLOCK, SGU_BLOCK), 1)
    chunk_causal = (pos_s // CHUNK) <= (pos_t // CHUNK)
    n_pos_blocks = tm // SGU_BLOCK
    per_group = []
    for g in range(SGU_GROUPS):
        c0 = g * SGU_GROUP_DIM
        wg = jnp.where(chunk_causal, sw_ref[g], 0.0).astype(BF16)
        rhs = jnp.concatenate(
            [vb[nb * SGU_BLOCK:(nb + 1) * SGU_BLOCK, c0:c0 + SGU_GROUP_DIM]
             for nb in range(n_pos_blocks)], axis=1)
        per_group.append(jnp.dot(wg, rhs, preferred_element_type=F32))
    mixed = jnp.concatenate(
        [jnp.concatenate([per_group[g][:, nb * SGU_GROUP_DIM:(nb + 1) * SGU_GROUP_DIM]
                          for g in range(SGU_GROUPS)], axis=1) + sbias_ref[...]
         for nb in range(n_pos_blocks)], axis=0)
    out_b = (u * mixed).astype(BF16)

    pa = jnp.dot(oa_ref[...], wpa_ref[...], preferred_element_type=F32)
    pb = jnp.dot(out_b, wpb_ref[...], preferred_element_type=F32)
    ga = jnp.concatenate([ga0_ref[...], ga1_ref[...]], axis=1).astype(F32)
    gb = jnp.concatenate([gb0_ref[...], gb1_ref[...]], axis=1).astype(F32)
    merged = _sigmoid(ga) * pa + _sigmoid(gb) * pb
    y = jnp.dot(merged.astype(BF16), wo_ref[...], preferred_element_type=F32)

    r = alpha * x_ref[...] + g1_ref[0] * y
    x1 = _ln_rows(r) * ln1g_ref[...] + ln1b_ref[...]
    x1_ref[...] = x1
    h2 = _ln_rows(x1) * (1.0 + sc2_ref[0]) + sh2_ref[0]
    chunks = h2.shape[1] // LANES
    for c in range(chunks):
        h2_ref[pl.ds(c, tm, stride=chunks), :] = h2[:, c * LANES:(c + 1) * LANES]

    h_hi = h2.astype(BF16)
    h_lo = (h2 - h_hi.astype(F32)).astype(BF16)
    hi_w = jnp.dot(h_hi, wr_ref[...], preferred_element_type=F32)
    lo_w = jnp.dot(h_lo, wr_ref[:, :LANES], preferred_element_type=F32)
    logits = (hi_w[:, :LANES] + hi_w[:, LANES:]) + lo_w + br_ref[...]
    lane = lax.broadcasted_iota(I32, logits.shape, 1)
    gl = jnp.where(lane < N_GROUPS, logits, NEG_BIG)
    gmax = jnp.max(gl, axis=1, keepdims=True)
    g_sel = jnp.min(jnp.where(gl == gmax, lane, LANES), axis=1, keepdims=True)
    p_group = 1.0 / jnp.sum(jnp.exp(gl - gmax), axis=1, keepdims=True)
    lo_lane = EXPERT_LANE0 + g_sel * EXPERTS_PER_GROUP
    in_group = (lane >= lo_lane) & (lane < lo_lane + EXPERTS_PER_GROUP)
    el = jnp.where(in_group, logits, NEG_BIG)
    m1 = jnp.max(el, axis=1, keepdims=True)
    i1 = jnp.min(jnp.where(el == m1, lane, LANES), axis=1, keepdims=True)
    el2 = jnp.where(lane == i1, NEG_BIG, el)
    m2 = jnp.max(el2, axis=1, keepdims=True)
    i2 = jnp.min(jnp.where(el2 == m2, lane, LANES), axis=1, keepdims=True)
    e21 = jnp.exp(m2 - m1)
    w1 = p_group / (1.0 + e21)
    w2 = p_group * e21 / (1.0 + e21)
    route = jnp.where(lane == 0, w1, 0.0)
    route = jnp.where(lane == 1, w2, route)
    route = jnp.where(lane == 2, (i1 - EXPERT_LANE0).astype(F32), route)
    route = jnp.where(lane == 3, (i2 - EXPERT_LANE0).astype(F32), route)
    route_ref[...] = route
    oh1_ref[...] = jnp.where(lane == i1, 1.0, 0.0).astype(BF16)
    oh2_ref[...] = jnp.where(lane == i2, 1.0, 0.0).astype(BF16)


def _mix(out_a, proj, x2, mod3, sgu_w, sgu_bias_full, sgu_g, sgu_b, wpa, wpb, wo,
         ln1_g, ln1_b, wr, br, seq, alpha, tm=256):
    n, d = x2.shape
    tm = min(tm, seq)
    def pcol(start, width):
        assert start % width == 0
        return start // width
    su_c = pcol(_COL_SU, SGU_WIDTH)
    sv_c = pcol(_COL_SV, SGU_WIDTH)
    half = d // 2
    ga_c = pcol(_COL_GA, half)
    gb_c = pcol(_COL_GB(d), half)
    bidx = lambda i: (i * tm) // seq
    row = lambda i: (i, 0)
    const2 = lambda i: (0, 0)
    const3 = lambda i: (0, 0, 0)
    return pl.pallas_call(
        functools.partial(_mix_kernel, tm=tm, alpha=alpha),
        grid=(n // tm,),
        in_specs=[pl.BlockSpec((tm, SB_WIDTH), row),
                  pl.BlockSpec((tm, SGU_WIDTH), lambda i: (i, su_c)),
                  pl.BlockSpec((tm, SGU_WIDTH), lambda i: (i, sv_c)),
                  pl.BlockSpec((tm, half), lambda i: (i, ga_c)),
                  pl.BlockSpec((tm, half), lambda i: (i, ga_c + 1)),
                  pl.BlockSpec((tm, half), lambda i: (i, gb_c)),
                  pl.BlockSpec((tm, half), lambda i: (i, gb_c + 1)),
                  pl.BlockSpec((tm, d), row),
                  pl.BlockSpec((SGU_GROUPS, SGU_BLOCK, SGU_BLOCK), const3),
                  pl.BlockSpec((SGU_BLOCK, SGU_WIDTH), const2),
                  pl.BlockSpec((1, SGU_WIDTH), const2),
                  pl.BlockSpec((1, SGU_WIDTH), const2),
                  pl.BlockSpec((SB_WIDTH, d), const2),
                  pl.BlockSpec((SGU_WIDTH, d), const2),
                  pl.BlockSpec((d, d), const2),
                  pl.BlockSpec((1, 1, d), lambda i: (bidx(i) * 6 + 2, 0, 0)),
                  pl.BlockSpec((1, d), const2),
                  pl.BlockSpec((1, d), const2),
                  pl.BlockSpec((1, 1, d), lambda i: (bidx(i) * 6 + 3, 0, 0)),
                  pl.BlockSpec((1, 1, d), lambda i: (bidx(i) * 6 + 4, 0, 0)),
                  pl.BlockSpec((d, 2 * LANES), const2),
                  pl.BlockSpec((1, LANES), const2)],
        out_specs=[pl.BlockSpec((tm, d), row),
                   pl.BlockSpec((tm * (d // LANES), LANES), row),
                   pl.BlockSpec((tm, LANES), row),
                   pl.BlockSpec((tm, LANES), row),
                   pl.BlockSpec((tm, LANES), row)],
        out_shape=[jax.ShapeDtypeStruct((n, d), F32),
                   jax.ShapeDtypeStruct((n * (d // LANES), LANES), F32),
                   jax.ShapeDtypeStruct((n, LANES), F32),
                   jax.ShapeDtypeStruct((n, LANES), BF16),
                   jax.ShapeDtypeStruct((n, LANES), BF16)],
        compiler_params=_cparams("arbitrary"),
        name="mix",
    )(out_a, proj, proj, proj, proj, proj, proj, x2, sgu_w, sgu_bias_full, sgu_g, sgu_b,
      wpa, wpb, wo, mod3, ln1_g, ln1_b, mod3, mod3, wr, br)


def _plan_kernel(oh1_ref, oh2_ref, dest_ref, blk_ref, cnt_ref, start_ref, *, tm, bm, nblk_pad):
    phase = pl.program_id(0)
    i = pl.program_id(1)
    oh1 = oh1_ref[...]
    oh2 = oh2_ref[...]
    both = oh1 + oh2
    lane = lax.broadcasted_iota(I32, (SUBLANES, LANES), 1)

    @pl.when((phase == 0) & (i == 0))
    def _():
        cnt_ref[...] = jnp.zeros_like(cnt_ref)

    @pl.when(phase == 0)
    def _():
        ones = jnp.ones((SUBLANES, tm), BF16)
        cnt_ref[...] += jnp.dot(ones, both, preferred_element_type=F32)

    @pl.when((phase == 1) & (i == 0))
    def _():
        cnt = cnt_ref[...]
        padded = jnp.floor((cnt + (bm - 1)) * (1.0 / bm)) * bm
        r = lax.broadcasted_iota(I32, (LANES, LANES), 0)
        c = lax.broadcasted_iota(I32, (LANES, LANES), 1)
        upper = (r < c).astype(F32)
        starts = jnp.dot(padded, upper, preferred_element_type=F32,
                         precision=lax.Precision.HIGHEST)
        start_ref[...] = starts
        cnt_ref[...] = jnp.zeros_like(cnt_ref)
        ends = starts + padded
        is_exp = (lane[0:1] >= EXPERT_LANE0) & (lane[0:1] < EXPERT_LANE0 + N_EXPERTS)
        bstart = (lax.broadcasted_iota(I32, (nblk_pad, LANES), 0) * bm).astype(F32)
        done = jnp.where(is_exp & (ends[0:1] <= bstart), 1.0, 0.0)
        bexp = jnp.minimum(jnp.sum(done, axis=1, keepdims=True), N_EXPERTS - 1.0)
        total = jnp.sum(jnp.where(is_exp, padded[0:1], 0.0), axis=1, keepdims=True)
        blane = lax.broadcasted_iota(I32, (nblk_pad, LANES), 1)
        blk = jnp.where(blane == 0, bexp, jnp.where(blane == 1, total * (1.0 / bm), 0.0))
        blk_ref[...] = blk.astype(I32)

    @pl.when(phase == 1)
    def _():
        r = lax.broadcasted_iota(I32, (tm, tm), 0)
        c = lax.broadcasted_iota(I32, (tm, tm), 1)
        lower = (c < r).astype(BF16)
        before = jnp.dot(lower, both, preferred_element_type=F32) + cnt_ref[0:1]
        pos = before + start_ref[0:1]
        d1 = jnp.sum(pos * oh1.astype(F32), axis=1, keepdims=True)
        d2 = jnp.sum(pos * oh2.astype(F32), axis=1, keepdims=True)
        dlane = lax.broadcasted_iota(I32, (tm, LANES), 1)
        dest = jnp.where(dlane == 0, d1, jnp.where(dlane == 1, d2, 0.0))
        dest_ref[...] = dest.astype(I32)
        ones = jnp.ones((SUBLANES, tm), BF16)
        cnt_ref[...] += jnp.dot(ones, both, preferred_element_type=F32)


def _plan(oh1, oh2, bm, nblk, tm=512):
    n = oh1.shape[0]
    tm = min(tm, n)
    nblk_pad = -(-nblk // SUBLANES) * SUBLANES
    dest, blk = pl.pallas_call(
        functools.partial(_plan_kernel, tm=tm, bm=bm, nblk_pad=nblk_pad),
        grid=(2, n // tm),
        in_specs=[pl.BlockSpec((tm, LANES), lambda p, i: (i, 0)),
                  pl.BlockSpec((tm, LANES), lambda p, i: (i, 0))],
        out_specs=[pl.BlockSpec((tm, LANES), lambda p, i: (i * p, 0)),
                   pl.BlockSpec((nblk_pad, LANES), lambda p, i: (0, 0))],
        out_shape=[jax.ShapeDtypeStruct((n, LANES), I32),
                   jax.ShapeDtypeStruct((nblk_pad, LANES), I32)],
        scratch_shapes=[pltpu.VMEM((SUBLANES, LANES), F32),
                        pltpu.VMEM((SUBLANES, LANES), F32)],
        compiler_params=_cparams("arbitrary", "arbitrary"),
        name="plan",
    )(oh1, oh2)
    return dest, blk


ROW_UNROLL = 8


def _invert_kernel(d1_ref, d2_ref, zero_ref, rt_ref, sem, *, tb):
    i = pl.program_id(0)

    @pl.when(i == 0)
    def _():
        clear = pltpu.make_async_copy(zero_ref, rt_ref, sem)
        clear.start()
        clear.wait()

    def put(j, carry):
        for u in range(ROW_UNROLL):
            t = j * ROW_UNROLL + u
            rt_ref[d1_ref[0, 0, t]] = i * tb + t
            rt_ref[d2_ref[0, 0, t]] = i * tb + t
        return carry

    lax.fori_loop(0, tb // ROW_UNROLL, put, 0)


def _invert(dest1, dest2, n_rows, tb=2048):
    n = dest1.shape[0]
    tb = min(tb, n)
    assert n % tb == 0 and tb % ROW_UNROLL == 0
    blocks = lambda i: (i, 0, 0)
    return pl.pallas_call(
        functools.partial(_invert_kernel, tb=tb),
        grid=(n // tb,),
        in_specs=[pl.BlockSpec((1, 1, tb), blocks, memory_space=pltpu.SMEM),
                  pl.BlockSpec((1, 1, tb), blocks, memory_space=pltpu.SMEM),
                  pl.BlockSpec(memory_space=pl.ANY)],
        out_specs=pl.BlockSpec(memory_space=pltpu.SMEM),
        out_shape=jax.ShapeDtypeStruct((n_rows,), I32),
        scratch_shapes=[pltpu.SemaphoreType.DMA(())],
        compiler_params=_cparams("arbitrary"),
        name="invert",
    )(dest1.reshape(n // tb, 1, tb), dest2.reshape(n // tb, 1, tb), jnp.zeros((n_rows,), I32))


def _slab(idx, chunks):
    return pl.ds(pl.multiple_of(idx * chunks, chunks), chunks)


def _rows_from_slabs(buf_ref, slot, rows, chunks):
    return jnp.concatenate(
        [buf_ref[slot, pl.ds(c, rows, stride=chunks), :] for c in range(chunks)], axis=1)


GATHER_AHEAD = 6


def _experts_kernel(be_ref, na_ref, first_ref, wslot_ref, nxt_ref, *refs, bm, chunks):
    head_refs = refs[:GATHER_AHEAD]
    (rtn_ref, h_ref, wg_hbm, wu_hbm, wd_hbm, o_ref,
     xbuf, sem, wgf, wuf, wdf, wsem, wgb, wub, wdb) = refs[GATHER_AHEAD:]
    i = pl.program_id(0)
    n_active = na_ref[0]

    def gather(tok_ref, slot):
        def trip(r8, carry):
            for u in range(ROW_UNROLL):
                r = r8 * ROW_UNROLL + u
                pltpu.make_async_copy(h_ref.at[_slab(tok_ref[0, 0, r], chunks), :],
                                      xbuf.at[slot, _slab(r, chunks), :], sem.at[slot]).start()
            return carry
        lax.fori_loop(0, bm // ROW_UNROLL, trip, 0)

    def weight_copies(e, ws):
        return (pltpu.make_async_copy(wg_hbm.at[e], wgf.at[ws], wsem.at[ws]),
                pltpu.make_async_copy(wu_hbm.at[e], wuf.at[ws], wsem.at[ws]),
                pltpu.make_async_copy(wd_hbm.at[e], wdf.at[ws], wsem.at[ws]))

    def fetch_weights(e, ws):
        for cp in weight_copies(e, ws):
            cp.start(priority=1)

    def gather_wait(slot):
        pltpu.make_async_copy(h_ref.at[pl.ds(0, bm * chunks), :], xbuf.at[slot],
                              sem.at[slot]).wait()

    @pl.when(i == 0)
    def _():
        fetch_weights(be_ref[0], wslot_ref[0])
        gather(head_refs[0], 0)
        for k in range(1, GATHER_AHEAD):
            pl.when(k < n_active)(functools.partial(gather, head_refs[k], k))

    @pl.when(i < n_active)
    def _():
        slot = lax.rem(i, GATHER_AHEAD + 1)
        gather_wait(slot)

        @pl.when(i + GATHER_AHEAD < n_active)
        def _():
            gather(rtn_ref, lax.rem(i + GATHER_AHEAD, GATHER_AHEAD + 1))

        @pl.when(first_ref[i] == 1)
        def _():
            ws = wslot_ref[i]
            for cp in weight_copies(be_ref[i], ws):
                cp.wait()

            @pl.when(nxt_ref[i] >= 0)
            def _():
                fetch_weights(nxt_ref[i], 1 - ws)

            wgb[...] = wgf[ws].astype(BF16)
            wub[...] = wuf[ws].astype(BF16)
            wdb[...] = wdf[ws].astype(BF16)

        x = _rows_from_slabs(xbuf, slot, bm, chunks).astype(BF16)
        g = jnp.dot(x, wgb[...], preferred_element_type=F32)
        u = jnp.dot(x, wub[...], preferred_element_type=F32)
        hb = (g * _sigmoid(g) * u).astype(BF16)
        y = jnp.dot(hb, wdb[...], preferred_element_type=F32)
        for c in range(chunks):
            o_ref[pl.ds(c, bm, stride=chunks), :] = y[:, c * LANES:(c + 1) * LANES]

    @pl.when(i >= n_active)
    def _():
        o_ref[...] = jnp.zeros_like(o_ref)


def _experts(block_expert, n_active, row_tok, h2s, w_gate, w_up, w_down, bm):
    n_rows = row_tok.shape[0]
    nblk = n_rows // bm
    d, de = w_gate.shape[1], w_gate.shape[2]
    chunks = d // LANES
    assert bm % ROW_UNROLL == 0
    assert nblk > GATHER_AHEAD
    row_blocks = row_tok.reshape(nblk, 1, bm)

    idx = jnp.arange(nblk, dtype=I32)
    prev = jnp.concatenate([block_expert[:1] - 1, block_expert[:-1]])
    first = ((block_expert != prev) & (idx < n_active[0])).astype(I32)
    wslot = ((jnp.cumsum(first) - 1) % 2).astype(I32)
    run_start = lax.cummin(jnp.where(first > 0, idx, nblk), axis=0, reverse=True)
    next_start = jnp.concatenate([run_start[1:], jnp.full((1,), nblk, I32)])
    nxt = jnp.where(next_start < nblk,
                    block_expert[jnp.minimum(next_start, nblk - 1)], -1).astype(I32)

    return pl.pallas_call(
        functools.partial(_experts_kernel, bm=bm, chunks=chunks),
        grid_spec=pltpu.PrefetchScalarGridSpec(
            num_scalar_prefetch=5,
            grid=(nblk,),
            in_specs=[pl.BlockSpec((1, 1, bm), functools.partial(lambda k, i, *_: (k, 0, 0), k),
                                   memory_space=pltpu.SMEM) for k in range(GATHER_AHEAD)]
                     + [pl.BlockSpec((1, 1, bm),
                                     lambda i, *_: (jnp.minimum(i + GATHER_AHEAD, nblk - 1), 0, 0),
                                     memory_space=pltpu.SMEM)]
                     + [pl.BlockSpec(memory_space=pl.ANY)] * 4,
            out_specs=pl.BlockSpec((bm * chunks, LANES), lambda i, *_: (i, 0)),
            scratch_shapes=[pltpu.VMEM((GATHER_AHEAD + 1, bm * chunks, LANES), F32),
                            pltpu.SemaphoreType.DMA((GATHER_AHEAD + 1,)),
                            pltpu.VMEM((2, d, de), F32),
                            pltpu.VMEM((2, d, de), F32),
                            pltpu.VMEM((2, de, d), F32),
                            pltpu.SemaphoreType.DMA((2,)),
                            pltpu.VMEM((d, de), BF16),
                            pltpu.VMEM((d, de), BF16),
                            pltpu.VMEM((de, d), BF16)],
        ),
        out_shape=jax.ShapeDtypeStruct((n_rows * chunks, LANES), F32),
        compiler_params=_cparams("arbitrary"),
        name="experts",
    )(block_expert, n_active, first, wslot, nxt, *([row_blocks] * (GATHER_AHEAD + 1)),
      h2s, w_gate, w_up, w_down)


def _combine_kernel(d1_ref, d2_ref, ys_ref, x1_ref, route_ref, g2_ref, lg_ref, lb_ref,
                    o_ref, y1buf, y2buf, sem, *, tm, chunks, alpha):
    i = pl.program_id(0)
    n_tiles = pl.num_programs(0)

    def row_copies(tile, t, slot):
        tok = tile * tm + t
        return (pltpu.make_async_copy(ys_ref.at[_slab(d1_ref[tok], chunks), :],
                                      y1buf.at[slot, _slab(t, chunks), :], sem.at[slot]),
                pltpu.make_async_copy(ys_ref.at[_slab(d2_ref[tok], chunks), :],
                                      y2buf.at[slot, _slab(t, chunks), :], sem.at[slot]))

    def gather(tile, slot):
        def trip(t8, carry):
            for u in range(ROW_UNROLL):
                for k, cp in enumerate(row_copies(tile, t8 * ROW_UNROLL + u, slot)):
                    cp.start(priority=k)
            return carry
        lax.fori_loop(0, tm // ROW_UNROLL, trip, 0)

    def gather_wait(slot):
        for buf in (y1buf, y2buf):
            pltpu.make_async_copy(ys_ref.at[pl.ds(0, tm * chunks), :], buf.at[slot],
                                  sem.at[slot]).wait()

    @pl.when(i == 0)
    def _():
        gather(0, 0)

    slot = lax.rem(i, 2)
    gather_wait(slot)

    @pl.when(i + 1 < n_tiles)
    def _():
        gather(i + 1, 1 - slot)

    route = route_ref[...]
    y = (route[:, 0:1] * _rows_from_slabs(y1buf, slot, tm, chunks)
         + route[:, 1:2] * _rows_from_slabs(y2buf, slot, tm, chunks))
    r = alpha * x1_ref[...] + g2_ref[0] * y
    o_ref[...] = _ln_rows(r) * lg_ref[...] + lb_ref[...]


def _combine(dest1, dest2, ys, x1, route, mod3, ln2_g, ln2_b, seq, alpha, tm=256):
    n, d = x1.shape
    tm = min(tm, seq)
    chunks = d // LANES
    assert tm % ROW_UNROLL == 0
    tok = lambda i, d1, d2: (i, 0)
    const = lambda i, d1, d2: (0, 0)
    return pl.pallas_call(
        functools.partial(_combine_kernel, tm=tm, chunks=chunks, alpha=alpha),
        grid_spec=pltpu.PrefetchScalarGridSpec(
            num_scalar_prefetch=2,
            grid=(n // tm,),
            in_specs=[pl.BlockSpec(memory_space=pl.ANY),
                      pl.BlockSpec((tm, d), tok),
                      pl.BlockSpec((tm, LANES), tok),
                      pl.BlockSpec((1, 1, d), lambda i, d1, d2: (((i * tm) // seq) * 6 + 5, 0, 0)),
                      pl.BlockSpec((1, d), const),
                      pl.BlockSpec((1, d), const)],
            out_specs=pl.BlockSpec((tm, d), tok),
            scratch_shapes=[pltpu.VMEM((2, tm * chunks, LANES), F32),
                            pltpu.VMEM((2, tm * chunks, LANES), F32),
                            pltpu.SemaphoreType.DMA((2,))],
        ),
        out_shape=jax.ShapeDtypeStruct((n, d), F32),
        compiler_params=_cparams("arbitrary"),
        name="combine",
    )(dest1, dest2, ys, x1, route, mod3, ln2_g, ln2_b)


EXPERT_BLOCK_ROWS = 256


def kernel(x, c, w_ada, b_ada, w_in, sgu_w, sgu_b, sgu_ln_g, sgu_ln_b, w_proj_a, w_proj_b,
           w_out, ln1_g, ln1_b, w_group, b_group, w_router, b_router, w_gate, w_up, w_down,
           ln2_g, ln2_b):
    b, s, d = x.shape
    n = b * s
    depth = w_ada.shape[0]
    alpha = (2.0 * depth) ** 0.25
    sub = d // LANES
    bm = EXPERT_BLOCK_ROWS
    n_assign = 2 * n
    n_rows = n_assign + N_EXPERTS * bm
    nblk = n_rows // bm

    in_cols = w_in.shape[2]
    colscale = jnp.ones((1, in_cols), F32).at[:, _COL_Q:_COL_K].set(
        SB_HEAD_DIM ** -0.5 * LOG2E)

    for l in range(depth):
        mod = _ada(c, w_ada[l], b_ada[l])
        mod3 = mod.reshape(b * 6, 1, d)

        x2 = x.reshape(n, d)
        proj = _inproj(x2, mod3, w_in[l].astype(BF16), colscale, s)
        out_a = _attention(proj.reshape(b, s, in_cols), d).reshape(n, SB_WIDTH)

        wr = jnp.zeros((d, LANES), F32)
        wr = wr.at[:, GROUP_LANE0:GROUP_LANE0 + N_GROUPS].set(w_group[l])
        wr = wr.at[:, EXPERT_LANE0:EXPERT_LANE0 + N_EXPERTS].set(w_router[l])
        wr_hi = wr.astype(BF16)
        wr_lo = (wr - wr_hi.astype(F32)).astype(BF16)
        wr = jnp.concatenate([wr_hi, wr_lo], axis=1)
        br = jnp.zeros((1, LANES), F32)
        br = br.at[0, GROUP_LANE0:GROUP_LANE0 + N_GROUPS].set(b_group[l])
        br = br.at[0, EXPERT_LANE0:EXPERT_LANE0 + N_EXPERTS].set(b_router[l])
        sgu_bias_full = jnp.repeat(sgu_b[l].T, SGU_GROUP_DIM, axis=1)

        x1, h2s, route, oh1, oh2 = _mix(
            out_a, proj, x2, mod3, sgu_w[l], sgu_bias_full,
            sgu_ln_g[l].reshape(1, -1), sgu_ln_b[l].reshape(1, -1),
            w_proj_a[l].astype(BF16), w_proj_b[l].astype(BF16), w_out[l].astype(BF16),
            ln1_g[l].reshape(1, d), ln1_b[l].reshape(1, d), wr, br, s, alpha)

        dest, blk = _plan(oh1, oh2, bm, nblk)
        dest1 = dest[:, 0]
        dest2 = dest[:, 1]
        block_expert = blk[:nblk, 0]
        n_active = blk[0:1, 1]

        row_tok = _invert(dest1, dest2, n_rows)
        ys = _experts(block_expert, n_active, row_tok, h2s, w_gate[l], w_up[l], w_down[l], bm)
        out = _combine(dest1, dest2, ys, x1, route, mod3, ln2_g[l].reshape(1, d),
                       ln2_b[l].reshape(1, d), s, alpha)
        x = out.reshape(b, s, d)
    return x
```

```python
import functools
import math

import jax
import jax.numpy as jnp
from jax import lax
from jax.experimental import pallas as pl
from jax.experimental.pallas import tpu as pltpu

F32 = jnp.float32
BF16 = jnp.bfloat16
I32 = jnp.int32

CHUNK = 64
SB_HEADS = 8
SB_HEAD_DIM = 128
SB_WIDTH = SB_HEADS * SB_HEAD_DIM
SGU_GROUPS = 8
SGU_GROUP_DIM = 128
SGU_WIDTH = SGU_GROUPS * SGU_GROUP_DIM
SGU_BLOCK = 128
N_GROUPS = 4
EXPERTS_PER_GROUP = 8
N_EXPERTS = N_GROUPS * EXPERTS_PER_GROUP
D_EXPERT = 512
LN_EPS = 1e-5

LANES = 128
SUBLANES = 8
VMEM_LIMIT_BYTES = 60000 * 1024

GROUP_LANE0 = 0
EXPERT_LANE0 = N_GROUPS

LOG2E = 1.4426950408889634
NEG_BIG = -1e30

_COL_Q = 0
_COL_K = SB_WIDTH
_COL_V = 2 * SB_WIDTH
_COL_SU = 3 * SB_WIDTH
_COL_SV = 3 * SB_WIDTH + SGU_WIDTH
_COL_GA = 3 * SB_WIDTH + 2 * SGU_WIDTH


def _COL_GB(d):
    return _COL_GA + d


def _cparams(*sem):
    return pltpu.CompilerParams(dimension_semantics=sem, vmem_limit_bytes=VMEM_LIMIT_BYTES)


def _ln_rows(x):
    mu = jnp.mean(x, axis=-1, keepdims=True)
    xc = x - mu
    var = jnp.mean(xc * xc, axis=-1, keepdims=True)
    return xc * lax.rsqrt(var + LN_EPS)


def _sigmoid(x):
    return 1.0 / (1.0 + jnp.exp(-x))


def _ada_kernel(c_ref, w_ref, b_ref, o_ref, *, batch):
    tn = w_ref.shape[1]
    rows = []
    for b in range(batch):
        c = c_ref[b]
        ca = c * _sigmoid(c)
        rows.append(jnp.concatenate(
            [jnp.sum(ca * w_ref[:, j * LANES:(j + 1) * LANES], axis=0, keepdims=True)
             for j in range(tn // LANES)], axis=1) + b_ref[...])
    rows.append(jnp.zeros((SUBLANES - batch, tn), F32))
    o_ref[...] = jnp.concatenate(rows, axis=0)


def _ada(c, w_ada, b_ada, tn=1024):
    b, d = c.shape
    n_out = w_ada.shape[1]
    assert b < SUBLANES
    c_lanes = jnp.broadcast_to(c[:, :, None], (b, d, LANES))
    out = pl.pallas_call(
        functools.partial(_ada_kernel, batch=b),
        grid=(n_out // tn,),
        in_specs=[pl.BlockSpec((b, d, LANES), lambda j: (0, 0, 0)),
                  pl.BlockSpec((d, tn), lambda j: (0, j)),
                  pl.BlockSpec((1, tn), lambda j: (0, j))],
        out_specs=pl.BlockSpec((SUBLANES, tn), lambda j: (0, j)),
        out_shape=jax.ShapeDtypeStruct((SUBLANES, n_out), F32),
        compiler_params=_cparams("arbitrary"),
        name="ada",
    )(c_lanes, w_ada, b_ada.reshape(1, n_out))
    return out[:b]


def _inproj_kernel(x_ref, sh_ref, sc_ref, w_ref, cs_ref, o_ref, h_ref):
    @pl.when(pl.program_id(1) == 0)
    def _():
        h = _ln_rows(x_ref[...]) * (1.0 + sc_ref[0]) + sh_ref[0]
        h_ref[...] = h.astype(BF16)

    acc = jnp.dot(h_ref[...], w_ref[...], preferred_element_type=F32)
    o_ref[...] = (acc * cs_ref[...]).astype(BF16)


def _inproj(x2, mod3, w_in_bf, colscale, seq, tm=1024, tn=1024):
    n, d = x2.shape
    n_cols = w_in_bf.shape[1]
    tm = min(tm, seq)
    return pl.pallas_call(
        _inproj_kernel,
        grid=(n // tm, n_cols // tn),
        in_specs=[pl.BlockSpec((tm, d), lambda i, j: (i, 0)),
                  pl.BlockSpec((1, 1, d), lambda i, j: (((i * tm) // seq) * 6 + 0, 0, 0)),
                  pl.BlockSpec((1, 1, d), lambda i, j: (((i * tm) // seq) * 6 + 1, 0, 0)),
                  pl.BlockSpec((d, tn), lambda i, j: (0, j)),
                  pl.BlockSpec((1, tn), lambda i, j: (0, j))],
        out_specs=pl.BlockSpec((tm, tn), lambda i, j: (i, j)),
        out_shape=jax.ShapeDtypeStruct((n, n_cols), BF16),
        scratch_shapes=[pltpu.VMEM((tm, d), BF16)],
        compiler_params=_cparams("arbitrary", "arbitrary"),
        name="inproj",
    )(x2, mod3, mod3, w_in_bf, colscale)


Z2_MAX = 100.0
STICK_EXIT_LOG2 = 150.0


def _attn_kernel(q_ref, k_ref, v_ref, o_ref, *, tq, tk, heads):
    qi = pl.program_id(2)
    nd = tq // tk
    jd = qi * nd
    trow = lax.broadcasted_iota(I32, (tk, tk), 0)
    tcol = lax.broadcasted_iota(I32, (tk, tk), 1)
    tri = (trow > tcol).astype(BF16)
    hd = SB_HEAD_DIM
    qs = [q_ref[0, :, h * hd:(h + 1) * hd] for h in range(heads)]

    def block(j, runs, accs, causal):
        off = pl.multiple_of(j * tk, tk)
        hs = range(heads)
        z2s = [lax.dot_general(qs[h], k_ref[0, pl.ds(off, tk), h * hd:(h + 1) * hd],
                               (((1,), (1,)), ((), ())), preferred_element_type=F32) for h in hs]
        z2s = [lax.clamp(-Z2_MAX, z2, Z2_MAX) for z2 in z2s]
        sps = [jnp.log(1.0 + jnp.exp2(z2)) for z2 in z2s]
        if causal is not None:
            sps = [jnp.where(causal, sp, 0.0) for sp in sps]
        laters = [jnp.dot(sp.astype(BF16), tri, preferred_element_type=F32) for sp in sps]
        ws = [jnp.exp2(z2s[h] - LOG2E * (sps[h] + (laters[h] + runs[h]))) for h in hs]
        if causal is not None:
            ws = [jnp.where(causal, w, 0.0) for w in ws]
        new_accs = [accs[h] + jnp.dot(ws[h].astype(BF16),
                                      v_ref[0, pl.ds(off, tk), h * hd:(h + 1) * hd],
                                      preferred_element_type=F32) for h in hs]
        new_runs = [runs[h] + jnp.sum(sps[h], axis=1, keepdims=True) for h in hs]
        return tuple(new_runs), tuple(new_accs)

    def mass_left(runs):
        low = runs[0]
        for r in runs[1:]:
            low = jnp.minimum(low, r)
        return (jnp.min(low) * LOG2E < STICK_EXIT_LOG2).astype(I32)

    def diag_body(t, carry):
        c = nd - 1 - t
        row = lax.broadcasted_iota(I32, (tq, tk), 0)
        col = lax.broadcasted_iota(I32, (tq, tk), 1)
        causal = col + c * tk < row
        return block(jd + c, carry[0], carry[1], causal)

    def more(carry):
        t, go, _, _ = carry
        return (t < jd) & (go > 0)

    def body(carry):
        t, _, runs, accs = carry
        runs, accs = block(jd - 1 - t, runs, accs, None)
        return t + 1, mass_left(runs), runs, accs

    runs = tuple(jnp.zeros((tq, 1), F32) for _ in range(heads))
    accs = tuple(jnp.zeros((tq, hd), F32) for _ in range(heads))
    runs, accs = lax.fori_loop(0, nd, diag_body, (runs, accs))
    _, _, runs, accs = lax.while_loop(more, body, (jnp.int32(0), mass_left(runs), runs, accs))
    for h in range(heads):
        o_ref[0, :, h * hd:(h + 1) * hd] = accs[h].astype(BF16)


def _attention(proj3, d, tq=256, tk=256, heads=4):
    b, s, _ = proj3.shape
    tk = min(tk, s)
    tq = min(tq, s)
    assert tq % tk == 0 and s % tq == 0
    width = heads * SB_HEAD_DIM
    qc = _COL_Q // width
    kc = _COL_K // width
    vc = _COL_V // width
    return pl.pallas_call(
        functools.partial(_attn_kernel, tq=tq, tk=tk, heads=heads),
        grid=(b, SB_HEADS // heads, s // tq),
        in_specs=[pl.BlockSpec((1, tq, width), lambda bi, g, qi: (bi, qi, qc + g)),
                  pl.BlockSpec((1, s, width), lambda bi, g, qi: (bi, 0, kc + g)),
                  pl.BlockSpec((1, s, width), lambda bi, g, qi: (bi, 0, vc + g))],
        out_specs=pl.BlockSpec((1, tq, width), lambda bi, g, qi: (bi, qi, g)),
        out_shape=jax.ShapeDtypeStruct((b, s, SB_WIDTH), BF16),
        compiler_params=_cparams("arbitrary", "arbitrary", "arbitrary"),
        name="attn",
    )(proj3, proj3, proj3)


def _erf(x):
    return lax.erf(x)


def _gelu(x):
    return 0.5 * x * (1.0 + _erf(x * (1.0 / math.sqrt(2.0))))


def _mix_kernel(oa_ref, su_ref, sv_ref, ga0_ref, ga1_ref, gb0_ref, gb1_ref, x_ref,
                sw_ref, sbias_ref, sg_ref, sb_ref,
                wpa_ref, wpb_ref, wo_ref,
                g1_ref, ln1g_ref, ln1b_ref, sh2_ref, sc2_ref,
                wr_ref, br_ref,
                x1_ref, h2_ref, route_ref, oh1_ref, oh2_ref, *, tm, alpha):
    u = _gelu(su_ref[...].astype(F32))
    v = _gelu(sv_ref[...].astype(F32))
    v = _ln_rows(v) * sg_ref[...] + sb_ref[...]
    vb = v.astype(BF16)
    pos_t = lax.broadcasted_iota(I32, (SGU_BLOCK, SGU_BLOCK), 0)
    pos_s = lax.broadcasted_iota(I32, (SGU_BLOCK, SGU_BLOCK), 1)
    chunk_causal = (pos_s // CHUNK) <= (pos_t // CHUNK)
    n_pos_blocks = tm // SGU_BLOCK
    per_group = []
    for g in range(SGU_GROUPS):
        c0 = g * SGU_GROUP_DIM
        wg = jnp.where(chunk_causal, sw_ref[g], 0.0).astype(BF16)
        rhs = jnp.concatenate(
            [vb[nb * SGU_BLOCK:(nb + 1) * SGU_BLOCK, c0:c0 + SGU_GROUP_DIM]
             for nb in range(n_pos_blocks)], axis=1)
        per_group.append(jnp.dot(wg, rhs, preferred_element_type=F32))
    mixed = jnp.concatenate(
        [jnp.concatenate([per_group[g][:, nb * SGU_GROUP_DIM:(nb + 1) * SGU_GROUP_DIM]
                          for g in range(SGU_GROUPS)], axis=1) + sbias_ref[...]
         for nb in range(n_pos_blocks)], axis=0)
    out_b = (u * mixed).astype(BF16)

    pa = jnp.dot(oa_ref[...], wpa_ref[...], preferred_element_type=F32)
    pb = jnp.dot(out_b, wpb_ref[...], preferred_element_type=F32)
    ga = jnp.concatenate([ga0_ref[...], ga1_ref[...]], axis=1).astype(F32)
    gb = jnp.concatenate([gb0_ref[...], gb1_ref[...]], axis=1).astype(F32)
    merged = _sigmoid(ga) * pa + _sigmoid(gb) * pb
    y = jnp.dot(merged.astype(BF16), wo_ref[...], preferred_element_type=F32)

    r = alpha * x_ref[...] + g1_ref[0] * y
    x1 = _ln_rows(r) * ln1g_ref[...] + ln1b_ref[...]
    x1_ref[...] = x1
    h2 = _ln_rows(x1) * (1.0 + sc2_ref[0]) + sh2_ref[0]
    chunks = h2.shape[1] // LANES
    for c in range(chunks):
        h2_ref[pl.ds(c, tm, stride=chunks), :] = h2[:, c * LANES:(c + 1) * LANES]

    h_hi = h2.astype(BF16)
    h_lo = (h2 - h_hi.astype(F32)).astype(BF16)
    hi_w = jnp.dot(h_hi, wr_ref[...], preferred_element_type=F32)
    lo_w = jnp.dot(h_lo, wr_ref[:, :LANES], preferred_element_type=F32)
    logits = (hi_w[:, :LANES] + hi_w[:, LANES:]) + lo_w + br_ref[...]
    lane = lax.broadcasted_iota(I32, logits.shape, 1)
    gl = jnp.where(lane < N_GROUPS, logits, NEG_BIG)
    gmax = jnp.max(gl, axis=1, keepdims=True)
    g_sel = jnp.min(jnp.where(gl == gmax, lane, LANES), axis=1, keepdims=True)
    p_group = 1.0 / jnp.sum(jnp.exp(gl - gmax), axis=1, keepdims=True)
    lo_lane = EXPERT_LANE0 + g_sel * EXPERTS_PER_GROUP
    in_group = (lane >= lo_lane) & (lane < lo_lane + EXPERTS_PER_GROUP)
    el = jnp.where(in_group, logits, NEG_BIG)
    m1 = jnp.max(el, axis=1, keepdims=True)
    i1 = jnp.min(jnp.where(el == m1, lane, LANES), axis=1, keepdims=True)
    el2 = jnp.where(lane == i1, NEG_BIG, el)
    m2 = jnp.max(el2, axis=1, keepdims=True)
    i2 = jnp.min(jnp.where(el2 == m2, lane, LANES), axis=1, keepdims=True)
    e21 = jnp.exp(m2 - m1)
    w1 = p_group / (1.0 + e21)
    w2 = p_group * e21 / (1.0 + e21)
    route = jnp.where(lane == 0, w1, 0.0)
    route = jnp.where(lane == 1, w2, route)
    route = jnp.where(lane == 2, (i1 - EXPERT_LANE0).astype(F32), route)
    route = jnp.where(lane == 3, (i2 - EXPERT_LANE0).astype(F32), route)
    route_ref[...] = route
    oh1_ref[...] = jnp.where(lane == i1, 1.0, 0.0).astype(BF16)
    oh2_ref[...] = jnp.where(lane == i2, 1.0, 0.0).astype(BF16)


def _mix(out_a, proj, x2, mod3, sgu_w, sgu_bias_full, sgu_g, sgu_b, wpa, wpb, wo,
         ln1_g, ln1_b, wr, br, seq, alpha, tm=256):
    n, d = x2.shape
    tm = min(tm, seq)
    def pcol(start, width):
        assert start % width == 0
        return start // width
    su_c = pcol(_COL_SU, SGU_WIDTH)
    sv_c = pcol(_COL_SV, SGU_WIDTH)
    half = d // 2
    ga_c = pcol(_COL_GA, half)
    gb_c = pcol(_COL_GB(d), half)
    bidx = lambda i: (i * tm) // seq
    row = lambda i: (i, 0)
    const2 = lambda i: (0, 0)
    const3 = lambda i: (0, 0, 0)
    return pl.pallas_call(
        functools.partial(_mix_kernel, tm=tm, alpha=alpha),
        grid=(n // tm,),
        in_specs=[pl.BlockSpec((tm, SB_WIDTH), row),
                  pl.BlockSpec((tm, SGU_WIDTH), lambda i: (i, su_c)),
                  pl.BlockSpec((tm, SGU_WIDTH), lambda i: (i, sv_c)),
                  pl.BlockSpec((tm, half), lambda i: (i, ga_c)),
                  pl.BlockSpec((tm, half), lambda i: (i, ga_c + 1)),
                  pl.BlockSpec((tm, half), lambda i: (i, gb_c)),
                  pl.BlockSpec((tm, half), lambda i: (i, gb_c + 1)),
                  pl.BlockSpec((tm, d), row),
                  pl.BlockSpec((SGU_GROUPS, SGU_BLOCK, SGU_BLOCK), const3),
                  pl.BlockSpec((SGU_BLOCK, SGU_WIDTH), const2),
                  pl.BlockSpec((1, SGU_WIDTH), const2),
                  pl.BlockSpec((1, SGU_WIDTH), const2),
                  pl.BlockSpec((SB_WIDTH, d), const2),
                  pl.BlockSpec((SGU_WIDTH, d), const2),
                  pl.BlockSpec((d, d), const2),
                  pl.BlockSpec((1, 1, d), lambda i: (bidx(i) * 6 + 2, 0, 0)),
                  pl.BlockSpec((1, d), const2),
                  pl.BlockSpec((1, d), const2),
                  pl.BlockSpec((1, 1, d), lambda i: (bidx(i) * 6 + 3, 0, 0)),
                  pl.BlockSpec((1, 1, d), lambda i: (bidx(i) * 6 + 4, 0, 0)),
                  pl.BlockSpec((d, 2 * LANES), const2),
                  pl.BlockSpec((1, LANES), const2)],
        out_specs=[pl.BlockSpec((tm, d), row),
                   pl.BlockSpec((tm * (d // LANES), LANES), row),
                   pl.BlockSpec((tm, LANES), row),
                   pl.BlockSpec((tm, LANES), row),
                   pl.BlockSpec((tm, LANES), row)],
        out_shape=[jax.ShapeDtypeStruct((n, d), F32),
                   jax.ShapeDtypeStruct((n * (d // LANES), LANES), F32),
                   jax.ShapeDtypeStruct((n, LANES), F32),
                   jax.ShapeDtypeStruct((n, LANES), BF16),
                   jax.ShapeDtypeStruct((n, LANES), BF16)],
        compiler_params=_cparams("arbitrary"),
        name="mix",
    )(out_a, proj, proj, proj, proj, proj, proj, x2, sgu_w, sgu_bias_full, sgu_g, sgu_b,
      wpa, wpb, wo, mod3, ln1_g, ln1_b, mod3, mod3, wr, br)


def _plan_kernel(oh1_ref, oh2_ref, dest_ref, blk_ref, cnt_ref, start_ref, *, tm, bm, nblk_pad):
    phase = pl.program_id(0)
    i = pl.program_id(1)
    oh1 = oh1_ref[...]
    oh2 = oh2_ref[...]
    both = oh1 + oh2
    lane = lax.broadcasted_iota(I32, (SUBLANES, LANES), 1)

    @pl.when((phase == 0) & (i == 0))
    def _():
        cnt_ref[...] = jnp.zeros_like(cnt_ref)

    @pl.when(phase == 0)
    def _():
        ones = jnp.ones((SUBLANES, tm), BF16)
        cnt_ref[...] += jnp.dot(ones, both, preferred_element_type=F32)

    @pl.when((phase == 1) & (i == 0))
    def _():
        cnt = cnt_ref[...]
        padded = jnp.floor((cnt + (bm - 1)) * (1.0 / bm)) * bm
        r = lax.broadcasted_iota(I32, (LANES, LANES), 0)
        c = lax.broadcasted_iota(I32, (LANES, LANES), 1)
        upper = (r < c).astype(F32)
        starts = jnp.dot(padded, upper, preferred_element_type=F32,
                         precision=lax.Precision.HIGHEST)
        start_ref[...] = starts
        cnt_ref[...] = jnp.zeros_like(cnt_ref)
        ends = starts + padded
        is_exp = (lane[0:1] >= EXPERT_LANE0) & (lane[0:1] < EXPERT_LANE0 + N_EXPERTS)
        bstart = (lax.broadcasted_iota(I32, (nblk_pad, LANES), 0) * bm).astype(F32)
        done = jnp.where(is_exp & (ends[0:1] <= bstart), 1.0, 0.0)
        bexp = jnp.minimum(jnp.sum(done, axis=1, keepdims=True), N_EXPERTS - 1.0)
        total = jnp.sum(jnp.where(is_exp, padded[0:1], 0.0), axis=1, keepdims=True)
        blane = lax.broadcasted_iota(I32, (nblk_pad, LANES), 1)
        blk = jnp.where(blane == 0, bexp, jnp.where(blane == 1, total * (1.0 / bm), 0.0))
        blk_ref[...] = blk.astype(I32)

    @pl.when(phase == 1)
    def _():
        r = lax.broadcasted_iota(I32, (tm, tm), 0)
        c = lax.broadcasted_iota(I32, (tm, tm), 1)
        lower = (c < r).astype(BF16)
        before = jnp.dot(lower, both, preferred_element_type=F32) + cnt_ref[0:1]
        pos = before + start_ref[0:1]
        d1 = jnp.sum(pos * oh1.astype(F32), axis=1, keepdims=True)
        d2 = jnp.sum(pos * oh2.astype(F32), axis=1, keepdims=True)
        dlane = lax.broadcasted_iota(I32, (tm, LANES), 1)
        dest = jnp.where(dlane == 0, d1, jnp.where(dlane == 1, d2, 0.0))
        dest_ref[...] = dest.astype(I32)
        ones = jnp.ones((SUBLANES, tm), BF16)
        cnt_ref[...] += jnp.dot(ones, both, preferred_element_type=F32)


def _plan(oh1, oh2, bm, nblk, tm=1024):
    n = oh1.shape[0]
    tm = min(tm, n)
    nblk_pad = -(-nblk // SUBLANES) * SUBLANES
    dest, blk = pl.pallas_call(
        functools.partial(_plan_kernel, tm=tm, bm=bm, nblk_pad=nblk_pad),
        grid=(2, n // tm),
        in_specs=[pl.BlockSpec((tm, LANES), lambda p, i: (i, 0)),
                  pl.BlockSpec((tm, LANES), lambda p, i: (i, 0))],
        out_specs=[pl.BlockSpec((tm, LANES), lambda p, i: (i * p, 0)),
                   pl.BlockSpec((nblk_pad, LANES), lambda p, i: (0, 0))],
        out_shape=[jax.ShapeDtypeStruct((n, LANES), I32),
                   jax.ShapeDtypeStruct((nblk_pad, LANES), I32)],
        scratch_shapes=[pltpu.VMEM((SUBLANES, LANES), F32),
                        pltpu.VMEM((SUBLANES, LANES), F32)],
        compiler_params=_cparams("arbitrary", "arbitrary"),
        name="plan",
    )(oh1, oh2)
    return dest, blk


ROW_UNROLL = 8


def _invert_kernel(d1_ref, d2_ref, zero_ref, rt_ref, sem, *, tb):
    i = pl.program_id(0)

    @pl.when(i == 0)
    def _():
        clear = pltpu.make_async_copy(zero_ref, rt_ref, sem)
        clear.start()
        clear.wait()

    def put(j, carry):
        for u in range(ROW_UNROLL):
            t = j * ROW_UNROLL + u
            rt_ref[d1_ref[0, 0, t]] = i * tb + t
            rt_ref[d2_ref[0, 0, t]] = i * tb + t
        return carry

    lax.fori_loop(0, tb // ROW_UNROLL, put, 0)


def _invert(dest1, dest2, n_rows, tb=2048):
    n = dest1.shape[0]
    tb = min(tb, n)
    assert n % tb == 0 and tb % ROW_UNROLL == 0
    blocks = lambda i: (i, 0, 0)
    return pl.pallas_call(
        functools.partial(_invert_kernel, tb=tb),
        grid=(n // tb,),
        in_specs=[pl.BlockSpec((1, 1, tb), blocks, memory_space=pltpu.SMEM),
                  pl.BlockSpec((1, 1, tb), blocks, memory_space=pltpu.SMEM),
                  pl.BlockSpec(memory_space=pl.ANY)],
        out_specs=pl.BlockSpec(memory_space=pltpu.SMEM),
        out_shape=jax.ShapeDtypeStruct((n_rows,), I32),
        scratch_shapes=[pltpu.SemaphoreType.DMA(())],
        compiler_params=_cparams("arbitrary"),
        name="invert",
    )(dest1.reshape(n // tb, 1, tb), dest2.reshape(n // tb, 1, tb), jnp.zeros((n_rows,), I32))


def _slab(idx, chunks):
    return pl.ds(pl.multiple_of(idx * chunks, chunks), chunks)


def _rows_from_slabs(buf_ref, slot, rows, chunks):
    return jnp.concatenate(
        [buf_ref[slot, pl.ds(c, rows, stride=chunks), :] for c in range(chunks)], axis=1)


GATHER_AHEAD = 6


def _experts_kernel(be_ref, na_ref, first_ref, wslot_ref, nxt_ref, *refs, bm, chunks):
    head_refs = refs[:GATHER_AHEAD]
    (rtn_ref, h_ref, wg_hbm, wu_hbm, wd_hbm, o_ref,
     xbuf, sem, wgf, wuf, wdf, wsem, wgb, wub, wdb) = refs[GATHER_AHEAD:]
    i = pl.program_id(0)
    n_active = na_ref[0]

    def gather(tok_ref, slot):
        def trip(r8, carry):
            for u in range(ROW_UNROLL):
                r = r8 * ROW_UNROLL + u
                pltpu.make_async_copy(h_ref.at[_slab(tok_ref[0, 0, r], chunks), :],
                                      xbuf.at[slot, _slab(r, chunks), :], sem.at[slot]).start()
            return carry
        lax.fori_loop(0, bm // ROW_UNROLL, trip, 0)

    def weight_copies(e, ws):
        return (pltpu.make_async_copy(wg_hbm.at[e], wgf.at[ws], wsem.at[ws]),
                pltpu.make_async_copy(wu_hbm.at[e], wuf.at[ws], wsem.at[ws]),
                pltpu.make_async_copy(wd_hbm.at[e], wdf.at[ws], wsem.at[ws]))

    def fetch_weights(e, ws):
        for cp in weight_copies(e, ws):
            cp.start(priority=1)

    def gather_wait(slot):
        pltpu.make_async_copy(h_ref.at[pl.ds(0, bm * chunks), :], xbuf.at[slot],
                              sem.at[slot]).wait()

    @pl.when(i == 0)
    def _():
        fetch_weights(be_ref[0], wslot_ref[0])
        gather(head_refs[0], 0)
        for k in range(1, GATHER_AHEAD):
            pl.when(k < n_active)(functools.partial(gather, head_refs[k], k))

    @pl.when(i < n_active)
    def _():
        slot = lax.rem(i, GATHER_AHEAD + 1)
        gather_wait(slot)

        @pl.when(i + GATHER_AHEAD < n_active)
        def _():
            gather(rtn_ref, lax.rem(i + GATHER_AHEAD, GATHER_AHEAD + 1))

        @pl.when(first_ref[i] == 1)
        def _():
            ws = wslot_ref[i]
            for cp in weight_copies(be_ref[i], ws):
                cp.wait()

            @pl.when(nxt_ref[i] >= 0)
            def _():
                fetch_weights(nxt_ref[i], 1 - ws)

            wgb[...] = wgf[ws].astype(BF16)
            wub[...] = wuf[ws].astype(BF16)
            wdb[...] = wdf[ws].astype(BF16)

        x = _rows_from_slabs(xbuf, slot, bm, chunks).astype(BF16)
        g = jnp.dot(x, wgb[...], preferred_element_type=F32)
        u = jnp.dot(x, wub[...], preferred_element_type=F32)
        hb = (g * _sigmoid(g) * u).astype(BF16)
        y = jnp.dot(hb, wdb[...], preferred_element_type=F32)
        for c in range(chunks):
            o_ref[pl.ds(c, bm, stride=chunks), :] = y[:, c * LANES:(c + 1) * LANES]

    @pl.when(i >= n_active)
    def _():
        o_ref[...] = jnp.zeros_like(o_ref)


def _experts(block_expert, n_active, row_tok, h2s, w_gate, w_up, w_down, bm):
    n_rows = row_tok.shape[0]
    nblk = n_rows // bm
    d, de = w_gate.shape[1], w_gate.shape[2]
    chunks = d // LANES
    assert bm % ROW_UNROLL == 0
    assert nblk > GATHER_AHEAD
    row_blocks = row_tok.reshape(nblk, 1, bm)

    idx = jnp.arange(nblk, dtype=I32)
    prev = jnp.concatenate([block_expert[:1] - 1, block_expert[:-1]])
    first = ((block_expert != prev) & (idx < n_active[0])).astype(I32)
    wslot = ((jnp.cumsum(first) - 1) % 2).astype(I32)
    run_start = lax.cummin(jnp.where(first > 0, idx, nblk), axis=0, reverse=True)
    next_start = jnp.concatenate([run_start[1:], jnp.full((1,), nblk, I32)])
    nxt = jnp.where(next_start < nblk,
                    block_expert[jnp.minimum(next_start, nblk - 1)], -1).astype(I32)

    return pl.pallas_call(
        functools.partial(_experts_kernel, bm=bm, chunks=chunks),
        grid_spec=pltpu.PrefetchScalarGridSpec(
            num_scalar_prefetch=5,
            grid=(nblk,),
            in_specs=[pl.BlockSpec((1, 1, bm), functools.partial(lambda k, i, *_: (k, 0, 0), k),
                                   memory_space=pltpu.SMEM) for k in range(GATHER_AHEAD)]
                     + [pl.BlockSpec((1, 1, bm),
                                     lambda i, *_: (jnp.minimum(i + GATHER_AHEAD, nblk - 1), 0, 0),
                                     memory_space=pltpu.SMEM)]
                     + [pl.BlockSpec(memory_space=pl.ANY)] * 4,
            out_specs=pl.BlockSpec((bm * chunks, LANES), lambda i, *_: (i, 0)),
            scratch_shapes=[pltpu.VMEM((GATHER_AHEAD + 1, bm * chunks, LANES), F32),
                            pltpu.SemaphoreType.DMA((GATHER_AHEAD + 1,)),
                            pltpu.VMEM((2, d, de), F32),
                            pltpu.VMEM((2, d, de), F32),
                            pltpu.VMEM((2, de, d), F32),
                            pltpu.SemaphoreType.DMA((2,)),
                            pltpu.VMEM((d, de), BF16),
                            pltpu.VMEM((d, de), BF16),
                            pltpu.VMEM((de, d), BF16)],
        ),
        out_shape=jax.ShapeDtypeStruct((n_rows * chunks, LANES), F32),
        compiler_params=_cparams("arbitrary"),
        name="experts",
    )(block_expert, n_active, first, wslot, nxt, *([row_blocks] * (GATHER_AHEAD + 1)),
      h2s, w_gate, w_up, w_down)


def _combine_kernel(d1_ref, d2_ref, ys_ref, x1_ref, route_ref, g2_ref, lg_ref, lb_ref,
                    o_ref, y1buf, y2buf, sem, *, tm, chunks, alpha):
    i = pl.program_id(0)
    n_tiles = pl.num_programs(0)

    def row_copies(tile, t, slot):
        tok = tile * tm + t
        return (pltpu.make_async_copy(ys_ref.at[_slab(d1_ref[tok], chunks), :],
                                      y1buf.at[slot, _slab(t, chunks), :], sem.at[slot]),
                pltpu.make_async_copy(ys_ref.at[_slab(d2_ref[tok], chunks), :],
                                      y2buf.at[slot, _slab(t, chunks), :], sem.at[slot]))

    def gather(tile, slot):
        def trip(t8, carry):
            for u in range(ROW_UNROLL):
                for k, cp in enumerate(row_copies(tile, t8 * ROW_UNROLL + u, slot)):
                    cp.start(priority=k)
            return carry
        lax.fori_loop(0, tm // ROW_UNROLL, trip, 0)

    def gather_wait(slot):
        for buf in (y1buf, y2buf):
            pltpu.make_async_copy(ys_ref.at[pl.ds(0, tm * chunks), :], buf.at[slot],
                                  sem.at[slot]).wait()

    @pl.when(i == 0)
    def _():
        gather(0, 0)

    slot = lax.rem(i, 2)
    gather_wait(slot)

    @pl.when(i + 1 < n_tiles)
    def _():
        gather(i + 1, 1 - slot)

    route = route_ref[...]
    y = (route[:, 0:1] * _rows_from_slabs(y1buf, slot, tm, chunks)
         + route[:, 1:2] * _rows_from_slabs(y2buf, slot, tm, chunks))
    r = alpha * x1_ref[...] + g2_ref[0] * y
    o_ref[...] = _ln_rows(r) * lg_ref[...] + lb_ref[...]


def _combine(dest1, dest2, ys, x1, route, mod3, ln2_g, ln2_b, seq, alpha, tm=256):
    n, d = x1.shape
    tm = min(tm, seq)
    chunks = d // LANES
    assert tm % ROW_UNROLL == 0
    tok = lambda i, d1, d2: (i, 0)
    const = lambda i, d1, d2: (0, 0)
    return pl.pallas_call(
        functools.partial(_combine_kernel, tm=tm, chunks=chunks, alpha=alpha),
        grid_spec=pltpu.PrefetchScalarGridSpec(
            num_scalar_prefetch=2,
            grid=(n // tm,),
            in_specs=[pl.BlockSpec(memory_space=pl.ANY),
                      pl.BlockSpec((tm, d), tok),
                      pl.BlockSpec((tm, LANES), tok),
                      pl.BlockSpec((1, 1, d), lambda i, d1, d2: (((i * tm) // seq) * 6 + 5, 0, 0)),
                      pl.BlockSpec((1, d), const),
                      pl.BlockSpec((1, d), const)],
            out_specs=pl.BlockSpec((tm, d), tok),
            scratch_shapes=[pltpu.VMEM((2, tm * chunks, LANES), F32),
                            pltpu.VMEM((2, tm * chunks, LANES), F32),
                            pltpu.SemaphoreType.DMA((2,))],
        ),
        out_shape=jax.ShapeDtypeStruct((n, d), F32),
        compiler_params=_cparams("arbitrary"),
        name="combine",
    )(dest1, dest2, ys, x1, route, mod3, ln2_g, ln2_b)


EXPERT_BLOCK_ROWS = 256


def kernel(x, c, w_ada, b_ada, w_in, sgu_w, sgu_b, sgu_ln_g, sgu_ln_b, w_proj_a, w_proj_b,
           w_out, ln1_g, ln1_b, w_group, b_group, w_router, b_router, w_gate, w_up, w_down,
           ln2_g, ln2_b):
    b, s, d = x.shape
    n = b * s
    depth = w_ada.shape[0]
    alpha = (2.0 * depth) ** 0.25
    sub = d // LANES
    bm = EXPERT_BLOCK_ROWS
    n_assign = 2 * n
    n_rows = n_assign + N_EXPERTS * bm
    nblk = n_rows // bm

    in_cols = w_in.shape[2]
    colscale = jnp.ones((1, in_cols), F32).at[:, _COL_Q:_COL_K].set(
        SB_HEAD_DIM ** -0.5 * LOG2E)

    for l in range(depth):
        mod = _ada(c, w_ada[l], b_ada[l])
        mod3 = mod.reshape(b * 6, 1, d)

        x2 = x.reshape(n, d)
        proj = _inproj(x2, mod3, w_in[l].astype(BF16), colscale, s)
        out_a = _attention(proj.reshape(b, s, in_cols), d).reshape(n, SB_WIDTH)

        wr = jnp.zeros((d, LANES), F32)
        wr = wr.at[:, GROUP_LANE0:GROUP_LANE0 + N_GROUPS].set(w_group[l])
        wr = wr.at[:, EXPERT_LANE0:EXPERT_LANE0 + N_EXPERTS].set(w_router[l])
        wr_hi = wr.astype(BF16)
        wr_lo = (wr - wr_hi.astype(F32)).astype(BF16)
        wr = jnp.concatenate([wr_hi, wr_lo], axis=1)
        br = jnp.zeros((1, LANES), F32)
        br = br.at[0, GROUP_LANE0:GROUP_LANE0 + N_GROUPS].set(b_group[l])
        br = br.at[0, EXPERT_LANE0:EXPERT_LANE0 + N_EXPERTS].set(b_router[l])
        sgu_bias_full = jnp.repeat(sgu_b[l].T, SGU_GROUP_DIM, axis=1)

        x1, h2s, route, oh1, oh2 = _mix(
            out_a, proj, x2, mod3, sgu_w[l], sgu_bias_full,
            sgu_ln_g[l].reshape(1, -1), sgu_ln_b[l].reshape(1, -1),
            w_proj_a[l].astype(BF16), w_proj_b[l].astype(BF16), w_out[l].astype(BF16),
            ln1_g[l].reshape(1, d), ln1_b[l].reshape(1, d), wr, br, s, alpha)

        dest, blk = _plan(oh1, oh2, bm, nblk)
        dest1 = dest[:, 0]
        dest2 = dest[:, 1]
        block_expert = blk[:nblk, 0]
        n_active = blk[0:1, 1]

        row_tok = _invert(dest1, dest2, n_rows)
        ys = _experts(block_expert, n_active, row_tok, h2s, w_gate[l], w_up[l], w_down[l], bm)
        out = _combine(dest1, dest2, ys, x1, route, mod3, ln2_g[l].reshape(1, d),
                       ln2_b[l].reshape(1, d), s, alpha)
        x = out.reshape(b, s, d)
    return x
```

```python
import functools
import math

import jax
import jax.numpy as jnp
from jax import lax
from jax.experimental import pallas as pl
from jax.experimental.pallas import tpu as pltpu

F32 = jnp.float32
BF16 = jnp.bfloat16
I32 = jnp.int32

CHUNK = 64
SB_HEADS = 8
SB_HEAD_DIM = 128
SB_WIDTH = SB_HEADS * SB_HEAD_DIM
SGU_GROUPS = 8
SGU_GROUP_DIM = 128
SGU_WIDTH = SGU_GROUPS * SGU_GROUP_DIM
SGU_BLOCK = 128
N_GROUPS = 4
EXPERTS_PER_GROUP = 8
N_EXPERTS = N_GROUPS * EXPERTS_PER_GROUP
LN_EPS = 1e-5

LANES = 128
SUBLANES = 8
VMEM_LIMIT_BYTES = 60000 * 1024

GROUP_LANE0 = 0
EXPERT_LANE0 = N_GROUPS

LOG2E = 1.4426950408889634
NEG_BIG = -1e30

_COL_Q = 0
_COL_K = SB_WIDTH
_COL_V = 2 * SB_WIDTH
_COL_SU = 3 * SB_WIDTH
_COL_SV = 3 * SB_WIDTH + SGU_WIDTH
_COL_GA = 3 * SB_WIDTH + 2 * SGU_WIDTH


def _COL_GB(d):
    return _COL_GA + d


def _cparams(*sem):
    return pltpu.CompilerParams(dimension_semantics=sem, vmem_limit_bytes=VMEM_LIMIT_BYTES)


def _ln_rows(x):
    mu = jnp.mean(x, axis=-1, keepdims=True)
    xc = x - mu
    var = jnp.mean(xc * xc, axis=-1, keepdims=True)
    return xc * lax.rsqrt(var + LN_EPS)


def _sigmoid(x):
    return 1.0 / (1.0 + jnp.exp(-x))


def _ada_kernel(c_ref, w_ref, b_ref, o_ref, *, batch):
    tn = w_ref.shape[1]
    rows = []
    for b in range(batch):
        c = c_ref[b]
        ca = c * _sigmoid(c)
        rows.append(jnp.concatenate(
            [jnp.sum(ca * w_ref[:, j * LANES:(j + 1) * LANES], axis=0, keepdims=True)
             for j in range(tn // LANES)], axis=1) + b_ref[...])
    rows.append(jnp.zeros((SUBLANES - batch, tn), F32))
    o_ref[...] = jnp.concatenate(rows, axis=0)


def _ada(c, w_ada, b_ada, tn=1024):
    b, d = c.shape
    n_out = w_ada.shape[1]
    assert b < SUBLANES
    c_lanes = jnp.broadcast_to(c[:, :, None], (b, d, LANES))
    out = pl.pallas_call(
        functools.partial(_ada_kernel, batch=b),
        grid=(n_out // tn,),
        in_specs=[pl.BlockSpec((b, d, LANES), lambda j: (0, 0, 0)),
                  pl.BlockSpec((d, tn), lambda j: (0, j)),
                  pl.BlockSpec((1, tn), lambda j: (0, j))],
        out_specs=pl.BlockSpec((SUBLANES, tn), lambda j: (0, j)),
        out_shape=jax.ShapeDtypeStruct((SUBLANES, n_out), F32),
        compiler_params=_cparams("arbitrary"),
        name="ada",
    )(c_lanes, w_ada, b_ada.reshape(1, n_out))
    return out[:b]


def _inproj_kernel(x_ref, sh_ref, sc_ref, w_ref, cs_ref, o_ref, h_ref):
    @pl.when(pl.program_id(1) == 0)
    def _():
        h = _ln_rows(x_ref[...]) * (1.0 + sc_ref[0]) + sh_ref[0]
        h_ref[...] = h.astype(BF16)

    acc = jnp.dot(h_ref[...], w_ref[...], preferred_element_type=F32)
    o_ref[...] = (acc * cs_ref[...]).astype(BF16)


def _inproj(x2, mod3, w_in_bf, colscale, seq, tm=1024, tn=1536):
    n, d = x2.shape
    n_cols = w_in_bf.shape[1]
    tm = min(tm, seq)
    return pl.pallas_call(
        _inproj_kernel,
        grid=(n // tm, n_cols // tn),
        in_specs=[pl.BlockSpec((tm, d), lambda i, j: (i, 0)),
                  pl.BlockSpec((1, 1, d), lambda i, j: (((i * tm) // seq) * 6 + 0, 0, 0)),
                  pl.BlockSpec((1, 1, d), lambda i, j: (((i * tm) // seq) * 6 + 1, 0, 0)),
                  pl.BlockSpec((d, tn), lambda i, j: (0, j)),
                  pl.BlockSpec((1, tn), lambda i, j: (0, j))],
        out_specs=pl.BlockSpec((tm, tn), lambda i, j: (i, j)),
        out_shape=jax.ShapeDtypeStruct((n, n_cols), BF16),
        scratch_shapes=[pltpu.VMEM((tm, d), BF16)],
        compiler_params=_cparams("arbitrary", "arbitrary"),
        name="inproj",
    )(x2, mod3, mod3, w_in_bf, colscale)


Z2_MAX = 100.0
STICK_EXIT_LOG2 = 150.0


def _attn_kernel(q_ref, k_ref, v_ref, o_ref, *, tq, tk, heads):
    qi = pl.program_id(2)
    nd = tq // tk
    jd = qi * nd
    trow = lax.broadcasted_iota(I32, (tk, tk), 0)
    tcol = lax.broadcasted_iota(I32, (tk, tk), 1)
    tri = (trow > tcol).astype(BF16)
    hd = SB_HEAD_DIM
    qs = [q_ref[0, :, h * hd:(h + 1) * hd] for h in range(heads)]

    def block(j, runs, accs, causal):
        off = pl.multiple_of(j * tk, tk)
        hs = range(heads)
        z2s = [lax.dot_general(qs[h], k_ref[0, pl.ds(off, tk), h * hd:(h + 1) * hd],
                               (((1,), (1,)), ((), ())), preferred_element_type=F32) for h in hs]
        z2s = [lax.clamp(-Z2_MAX, z2, Z2_MAX) for z2 in z2s]
        sps = [jnp.log(1.0 + jnp.exp2(z2)) for z2 in z2s]
        if causal is not None:
            sps = [jnp.where(causal, sp, 0.0) for sp in sps]
        laters = [jnp.dot(sp.astype(BF16), tri, preferred_element_type=F32) for sp in sps]
        ws = [jnp.exp2(z2s[h] - LOG2E * (sps[h] + (laters[h] + runs[h]))) for h in hs]
        if causal is not None:
            ws = [jnp.where(causal, w, 0.0) for w in ws]
        new_accs = [accs[h] + jnp.dot(ws[h].astype(BF16),
                                      v_ref[0, pl.ds(off, tk), h * hd:(h + 1) * hd],
                                      preferred_element_type=F32) for h in hs]
        new_runs = [runs[h] + jnp.sum(sps[h], axis=1, keepdims=True) for h in hs]
        return tuple(new_runs), tuple(new_accs)

    def mass_left(runs):
        low = runs[0]
        for r in runs[1:]:
            low = jnp.minimum(low, r)
        return (jnp.min(low) * LOG2E < STICK_EXIT_LOG2).astype(I32)

    def diag_body(t, carry):
        c = nd - 1 - t
        row = lax.broadcasted_iota(I32, (tq, tk), 0)
        col = lax.broadcasted_iota(I32, (tq, tk), 1)
        causal = col + c * tk < row
        return block(jd + c, carry[0], carry[1], causal)

    def more(carry):
        t, go, _, _ = carry
        return (t < jd) & (go > 0)

    def body(carry):
        t, _, runs, accs = carry
        runs, accs = block(jd - 1 - t, runs, accs, None)
        return t + 1, mass_left(runs), runs, accs

    runs = tuple(jnp.zeros((tq, 1), F32) for _ in range(heads))
    accs = tuple(jnp.zeros((tq, hd), F32) for _ in range(heads))
    runs, accs = lax.fori_loop(0, nd, diag_body, (runs, accs))
    _, _, runs, accs = lax.while_loop(more, body, (jnp.int32(0), mass_left(runs), runs, accs))
    for h in range(heads):
        o_ref[0, :, h * hd:(h + 1) * hd] = accs[h].astype(BF16)


def _attention(proj3, d, tq=256, tk=256, heads=4):
    b, s, _ = proj3.shape
    tk = min(tk, s)
    tq = min(tq, s)
    assert tq % tk == 0 and s % tq == 0
    width = heads * SB_HEAD_DIM
    qc = _COL_Q // width
    kc = _COL_K // width
    vc = _COL_V // width
    return pl.pallas_call(
        functools.partial(_attn_kernel, tq=tq, tk=tk, heads=heads),
        grid=(b, SB_HEADS // heads, s // tq),
        in_specs=[pl.BlockSpec((1, tq, width), lambda bi, g, qi: (bi, qi, qc + g)),
                  pl.BlockSpec((1, s, width), lambda bi, g, qi: (bi, 0, kc + g)),
                  pl.BlockSpec((1, s, width), lambda bi, g, qi: (bi, 0, vc + g))],
        out_specs=pl.BlockSpec((1, tq, width), lambda bi, g, qi: (bi, qi, g)),
        out_shape=jax.ShapeDtypeStruct((b, s, SB_WIDTH), BF16),
        compiler_params=_cparams("arbitrary", "arbitrary", "arbitrary"),
        name="attn",
    )(proj3, proj3, proj3)


def _erf(x):
    return lax.erf(x)


def _gelu(x):
    return 0.5 * x * (1.0 + _erf(x * (1.0 / math.sqrt(2.0))))


def _mix_kernel(oa_ref, su_ref, sv_ref, ga0_ref, ga1_ref, gb0_ref, gb1_ref, x_ref,
                sw_ref, sbias_ref, sg_ref, sb_ref,
                wpa_ref, wpb_ref, wo_ref,
                g1_ref, ln1g_ref, ln1b_ref, sh2_ref, sc2_ref,
                wr_ref, br_ref,
                x1_ref, h2_ref, route_ref, oh1_ref, oh2_ref, *, tm, alpha):
    pa = jnp.dot(oa_ref[...], wpa_ref[...], preferred_element_type=F32)

    u = _gelu(su_ref[...].astype(F32))
    v = _gelu(sv_ref[...].astype(F32))
    v = _ln_rows(v) * sg_ref[...] + sb_ref[...]
    vb = v.astype(BF16)
    pos_t = lax.broadcasted_iota(I32, (SGU_BLOCK, SGU_BLOCK), 0)
    pos_s = lax.broadcasted_iota(I32, (SGU_BLOCK, SGU_BLOCK), 1)
    chunk_causal = (pos_s // CHUNK) <= (pos_t // CHUNK)
    n_pos_blocks = tm // SGU_BLOCK
    per_group = []
    for g in range(SGU_GROUPS):
        c0 = g * SGU_GROUP_DIM
        wg = jnp.where(chunk_causal, sw_ref[g], 0.0).astype(BF16)
        rhs = jnp.concatenate(
            [vb[nb * SGU_BLOCK:(nb + 1) * SGU_BLOCK, c0:c0 + SGU_GROUP_DIM]
             for nb in range(n_pos_blocks)], axis=1)
        per_group.append(jnp.dot(wg, rhs, preferred_element_type=F32))
    mixed = jnp.concatenate(
        [jnp.concatenate([per_group[g][:, nb * SGU_GROUP_DIM:(nb + 1) * SGU_GROUP_DIM]
                          for g in range(SGU_GROUPS)], axis=1) + sbias_ref[...]
         for nb in range(n_pos_blocks)], axis=0)
    out_b = (u * mixed).astype(BF16)

    pb = jnp.dot(out_b, wpb_ref[...], preferred_element_type=F32)
    ga = jnp.concatenate([ga0_ref[...], ga1_ref[...]], axis=1).astype(F32)
    gb = jnp.concatenate([gb0_ref[...], gb1_ref[...]], axis=1).astype(F32)
    merged = _sigmoid(ga) * pa + _sigmoid(gb) * pb
    y = jnp.dot(merged.astype(BF16), wo_ref[...], preferred_element_type=F32)

    r = alpha * x_ref[...] + g1_ref[0] * y
    x1 = _ln_rows(r) * ln1g_ref[...] + ln1b_ref[...]
    x1_ref[...] = x1
    h2 = _ln_rows(x1) * (1.0 + sc2_ref[0]) + sh2_ref[0]
    chunks = h2.shape[1] // LANES
    for c in range(chunks):
        h2_ref[pl.ds(c, tm, stride=chunks), :] = h2[:, c * LANES:(c + 1) * LANES]

    h_hi = h2.astype(BF16)
    h_lo = (h2 - h_hi.astype(F32)).astype(BF16)
    hi_w = jnp.dot(h_hi, wr_ref[...], preferred_element_type=F32)
    lo_w = jnp.dot(h_lo, wr_ref[:, :LANES], preferred_element_type=F32)
    logits = (hi_w[:, :LANES] + hi_w[:, LANES:]) + lo_w + br_ref[...]
    lane = lax.broadcasted_iota(I32, logits.shape, 1)
    gl = jnp.where(lane < N_GROUPS, logits, NEG_BIG)
    gmax = jnp.max(gl, axis=1, keepdims=True)
    g_sel = jnp.min(jnp.where(gl == gmax, lane, LANES), axis=1, keepdims=True)
    p_group = 1.0 / jnp.sum(jnp.exp(gl - gmax), axis=1, keepdims=True)
    lo_lane = EXPERT_LANE0 + g_sel * EXPERTS_PER_GROUP
    in_group = (lane >= lo_lane) & (lane < lo_lane + EXPERTS_PER_GROUP)
    el = jnp.where(in_group, logits, NEG_BIG)
    m1 = jnp.max(el, axis=1, keepdims=True)
    i1 = jnp.min(jnp.where(el == m1, lane, LANES), axis=1, keepdims=True)
    el2 = jnp.where(lane == i1, NEG_BIG, el)
    m2 = jnp.max(el2, axis=1, keepdims=True)
    i2 = jnp.min(jnp.where(el2 == m2, lane, LANES), axis=1, keepdims=True)
    e21 = jnp.exp(m2 - m1)
    w1 = p_group / (1.0 + e21)
    w2 = p_group * e21 / (1.0 + e21)
    route = jnp.where(lane == 0, w1, 0.0)
    route = jnp.where(lane == 1, w2, route)
    route = jnp.where(lane == 2, (i1 - EXPERT_LANE0).astype(F32), route)
    route = jnp.where(lane == 3, (i2 - EXPERT_LANE0).astype(F32), route)
    route_ref[...] = route
    oh1_ref[...] = jnp.where(lane == i1, 1.0, 0.0).astype(BF16)
    oh2_ref[...] = jnp.where(lane == i2, 1.0, 0.0).astype(BF16)


def _mix(out_a, proj, x2, mod3, sgu_w, sgu_bias_full, sgu_g, sgu_b, wpa, wpb, wo,
         ln1_g, ln1_b, wr, br, seq, alpha, tm=256):
    n, d = x2.shape
    tm = min(tm, seq)
    def pcol(start, width):
        assert start % width == 0
        return start // width
    su_c = pcol(_COL_SU, SGU_WIDTH)
    sv_c = pcol(_COL_SV, SGU_WIDTH)
    half = d // 2
    ga_c = pcol(_COL_GA, half)
    gb_c = pcol(_COL_GB(d), half)
    bidx = lambda i: (i * tm) // seq
    row = lambda i: (i, 0)
    const2 = lambda i: (0, 0)
    const3 = lambda i: (0, 0, 0)
    return pl.pallas_call(
        functools.partial(_mix_kernel, tm=tm, alpha=alpha),
        grid=(n // tm,),
        in_specs=[pl.BlockSpec((tm, SB_WIDTH), row),
                  pl.BlockSpec((tm, SGU_WIDTH), lambda i: (i, su_c)),
                  pl.BlockSpec((tm, SGU_WIDTH), lambda i: (i, sv_c)),
                  pl.BlockSpec((tm, half), lambda i: (i, ga_c)),
                  pl.BlockSpec((tm, half), lambda i: (i, ga_c + 1)),
                  pl.BlockSpec((tm, half), lambda i: (i, gb_c)),
                  pl.BlockSpec((tm, half), lambda i: (i, gb_c + 1)),
                  pl.BlockSpec((tm, d), row),
                  pl.BlockSpec((SGU_GROUPS, SGU_BLOCK, SGU_BLOCK), const3),
                  pl.BlockSpec((SGU_BLOCK, SGU_WIDTH), const2),
                  pl.BlockSpec((1, SGU_WIDTH), const2),
                  pl.BlockSpec((1, SGU_WIDTH), const2),
                  pl.BlockSpec((SB_WIDTH, d), const2),
                  pl.BlockSpec((SGU_WIDTH, d), const2),
                  pl.BlockSpec((d, d), const2),
                  pl.BlockSpec((1, 1, d), lambda i: (bidx(i) * 6 + 2, 0, 0)),
                  pl.BlockSpec((1, d), const2),
                  pl.BlockSpec((1, d), const2),
                  pl.BlockSpec((1, 1, d), lambda i: (bidx(i) * 6 + 3, 0, 0)),
                  pl.BlockSpec((1, 1, d), lambda i: (bidx(i) * 6 + 4, 0, 0)),
                  pl.BlockSpec((d, 2 * LANES), const2),
                  pl.BlockSpec((1, LANES), const2)],
        out_specs=[pl.BlockSpec((tm, d), row),
                   pl.BlockSpec((tm * (d // LANES), LANES), row),
                   pl.BlockSpec((tm, LANES), row),
                   pl.BlockSpec((tm, LANES), row),
                   pl.BlockSpec((tm, LANES), row)],
        out_shape=[jax.ShapeDtypeStruct((n, d), F32),
                   jax.ShapeDtypeStruct((n * (d // LANES), LANES), F32),
                   jax.ShapeDtypeStruct((n, LANES), F32),
                   jax.ShapeDtypeStruct((n, LANES), BF16),
                   jax.ShapeDtypeStruct((n, LANES), BF16)],
        compiler_params=_cparams("arbitrary"),
        name="mix",
    )(out_a, proj, proj, proj, proj, proj, proj, x2, sgu_w, sgu_bias_full, sgu_g, sgu_b,
      wpa, wpb, wo, mod3, ln1_g, ln1_b, mod3, mod3, wr, br)


def _plan_kernel(oh1_ref, oh2_ref, dest_ref, blk_ref, cnt_ref, start_ref, *, tm, bm, nblk_pad):
    phase = pl.program_id(0)
    i = pl.program_id(1)
    oh1 = oh1_ref[...]
    oh2 = oh2_ref[...]
    both = oh1 + oh2
    lane = lax.broadcasted_iota(I32, (SUBLANES, LANES), 1)

    @pl.when((phase == 0) & (i == 0))
    def _():
        cnt_ref[...] = jnp.zeros_like(cnt_ref)

    @pl.when(phase == 0)
    def _():
        ones = jnp.ones((SUBLANES, tm), BF16)
        cnt_ref[...] += jnp.dot(ones, both, preferred_element_type=F32)

    @pl.when((phase == 1) & (i == 0))
    def _():
        cnt = cnt_ref[...]
        padded = jnp.floor((cnt + (bm - 1)) * (1.0 / bm)) * bm
        r = lax.broadcasted_iota(I32, (LANES, LANES), 0)
        c = lax.broadcasted_iota(I32, (LANES, LANES), 1)
        upper = (r < c).astype(F32)
        starts = jnp.dot(padded, upper, preferred_element_type=F32,
                         precision=lax.Precision.HIGHEST)
        start_ref[...] = starts
        cnt_ref[...] = jnp.zeros_like(cnt_ref)
        ends = starts + padded
        is_exp = (lane[0:1] >= EXPERT_LANE0) & (lane[0:1] < EXPERT_LANE0 + N_EXPERTS)
        bstart = (lax.broadcasted_iota(I32, (nblk_pad, LANES), 0) * bm).astype(F32)
        done = jnp.where(is_exp & (ends[0:1] <= bstart), 1.0, 0.0)
        bexp = jnp.minimum(jnp.sum(done, axis=1, keepdims=True), N_EXPERTS - 1.0)
        total = jnp.sum(jnp.where(is_exp, padded[0:1], 0.0), axis=1, keepdims=True)
        blane = lax.broadcasted_iota(I32, (nblk_pad, LANES), 1)
        blk = jnp.where(blane == 0, bexp, jnp.where(blane == 1, total * (1.0 / bm), 0.0))
        blk_ref[...] = blk.astype(I32)

    @pl.when(phase == 1)
    def _():
        r = lax.broadcasted_iota(I32, (tm, tm), 0)
        c = lax.broadcasted_iota(I32, (tm, tm), 1)
        lower = (c < r).astype(BF16)
        before = jnp.dot(lower, both, preferred_element_type=F32) + cnt_ref[0:1]
        pos = before + start_ref[0:1]
        d1 = jnp.sum(pos * oh1.astype(F32), axis=1, keepdims=True)
        d2 = jnp.sum(pos * oh2.astype(F32), axis=1, keepdims=True)
        dlane = lax.broadcasted_iota(I32, (tm, LANES), 1)
        dest = jnp.where(dlane == 0, d1, jnp.where(dlane == 1, d2, 0.0))
        dest_ref[...] = dest.astype(I32)
        ones = jnp.ones((SUBLANES, tm), BF16)
        cnt_ref[...] += jnp.dot(ones, both, preferred_element_type=F32)


def _plan(oh1, oh2, bm, nblk, tm=1024):
    n = oh1.shape[0]
    tm = min(tm, n)
    nblk_pad = -(-nblk // SUBLANES) * SUBLANES
    dest, blk = pl.pallas_call(
        functools.partial(_plan_kernel, tm=tm, bm=bm, nblk_pad=nblk_pad),
        grid=(2, n // tm),
        in_specs=[pl.BlockSpec((tm, LANES), lambda p, i: (i, 0)),
                  pl.BlockSpec((tm, LANES), lambda p, i: (i, 0))],
        out_specs=[pl.BlockSpec((tm, LANES), lambda p, i: (i * p, 0)),
                   pl.BlockSpec((nblk_pad, LANES), lambda p, i: (0, 0))],
        out_shape=[jax.ShapeDtypeStruct((n, LANES), I32),
                   jax.ShapeDtypeStruct((nblk_pad, LANES), I32)],
        scratch_shapes=[pltpu.VMEM((SUBLANES, LANES), F32),
                        pltpu.VMEM((SUBLANES, LANES), F32)],
        compiler_params=_cparams("arbitrary", "arbitrary"),
        name="plan",
    )(oh1, oh2)
    return dest, blk


ROW_UNROLL = 8


def _invert_kernel(d1_ref, d2_ref, zero_ref, rt_ref, sem, *, tb):
    i = pl.program_id(0)

    @pl.when(i == 0)
    def _():
        clear = pltpu.make_async_copy(zero_ref, rt_ref, sem)
        clear.start()
        clear.wait()

    def put(j, carry):
        for u in range(ROW_UNROLL):
            t = j * ROW_UNROLL + u
            rt_ref[d1_ref[0, 0, t]] = i * tb + t
            rt_ref[d2_ref[0, 0, t]] = i * tb + t
        return carry

    lax.fori_loop(0, tb // ROW_UNROLL, put, 0)


def _invert(dest1, dest2, n_rows, tb=2048):
    n = dest1.shape[0]
    tb = min(tb, n)
    assert n % tb == 0 and tb % ROW_UNROLL == 0
    blocks = lambda i: (i, 0, 0)
    return pl.pallas_call(
        functools.partial(_invert_kernel, tb=tb),
        grid=(n // tb,),
        in_specs=[pl.BlockSpec((1, 1, tb), blocks, memory_space=pltpu.SMEM),
                  pl.BlockSpec((1, 1, tb), blocks, memory_space=pltpu.SMEM),
                  pl.BlockSpec(memory_space=pl.ANY)],
        out_specs=pl.BlockSpec(memory_space=pltpu.SMEM),
        out_shape=jax.ShapeDtypeStruct((n_rows,), I32),
        scratch_shapes=[pltpu.SemaphoreType.DMA(())],
        compiler_params=_cparams("arbitrary"),
        name="invert",
    )(dest1.reshape(n // tb, 1, tb), dest2.reshape(n // tb, 1, tb), jnp.zeros((n_rows,), I32))


def _slab(idx, chunks):
    return pl.ds(pl.multiple_of(idx * chunks, chunks), chunks)


def _rows_from_slabs(buf_ref, slot, rows, chunks):
    return jnp.concatenate(
        [buf_ref[slot, pl.ds(c, rows, stride=chunks), :] for c in range(chunks)], axis=1)


GATHER_AHEAD = 6


def _experts_kernel(be_ref, na_ref, first_ref, wslot_ref, nxt_ref, *refs, bm, chunks):
    head_refs = refs[:GATHER_AHEAD]
    (rtn_ref, h_ref, wg_hbm, wu_hbm, wd_hbm, o_ref,
     xbuf, sem, wgf, wuf, wdf, wsem, wgb, wub, wdb) = refs[GATHER_AHEAD:]
    i = pl.program_id(0)
    n_active = na_ref[0]

    def gather(tok_ref, slot):
        def trip(r8, carry):
            for u in range(ROW_UNROLL):
                r = r8 * ROW_UNROLL + u
                pltpu.make_async_copy(h_ref.at[_slab(tok_ref[0, 0, r], chunks), :],
                                      xbuf.at[slot, _slab(r, chunks), :], sem.at[slot]).start()
            return carry
        lax.fori_loop(0, bm // ROW_UNROLL, trip, 0)

    def weight_copies(e, ws):
        return (pltpu.make_async_copy(wg_hbm.at[e], wgf.at[ws], wsem.at[ws]),
                pltpu.make_async_copy(wu_hbm.at[e], wuf.at[ws], wsem.at[ws]),
                pltpu.make_async_copy(wd_hbm.at[e], wdf.at[ws], wsem.at[ws]))

    def fetch_weights(e, ws):
        for cp in weight_copies(e, ws):
            cp.start(priority=1)

    def gather_wait(slot):
        pltpu.make_async_copy(h_ref.at[pl.ds(0, bm * chunks), :], xbuf.at[slot],
                              sem.at[slot]).wait()

    @pl.when(i == 0)
    def _():
        fetch_weights(be_ref[0], wslot_ref[0])
        gather(head_refs[0], 0)
        for k in range(1, GATHER_AHEAD):
            pl.when(k < n_active)(functools.partial(gather, head_refs[k], k))

    @pl.when(i < n_active)
    def _():
        slot = lax.rem(i, GATHER_AHEAD + 1)
        gather_wait(slot)

        @pl.when(i + GATHER_AHEAD < n_active)
        def _():
            gather(rtn_ref, lax.rem(i + GATHER_AHEAD, GATHER_AHEAD + 1))

        @pl.when(first_ref[i] == 1)
        def _():
            ws = wslot_ref[i]
            for cp in weight_copies(be_ref[i], ws):
                cp.wait()

            @pl.when(nxt_ref[i] >= 0)
            def _():
                fetch_weights(nxt_ref[i], 1 - ws)

            wgb[...] = wgf[ws].astype(BF16)
            wub[...] = wuf[ws].astype(BF16)
            wdb[...] = wdf[ws].astype(BF16)

        x = _rows_from_slabs(xbuf, slot, bm, chunks).astype(BF16)
        g = jnp.dot(x, wgb[...], preferred_element_type=F32)
        u = jnp.dot(x, wub[...], preferred_element_type=F32)
        hb = (g * _sigmoid(g) * u).astype(BF16)
        y = jnp.dot(hb, wdb[...], preferred_element_type=F32)
        for c in range(chunks):
            o_ref[pl.ds(c, bm, stride=chunks), :] = y[:, c * LANES:(c + 1) * LANES]

    @pl.when(i >= n_active)
    def _():
        o_ref[...] = jnp.zeros_like(o_ref)


def _experts(block_expert, n_active, row_tok, h2s, w_gate, w_up, w_down, bm):
    n_rows = row_tok.shape[0]
    nblk = n_rows // bm
    d, de = w_gate.shape[1], w_gate.shape[2]
    chunks = d // LANES
    assert bm % ROW_UNROLL == 0
    assert nblk > GATHER_AHEAD
    row_blocks = row_tok.reshape(nblk, 1, bm)

    idx = jnp.arange(nblk, dtype=I32)
    prev = jnp.concatenate([block_expert[:1] - 1, block_expert[:-1]])
    first = ((block_expert != prev) & (idx < n_active[0])).astype(I32)
    wslot = ((jnp.cumsum(first) - 1) % 2).astype(I32)
    run_start = lax.cummin(jnp.where(first > 0, idx, nblk), axis=0, reverse=True)
    next_start = jnp.concatenate([run_start[1:], jnp.full((1,), nblk, I32)])
    nxt = jnp.where(next_start < nblk,
                    block_expert[jnp.minimum(next_start, nblk - 1)], -1).astype(I32)

    return pl.pallas_call(
        functools.partial(_experts_kernel, bm=bm, chunks=chunks),
        grid_spec=pltpu.PrefetchScalarGridSpec(
            num_scalar_prefetch=5,
            grid=(nblk,),
            in_specs=[pl.BlockSpec((1, 1, bm), functools.partial(lambda k, i, *_: (k, 0, 0), k),
                                   memory_space=pltpu.SMEM) for k in range(GATHER_AHEAD)]
                     + [pl.BlockSpec((1, 1, bm),
                                     lambda i, *_: (jnp.minimum(i + GATHER_AHEAD, nblk - 1), 0, 0),
                                     memory_space=pltpu.SMEM)]
                     + [pl.BlockSpec(memory_space=pl.ANY)] * 4,
            out_specs=pl.BlockSpec((bm * chunks, LANES), lambda i, *_: (i, 0)),
            scratch_shapes=[pltpu.VMEM((GATHER_AHEAD + 1, bm * chunks, LANES), F32),
                            pltpu.SemaphoreType.DMA((GATHER_AHEAD + 1,)),
                            pltpu.VMEM((2, d, de), F32),
                            pltpu.VMEM((2, d, de), F32),
                            pltpu.VMEM((2, de, d), F32),
                            pltpu.SemaphoreType.DMA((2,)),
                            pltpu.VMEM((d, de), BF16),
                            pltpu.VMEM((d, de), BF16),
                            pltpu.VMEM((de, d), BF16)],
        ),
        out_shape=jax.ShapeDtypeStruct((n_rows * chunks, LANES), F32),
        compiler_params=_cparams("arbitrary"),
        name="experts",
    )(block_expert, n_active, first, wslot, nxt, *([row_blocks] * (GATHER_AHEAD + 1)),
      h2s, w_gate, w_up, w_down)


def _combine_kernel(d1_ref, d2_ref, ys_ref, x1_ref, route_ref, g2_ref, lg_ref, lb_ref,
                    o_ref, y1buf, y2buf, sem, *, tm, chunks, alpha):
    i = pl.program_id(0)
    n_tiles = pl.num_programs(0)

    def row_copies(tile, t, slot):
        tok = tile * tm + t
        return (pltpu.make_async_copy(ys_ref.at[_slab(d1_ref[tok], chunks), :],
                                      y1buf.at[slot, _slab(t, chunks), :], sem.at[slot]),
                pltpu.make_async_copy(ys_ref.at[_slab(d2_ref[tok], chunks), :],
                                      y2buf.at[slot, _slab(t, chunks), :], sem.at[slot]))

    def gather(tile, slot):
        def trip(t8, carry):
            for u in range(ROW_UNROLL):
                for k, cp in enumerate(row_copies(tile, t8 * ROW_UNROLL + u, slot)):
                    cp.start(priority=k)
            return carry
        lax.fori_loop(0, tm // ROW_UNROLL, trip, 0)

    def gather_wait(slot):
        for buf in (y1buf, y2buf):
            pltpu.make_async_copy(ys_ref.at[pl.ds(0, tm * chunks), :], buf.at[slot],
                                  sem.at[slot]).wait()

    @pl.when(i == 0)
    def _():
        gather(0, 0)

    slot = lax.rem(i, 2)
    gather_wait(slot)

    @pl.when(i + 1 < n_tiles)
    def _():
        gather(i + 1, 1 - slot)

    route = route_ref[...]
    y = (route[:, 0:1] * _rows_from_slabs(y1buf, slot, tm, chunks)
         + route[:, 1:2] * _rows_from_slabs(y2buf, slot, tm, chunks))
    r = alpha * x1_ref[...] + g2_ref[0] * y
    o_ref[...] = _ln_rows(r) * lg_ref[...] + lb_ref[...]


def _combine(dest1, dest2, ys, x1, route, mod3, ln2_g, ln2_b, seq, alpha, tm=256):
    n, d = x1.shape
    tm = min(tm, seq)
    chunks = d // LANES
    assert tm % ROW_UNROLL == 0
    tok = lambda i, d1, d2: (i, 0)
    const = lambda i, d1, d2: (0, 0)
    return pl.pallas_call(
        functools.partial(_combine_kernel, tm=tm, chunks=chunks, alpha=alpha),
        grid_spec=pltpu.PrefetchScalarGridSpec(
            num_scalar_prefetch=2,
            grid=(n // tm,),
            in_specs=[pl.BlockSpec(memory_space=pl.ANY),
                      pl.BlockSpec((tm, d), tok),
                      pl.BlockSpec((tm, LANES), tok),
                      pl.BlockSpec((1, 1, d), lambda i, d1, d2: (((i * tm) // seq) * 6 + 5, 0, 0)),
                      pl.BlockSpec((1, d), const),
                      pl.BlockSpec((1, d), const)],
            out_specs=pl.BlockSpec((tm, d), tok),
            scratch_shapes=[pltpu.VMEM((2, tm * chunks, LANES), F32),
                            pltpu.VMEM((2, tm * chunks, LANES), F32),
                            pltpu.SemaphoreType.DMA((2,))],
        ),
        out_shape=jax.ShapeDtypeStruct((n, d), F32),
        compiler_params=_cparams("arbitrary"),
        name="combine",
    )(dest1, dest2, ys, x1, route, mod3, ln2_g, ln2_b)


EXPERT_BLOCK_ROWS = 256


def kernel(x, c, w_ada, b_ada, w_in, sgu_w, sgu_b, sgu_ln_g, sgu_ln_b, w_proj_a, w_proj_b,
           w_out, ln1_g, ln1_b, w_group, b_group, w_router, b_router, w_gate, w_up, w_down,
           ln2_g, ln2_b):
    b, s, d = x.shape
    n = b * s
    depth = w_ada.shape[0]
    alpha = (2.0 * depth) ** 0.25
    sub = d // LANES
    bm = EXPERT_BLOCK_ROWS
    n_assign = 2 * n
    n_rows = n_assign + N_EXPERTS * bm
    nblk = n_rows // bm

    in_cols = w_in.shape[2]
    colscale = jnp.ones((1, in_cols), F32).at[:, _COL_Q:_COL_K].set(
        SB_HEAD_DIM ** -0.5 * LOG2E)

    for l in range(depth):
        mod = _ada(c, w_ada[l], b_ada[l])
        mod3 = mod.reshape(b * 6, 1, d)

        x2 = x.reshape(n, d)
        proj = _inproj(x2, mod3, w_in[l].astype(BF16), colscale, s)
        out_a = _attention(proj.reshape(b, s, in_cols), d).reshape(n, SB_WIDTH)

        wr = jnp.zeros((d, LANES), F32)
        wr = wr.at[:, GROUP_LANE0:GROUP_LANE0 + N_GROUPS].set(w_group[l])
        wr = wr.at[:, EXPERT_LANE0:EXPERT_LANE0 + N_EXPERTS].set(w_router[l])
        wr_hi = wr.astype(BF16)
        wr_lo = (wr - wr_hi.astype(F32)).astype(BF16)
        wr = jnp.concatenate([wr_hi, wr_lo], axis=1)
        br = jnp.zeros((1, LANES), F32)
        br = br.at[0, GROUP_LANE0:GROUP_LANE0 + N_GROUPS].set(b_group[l])
        br = br.at[0, EXPERT_LANE0:EXPERT_LANE0 + N_EXPERTS].set(b_router[l])
        sgu_bias_full = jnp.repeat(sgu_b[l].T, SGU_GROUP_DIM, axis=1)

        x1, h2s, route, oh1, oh2 = _mix(
            out_a, proj, x2, mod3, sgu_w[l], sgu_bias_full,
            sgu_ln_g[l].reshape(1, -1), sgu_ln_b[l].reshape(1, -1),
            w_proj_a[l].astype(BF16), w_proj_b[l].astype(BF16), w_out[l].astype(BF16),
            ln1_g[l].reshape(1, d), ln1_b[l].reshape(1, d), wr, br, s, alpha)

        dest, blk = _plan(oh1, oh2, bm, nblk)
        dest1 = dest[:, 0]
        dest2 = dest[:, 1]
        block_expert = blk[:nblk, 0]
        n_active = blk[0:1, 1]

        row_tok = _invert(dest1, dest2, n_rows)
        ys = _experts(block_expert, n_active, row_tok, h2s, w_gate[l], w_up[l], w_down[l], bm)
        out = _combine(dest1, dest2, ys, x1, route, mod3, ln2_g[l].reshape(1, d),
                       ln2_b[l].reshape(1, d), s, alpha)
        x = out.reshape(b, s, d)
    return x
```

```python
import functools
import math

import jax
import jax.numpy as jnp
from jax import lax
from jax.experimental import pallas as pl
from jax.experimental.pallas import tpu as pltpu

F32 = jnp.float32
BF16 = jnp.bfloat16
I32 = jnp.int32

CHUNK = 64
SB_HEADS = 8
SB_HEAD_DIM = 128
SB_WIDTH = SB_HEADS * SB_HEAD_DIM
SGU_GROUPS = 8
SGU_GROUP_DIM = 128
SGU_WIDTH = SGU_GROUPS * SGU_GROUP_DIM
SGU_BLOCK = 128
N_GROUPS = 4
EXPERTS_PER_GROUP = 8
N_EXPERTS = N_GROUPS * EXPERTS_PER_GROUP
LN_EPS = 1e-5

LANES = 128
SUBLANES = 8
VMEM_LIMIT_BYTES = 60000 * 1024

GROUP_LANE0 = 0
EXPERT_LANE0 = N_GROUPS

LOG2E = 1.4426950408889634
NEG_BIG = -1e30

_COL_Q = 0
_COL_K = SB_WIDTH
_COL_V = 2 * SB_WIDTH
_COL_SU = 3 * SB_WIDTH
_COL_SV = 3 * SB_WIDTH + SGU_WIDTH
_COL_GA = 3 * SB_WIDTH + 2 * SGU_WIDTH


def _COL_GB(d):
    return _COL_GA + d


def _cparams(*sem):
    return pltpu.CompilerParams(dimension_semantics=sem, vmem_limit_bytes=VMEM_LIMIT_BYTES)


def _ln_rows(x):
    mu = jnp.mean(x, axis=-1, keepdims=True)
    xc = x - mu
    var = jnp.mean(xc * xc, axis=-1, keepdims=True)
    return xc * lax.rsqrt(var + LN_EPS)


def _sigmoid(x):
    return 1.0 / (1.0 + jnp.exp(-x))


def _ada_kernel(c_ref, w_ref, b_ref, o_ref, *, batch):
    tn = w_ref.shape[1]
    rows = []
    for b in range(batch):
        c = c_ref[b]
        ca = c * _sigmoid(c)
        rows.append(jnp.concatenate(
            [jnp.sum(ca * w_ref[:, j * LANES:(j + 1) * LANES], axis=0, keepdims=True)
             for j in range(tn // LANES)], axis=1) + b_ref[...])
    rows.append(jnp.zeros((SUBLANES - batch, tn), F32))
    o_ref[...] = jnp.concatenate(rows, axis=0)


def _ada(c, w_ada, b_ada, tn=1024):
    b, d = c.shape
    n_out = w_ada.shape[1]
    assert b < SUBLANES
    c_lanes = jnp.broadcast_to(c[:, :, None], (b, d, LANES))
    out = pl.pallas_call(
        functools.partial(_ada_kernel, batch=b),
        grid=(n_out // tn,),
        in_specs=[pl.BlockSpec((b, d, LANES), lambda j: (0, 0, 0)),
                  pl.BlockSpec((d, tn), lambda j: (0, j)),
                  pl.BlockSpec((1, tn), lambda j: (0, j))],
        out_specs=pl.BlockSpec((SUBLANES, tn), lambda j: (0, j)),
        out_shape=jax.ShapeDtypeStruct((SUBLANES, n_out), F32),
        compiler_params=_cparams("arbitrary"),
        name="ada",
    )(c_lanes, w_ada, b_ada.reshape(1, n_out))
    return out[:b]


def _inproj_kernel(x_ref, sh_ref, sc_ref, w_ref, cs_ref, o_ref, h_ref):
    @pl.when(pl.program_id(1) == 0)
    def _():
        h = _ln_rows(x_ref[...]) * (1.0 + sc_ref[0]) + sh_ref[0]
        h_ref[...] = h.astype(BF16)

    acc = jnp.dot(h_ref[...], w_ref[...], preferred_element_type=F32)
    o_ref[...] = (acc * cs_ref[...]).astype(BF16)


def _inproj(x2, mod3, w_in_bf, colscale, seq, tm=1024, tn=1536):
    n, d = x2.shape
    n_cols = w_in_bf.shape[1]
    tm = min(tm, seq)
    return pl.pallas_call(
        _inproj_kernel,
        grid=(n // tm, n_cols // tn),
        in_specs=[pl.BlockSpec((tm, d), lambda i, j: (i, 0)),
                  pl.BlockSpec((1, 1, d), lambda i, j: (((i * tm) // seq) * 6 + 0, 0, 0)),
                  pl.BlockSpec((1, 1, d), lambda i, j: (((i * tm) // seq) * 6 + 1, 0, 0)),
                  pl.BlockSpec((d, tn), lambda i, j: (0, j)),
                  pl.BlockSpec((1, tn), lambda i, j: (0, j))],
        out_specs=pl.BlockSpec((tm, tn), lambda i, j: (i, j)),
        out_shape=jax.ShapeDtypeStruct((n, n_cols), BF16),
        scratch_shapes=[pltpu.VMEM((tm, d), BF16)],
        compiler_params=_cparams("arbitrary", "arbitrary"),
        name="inproj",
    )(x2, mod3, mod3, w_in_bf, colscale)


Z2_MAX = 100.0
STICK_EXIT_LOG2 = 150.0


def _attn_kernel(q_ref, k_ref, v_ref, o_ref, *, tq, tk, heads):
    qi = pl.program_id(2)
    nd = tq // tk
    jd = qi * nd
    trow = lax.broadcasted_iota(I32, (tk, tk), 0)
    tcol = lax.broadcasted_iota(I32, (tk, tk), 1)
    tri = (trow > tcol).astype(BF16)
    hd = SB_HEAD_DIM
    qs = [q_ref[0, :, h * hd:(h + 1) * hd] for h in range(heads)]

    def block(j, runs, accs, causal):
        off = pl.multiple_of(j * tk, tk)
        hs = range(heads)
        z2s = [lax.dot_general(qs[h], k_ref[0, pl.ds(off, tk), h * hd:(h + 1) * hd],
                               (((1,), (1,)), ((), ())), preferred_element_type=F32) for h in hs]
        z2s = [lax.clamp(-Z2_MAX, z2, Z2_MAX) for z2 in z2s]
        sps = [jnp.log(1.0 + jnp.exp2(z2)) for z2 in z2s]
        if causal is not None:
            sps = [jnp.where(causal, sp, 0.0) for sp in sps]
        laters = [jnp.dot(sp.astype(BF16), tri, preferred_element_type=F32) for sp in sps]
        ws = [jnp.exp2(z2s[h] - LOG2E * (sps[h] + (laters[h] + runs[h]))) for h in hs]
        if causal is not None:
            ws = [jnp.where(causal, w, 0.0) for w in ws]
        new_accs = [accs[h] + jnp.dot(ws[h].astype(BF16),
                                      v_ref[0, pl.ds(off, tk), h * hd:(h + 1) * hd],
                                      preferred_element_type=F32) for h in hs]
        new_runs = [runs[h] + jnp.sum(sps[h], axis=1, keepdims=True) for h in hs]
        return tuple(new_runs), tuple(new_accs)

    def mass_left(runs):
        low = runs[0]
        for r in runs[1:]:
            low = jnp.minimum(low, r)
        return (jnp.min(low) * LOG2E < STICK_EXIT_LOG2).astype(I32)

    def diag_body(t, carry):
        c = nd - 1 - t
        row = lax.broadcasted_iota(I32, (tq, tk), 0)
        col = lax.broadcasted_iota(I32, (tq, tk), 1)
        causal = col + c * tk < row
        return block(jd + c, carry[0], carry[1], causal)

    def more(carry):
        t, go, _, _ = carry
        return (t < jd) & (go > 0)

    def body(carry):
        t, _, runs, accs = carry
        runs, accs = block(jd - 1 - t, runs, accs, None)
        return t + 1, mass_left(runs), runs, accs

    runs = tuple(jnp.zeros((tq, 1), F32) for _ in range(heads))
    accs = tuple(jnp.zeros((tq, hd), F32) for _ in range(heads))
    runs, accs = lax.fori_loop(0, nd, diag_body, (runs, accs))
    _, _, runs, accs = lax.while_loop(more, body, (jnp.int32(0), mass_left(runs), runs, accs))
    for h in range(heads):
        o_ref[0, :, h * hd:(h + 1) * hd] = accs[h].astype(BF16)


def _attention(proj3, d, tq=256, tk=256, heads=4):
    b, s, _ = proj3.shape
    tk = min(tk, s)
    tq = min(tq, s)
    assert tq % tk == 0 and s % tq == 0
    width = heads * SB_HEAD_DIM
    qc = _COL_Q // width
    kc = _COL_K // width
    vc = _COL_V // width
    return pl.pallas_call(
        functools.partial(_attn_kernel, tq=tq, tk=tk, heads=heads),
        grid=(b, SB_HEADS // heads, s // tq),
        in_specs=[pl.BlockSpec((1, tq, width), lambda bi, g, qi: (bi, qi, qc + g)),
                  pl.BlockSpec((1, s, width), lambda bi, g, qi: (bi, 0, kc + g)),
                  pl.BlockSpec((1, s, width), lambda bi, g, qi: (bi, 0, vc + g))],
        out_specs=pl.BlockSpec((1, tq, width), lambda bi, g, qi: (bi, qi, g)),
        out_shape=jax.ShapeDtypeStruct((b, s, SB_WIDTH), BF16),
        compiler_params=_cparams("arbitrary", "arbitrary", "arbitrary"),
        name="attn",
    )(proj3, proj3, proj3)


def _erf(x):
    return lax.erf(x)


def _gelu(x):
    return 0.5 * x * (1.0 + _erf(x * (1.0 / math.sqrt(2.0))))


def _mix_kernel(oa_ref, su_ref, sv_ref, ga0_ref, ga1_ref, gb0_ref, gb1_ref, x_ref,
                sw_ref, sbias_ref, sg_ref, sb_ref,
                wpa_ref, wpb_ref, wo_ref,
                g1_ref, ln1g_ref, ln1b_ref, sh2_ref, sc2_ref,
                wr_ref, br_ref,
                x1_ref, h2_ref, route_ref, oh1_ref, oh2_ref, *, tm, alpha):
    u = _gelu(su_ref[...].astype(F32))
    v = _gelu(sv_ref[...].astype(F32))
    v = _ln_rows(v) * sg_ref[...] + sb_ref[...]
    vb = v.astype(BF16)
    pos_t = lax.broadcasted_iota(I32, (SGU_BLOCK, SGU_BLOCK), 0)
    pos_s = lax.broadcasted_iota(I32, (SGU_BLOCK, SGU_BLOCK), 1)
    chunk_causal = (pos_s // CHUNK) <= (pos_t // CHUNK)
    n_pos_blocks = tm // SGU_BLOCK
    per_group = []
    for g in range(SGU_GROUPS):
        c0 = g * SGU_GROUP_DIM
        wg = jnp.where(chunk_causal, sw_ref[g], 0.0).astype(BF16)
        rhs = jnp.concatenate(
            [vb[nb * SGU_BLOCK:(nb + 1) * SGU_BLOCK, c0:c0 + SGU_GROUP_DIM]
             for nb in range(n_pos_blocks)], axis=1)
        per_group.append(jnp.dot(wg, rhs, preferred_element_type=F32))
    mixed = jnp.concatenate(
        [jnp.concatenate([per_group[g][:, nb * SGU_GROUP_DIM:(nb + 1) * SGU_GROUP_DIM]
                          for g in range(SGU_GROUPS)], axis=1) + sbias_ref[...]
         for nb in range(n_pos_blocks)], axis=0)
    out_b = (u * mixed).astype(BF16)

    pa = jnp.dot(oa_ref[...], wpa_ref[...], preferred_element_type=F32)
    pb = jnp.dot(out_b, wpb_ref[...], preferred_element_type=F32)
    ga = jnp.concatenate([ga0_ref[...], ga1_ref[...]], axis=1).astype(F32)
    gb = jnp.concatenate([gb0_ref[...], gb1_ref[...]], axis=1).astype(F32)
    merged = _sigmoid(ga) * pa + _sigmoid(gb) * pb
    y = jnp.dot(merged.astype(BF16), wo_ref[...], preferred_element_type=F32)

    r = alpha * x_ref[...] + g1_ref[0] * y
    x1 = _ln_rows(r) * ln1g_ref[...] + ln1b_ref[...]
    x1_ref[...] = x1
    h2 = _ln_rows(x1) * (1.0 + sc2_ref[0]) + sh2_ref[0]
    chunks = h2.shape[1] // LANES
    for c in range(chunks):
        h2_ref[pl.ds(c, tm, stride=chunks), :] = h2[:, c * LANES:(c + 1) * LANES]

    h_hi = h2.astype(BF16)
    h_lo = (h2 - h_hi.astype(F32)).astype(BF16)
    hi_w = jnp.dot(h_hi, wr_ref[...], preferred_element_type=F32)
    lo_w = jnp.dot(h_lo, wr_ref[:, :LANES], preferred_element_type=F32)
    logits = (hi_w[:, :LANES] + hi_w[:, LANES:]) + lo_w + br_ref[...]
    lane = lax.broadcasted_iota(I32, logits.shape, 1)
    gl = jnp.where(lane < N_GROUPS, logits, NEG_BIG)
    gmax = jnp.max(gl, axis=1, keepdims=True)
    g_sel = jnp.min(jnp.where(gl == gmax, lane, LANES), axis=1, keepdims=True)
    p_group = 1.0 / jnp.sum(jnp.exp(gl - gmax), axis=1, keepdims=True)
    lo_lane = EXPERT_LANE0 + g_sel * EXPERTS_PER_GROUP
    in_group = (lane >= lo_lane) & (lane < lo_lane + EXPERTS_PER_GROUP)
    el = jnp.where(in_group, logits, NEG_BIG)
    m1 = jnp.max(el, axis=1, keepdims=True)
    i1 = jnp.min(jnp.where(el == m1, lane, LANES), axis=1, keepdims=True)
    el2 = jnp.where(lane == i1, NEG_BIG, el)
    m2 = jnp.max(el2, axis=1, keepdims=True)
    i2 = jnp.min(jnp.where(el2 == m2, lane, LANES), axis=1, keepdims=True)
    e21 = jnp.exp(m2 - m1)
    w1 = p_group / (1.0 + e21)
    w2 = p_group * e21 / (1.0 + e21)
    route = jnp.where(lane == 0, w1, 0.0)
    route = jnp.where(lane == 1, w2, route)
    route = jnp.where(lane == 2, (i1 - EXPERT_LANE0).astype(F32), route)
    route = jnp.where(lane == 3, (i2 - EXPERT_LANE0).astype(F32), route)
    route_ref[...] = route
    oh1_ref[...] = jnp.where(lane == i1, 1.0, 0.0).astype(BF16)
    oh2_ref[...] = jnp.where(lane == i2, 1.0, 0.0).astype(BF16)


def _mix(out_a, proj, x2, mod3, sgu_w, sgu_bias_full, sgu_g, sgu_b, wpa, wpb, wo,
         ln1_g, ln1_b, wr, br, seq, alpha, tm=256):
    n, d = x2.shape
    tm = min(tm, seq)
    def pcol(start, width):
        assert start % width == 0
        return start // width
    su_c = pcol(_COL_SU, SGU_WIDTH)
    sv_c = pcol(_COL_SV, SGU_WIDTH)
    half = d // 2
    ga_c = pcol(_COL_GA, half)
    gb_c = pcol(_COL_GB(d), half)
    bidx = lambda i: (i * tm) // seq
    row = lambda i: (i, 0)
    const2 = lambda i: (0, 0)
    const3 = lambda i: (0, 0, 0)
    return pl.pallas_call(
        functools.partial(_mix_kernel, tm=tm, alpha=alpha),
        grid=(n // tm,),
        in_specs=[pl.BlockSpec((tm, SB_WIDTH), row),
                  pl.BlockSpec((tm, SGU_WIDTH), lambda i: (i, su_c)),
                  pl.BlockSpec((tm, SGU_WIDTH), lambda i: (i, sv_c)),
                  pl.BlockSpec((tm, half), lambda i: (i, ga_c)),
                  pl.BlockSpec((tm, half), lambda i: (i, ga_c + 1)),
                  pl.BlockSpec((tm, half), lambda i: (i, gb_c)),
                  pl.BlockSpec((tm, half), lambda i: (i, gb_c + 1)),
                  pl.BlockSpec((tm, d), row),
                  pl.BlockSpec((SGU_GROUPS, SGU_BLOCK, SGU_BLOCK), const3),
                  pl.BlockSpec((SGU_BLOCK, SGU_WIDTH), const2),
                  pl.BlockSpec((1, SGU_WIDTH), const2),
                  pl.BlockSpec((1, SGU_WIDTH), const2),
                  pl.BlockSpec((SB_WIDTH, d), const2),
                  pl.BlockSpec((SGU_WIDTH, d), const2),
                  pl.BlockSpec((d, d), const2),
                  pl.BlockSpec((1, 1, d), lambda i: (bidx(i) * 6 + 2, 0, 0)),
                  pl.BlockSpec((1, d), const2),
                  pl.BlockSpec((1, d), const2),
                  pl.BlockSpec((1, 1, d), lambda i: (bidx(i) * 6 + 3, 0, 0)),
                  pl.BlockSpec((1, 1, d), lambda i: (bidx(i) * 6 + 4, 0, 0)),
                  pl.BlockSpec((d, 2 * LANES), const2),
                  pl.BlockSpec((1, LANES), const2)],
        out_specs=[pl.BlockSpec((tm, d), row),
                   pl.BlockSpec((tm * (d // LANES), LANES), row),
                   pl.BlockSpec((tm, LANES), row),
                   pl.BlockSpec((tm, LANES), row),
                   pl.BlockSpec((tm, LANES), row)],
        out_shape=[jax.ShapeDtypeStruct((n, d), F32),
                   jax.ShapeDtypeStruct((n * (d // LANES), LANES), F32),
                   jax.ShapeDtypeStruct((n, LANES), F32),
                   jax.ShapeDtypeStruct((n, LANES), BF16),
                   jax.ShapeDtypeStruct((n, LANES), BF16)],
        compiler_params=_cparams("arbitrary"),
        name="mix",
    )(out_a, proj, proj, proj, proj, proj, proj, x2, sgu_w, sgu_bias_full, sgu_g, sgu_b,
      wpa, wpb, wo, mod3, ln1_g, ln1_b, mod3, mod3, wr, br)


def _plan_kernel(oh1_ref, oh2_ref, dest_ref, blk_ref, cnt_ref, start_ref, *, tm, bm, nblk_pad):
    phase = pl.program_id(0)
    i = pl.program_id(1)
    oh1 = oh1_ref[...]
    oh2 = oh2_ref[...]
    both = oh1 + oh2
    lane = lax.broadcasted_iota(I32, (SUBLANES, LANES), 1)

    @pl.when((phase == 0) & (i == 0))
    def _():
        cnt_ref[...] = jnp.zeros_like(cnt_ref)

    @pl.when(phase == 0)
    def _():
        ones = jnp.ones((SUBLANES, tm), BF16)
        cnt_ref[...] += jnp.dot(ones, both, preferred_element_type=F32)

    @pl.when((phase == 1) & (i == 0))
    def _():
        cnt = cnt_ref[...]
        padded = jnp.floor((cnt + (bm - 1)) * (1.0 / bm)) * bm
        r = lax.broadcasted_iota(I32, (LANES, LANES), 0)
        c = lax.broadcasted_iota(I32, (LANES, LANES), 1)
        upper = (r < c).astype(F32)
        starts = jnp.dot(padded, upper, preferred_element_type=F32,
                         precision=lax.Precision.HIGHEST)
        start_ref[...] = starts
        cnt_ref[...] = jnp.zeros_like(cnt_ref)
        ends = starts + padded
        is_exp = (lane[0:1] >= EXPERT_LANE0) & (lane[0:1] < EXPERT_LANE0 + N_EXPERTS)
        bstart = (lax.broadcasted_iota(I32, (nblk_pad, LANES), 0) * bm).astype(F32)
        done = jnp.where(is_exp & (ends[0:1] <= bstart), 1.0, 0.0)
        bexp = jnp.minimum(jnp.sum(done, axis=1, keepdims=True), N_EXPERTS - 1.0)
        total = jnp.sum(jnp.where(is_exp, padded[0:1], 0.0), axis=1, keepdims=True)
        blane = lax.broadcasted_iota(I32, (nblk_pad, LANES), 1)
        blk = jnp.where(blane == 0, bexp, jnp.where(blane == 1, total * (1.0 / bm), 0.0))
        blk_ref[...] = blk.astype(I32)

    @pl.when(phase == 1)
    def _():
        r = lax.broadcasted_iota(I32, (tm, tm), 0)
        c = lax.broadcasted_iota(I32, (tm, tm), 1)
        lower = (c < r).astype(BF16)
        before = jnp.dot(lower, both, preferred_element_type=F32) + cnt_ref[0:1]
        pos = before + start_ref[0:1]
        d1 = jnp.sum(pos * oh1.astype(F32), axis=1, keepdims=True)
        d2 = jnp.sum(pos * oh2.astype(F32), axis=1, keepdims=True)
        dlane = lax.broadcasted_iota(I32, (tm, LANES), 1)
        dest = jnp.where(dlane == 0, d1, jnp.where(dlane == 1, d2, 0.0))
        dest_ref[...] = dest.astype(I32)
        ones = jnp.ones((SUBLANES, tm), BF16)
        cnt_ref[...] += jnp.dot(ones, both, preferred_element_type=F32)


def _plan(oh1, oh2, bm, nblk, tm=1024):
    n = oh1.shape[0]
    tm = min(tm, n)
    nblk_pad = -(-nblk // SUBLANES) * SUBLANES
    dest, blk = pl.pallas_call(
        functools.partial(_plan_kernel, tm=tm, bm=bm, nblk_pad=nblk_pad),
        grid=(2, n // tm),
        in_specs=[pl.BlockSpec((tm, LANES), lambda p, i: (i, 0)),
                  pl.BlockSpec((tm, LANES), lambda p, i: (i, 0))],
        out_specs=[pl.BlockSpec((tm, LANES), lambda p, i: (i * p, 0)),
                   pl.BlockSpec((nblk_pad, LANES), lambda p, i: (0, 0))],
        out_shape=[jax.ShapeDtypeStruct((n, LANES), I32),
                   jax.ShapeDtypeStruct((nblk_pad, LANES), I32)],
        scratch_shapes=[pltpu.VMEM((SUBLANES, LANES), F32),
                        pltpu.VMEM((SUBLANES, LANES), F32)],
        compiler_params=_cparams("arbitrary", "arbitrary"),
        name="plan",
    )(oh1, oh2)
    return dest, blk


ROW_UNROLL = 8


def _invert_kernel(d1_ref, d2_ref, zero_ref, rt_ref, sem, *, tb):
    i = pl.program_id(0)

    @pl.when(i == 0)
    def _():
        clear = pltpu.make_async_copy(zero_ref, rt_ref, sem)
        clear.start()
        clear.wait()

    def put(j, carry):
        for u in range(ROW_UNROLL):
            t = j * ROW_UNROLL + u
            rt_ref[d1_ref[0, 0, t]] = i * tb + t
            rt_ref[d2_ref[0, 0, t]] = i * tb + t
        return carry

    lax.fori_loop(0, tb // ROW_UNROLL, put, 0)


def _invert(dest1, dest2, n_rows, tb=2048):
    n = dest1.shape[0]
    tb = min(tb, n)
    assert n % tb == 0 and tb % ROW_UNROLL == 0
    blocks = lambda i: (i, 0, 0)
    return pl.pallas_call(
        functools.partial(_invert_kernel, tb=tb),
        grid=(n // tb,),
        in_specs=[pl.BlockSpec((1, 1, tb), blocks, memory_space=pltpu.SMEM),
                  pl.BlockSpec((1, 1, tb), blocks, memory_space=pltpu.SMEM),
                  pl.BlockSpec(memory_space=pl.ANY)],
        out_specs=pl.BlockSpec(memory_space=pltpu.SMEM),
        out_shape=jax.ShapeDtypeStruct((n_rows,), I32),
        scratch_shapes=[pltpu.SemaphoreType.DMA(())],
        compiler_params=_cparams("arbitrary"),
        name="invert",
    )(dest1.reshape(n // tb, 1, tb), dest2.reshape(n // tb, 1, tb), jnp.zeros((n_rows,), I32))


def _slab(idx, chunks):
    return pl.ds(pl.multiple_of(idx * chunks, chunks), chunks)


def _rows_from_slabs(buf_ref, slot, rows, chunks):
    return jnp.concatenate(
        [buf_ref[slot, pl.ds(c, rows, stride=chunks), :] for c in range(chunks)], axis=1)


GATHER_AHEAD = 6


def _experts_kernel(be_ref, na_ref, first_ref, wslot_ref, nxt_ref, *refs, bm, chunks):
    head_refs = refs[:GATHER_AHEAD]
    (rtn_ref, h_ref, wg_hbm, wu_hbm, wd_hbm, o_ref,
     xbuf, sem, wgf, wuf, wdf, wsem, wgb, wub, wdb) = refs[GATHER_AHEAD:]
    i = pl.program_id(0)
    n_active = na_ref[0]

    def gather(tok_ref, slot):
        def trip(r8, carry):
            for u in range(ROW_UNROLL):
                r = r8 * ROW_UNROLL + u
                pltpu.make_async_copy(h_ref.at[_slab(tok_ref[0, 0, r], chunks), :],
                                      xbuf.at[slot, _slab(r, chunks), :], sem.at[slot]).start()
            return carry
        lax.fori_loop(0, bm // ROW_UNROLL, trip, 0)

    def weight_copies(e, ws):
        return (pltpu.make_async_copy(wg_hbm.at[e], wgf.at[ws], wsem.at[ws]),
                pltpu.make_async_copy(wu_hbm.at[e], wuf.at[ws], wsem.at[ws]),
                pltpu.make_async_copy(wd_hbm.at[e], wdf.at[ws], wsem.at[ws]))

    def fetch_weights(e, ws):
        for cp in weight_copies(e, ws):
            cp.start(priority=1)

    def gather_wait(slot):
        pltpu.make_async_copy(h_ref.at[pl.ds(0, bm * chunks), :], xbuf.at[slot],
                              sem.at[slot]).wait()

    @pl.when(i == 0)
    def _():
        fetch_weights(be_ref[0], wslot_ref[0])
        gather(head_refs[0], 0)
        for k in range(1, GATHER_AHEAD):
            pl.when(k < n_active)(functools.partial(gather, head_refs[k], k))

    @pl.when(i < n_active)
    def _():
        slot = lax.rem(i, GATHER_AHEAD + 1)
        gather_wait(slot)

        @pl.when(i + GATHER_AHEAD < n_active)
        def _():
            gather(rtn_ref, lax.rem(i + GATHER_AHEAD, GATHER_AHEAD + 1))

        @pl.when(first_ref[i] == 1)
        def _():
            ws = wslot_ref[i]
            for cp in weight_copies(be_ref[i], ws):
                cp.wait()

            @pl.when(nxt_ref[i] >= 0)
            def _():
                fetch_weights(nxt_ref[i], 1 - ws)

            wgb[...] = wgf[ws].astype(BF16)
            wub[...] = wuf[ws].astype(BF16)
            wdb[...] = wdf[ws].astype(BF16)

        x = _rows_from_slabs(xbuf, slot, bm, chunks).astype(BF16)
        g = jnp.dot(x, wgb[...], preferred_element_type=F32)
        u = jnp.dot(x, wub[...], preferred_element_type=F32)
        hb = (g * _sigmoid(g) * u).astype(BF16)
        y = jnp.dot(hb, wdb[...], preferred_element_type=F32)
        for c in range(chunks):
            o_ref[pl.ds(c, bm, stride=chunks), :] = y[:, c * LANES:(c + 1) * LANES]

    @pl.when(i >= n_active)
    def _():
        o_ref[...] = jnp.zeros_like(o_ref)


def _experts(block_expert, n_active, row_tok, h2s, w_gate, w_up, w_down, bm):
    n_rows = row_tok.shape[0]
    nblk = n_rows // bm
    d, de = w_gate.shape[1], w_gate.shape[2]
    chunks = d // LANES
    assert bm % ROW_UNROLL == 0
    assert nblk > GATHER_AHEAD
    row_blocks = row_tok.reshape(nblk, 1, bm)

    idx = jnp.arange(nblk, dtype=I32)
    prev = jnp.concatenate([block_expert[:1] - 1, block_expert[:-1]])
    first = ((block_expert != prev) & (idx < n_active[0])).astype(I32)
    wslot = ((jnp.cumsum(first) - 1) % 2).astype(I32)
    run_start = lax.cummin(jnp.where(first > 0, idx, nblk), axis=0, reverse=True)
    next_start = jnp.concatenate([run_start[1:], jnp.full((1,), nblk, I32)])
    nxt = jnp.where(next_start < nblk,
                    block_expert[jnp.minimum(next_start, nblk - 1)], -1).astype(I32)

    return pl.pallas_call(
        functools.partial(_experts_kernel, bm=bm, chunks=chunks),
        grid_spec=pltpu.PrefetchScalarGridSpec(
            num_scalar_prefetch=5,
            grid=(nblk,),
            in_specs=[pl.BlockSpec((1, 1, bm), functools.partial(lambda k, i, *_: (k, 0, 0), k),
                                   memory_space=pltpu.SMEM) for k in range(GATHER_AHEAD)]
                     + [pl.BlockSpec((1, 1, bm),
                                     lambda i, *_: (jnp.minimum(i + GATHER_AHEAD, nblk - 1), 0, 0),
                                     memory_space=pltpu.SMEM)]
                     + [pl.BlockSpec(memory_space=pl.ANY)] * 4,
            out_specs=pl.BlockSpec((bm * chunks, LANES), lambda i, *_: (i, 0)),
            scratch_shapes=[pltpu.VMEM((GATHER_AHEAD + 1, bm * chunks, LANES), F32),
                            pltpu.SemaphoreType.DMA((GATHER_AHEAD + 1,)),
                            pltpu.VMEM((2, d, de), F32),
                            pltpu.VMEM((2, d, de), F32),
                            pltpu.VMEM((2, de, d), F32),
                            pltpu.SemaphoreType.DMA((2,)),
                            pltpu.VMEM((d, de), BF16),
                            pltpu.VMEM((d, de), BF16),
                            pltpu.VMEM((de, d), BF16)],
        ),
        out_shape=jax.ShapeDtypeStruct((n_rows * chunks, LANES), F32),
        compiler_params=_cparams("arbitrary"),
        name="experts",
    )(block_expert, n_active, first, wslot, nxt, *([row_blocks] * (GATHER_AHEAD + 1)),
      h2s, w_gate, w_up, w_down)


def _combine_kernel(d1_ref, d2_ref, ys_ref, x1_ref, route_ref, g2_ref, lg_ref, lb_ref,
                    o_ref, y1buf, y2buf, sem, *, tm, chunks, alpha):
    i = pl.program_id(0)
    n_tiles = pl.num_programs(0)

    def row_copies(tile, t, slot):
        tok = tile * tm + t
        return (pltpu.make_async_copy(ys_ref.at[_slab(d1_ref[tok], chunks), :],
                                      y1buf.at[slot, _slab(t, chunks), :], sem.at[slot]),
                pltpu.make_async_copy(ys_ref.at[_slab(d2_ref[tok], chunks), :],
                                      y2buf.at[slot, _slab(t, chunks), :], sem.at[slot]))

    def gather(tile, slot):
        def trip(t8, carry):
            for u in range(ROW_UNROLL):
                for k, cp in enumerate(row_copies(tile, t8 * ROW_UNROLL + u, slot)):
                    cp.start(priority=k)
            return carry
        lax.fori_loop(0, tm // ROW_UNROLL, trip, 0)

    def gather_wait(slot):
        for buf in (y1buf, y2buf):
            pltpu.make_async_copy(ys_ref.at[pl.ds(0, tm * chunks), :], buf.at[slot],
                                  sem.at[slot]).wait()

    @pl.when(i == 0)
    def _():
        gather(0, 0)

    slot = lax.rem(i, 2)
    gather_wait(slot)

    @pl.when(i + 1 < n_tiles)
    def _():
        gather(i + 1, 1 - slot)

    route = route_ref[...]
    y = (route[:, 0:1] * _rows_from_slabs(y1buf, slot, tm, chunks)
         + route[:, 1:2] * _rows_from_slabs(y2buf, slot, tm, chunks))
    r = alpha * x1_ref[...] + g2_ref[0] * y
    o_ref[...] = _ln_rows(r) * lg_ref[...] + lb_ref[...]


def _combine(dest1, dest2, ys, x1, route, mod3, ln2_g, ln2_b, seq, alpha, tm=512):
    n, d = x1.shape
    tm = min(tm, seq)
    chunks = d // LANES
    assert tm % ROW_UNROLL == 0
    tok = lambda i, d1, d2: (i, 0)
    const = lambda i, d1, d2: (0, 0)
    return pl.pallas_call(
        functools.partial(_combine_kernel, tm=tm, chunks=chunks, alpha=alpha),
        grid_spec=pltpu.PrefetchScalarGridSpec(
            num_scalar_prefetch=2,
            grid=(n // tm,),
            in_specs=[pl.BlockSpec(memory_space=pl.ANY),
                      pl.BlockSpec((tm, d), tok),
                      pl.BlockSpec((tm, LANES), tok),
                      pl.BlockSpec((1, 1, d), lambda i, d1, d2: (((i * tm) // seq) * 6 + 5, 0, 0)),
                      pl.BlockSpec((1, d), const),
                      pl.BlockSpec((1, d), const)],
            out_specs=pl.BlockSpec((tm, d), tok),
            scratch_shapes=[pltpu.VMEM((2, tm * chunks, LANES), F32),
                            pltpu.VMEM((2, tm * chunks, LANES), F32),
                            pltpu.SemaphoreType.DMA((2,))],
        ),
        out_shape=jax.ShapeDtypeStruct((n, d), F32),
        compiler_params=_cparams("arbitrary"),
        name="combine",
    )(dest1, dest2, ys, x1, route, mod3, ln2_g, ln2_b)


EXPERT_BLOCK_ROWS = 256


def kernel(x, c, w_ada, b_ada, w_in, sgu_w, sgu_b, sgu_ln_g, sgu_ln_b, w_proj_a, w_proj_b,
           w_out, ln1_g, ln1_b, w_group, b_group, w_router, b_router, w_gate, w_up, w_down,
           ln2_g, ln2_b):
    b, s, d = x.shape
    n = b * s
    depth = w_ada.shape[0]
    alpha = (2.0 * depth) ** 0.25
    sub = d // LANES
    bm = EXPERT_BLOCK_ROWS
    n_assign = 2 * n
    n_rows = n_assign + N_EXPERTS * bm
    nblk = n_rows // bm

    in_cols = w_in.shape[2]
    colscale = jnp.ones((1, in_cols), F32).at[:, _COL_Q:_COL_K].set(
        SB_HEAD_DIM ** -0.5 * LOG2E)

    for l in range(depth):
        mod = _ada(c, w_ada[l], b_ada[l])
        mod3 = mod.reshape(b * 6, 1, d)

        x2 = x.reshape(n, d)
        proj = _inproj(x2, mod3, w_in[l].astype(BF16), colscale, s)
        out_a = _attention(proj.reshape(b, s, in_cols), d).reshape(n, SB_WIDTH)

        wr = jnp.zeros((d, LANES), F32)
        wr = wr.at[:, GROUP_LANE0:GROUP_LANE0 + N_GROUPS].set(w_group[l])
        wr = wr.at[:, EXPERT_LANE0:EXPERT_LANE0 + N_EXPERTS].set(w_router[l])
        wr_hi = wr.astype(BF16)
        wr_lo = (wr - wr_hi.astype(F32)).astype(BF16)
        wr = jnp.concatenate([wr_hi, wr_lo], axis=1)
        br = jnp.zeros((1, LANES), F32)
        br = br.at[0, GROUP_LANE0:GROUP_LANE0 + N_GROUPS].set(b_group[l])
        br = br.at[0, EXPERT_LANE0:EXPERT_LANE0 + N_EXPERTS].set(b_router[l])
        sgu_bias_full = jnp.repeat(sgu_b[l].T, SGU_GROUP_DIM, axis=1)

        x1, h2s, route, oh1, oh2 = _mix(
            out_a, proj, x2, mod3, sgu_w[l], sgu_bias_full,
            sgu_ln_g[l].reshape(1, -1), sgu_ln_b[l].reshape(1, -1),
            w_proj_a[l].astype(BF16), w_proj_b[l].astype(BF16), w_out[l].astype(BF16),
            ln1_g[l].reshape(1, d), ln1_b[l].reshape(1, d), wr, br, s, alpha)

        dest, blk = _plan(oh1, oh2, bm, nblk)
        dest1 = dest[:, 0]
        dest2 = dest[:, 1]
        block_expert = blk[:nblk, 0]
        n_active = blk[0:1, 1]

        row_tok = _invert(dest1, dest2, n_rows)
        ys = _experts(block_expert, n_active, row_tok, h2s, w_gate[l], w_up[l], w_down[l], bm)
        out = _combine(dest1, dest2, ys, x1, route, mod3, ln2_g[l].reshape(1, d),
                       ln2_b[l].reshape(1, d), s, alpha)
        x = out.reshape(b, s, d)
    return x
```

```python
import functools
import math

import jax
import jax.numpy as jnp
from jax import lax
from jax.experimental import pallas as pl
from jax.experimental.pallas import tpu as pltpu

F32 = jnp.float32
BF16 = jnp.bfloat16
I32 = jnp.int32

CHUNK = 64
SB_HEADS = 8
SB_HEAD_DIM = 128
SB_WIDTH = SB_HEADS * SB_HEAD_DIM
SGU_GROUPS = 8
SGU_GROUP_DIM = 128
SGU_WIDTH = SGU_GROUPS * SGU_GROUP_DIM
SGU_BLOCK = 128
N_GROUPS = 4
EXPERTS_PER_GROUP = 8
N_EXPERTS = N_GROUPS * EXPERTS_PER_GROUP
LN_EPS = 1e-5

LANES = 128
SUBLANES = 8
VMEM_LIMIT_BYTES = 60000 * 1024

GROUP_LANE0 = 0
EXPERT_LANE0 = N_GROUPS

LOG2E = 1.4426950408889634
NEG_BIG = -1e30

_COL_Q = 0
_COL_K = SB_WIDTH
_COL_V = 2 * SB_WIDTH
_COL_SU = 3 * SB_WIDTH
_COL_SV = 3 * SB_WIDTH + SGU_WIDTH
_COL_GA = 3 * SB_WIDTH + 2 * SGU_WIDTH


def _COL_GB(d):
    return _COL_GA + d


def _cparams(*sem):
    return pltpu.CompilerParams(dimension_semantics=sem, vmem_limit_bytes=VMEM_LIMIT_BYTES)


def _ln_rows(x):
    mu = jnp.mean(x, axis=-1, keepdims=True)
    xc = x - mu
    var = jnp.mean(xc * xc, axis=-1, keepdims=True)
    return xc * lax.rsqrt(var + LN_EPS)


def _sigmoid(x):
    return 1.0 / (1.0 + jnp.exp(-x))


def _ada_kernel(c_ref, w_ref, b_ref, o_ref, *, batch):
    tn = w_ref.shape[1]
    rows = []
    for b in range(batch):
        c = c_ref[b]
        ca = c * _sigmoid(c)
        rows.append(jnp.concatenate(
            [jnp.sum(ca * w_ref[:, j * LANES:(j + 1) * LANES], axis=0, keepdims=True)
             for j in range(tn // LANES)], axis=1) + b_ref[...])
    rows.append(jnp.zeros((SUBLANES - batch, tn), F32))
    o_ref[...] = jnp.concatenate(rows, axis=0)


def _ada(c, w_ada, b_ada, tn=1024):
    b, d = c.shape
    n_out = w_ada.shape[1]
    assert b < SUBLANES
    c_lanes = jnp.broadcast_to(c[:, :, None], (b, d, LANES))
    out = pl.pallas_call(
        functools.partial(_ada_kernel, batch=b),
        grid=(n_out // tn,),
        in_specs=[pl.BlockSpec((b, d, LANES), lambda j: (0, 0, 0)),
                  pl.BlockSpec((d, tn), lambda j: (0, j)),
                  pl.BlockSpec((1, tn), lambda j: (0, j))],
        out_specs=pl.BlockSpec((SUBLANES, tn), lambda j: (0, j)),
        out_shape=jax.ShapeDtypeStruct((SUBLANES, n_out), F32),
        compiler_params=_cparams("arbitrary"),
        name="ada",
    )(c_lanes, w_ada, b_ada.reshape(1, n_out))
    return out[:b]


def _inproj_kernel(x_ref, sh_ref, sc_ref, w_ref, cs_ref, o_ref, h_ref):
    @pl.when(pl.program_id(1) == 0)
    def _():
        h = _ln_rows(x_ref[...]) * (1.0 + sc_ref[0]) + sh_ref[0]
        h_ref[...] = h.astype(BF16)

    acc = jnp.dot(h_ref[...], w_ref[...], preferred_element_type=F32)
    o_ref[...] = (acc * cs_ref[...]).astype(BF16)


def _inproj(x2, mod3, w_in_bf, colscale, seq, tm=1024, tn=1536):
    n, d = x2.shape
    n_cols = w_in_bf.shape[1]
    tm = min(tm, seq)
    return pl.pallas_call(
        _inproj_kernel,
        grid=(n // tm, n_cols // tn),
        in_specs=[pl.BlockSpec((tm, d), lambda i, j: (i, 0)),
                  pl.BlockSpec((1, 1, d), lambda i, j: (((i * tm) // seq) * 6 + 0, 0, 0)),
                  pl.BlockSpec((1, 1, d), lambda i, j: (((i * tm) // seq) * 6 + 1, 0, 0)),
                  pl.BlockSpec((d, tn), lambda i, j: (0, j)),
                  pl.BlockSpec((1, tn), lambda i, j: (0, j))],
        out_specs=pl.BlockSpec((tm, tn), lambda i, j: (i, j)),
        out_shape=jax.ShapeDtypeStruct((n, n_cols), BF16),
        scratch_shapes=[pltpu.VMEM((tm, d), BF16)],
        compiler_params=_cparams("arbitrary", "arbitrary"),
        name="inproj",
    )(x2, mod3, mod3, w_in_bf, colscale)


Z2_MAX = 100.0
STICK_EXIT_LOG2 = 150.0


def _attn_kernel(q_ref, k_ref, v_ref, o_ref, *, tq, tk, heads):
    qi = pl.program_id(2)
    nd = tq // tk
    jd = qi * nd
    trow = lax.broadcasted_iota(I32, (tk, tk), 0)
    tcol = lax.broadcasted_iota(I32, (tk, tk), 1)
    tri = (trow > tcol).astype(BF16)
    hd = SB_HEAD_DIM
    qs = [q_ref[0, :, h * hd:(h + 1) * hd] for h in range(heads)]

    def block(j, runs, accs, causal):
        off = pl.multiple_of(j * tk, tk)
        hs = range(heads)
        z2s = [lax.dot_general(qs[h], k_ref[0, pl.ds(off, tk), h * hd:(h + 1) * hd],
                               (((1,), (1,)), ((), ())), preferred_element_type=F32) for h in hs]
        z2s = [lax.clamp(-Z2_MAX, z2, Z2_MAX) for z2 in z2s]
        sps = [jnp.log(1.0 + jnp.exp2(z2)) for z2 in z2s]
        if causal is not None:
            sps = [jnp.where(causal, sp, 0.0) for sp in sps]
        laters = [jnp.dot(sp.astype(BF16), tri, preferred_element_type=F32) for sp in sps]
        ws = [jnp.exp2(z2s[h] - LOG2E * (sps[h] + (laters[h] + runs[h]))) for h in hs]
        if causal is not None:
            ws = [jnp.where(causal, w, 0.0) for w in ws]
        new_accs = [accs[h] + jnp.dot(ws[h].astype(BF16),
                                      v_ref[0, pl.ds(off, tk), h * hd:(h + 1) * hd],
                                      preferred_element_type=F32) for h in hs]
        new_runs = [runs[h] + jnp.sum(sps[h], axis=1, keepdims=True) for h in hs]
        return tuple(new_runs), tuple(new_accs)

    def mass_left(runs):
        low = runs[0]
        for r in runs[1:]:
            low = jnp.minimum(low, r)
        return (jnp.min(low) * LOG2E < STICK_EXIT_LOG2).astype(I32)

    def diag_body(t, carry):
        c = nd - 1 - t
        row = lax.broadcasted_iota(I32, (tq, tk), 0)
        col = lax.broadcasted_iota(I32, (tq, tk), 1)
        causal = col + c * tk < row
        return block(jd + c, carry[0], carry[1], causal)

    def more(carry):
        t, go, _, _ = carry
        return (t < jd) & (go > 0)

    def body(carry):
        t, _, runs, accs = carry
        runs, accs = block(jd - 1 - t, runs, accs, None)
        return t + 1, mass_left(runs), runs, accs

    runs = tuple(jnp.zeros((tq, 1), F32) for _ in range(heads))
    accs = tuple(jnp.zeros((tq, hd), F32) for _ in range(heads))
    runs, accs = lax.fori_loop(0, nd, diag_body, (runs, accs))
    _, _, runs, accs = lax.while_loop(more, body, (jnp.int32(0), mass_left(runs), runs, accs))
    for h in range(heads):
        o_ref[0, :, h * hd:(h + 1) * hd] = accs[h].astype(BF16)


def _attention(proj3, d, tq=256, tk=256, heads=4):
    b, s, _ = proj3.shape
    tk = min(tk, s)
    tq = min(tq, s)
    assert tq % tk == 0 and s % tq == 0
    width = heads * SB_HEAD_DIM
    qc = _COL_Q // width
    kc = _COL_K // width
    vc = _COL_V // width
    return pl.pallas_call(
        functools.partial(_attn_kernel, tq=tq, tk=tk, heads=heads),
        grid=(b, SB_HEADS // heads, s // tq),
        in_specs=[pl.BlockSpec((1, tq, width), lambda bi, g, qi: (bi, qi, qc + g)),
                  pl.BlockSpec((1, s, width), lambda bi, g, qi: (bi, 0, kc + g)),
                  pl.BlockSpec((1, s, width), lambda bi, g, qi: (bi, 0, vc + g))],
        out_specs=pl.BlockSpec((1, tq, width), lambda bi, g, qi: (bi, qi, g)),
        out_shape=jax.ShapeDtypeStruct((b, s, SB_WIDTH), BF16),
        compiler_params=_cparams("arbitrary", "arbitrary", "arbitrary"),
        name="attn",
    )(proj3, proj3, proj3)


def _erf(x):
    return lax.erf(x)


def _gelu(x):
    return 0.5 * x * (1.0 + _erf(x * (1.0 / math.sqrt(2.0))))


def _mix_kernel(oa_ref, su_ref, sv_ref, ga0_ref, ga1_ref, gb0_ref, gb1_ref, x_ref,
                sw_ref, sbias_ref, sg_ref, sb_ref,
                wpa_ref, wpb_ref, wo_ref,
                g1_ref, ln1g_ref, ln1b_ref, sh2_ref, sc2_ref,
                wr_ref, br_ref,
                x1_ref, h2_ref, route_ref, oh1_ref, oh2_ref, *, tm, alpha):
    u = _gelu(su_ref[...].astype(F32))
    v = _gelu(sv_ref[...].astype(F32))
    v = _ln_rows(v) * sg_ref[...] + sb_ref[...]
    vb = v.astype(BF16)
    pos_t = lax.broadcasted_iota(I32, (SGU_BLOCK, SGU_BLOCK), 0)
    pos_s = lax.broadcasted_iota(I32, (SGU_BLOCK, SGU_BLOCK), 1)
    chunk_causal = (pos_s // CHUNK) <= (pos_t // CHUNK)
    n_pos_blocks = tm // SGU_BLOCK
    per_group = []
    for g in range(SGU_GROUPS):
        c0 = g * SGU_GROUP_DIM
        wg = jnp.where(chunk_causal, sw_ref[g], 0.0).astype(BF16)
        rhs = jnp.concatenate(
            [vb[nb * SGU_BLOCK:(nb + 1) * SGU_BLOCK, c0:c0 + SGU_GROUP_DIM]
             for nb in range(n_pos_blocks)], axis=1)
        per_group.append(jnp.dot(wg, rhs, preferred_element_type=F32))
    mixed = jnp.concatenate(
        [jnp.concatenate([per_group[g][:, nb * SGU_GROUP_DIM:(nb + 1) * SGU_GROUP_DIM]
                          for g in range(SGU_GROUPS)], axis=1) + sbias_ref[...]
         for nb in range(n_pos_blocks)], axis=0)
    out_b = (u * mixed).astype(BF16)

    pa = jnp.dot(oa_ref[...], wpa_ref[...], preferred_element_type=F32)
    pb = jnp.dot(out_b, wpb_ref[...], preferred_element_type=F32)
    ga = jnp.concatenate([ga0_ref[...], ga1_ref[...]], axis=1).astype(F32)
    gb = jnp.concatenate([gb0_ref[...], gb1_ref[...]], axis=1).astype(F32)
    merged = _sigmoid(ga) * pa + _sigmoid(gb) * pb
    y = jnp.dot(merged.astype(BF16), wo_ref[...], preferred_element_type=F32)

    r = alpha * x_ref[...] + g1_ref[0] * y
    x1 = _ln_rows(r) * ln1g_ref[...] + ln1b_ref[...]
    x1_ref[...] = x1
    h2 = _ln_rows(x1) * (1.0 + sc2_ref[0]) + sh2_ref[0]
    chunks = h2.shape[1] // LANES
    for c in range(chunks):
        h2_ref[pl.ds(c, tm, stride=chunks), :] = h2[:, c * LANES:(c + 1) * LANES]

    h_hi = h2.astype(BF16)
    h_lo = (h2 - h_hi.astype(F32)).astype(BF16)
    hi_w = jnp.dot(h_hi, wr_ref[...], preferred_element_type=F32)
    lo_w = jnp.dot(h_lo, wr_ref[:, :LANES], preferred_element_type=F32)
    logits = (hi_w[:, :LANES] + hi_w[:, LANES:]) + lo_w + br_ref[...]
    lane = lax.broadcasted_iota(I32, logits.shape, 1)
    gl = jnp.where(lane < N_GROUPS, logits, NEG_BIG)
    gmax = jnp.max(gl, axis=1, keepdims=True)
    g_sel = jnp.min(jnp.where(gl == gmax, lane, LANES), axis=1, keepdims=True)
    p_group = 1.0 / jnp.sum(jnp.exp(gl - gmax), axis=1, keepdims=True)
    lo_lane = EXPERT_LANE0 + g_sel * EXPERTS_PER_GROUP
    in_group = (lane >= lo_lane) & (lane < lo_lane + EXPERTS_PER_GROUP)
    el = jnp.where(in_group, logits, NEG_BIG)
    m1 = jnp.max(el, axis=1, keepdims=True)
    i1 = jnp.min(jnp.where(el == m1, lane, LANES), axis=1, keepdims=True)
    el2 = jnp.where(lane == i1, NEG_BIG, el)
    m2 = jnp.max(el2, axis=1, keepdims=True)
    i2 = jnp.min(jnp.where(el2 == m2, lane, LANES), axis=1, keepdims=True)
    e21 = jnp.exp(m2 - m1)
    w1 = p_group / (1.0 + e21)
    w2 = p_group * e21 / (1.0 + e21)
    route = jnp.where(lane == 0, w1, 0.0)
    route = jnp.where(lane == 1, w2, route)
    route = jnp.where(lane == 2, (i1 - EXPERT_LANE0).astype(F32), route)
    route = jnp.where(lane == 3, (i2 - EXPERT_LANE0).astype(F32), route)
    route_ref[...] = route
    oh1_ref[...] = jnp.where(lane == i1, 1.0, 0.0).astype(BF16)
    oh2_ref[...] = jnp.where(lane == i2, 1.0, 0.0).astype(BF16)


def _mix(out_a, proj, x2, mod3, sgu_w, sgu_bias_full, sgu_g, sgu_b, wpa, wpb, wo,
         ln1_g, ln1_b, wr, br, seq, alpha, tm=256):
    n, d = x2.shape
    tm = min(tm, seq)
    def pcol(start, width):
        assert start % width == 0
        return start // width
    su_c = pcol(_COL_SU, SGU_WIDTH)
    sv_c = pcol(_COL_SV, SGU_WIDTH)
    half = d // 2
    ga_c = pcol(_COL_GA, half)
    gb_c = pcol(_COL_GB(d), half)
    bidx = lambda i: (i * tm) // seq
    row = lambda i: (i, 0)
    const2 = lambda i: (0, 0)
    const3 = lambda i: (0, 0, 0)
    return pl.pallas_call(
        functools.partial(_mix_kernel, tm=tm, alpha=alpha),
        grid=(n // tm,),
        in_specs=[pl.BlockSpec((tm, SB_WIDTH), row),
                  pl.BlockSpec((tm, SGU_WIDTH), lambda i: (i, su_c)),
                  pl.BlockSpec((tm, SGU_WIDTH), lambda i: (i, sv_c)),
                  pl.BlockSpec((tm, half), lambda i: (i, ga_c)),
                  pl.BlockSpec((tm, half), lambda i: (i, ga_c + 1)),
                  pl.BlockSpec((tm, half), lambda i: (i, gb_c)),
                  pl.BlockSpec((tm, half), lambda i: (i, gb_c + 1)),
                  pl.BlockSpec((tm, d), row),
                  pl.BlockSpec((SGU_GROUPS, SGU_BLOCK, SGU_BLOCK), const3),
                  pl.BlockSpec((SGU_BLOCK, SGU_WIDTH), const2),
                  pl.BlockSpec((1, SGU_WIDTH), const2),
                  pl.BlockSpec((1, SGU_WIDTH), const2),
                  pl.BlockSpec((SB_WIDTH, d), const2),
                  pl.BlockSpec((SGU_WIDTH, d), const2),
                  pl.BlockSpec((d, d), const2),
                  pl.BlockSpec((1, 1, d), lambda i: (bidx(i) * 6 + 2, 0, 0)),
                  pl.BlockSpec((1, d), const2),
                  pl.BlockSpec((1, d), const2),
                  pl.BlockSpec((1, 1, d), lambda i: (bidx(i) * 6 + 3, 0, 0)),
                  pl.BlockSpec((1, 1, d), lambda i: (bidx(i) * 6 + 4, 0, 0)),
                  pl.BlockSpec((d, 2 * LANES), const2),
                  pl.BlockSpec((1, LANES), const2)],
        out_specs=[pl.BlockSpec((tm, d), row),
                   pl.BlockSpec((tm * (d // LANES), LANES), row),
                   pl.BlockSpec((tm, LANES), row),
                   pl.BlockSpec((tm, LANES), row),
                   pl.BlockSpec((tm, LANES), row)],
        out_shape=[jax.ShapeDtypeStruct((n, d), F32),
                   jax.ShapeDtypeStruct((n * (d // LANES), LANES), F32),
                   jax.ShapeDtypeStruct((n, LANES), F32),
                   jax.ShapeDtypeStruct((n, LANES), BF16),
                   jax.ShapeDtypeStruct((n, LANES), BF16)],
        compiler_params=_cparams("arbitrary"),
        name="mix",
    )(out_a, proj, proj, proj, proj, proj, proj, x2, sgu_w, sgu_bias_full, sgu_g, sgu_b,
      wpa, wpb, wo, mod3, ln1_g, ln1_b, mod3, mod3, wr, br)


def _plan_kernel(oh1_ref, oh2_ref, dest_ref, blk_ref, cnt_ref, start_ref, *, tm, bm, nblk_pad):
    phase = pl.program_id(0)
    i = pl.program_id(1)
    oh1 = oh1_ref[...]
    oh2 = oh2_ref[...]
    both = oh1 + oh2
    lane = lax.broadcasted_iota(I32, (SUBLANES, LANES), 1)

    @pl.when((phase == 0) & (i == 0))
    def _():
        cnt_ref[...] = jnp.zeros_like(cnt_ref)

    @pl.when(phase == 0)
    def _():
        ones = jnp.ones((SUBLANES, tm), BF16)
        cnt_ref[...] += jnp.dot(ones, both, preferred_element_type=F32)

    @pl.when((phase == 1) & (i == 0))
    def _():
        cnt = cnt_ref[...]
        padded = jnp.floor((cnt + (bm - 1)) * (1.0 / bm)) * bm
        r = lax.broadcasted_iota(I32, (LANES, LANES), 0)
        c = lax.broadcasted_iota(I32, (LANES, LANES), 1)
        upper = (r < c).astype(F32)
        starts = jnp.dot(padded, upper, preferred_element_type=F32,
                         precision=lax.Precision.HIGHEST)
        start_ref[...] = starts
        cnt_ref[...] = jnp.zeros_like(cnt_ref)
        ends = starts + padded
        is_exp = (lane[0:1] >= EXPERT_LANE0) & (lane[0:1] < EXPERT_LANE0 + N_EXPERTS)
        bstart = (lax.broadcasted_iota(I32, (nblk_pad, LANES), 0) * bm).astype(F32)
        done = jnp.where(is_exp & (ends[0:1] <= bstart), 1.0, 0.0)
        bexp = jnp.minimum(jnp.sum(done, axis=1, keepdims=True), N_EXPERTS - 1.0)
        total = jnp.sum(jnp.where(is_exp, padded[0:1], 0.0), axis=1, keepdims=True)
        blane = lax.broadcasted_iota(I32, (nblk_pad, LANES), 1)
        blk = jnp.where(blane == 0, bexp, jnp.where(blane == 1, total * (1.0 / bm), 0.0))
        blk_ref[...] = blk.astype(I32)

    @pl.when(phase == 1)
    def _():
        r = lax.broadcasted_iota(I32, (tm, tm), 0)
        c = lax.broadcasted_iota(I32, (tm, tm), 1)
        lower = (c < r).astype(BF16)
        before = jnp.dot(lower, both, preferred_element_type=F32) + cnt_ref[0:1]
        pos = before + start_ref[0:1]
        d1 = jnp.sum(pos * oh1.astype(F32), axis=1, keepdims=True)
        d2 = jnp.sum(pos * oh2.astype(F32), axis=1, keepdims=True)
        dlane = lax.broadcasted_iota(I32, (tm, LANES), 1)
        dest = jnp.where(dlane == 0, d1, jnp.where(dlane == 1, d2, 0.0))
        dest_ref[...] = dest.astype(I32)
        ones = jnp.ones((SUBLANES, tm), BF16)
        cnt_ref[...] += jnp.dot(ones, both, preferred_element_type=F32)


def _plan(oh1, oh2, bm, nblk, tm=1024):
    n = oh1.shape[0]
    tm = min(tm, n)
    nblk_pad = -(-nblk // SUBLANES) * SUBLANES
    dest, blk = pl.pallas_call(
        functools.partial(_plan_kernel, tm=tm, bm=bm, nblk_pad=nblk_pad),
        grid=(2, n // tm),
        in_specs=[pl.BlockSpec((tm, LANES), lambda p, i: (i, 0)),
                  pl.BlockSpec((tm, LANES), lambda p, i: (i, 0))],
        out_specs=[pl.BlockSpec((tm, LANES), lambda p, i: (i * p, 0)),
                   pl.BlockSpec((nblk_pad, LANES), lambda p, i: (0, 0))],
        out_shape=[jax.ShapeDtypeStruct((n, LANES), I32),
                   jax.ShapeDtypeStruct((nblk_pad, LANES), I32)],
        scratch_shapes=[pltpu.VMEM((SUBLANES, LANES), F32),
                        pltpu.VMEM((SUBLANES, LANES), F32)],
        compiler_params=_cparams("arbitrary", "arbitrary"),
        name="plan",
    )(oh1, oh2)
    return dest, blk


ROW_UNROLL = 8


def _invert_kernel(d1_ref, d2_ref, zero_ref, rt_ref, sem, *, tb):
    i = pl.program_id(0)

    @pl.when(i == 0)
    def _():
        clear = pltpu.make_async_copy(zero_ref, rt_ref, sem)
        clear.start()
        clear.wait()

    def put(j, carry):
        for u in range(ROW_UNROLL):
            t = j * ROW_UNROLL + u
            rt_ref[d1_ref[0, 0, t]] = i * tb + t
            rt_ref[d2_ref[0, 0, t]] = i * tb + t
        return carry

    lax.fori_loop(0, tb // ROW_UNROLL, put, 0)


def _invert(dest1, dest2, n_rows, tb=2048):
    n = dest1.shape[0]
    tb = min(tb, n)
    assert n % tb == 0 and tb % ROW_UNROLL == 0
    blocks = lambda i: (i, 0, 0)
    return pl.pallas_call(
        functools.partial(_invert_kernel, tb=tb),
        grid=(n // tb,),
        in_specs=[pl.BlockSpec((1, 1, tb), blocks, memory_space=pltpu.SMEM),
                  pl.BlockSpec((1, 1, tb), blocks, memory_space=pltpu.SMEM),
                  pl.BlockSpec(memory_space=pl.ANY)],
        out_specs=pl.BlockSpec(memory_space=pltpu.SMEM),
        out_shape=jax.ShapeDtypeStruct((n_rows,), I32),
        scratch_shapes=[pltpu.SemaphoreType.DMA(())],
        compiler_params=_cparams("arbitrary"),
        name="invert",
    )(dest1.reshape(n // tb, 1, tb), dest2.reshape(n // tb, 1, tb), jnp.zeros((n_rows,), I32))


def _slab(idx, chunks):
    return pl.ds(pl.multiple_of(idx * chunks, chunks), chunks)


def _rows_from_slabs(buf_ref, slot, rows, chunks):
    return jnp.concatenate(
        [buf_ref[slot, pl.ds(c, rows, stride=chunks), :] for c in range(chunks)], axis=1)


GATHER_AHEAD = 6


def _experts_kernel(be_ref, na_ref, first_ref, wslot_ref, nxt_ref, *refs, bm, chunks):
    head_refs = refs[:GATHER_AHEAD]
    (rtn_ref, h_ref, wg_hbm, wu_hbm, wd_hbm, o_ref,
     xbuf, sem, wgf, wuf, wdf, wsem, wgb, wub, wdb) = refs[GATHER_AHEAD:]
    i = pl.program_id(0)
    n_active = na_ref[0]

    def gather(tok_ref, slot):
        def trip(r8, carry):
            for u in range(ROW_UNROLL):
                r = r8 * ROW_UNROLL + u
                pltpu.make_async_copy(h_ref.at[_slab(tok_ref[0, 0, r], chunks), :],
                                      xbuf.at[slot, _slab(r, chunks), :], sem.at[slot]).start()
            return carry
        lax.fori_loop(0, bm // ROW_UNROLL, trip, 0)

    def weight_copies(e, ws):
        return (pltpu.make_async_copy(wg_hbm.at[e], wgf.at[ws], wsem.at[ws]),
                pltpu.make_async_copy(wu_hbm.at[e], wuf.at[ws], wsem.at[ws]),
                pltpu.make_async_copy(wd_hbm.at[e], wdf.at[ws], wsem.at[ws]))

    def fetch_weights(e, ws):
        for cp in weight_copies(e, ws):
            cp.start(priority=1)

    def gather_wait(slot):
        pltpu.make_async_copy(h_ref.at[pl.ds(0, bm * chunks), :], xbuf.at[slot],
                              sem.at[slot]).wait()

    @pl.when(i == 0)
    def _():
        fetch_weights(be_ref[0], wslot_ref[0])
        gather(head_refs[0], 0)
        for k in range(1, GATHER_AHEAD):
            pl.when(k < n_active)(functools.partial(gather, head_refs[k], k))

    @pl.when(i < n_active)
    def _():
        slot = lax.rem(i, GATHER_AHEAD + 1)
        gather_wait(slot)

        @pl.when(i + GATHER_AHEAD < n_active)
        def _():
            gather(rtn_ref, lax.rem(i + GATHER_AHEAD, GATHER_AHEAD + 1))

        @pl.when(first_ref[i] == 1)
        def _():
            ws = wslot_ref[i]
            for cp in weight_copies(be_ref[i], ws):
                cp.wait()

            @pl.when(nxt_ref[i] >= 0)
            def _():
                fetch_weights(nxt_ref[i], 1 - ws)

            wgb[...] = wgf[ws].astype(BF16)
            wub[...] = wuf[ws].astype(BF16)
            wdb[...] = wdf[ws].astype(BF16)

        x = _rows_from_slabs(xbuf, slot, bm, chunks).astype(BF16)
        g = jnp.dot(x, wgb[...], preferred_element_type=F32)
        u = jnp.dot(x, wub[...], preferred_element_type=F32)
        hb = (g * _sigmoid(g) * u).astype(BF16)
        y = jnp.dot(hb, wdb[...], preferred_element_type=F32)
        for c in range(chunks):
            o_ref[pl.ds(c, bm, stride=chunks), :] = y[:, c * LANES:(c + 1) * LANES]

    @pl.when(i >= n_active)
    def _():
        o_ref[...] = jnp.zeros_like(o_ref)


def _experts(block_expert, n_active, row_tok, h2s, w_gate, w_up, w_down, bm):
    n_rows = row_tok.shape[0]
    nblk = n_rows // bm
    d, de = w_gate.shape[1], w_gate.shape[2]
    chunks = d // LANES
    assert bm % ROW_UNROLL == 0
    assert nblk > GATHER_AHEAD
    row_blocks = row_tok.reshape(nblk, 1, bm)

    idx = jnp.arange(nblk, dtype=I32)
    prev = jnp.concatenate([block_expert[:1] - 1, block_expert[:-1]])
    first = ((block_expert != prev) & (idx < n_active[0])).astype(I32)
    wslot = ((jnp.cumsum(first) - 1) % 2).astype(I32)
    run_start = lax.cummin(jnp.where(first > 0, idx, nblk), axis=0, reverse=True)
    next_start = jnp.concatenate([run_start[1:], jnp.full((1,), nblk, I32)])
    nxt = jnp.where(next_start < nblk,
                    block_expert[jnp.minimum(next_start, nblk - 1)], -1).astype(I32)

    return pl.pallas_call(
        functools.partial(_experts_kernel, bm=bm, chunks=chunks),
        grid_spec=pltpu.PrefetchScalarGridSpec(
            num_scalar_prefetch=5,
            grid=(nblk,),
            in_specs=[pl.BlockSpec((1, 1, bm), functools.partial(lambda k, i, *_: (k, 0, 0), k),
                                   memory_space=pltpu.SMEM) for k in range(GATHER_AHEAD)]
                     + [pl.BlockSpec((1, 1, bm),
                                     lambda i, *_: (jnp.minimum(i + GATHER_AHEAD, nblk - 1), 0, 0),
                                     memory_space=pltpu.SMEM)]
                     + [pl.BlockSpec(memory_space=pl.ANY)] * 4,
            out_specs=pl.BlockSpec((bm * chunks, LANES), lambda i, *_: (i, 0)),
            scratch_shapes=[pltpu.VMEM((GATHER_AHEAD + 1, bm * chunks, LANES), F32),
                            pltpu.SemaphoreType.DMA((GATHER_AHEAD + 1,)),
                            pltpu.VMEM((2, d, de), F32),
                            pltpu.VMEM((2, d, de), F32),
                            pltpu.VMEM((2, de, d), F32),
                            pltpu.SemaphoreType.DMA((2,)),
                            pltpu.VMEM((d, de), BF16),
                            pltpu.VMEM((d, de), BF16),
                            pltpu.VMEM((de, d), BF16)],
        ),
        out_shape=jax.ShapeDtypeStruct((n_rows * chunks, LANES), F32),
        compiler_params=_cparams("arbitrary"),
        name="experts",
    )(block_expert, n_active, first, wslot, nxt, *([row_blocks] * (GATHER_AHEAD + 1)),
      h2s, w_gate, w_up, w_down)


def _combine_kernel(d1_ref, d2_ref, ys_ref, x1_ref, route_ref, g2_ref, lg_ref, lb_ref,
                    o_ref, y1buf, y2buf, sem, *, tm, chunks, alpha):
    i = pl.program_id(0)
    n_tiles = pl.num_programs(0)

    def row_copies(tile, t, slot):
        tok = tile * tm + t
        return (pltpu.make_async_copy(ys_ref.at[_slab(d1_ref[tok], chunks), :],
                                      y1buf.at[slot, _slab(t, chunks), :], sem.at[slot]),
                pltpu.make_async_copy(ys_ref.at[_slab(d2_ref[tok], chunks), :],
                                      y2buf.at[slot, _slab(t, chunks), :], sem.at[slot]))

    def gather(tile, slot):
        def trip(t8, carry):
            for u in range(ROW_UNROLL):
                for k, cp in enumerate(row_copies(tile, t8 * ROW_UNROLL + u, slot)):
                    cp.start(priority=k)
            return carry
        lax.fori_loop(0, tm // ROW_UNROLL, trip, 0)

    def gather_wait(slot):
        for buf in (y1buf, y2buf):
            pltpu.make_async_copy(ys_ref.at[pl.ds(0, tm * chunks), :], buf.at[slot],
                                  sem.at[slot]).wait()

    @pl.when(i == 0)
    def _():
        gather(0, 0)

    slot = lax.rem(i, 2)
    gather_wait(slot)

    @pl.when(i + 1 < n_tiles)
    def _():
        gather(i + 1, 1 - slot)

    route = route_ref[...]
    y = (route[:, 0:1] * _rows_from_slabs(y1buf, slot, tm, chunks)
         + route[:, 1:2] * _rows_from_slabs(y2buf, slot, tm, chunks))
    r = alpha * x1_ref[...] + g2_ref[0] * y
    o_ref[...] = _ln_rows(r) * lg_ref[...] + lb_ref[...]


def _combine(dest1, dest2, ys, x1, route, mod3, ln2_g, ln2_b, seq, alpha, tm=256):
    n, d = x1.shape
    tm = min(tm, seq)
    chunks = d // LANES
    assert tm % ROW_UNROLL == 0
    tok = lambda i, d1, d2: (i, 0)
    const = lambda i, d1, d2: (0, 0)
    return pl.pallas_call(
        functools.partial(_combine_kernel, tm=tm, chunks=chunks, alpha=alpha),
        grid_spec=pltpu.PrefetchScalarGridSpec(
            num_scalar_prefetch=2,
            grid=(n // tm,),
            in_specs=[pl.BlockSpec(memory_space=pl.ANY),
                      pl.BlockSpec((tm, d), tok),
                      pl.BlockSpec((tm, LANES), tok),
                      pl.BlockSpec((1, 1, d), lambda i, d1, d2: (((i * tm) // seq) * 6 + 5, 0, 0)),
                      pl.BlockSpec((1, d), const),
                      pl.BlockSpec((1, d), const)],
            out_specs=pl.BlockSpec((tm, d), tok),
            scratch_shapes=[pltpu.VMEM((2, tm * chunks, LANES), F32),
                            pltpu.VMEM((2, tm * chunks, LANES), F32),
                            pltpu.SemaphoreType.DMA((2,))],
        ),
        out_shape=jax.ShapeDtypeStruct((n, d), F32),
        compiler_params=_cparams("arbitrary"),
        name="combine",
    )(dest1, dest2, ys, x1, route, mod3, ln2_g, ln2_b)


EXPERT_BLOCK_ROWS = 256


def kernel(x, c, w_ada, b_ada, w_in, sgu_w, sgu_b, sgu_ln_g, sgu_ln_b, w_proj_a, w_proj_b,
           w_out, ln1_g, ln1_b, w_group, b_group, w_router, b_router, w_gate, w_up, w_down,
           ln2_g, ln2_b):
    b, s, d = x.shape
    n = b * s
    depth = w_ada.shape[0]
    alpha = (2.0 * depth) ** 0.25
    sub = d // LANES
    bm = EXPERT_BLOCK_ROWS
    n_assign = 2 * n
    n_rows = n_assign + N_EXPERTS * bm
    nblk = n_rows // bm

    in_cols = w_in.shape[2]
    colscale = jnp.ones((1, in_cols), F32).at[:, _COL_Q:_COL_K].set(
        SB_HEAD_DIM ** -0.5 * LOG2E)

    for l in range(depth):
        mod = _ada(c, w_ada[l], b_ada[l])
        mod3 = mod.reshape(b * 6, 1, d)

        x2 = x.reshape(n, d)
        proj = _inproj(x2, mod3, w_in[l].astype(BF16), colscale, s)
        out_a = _attention(proj.reshape(b, s, in_cols), d).reshape(n, SB_WIDTH)

        wr = jnp.zeros((d, LANES), F32)
        wr = wr.at[:, GROUP_LANE0:GROUP_LANE0 + N_GROUPS].set(w_group[l])
        wr = wr.at[:, EXPERT_LANE0:EXPERT_LANE0 + N_EXPERTS].set(w_router[l])
        wr_hi = wr.astype(BF16)
        wr_lo = (wr - wr_hi.astype(F32)).astype(BF16)
        wr = jnp.concatenate([wr_hi, wr_lo], axis=1)
        br = jnp.zeros((1, LANES), F32)
        br = br.at[0, GROUP_LANE0:GROUP_LANE0 + N_GROUPS].set(b_group[l])
        br = br.at[0, EXPERT_LANE0:EXPERT_LANE0 + N_EXPERTS].set(b_router[l])
        sgu_bias_full = jnp.repeat(sgu_b[l].T, SGU_GROUP_DIM, axis=1)

        x1, h2s, route, oh1, oh2 = _mix(
            out_a, proj, x2, mod3, sgu_w[l], sgu_bias_full,
            sgu_ln_g[l].reshape(1, -1), sgu_ln_b[l].reshape(1, -1),
            w_proj_a[l].astype(BF16), w_proj_b[l].astype(BF16), w_out[l].astype(BF16),
            ln1_g[l].reshape(1, d), ln1_b[l].reshape(1, d), wr, br, s, alpha)

        dest, blk = _plan(oh1, oh2, bm, nblk)
        dest1 = dest[:, 0]
        dest2 = dest[:, 1]
        block_expert = blk[:nblk, 0]
        n_active = blk[0:1, 1]

        row_tok = _invert(dest1, dest2, n_rows)
        ys = _experts(block_expert, n_active, row_tok, h2s, w_gate[l], w_up[l], w_down[l], bm)
        out = _combine(dest1, dest2, ys, x1, route, mod3, ln2_g[l].reshape(1, d),
                       ln2_b[l].reshape(1, d), s, alpha)
        x = out.reshape(b, s, d)
    return x
```

```python
import functools
import math

import jax
import jax.numpy as jnp
from jax import lax
from jax.experimental import pallas as pl
from jax.experimental.pallas import tpu as pltpu

F32 = jnp.float32
BF16 = jnp.bfloat16
I32 = jnp.int32

CHUNK = 64
SB_HEADS = 8
SB_HEAD_DIM = 128
SB_WIDTH = SB_HEADS * SB_HEAD_DIM
SGU_GROUPS = 8
SGU_GROUP_DIM = 128
SGU_WIDTH = SGU_GROUPS * SGU_GROUP_DIM
SGU_BLOCK = 128
N_GROUPS = 4
EXPERTS_PER_GROUP = 8
N_EXPERTS = N_GROUPS * EXPERTS_PER_GROUP
LN_EPS = 1e-5

LANES = 128
SUBLANES = 8
VMEM_LIMIT_BYTES = 60000 * 1024

GROUP_LANE0 = 0
EXPERT_LANE0 = N_GROUPS

LOG2E = 1.4426950408889634
NEG_BIG = -1e30

_COL_Q = 0
_COL_K = SB_WIDTH
_COL_V = 2 * SB_WIDTH
_COL_SU = 3 * SB_WIDTH
_COL_SV = 3 * SB_WIDTH + SGU_WIDTH
_COL_GA = 3 * SB_WIDTH + 2 * SGU_WIDTH


def _COL_GB(d):
    return _COL_GA + d


def _cparams(*sem):
    return pltpu.CompilerParams(dimension_semantics=sem, vmem_limit_bytes=VMEM_LIMIT_BYTES)


def _ln_rows(x):
    mu = jnp.mean(x, axis=-1, keepdims=True)
    xc = x - mu
    var = jnp.mean(xc * xc, axis=-1, keepdims=True)
    return xc * lax.rsqrt(var + LN_EPS)


def _sigmoid(x):
    return 1.0 / (1.0 + jnp.exp(-x))


def _ada_kernel(c_ref, w_ref, b_ref, o_ref, *, batch):
    tn = w_ref.shape[1]
    rows = []
    for b in range(batch):
        c = c_ref[b]
        ca = c * _sigmoid(c)
        rows.append(jnp.concatenate(
            [jnp.sum(ca * w_ref[:, j * LANES:(j + 1) * LANES], axis=0, keepdims=True)
             for j in range(tn // LANES)], axis=1) + b_ref[...])
    rows.append(jnp.zeros((SUBLANES - batch, tn), F32))
    o_ref[...] = jnp.concatenate(rows, axis=0)


def _ada(c, w_ada, b_ada, tn=1024):
    b, d = c.shape
    n_out = w_ada.shape[1]
    assert b < SUBLANES
    c_lanes = jnp.broadcast_to(c[:, :, None], (b, d, LANES))
    out = pl.pallas_call(
        functools.partial(_ada_kernel, batch=b),
        grid=(n_out // tn,),
        in_specs=[pl.BlockSpec((b, d, LANES), lambda j: (0, 0, 0)),
                  pl.BlockSpec((d, tn), lambda j: (0, j)),
                  pl.BlockSpec((1, tn), lambda j: (0, j))],
        out_specs=pl.BlockSpec((SUBLANES, tn), lambda j: (0, j)),
        out_shape=jax.ShapeDtypeStruct((SUBLANES, n_out), F32),
        compiler_params=_cparams("arbitrary"),
        name="ada",
    )(c_lanes, w_ada, b_ada.reshape(1, n_out))
    return out[:b]


def _inproj_kernel(x_ref, sh_ref, sc_ref, w_ref, cs_ref, o_ref, h_ref):
    @pl.when(pl.program_id(1) == 0)
    def _():
        h = _ln_rows(x_ref[...]) * (1.0 + sc_ref[0]) + sh_ref[0]
        h_ref[...] = h.astype(BF16)

    acc = jnp.dot(h_ref[...], w_ref[...], preferred_element_type=F32)
    o_ref[...] = (acc * cs_ref[...]).astype(BF16)


def _inproj(x2, mod3, w_in_bf, colscale, seq, tm=1024, tn=2304):
    n, d = x2.shape
    n_cols = w_in_bf.shape[1]
    tm = min(tm, seq)
    return pl.pallas_call(
        _inproj_kernel,
        grid=(n // tm, n_cols // tn),
        in_specs=[pl.BlockSpec((tm, d), lambda i, j: (i, 0)),
                  pl.BlockSpec((1, 1, d), lambda i, j: (((i * tm) // seq) * 6 + 0, 0, 0)),
                  pl.BlockSpec((1, 1, d), lambda i, j: (((i * tm) // seq) * 6 + 1, 0, 0)),
                  pl.BlockSpec((d, tn), lambda i, j: (0, j)),
                  pl.BlockSpec((1, tn), lambda i, j: (0, j))],
        out_specs=pl.BlockSpec((tm, tn), lambda i, j: (i, j)),
        out_shape=jax.ShapeDtypeStruct((n, n_cols), BF16),
        scratch_shapes=[pltpu.VMEM((tm, d), BF16)],
        compiler_params=_cparams("arbitrary", "arbitrary"),
        name="inproj",
    )(x2, mod3, mod3, w_in_bf, colscale)


Z2_MAX = 100.0
STICK_EXIT_LOG2 = 150.0


def _attn_kernel(q_ref, k_ref, v_ref, o_ref, *, tq, tk, heads):
    qi = pl.program_id(2)
    nd = tq // tk
    jd = qi * nd
    trow = lax.broadcasted_iota(I32, (tk, tk), 0)
    tcol = lax.broadcasted_iota(I32, (tk, tk), 1)
    tri = (trow > tcol).astype(BF16)
    hd = SB_HEAD_DIM
    qs = [q_ref[0, :, h * hd:(h + 1) * hd] for h in range(heads)]

    def block(j, runs, accs, causal):
        off = pl.multiple_of(j * tk, tk)
        hs = range(heads)
        z2s = [lax.dot_general(qs[h], k_ref[0, pl.ds(off, tk), h * hd:(h + 1) * hd],
                               (((1,), (1,)), ((), ())), preferred_element_type=F32) for h in hs]
        z2s = [lax.clamp(-Z2_MAX, z2, Z2_MAX) for z2 in z2s]
        sps = [jnp.log(1.0 + jnp.exp2(z2)) for z2 in z2s]
        if causal is not None:
            sps = [jnp.where(causal, sp, 0.0) for sp in sps]
        laters = [jnp.dot(sp.astype(BF16), tri, preferred_element_type=F32) for sp in sps]
        ws = [jnp.exp2(z2s[h] - LOG2E * (sps[h] + (laters[h] + runs[h]))) for h in hs]
        if causal is not None:
            ws = [jnp.where(causal, w, 0.0) for w in ws]
        new_accs = [accs[h] + jnp.dot(ws[h].astype(BF16),
                                      v_ref[0, pl.ds(off, tk), h * hd:(h + 1) * hd],
                                      preferred_element_type=F32) for h in hs]
        new_runs = [runs[h] + jnp.sum(sps[h], axis=1, keepdims=True) for h in hs]
        return tuple(new_runs), tuple(new_accs)

    def mass_left(runs):
        low = runs[0]
        for r in runs[1:]:
            low = jnp.minimum(low, r)
        return (jnp.min(low) * LOG2E < STICK_EXIT_LOG2).astype(I32)

    def diag_body(t, carry):
        c = nd - 1 - t
        row = lax.broadcasted_iota(I32, (tq, tk), 0)
        col = lax.broadcasted_iota(I32, (tq, tk), 1)
        causal = col + c * tk < row
        return block(jd + c, carry[0], carry[1], causal)

    def more(carry):
        t, go, _, _ = carry
        return (t < jd) & (go > 0)

    def body(carry):
        t, _, runs, accs = carry
        runs, accs = block(jd - 1 - t, runs, accs, None)
        return t + 1, mass_left(runs), runs, accs

    runs = tuple(jnp.zeros((tq, 1), F32) for _ in range(heads))
    accs = tuple(jnp.zeros((tq, hd), F32) for _ in range(heads))
    runs, accs = lax.fori_loop(0, nd, diag_body, (runs, accs))
    _, _, runs, accs = lax.while_loop(more, body, (jnp.int32(0), mass_left(runs), runs, accs))
    for h in range(heads):
        o_ref[0, :, h * hd:(h + 1) * hd] = accs[h].astype(BF16)


def _attention(proj3, d, tq=256, tk=256, heads=4):
    b, s, _ = proj3.shape
    tk = min(tk, s)
    tq = min(tq, s)
    assert tq % tk == 0 and s % tq == 0
    width = heads * SB_HEAD_DIM
    qc = _COL_Q // width
    kc = _COL_K // width
    vc = _COL_V // width
    return pl.pallas_call(
        functools.partial(_attn_kernel, tq=tq, tk=tk, heads=heads),
        grid=(b, SB_HEADS // heads, s // tq),
        in_specs=[pl.BlockSpec((1, tq, width), lambda bi, g, qi: (bi, qi, qc + g)),
                  pl.BlockSpec((1, s, width), lambda bi, g, qi: (bi, 0, kc + g)),
                  pl.BlockSpec((1, s, width), lambda bi, g, qi: (bi, 0, vc + g))],
        out_specs=pl.BlockSpec((1, tq, width), lambda bi, g, qi: (bi, qi, g)),
        out_shape=jax.ShapeDtypeStruct((b, s, SB_WIDTH), BF16),
        compiler_params=_cparams("arbitrary", "arbitrary", "arbitrary"),
        name="attn",
    )(proj3, proj3, proj3)


def _erf(x):
    return lax.erf(x)


def _gelu(x):
    return 0.5 * x * (1.0 + _erf(x * (1.0 / math.sqrt(2.0))))


def _mix_kernel(oa_ref, su_ref, sv_ref, ga0_ref, ga1_ref, gb0_ref, gb1_ref, x_ref,
                sw_ref, sbias_ref, sg_ref, sb_ref,
                wpa_ref, wpb_ref, wo_ref,
                g1_ref, ln1g_ref, ln1b_ref, sh2_ref, sc2_ref,
                wr_ref, br_ref,
                x1_ref, h2_ref, route_ref, oh1_ref, oh2_ref, *, tm, alpha):
    u = _gelu(su_ref[...].astype(F32))
    v = _gelu(sv_ref[...].astype(F32))
    v = _ln_rows(v) * sg_ref[...] + sb_ref[...]
    vb = v.astype(BF16)
    pos_t = lax.broadcasted_iota(I32, (SGU_BLOCK, SGU_BLOCK), 0)
    pos_s = lax.broadcasted_iota(I32, (SGU_BLOCK, SGU_BLOCK), 1)
    chunk_causal = (pos_s // CHUNK) <= (pos_t // CHUNK)
    n_pos_blocks = tm // SGU_BLOCK
    per_group = []
    for g in range(SGU_GROUPS):
        c0 = g * SGU_GROUP_DIM
        wg = jnp.where(chunk_causal, sw_ref[g], 0.0).astype(BF16)
        rhs = jnp.concatenate(
            [vb[nb * SGU_BLOCK:(nb + 1) * SGU_BLOCK, c0:c0 + SGU_GROUP_DIM]
             for nb in range(n_pos_blocks)], axis=1)
        per_group.append(jnp.dot(wg, rhs, preferred_element_type=F32))
    mixed = jnp.concatenate(
        [jnp.concatenate([per_group[g][:, nb * SGU_GROUP_DIM:(nb + 1) * SGU_GROUP_DIM]
                          for g in range(SGU_GROUPS)], axis=1) + sbias_ref[...]
         for nb in range(n_pos_blocks)], axis=0)
    out_b = (u * mixed).astype(BF16)

    pa = jnp.dot(oa_ref[...], wpa_ref[...], preferred_element_type=F32)
    pb = jnp.dot(out_b, wpb_ref[...], preferred_element_type=F32)
    ga = jnp.concatenate([ga0_ref[...], ga1_ref[...]], axis=1).astype(F32)
    gb = jnp.concatenate([gb0_ref[...], gb1_ref[...]], axis=1).astype(F32)
    merged = _sigmoid(ga) * pa + _sigmoid(gb) * pb
    y = jnp.dot(merged.astype(BF16), wo_ref[...], preferred_element_type=F32)

    r = alpha * x_ref[...] + g1_ref[0] * y
    x1 = _ln_rows(r) * ln1g_ref[...] + ln1b_ref[...]
    x1_ref[...] = x1
    h2 = _ln_rows(x1) * (1.0 + sc2_ref[0]) + sh2_ref[0]
    chunks = h2.shape[1] // LANES
    for c in range(chunks):
        h2_ref[pl.ds(c, tm, stride=chunks), :] = h2[:, c * LANES:(c + 1) * LANES]

    h_hi = h2.astype(BF16)
    h_lo = (h2 - h_hi.astype(F32)).astype(BF16)
    hi_w = jnp.dot(h_hi, wr_ref[...], preferred_element_type=F32)
    lo_w = jnp.dot(h_lo, wr_ref[:, :LANES], preferred_element_type=F32)
    logits = (hi_w[:, :LANES] + hi_w[:, LANES:]) + lo_w + br_ref[...]
    lane = lax.broadcasted_iota(I32, logits.shape, 1)
    gl = jnp.where(lane < N_GROUPS, logits, NEG_BIG)
    gmax = jnp.max(gl, axis=1, keepdims=True)
    g_sel = jnp.min(jnp.where(gl == gmax, lane, LANES), axis=1, keepdims=True)
    p_group = 1.0 / jnp.sum(jnp.exp(gl - gmax), axis=1, keepdims=True)
    lo_lane = EXPERT_LANE0 + g_sel * EXPERTS_PER_GROUP
    in_group = (lane >= lo_lane) & (lane < lo_lane + EXPERTS_PER_GROUP)
    el = jnp.where(in_group, logits, NEG_BIG)
    m1 = jnp.max(el, axis=1, keepdims=True)
    i1 = jnp.min(jnp.where(el == m1, lane, LANES), axis=1, keepdims=True)
    el2 = jnp.where(lane == i1, NEG_BIG, el)
    m2 = jnp.max(el2, axis=1, keepdims=True)
    i2 = jnp.min(jnp.where(el2 == m2, lane, LANES), axis=1, keepdims=True)
    e21 = jnp.exp(m2 - m1)
    w1 = p_group / (1.0 + e21)
    w2 = p_group * e21 / (1.0 + e21)
    route = jnp.where(lane == 0, w1, 0.0)
    route = jnp.where(lane == 1, w2, route)
    route = jnp.where(lane == 2, (i1 - EXPERT_LANE0).astype(F32), route)
    route = jnp.where(lane == 3, (i2 - EXPERT_LANE0).astype(F32), route)
    route_ref[...] = route
    oh1_ref[...] = jnp.where(lane == i1, 1.0, 0.0).astype(BF16)
    oh2_ref[...] = jnp.where(lane == i2, 1.0, 0.0).astype(BF16)


def _mix(out_a, proj, x2, mod3, sgu_w, sgu_bias_full, sgu_g, sgu_b, wpa, wpb, wo,
         ln1_g, ln1_b, wr, br, seq, alpha, tm=256):
    n, d = x2.shape
    tm = min(tm, seq)
    def pcol(start, width):
        assert start % width == 0
        return start // width
    su_c = pcol(_COL_SU, SGU_WIDTH)
    sv_c = pcol(_COL_SV, SGU_WIDTH)
    half = d // 2
    ga_c = pcol(_COL_GA, half)
    gb_c = pcol(_COL_GB(d), half)
    bidx = lambda i: (i * tm) // seq
    row = lambda i: (i, 0)
    const2 = lambda i: (0, 0)
    const3 = lambda i: (0, 0, 0)
    return pl.pallas_call(
        functools.partial(_mix_kernel, tm=tm, alpha=alpha),
        grid=(n // tm,),
        in_specs=[pl.BlockSpec((tm, SB_WIDTH), row),
                  pl.BlockSpec((tm, SGU_WIDTH), lambda i: (i, su_c)),
                  pl.BlockSpec((tm, SGU_WIDTH), lambda i: (i, sv_c)),
                  pl.BlockSpec((tm, half), lambda i: (i, ga_c)),
                  pl.BlockSpec((tm, half), lambda i: (i, ga_c + 1)),
                  pl.BlockSpec((tm, half), lambda i: (i, gb_c)),
                  pl.BlockSpec((tm, half), lambda i: (i, gb_c + 1)),
                  pl.BlockSpec((tm, d), row),
                  pl.BlockSpec((SGU_GROUPS, SGU_BLOCK, SGU_BLOCK), const3),
                  pl.BlockSpec((SGU_BLOCK, SGU_WIDTH), const2),
                  pl.BlockSpec((1, SGU_WIDTH), const2),
                  pl.BlockSpec((1, SGU_WIDTH), const2),
                  pl.BlockSpec((SB_WIDTH, d), const2),
                  pl.BlockSpec((SGU_WIDTH, d), const2),
                  pl.BlockSpec((d, d), const2),
                  pl.BlockSpec((1, 1, d), lambda i: (bidx(i) * 6 + 2, 0, 0)),
                  pl.BlockSpec((1, d), const2),
                  pl.BlockSpec((1, d), const2),
                  pl.BlockSpec((1, 1, d), lambda i: (bidx(i) * 6 + 3, 0, 0)),
                  pl.BlockSpec((1, 1, d), lambda i: (bidx(i) * 6 + 4, 0, 0)),
                  pl.BlockSpec((d, 2 * LANES), const2),
                  pl.BlockSpec((1, LANES), const2)],
        out_specs=[pl.BlockSpec((tm, d), row),
                   pl.BlockSpec((tm * (d // LANES), LANES), row),
                   pl.BlockSpec((tm, LANES), row),
                   pl.BlockSpec((tm, LANES), row),
                   pl.BlockSpec((tm, LANES), row)],
        out_shape=[jax.ShapeDtypeStruct((n, d), F32),
                   jax.ShapeDtypeStruct((n * (d // LANES), LANES), F32),
                   jax.ShapeDtypeStruct((n, LANES), F32),
                   jax.ShapeDtypeStruct((n, LANES), BF16),
                   jax.ShapeDtypeStruct((n, LANES), BF16)],
        compiler_params=_cparams("arbitrary"),
        name="mix",
    )(out_a, proj, proj, proj, proj, proj, proj, x2, sgu_w, sgu_bias_full, sgu_g, sgu_b,
      wpa, wpb, wo, mod3, ln1_g, ln1_b, mod3, mod3, wr, br)


def _plan_kernel(oh1_ref, oh2_ref, dest_ref, blk_ref, cnt_ref, start_ref, *, tm, bm, nblk_pad):
    phase = pl.program_id(0)
    i = pl.program_id(1)
    oh1 = oh1_ref[...]
    oh2 = oh2_ref[...]
    both = oh1 + oh2
    lane = lax.broadcasted_iota(I32, (SUBLANES, LANES), 1)

    @pl.when((phase == 0) & (i == 0))
    def _():
        cnt_ref[...] = jnp.zeros_like(cnt_ref)

    @pl.when(phase == 0)
    def _():
        ones = jnp.ones((SUBLANES, tm), BF16)
        cnt_ref[...] += jnp.dot(ones, both, preferred_element_type=F32)

    @pl.when((phase == 1) & (i == 0))
    def _():
        cnt = cnt_ref[...]
        padded = jnp.floor((cnt + (bm - 1)) * (1.0 / bm)) * bm
        r = lax.broadcasted_iota(I32, (LANES, LANES), 0)
        c = lax.broadcasted_iota(I32, (LANES, LANES), 1)
        upper = (r < c).astype(F32)
        starts = jnp.dot(padded, upper, preferred_element_type=F32,
                         precision=lax.Precision.HIGHEST)
        start_ref[...] = starts
        cnt_ref[...] = jnp.zeros_like(cnt_ref)
        ends = starts + padded
        is_exp = (lane[0:1] >= EXPERT_LANE0) & (lane[0:1] < EXPERT_LANE0 + N_EXPERTS)
        bstart = (lax.broadcasted_iota(I32, (nblk_pad, LANES), 0) * bm).astype(F32)
        done = jnp.where(is_exp & (ends[0:1] <= bstart), 1.0, 0.0)
        bexp = jnp.minimum(jnp.sum(done, axis=1, keepdims=True), N_EXPERTS - 1.0)
        total = jnp.sum(jnp.where(is_exp, padded[0:1], 0.0), axis=1, keepdims=True)
        blane = lax.broadcasted_iota(I32, (nblk_pad, LANES), 1)
        blk = jnp.where(blane == 0, bexp, jnp.where(blane == 1, total * (1.0 / bm), 0.0))
        blk_ref[...] = blk.astype(I32)

    @pl.when(phase == 1)
    def _():
        r = lax.broadcasted_iota(I32, (tm, tm), 0)
        c = lax.broadcasted_iota(I32, (tm, tm), 1)
        lower = (c < r).astype(BF16)
        before = jnp.dot(lower, both, preferred_element_type=F32) + cnt_ref[0:1]
        pos = before + start_ref[0:1]
        d1 = jnp.sum(pos * oh1.astype(F32), axis=1, keepdims=True)
        d2 = jnp.sum(pos * oh2.astype(F32), axis=1, keepdims=True)
        dlane = lax.broadcasted_iota(I32, (tm, LANES), 1)
        dest = jnp.where(dlane == 0, d1, jnp.where(dlane == 1, d2, 0.0))
        dest_ref[...] = dest.astype(I32)
        ones = jnp.ones((SUBLANES, tm), BF16)
        cnt_ref[...] += jnp.dot(ones, both, preferred_element_type=F32)


def _plan(oh1, oh2, bm, nblk, tm=1024):
    n = oh1.shape[0]
    tm = min(tm, n)
    nblk_pad = -(-nblk // SUBLANES) * SUBLANES
    dest, blk = pl.pallas_call(
        functools.partial(_plan_kernel, tm=tm, bm=bm, nblk_pad=nblk_pad),
        grid=(2, n // tm),
        in_specs=[pl.BlockSpec((tm, LANES), lambda p, i: (i, 0)),
                  pl.BlockSpec((tm, LANES), lambda p, i: (i, 0))],
        out_specs=[pl.BlockSpec((tm, LANES), lambda p, i: (i * p, 0)),
                   pl.BlockSpec((nblk_pad, LANES), lambda p, i: (0, 0))],
        out_shape=[jax.ShapeDtypeStruct((n, LANES), I32),
                   jax.ShapeDtypeStruct((nblk_pad, LANES), I32)],
        scratch_shapes=[pltpu.VMEM((SUBLANES, LANES), F32),
                        pltpu.VMEM((SUBLANES, LANES), F32)],
        compiler_params=_cparams("arbitrary", "arbitrary"),
        name="plan",
    )(oh1, oh2)
    return dest, blk


ROW_UNROLL = 8


def _invert_kernel(d1_ref, d2_ref, zero_ref, rt_ref, sem, *, tb):
    i = pl.program_id(0)

    @pl.when(i == 0)
    def _():
        clear = pltpu.make_async_copy(zero_ref, rt_ref, sem)
        clear.start()
        clear.wait()

    def put(j, carry):
        for u in range(ROW_UNROLL):
            t = j * ROW_UNROLL + u
            rt_ref[d1_ref[0, 0, t]] = i * tb + t
            rt_ref[d2_ref[0, 0, t]] = i * tb + t
        return carry

    lax.fori_loop(0, tb // ROW_UNROLL, put, 0)


def _invert(dest1, dest2, n_rows, tb=2048):
    n = dest1.shape[0]
    tb = min(tb, n)
    assert n % tb == 0 and tb % ROW_UNROLL == 0
    blocks = lambda i: (i, 0, 0)
    return pl.pallas_call(
        functools.partial(_invert_kernel, tb=tb),
        grid=(n // tb,),
        in_specs=[pl.BlockSpec((1, 1, tb), blocks, memory_space=pltpu.SMEM),
                  pl.BlockSpec((1, 1, tb), blocks, memory_space=pltpu.SMEM),
                  pl.BlockSpec(memory_space=pl.ANY)],
        out_specs=pl.BlockSpec(memory_space=pltpu.SMEM),
        out_shape=jax.ShapeDtypeStruct((n_rows,), I32),
        scratch_shapes=[pltpu.SemaphoreType.DMA(())],
        compiler_params=_cparams("arbitrary"),
        name="invert",
    )(dest1.reshape(n // tb, 1, tb), dest2.reshape(n // tb, 1, tb), jnp.zeros((n_rows,), I32))


def _slab(idx, chunks):
    return pl.ds(pl.multiple_of(idx * chunks, chunks), chunks)


def _rows_from_slabs(buf_ref, slot, rows, chunks):
    return jnp.concatenate(
        [buf_ref[slot, pl.ds(c, rows, stride=chunks), :] for c in range(chunks)], axis=1)


GATHER_AHEAD = 6


def _experts_kernel(be_ref, na_ref, first_ref, wslot_ref, nxt_ref, *refs, bm, chunks):
    head_refs = refs[:GATHER_AHEAD]
    (rtn_ref, h_ref, wg_hbm, wu_hbm, wd_hbm, o_ref,
     xbuf, sem, wgf, wuf, wdf, wsem, wgb, wub, wdb) = refs[GATHER_AHEAD:]
    i = pl.program_id(0)
    n_active = na_ref[0]

    def gather(tok_ref, slot):
        def trip(r8, carry):
            for u in range(ROW_UNROLL):
                r = r8 * ROW_UNROLL + u
                pltpu.make_async_copy(h_ref.at[_slab(tok_ref[0, 0, r], chunks), :],
                                      xbuf.at[slot, _slab(r, chunks), :], sem.at[slot]).start()
            return carry
        lax.fori_loop(0, bm // ROW_UNROLL, trip, 0)

    def weight_copies(e, ws):
        return (pltpu.make_async_copy(wg_hbm.at[e], wgf.at[ws], wsem.at[ws]),
                pltpu.make_async_copy(wu_hbm.at[e], wuf.at[ws], wsem.at[ws]),
                pltpu.make_async_copy(wd_hbm.at[e], wdf.at[ws], wsem.at[ws]))

    def fetch_weights(e, ws):
        for cp in weight_copies(e, ws):
            cp.start(priority=1)

    def gather_wait(slot):
        pltpu.make_async_copy(h_ref.at[pl.ds(0, bm * chunks), :], xbuf.at[slot],
                              sem.at[slot]).wait()

    @pl.when(i == 0)
    def _():
        fetch_weights(be_ref[0], wslot_ref[0])
        gather(head_refs[0], 0)
        for k in range(1, GATHER_AHEAD):
            pl.when(k < n_active)(functools.partial(gather, head_refs[k], k))

    @pl.when(i < n_active)
    def _():
        slot = lax.rem(i, GATHER_AHEAD + 1)
        gather_wait(slot)

        @pl.when(i + GATHER_AHEAD < n_active)
        def _():
            gather(rtn_ref, lax.rem(i + GATHER_AHEAD, GATHER_AHEAD + 1))

        @pl.when(first_ref[i] == 1)
        def _():
            ws = wslot_ref[i]
            for cp in weight_copies(be_ref[i], ws):
                cp.wait()

            @pl.when(nxt_ref[i] >= 0)
            def _():
                fetch_weights(nxt_ref[i], 1 - ws)

            wgb[...] = wgf[ws].astype(BF16)
            wub[...] = wuf[ws].astype(BF16)
            wdb[...] = wdf[ws].astype(BF16)

        x = _rows_from_slabs(xbuf, slot, bm, chunks).astype(BF16)
        g = jnp.dot(x, wgb[...], preferred_element_type=F32)
        u = jnp.dot(x, wub[...], preferred_element_type=F32)
        hb = (g * _sigmoid(g) * u).astype(BF16)
        y = jnp.dot(hb, wdb[...], preferred_element_type=F32)
        for c in range(chunks):
            o_ref[pl.ds(c, bm, stride=chunks), :] = y[:, c * LANES:(c + 1) * LANES]

    @pl.when(i >= n_active)
    def _():
        o_ref[...] = jnp.zeros_like(o_ref)


def _experts(block_expert, n_active, row_tok, h2s, w_gate, w_up, w_down, bm):
    n_rows = row_tok.shape[0]
    nblk = n_rows // bm
    d, de = w_gate.shape[1], w_gate.shape[2]
    chunks = d // LANES
    assert bm % ROW_UNROLL == 0
    assert nblk > GATHER_AHEAD
    row_blocks = row_tok.reshape(nblk, 1, bm)

    idx = jnp.arange(nblk, dtype=I32)
    prev = jnp.concatenate([block_expert[:1] - 1, block_expert[:-1]])
    first = ((block_expert != prev) & (idx < n_active[0])).astype(I32)
    wslot = ((jnp.cumsum(first) - 1) % 2).astype(I32)
    run_start = lax.cummin(jnp.where(first > 0, idx, nblk), axis=0, reverse=True)
    next_start = jnp.concatenate([run_start[1:], jnp.full((1,), nblk, I32)])
    nxt = jnp.where(next_start < nblk,
                    block_expert[jnp.minimum(next_start, nblk - 1)], -1).astype(I32)

    return pl.pallas_call(
        functools.partial(_experts_kernel, bm=bm, chunks=chunks),
        grid_spec=pltpu.PrefetchScalarGridSpec(
            num_scalar_prefetch=5,
            grid=(nblk,),
            in_specs=[pl.BlockSpec((1, 1, bm), functools.partial(lambda k, i, *_: (k, 0, 0), k),
                                   memory_space=pltpu.SMEM) for k in range(GATHER_AHEAD)]
                     + [pl.BlockSpec((1, 1, bm),
                                     lambda i, *_: (jnp.minimum(i + GATHER_AHEAD, nblk - 1), 0, 0),
                                     memory_space=pltpu.SMEM)]
                     + [pl.BlockSpec(memory_space=pl.ANY)] * 4,
            out_specs=pl.BlockSpec((bm * chunks, LANES), lambda i, *_: (i, 0)),
            scratch_shapes=[pltpu.VMEM((GATHER_AHEAD + 1, bm * chunks, LANES), F32),
                            pltpu.SemaphoreType.DMA((GATHER_AHEAD + 1,)),
                            pltpu.VMEM((2, d, de), F32),
                            pltpu.VMEM((2, d, de), F32),
                            pltpu.VMEM((2, de, d), F32),
                            pltpu.SemaphoreType.DMA((2,)),
                            pltpu.VMEM((d, de), BF16),
                            pltpu.VMEM((d, de), BF16),
                            pltpu.VMEM((de, d), BF16)],
        ),
        out_shape=jax.ShapeDtypeStruct((n_rows * chunks, LANES), F32),
        compiler_params=_cparams("arbitrary"),
        name="experts",
    )(block_expert, n_active, first, wslot, nxt, *([row_blocks] * (GATHER_AHEAD + 1)),
      h2s, w_gate, w_up, w_down)


def _combine_kernel(d1_ref, d2_ref, ys_ref, x1_ref, route_ref, g2_ref, lg_ref, lb_ref,
                    o_ref, y1buf, y2buf, sem, *, tm, chunks, alpha):
    i = pl.program_id(0)
    n_tiles = pl.num_programs(0)

    def row_copies(tile, t, slot):
        tok = tile * tm + t
        return (pltpu.make_async_copy(ys_ref.at[_slab(d1_ref[tok], chunks), :],
                                      y1buf.at[slot, _slab(t, chunks), :], sem.at[slot]),
                pltpu.make_async_copy(ys_ref.at[_slab(d2_ref[tok], chunks), :],
                                      y2buf.at[slot, _slab(t, chunks), :], sem.at[slot]))

    def gather(tile, slot):
        def trip(t8, carry):
            for u in range(ROW_UNROLL):
                for k, cp in enumerate(row_copies(tile, t8 * ROW_UNROLL + u, slot)):
                    cp.start(priority=k)
            return carry
        lax.fori_loop(0, tm // ROW_UNROLL, trip, 0)

    def gather_wait(slot):
        for buf in (y1buf, y2buf):
            pltpu.make_async_copy(ys_ref.at[pl.ds(0, tm * chunks), :], buf.at[slot],
                                  sem.at[slot]).wait()

    @pl.when(i == 0)
    def _():
        gather(0, 0)

    slot = lax.rem(i, 2)
    gather_wait(slot)

    @pl.when(i + 1 < n_tiles)
    def _():
        gather(i + 1, 1 - slot)

    route = route_ref[...]
    y = (route[:, 0:1] * _rows_from_slabs(y1buf, slot, tm, chunks)
         + route[:, 1:2] * _rows_from_slabs(y2buf, slot, tm, chunks))
    r = alpha * x1_ref[...] + g2_ref[0] * y
    o_ref[...] = _ln_rows(r) * lg_ref[...] + lb_ref[...]


def _combine(dest1, dest2, ys, x1, route, mod3, ln2_g, ln2_b, seq, alpha, tm=256):
    n, d = x1.shape
    tm = min(tm, seq)
    chunks = d // LANES
    assert tm % ROW_UNROLL == 0
    tok = lambda i, d1, d2: (i, 0)
    const = lambda i, d1, d2: (0, 0)
    return pl.pallas_call(
        functools.partial(_combine_kernel, tm=tm, chunks=chunks, alpha=alpha),
        grid_spec=pltpu.PrefetchScalarGridSpec(
            num_scalar_prefetch=2,
            grid=(n // tm,),
            in_specs=[pl.BlockSpec(memory_space=pl.ANY),
                      pl.BlockSpec((tm, d), tok),
                      pl.BlockSpec((tm, LANES), tok),
                      pl.BlockSpec((1, 1, d), lambda i, d1, d2: (((i * tm) // seq) * 6 + 5, 0, 0)),
                      pl.BlockSpec((1, d), const),
                      pl.BlockSpec((1, d), const)],
            out_specs=pl.BlockSpec((tm, d), tok),
            scratch_shapes=[pltpu.VMEM((2, tm * chunks, LANES), F32),
                            pltpu.VMEM((2, tm * chunks, LANES), F32),
                            pltpu.SemaphoreType.DMA((2,))],
        ),
        out_shape=jax.ShapeDtypeStruct((n, d), F32),
        compiler_params=_cparams("arbitrary"),
        name="combine",
    )(dest1, dest2, ys, x1, route, mod3, ln2_g, ln2_b)


EXPERT_BLOCK_ROWS = 256


def kernel(x, c, w_ada, b_ada, w_in, sgu_w, sgu_b, sgu_ln_g, sgu_ln_b, w_proj_a, w_proj_b,
           w_out, ln1_g, ln1_b, w_group, b_group, w_router, b_router, w_gate, w_up, w_down,
           ln2_g, ln2_b):
    b, s, d = x.shape
    n = b * s
    depth = w_ada.shape[0]
    alpha = (2.0 * depth) ** 0.25
    sub = d // LANES
    bm = EXPERT_BLOCK_ROWS
    n_assign = 2 * n
    n_rows = n_assign + N_EXPERTS * bm
    nblk = n_rows // bm

    in_cols = w_in.shape[2]
    colscale = jnp.ones((1, in_cols), F32).at[:, _COL_Q:_COL_K].set(
        SB_HEAD_DIM ** -0.5 * LOG2E)

    for l in range(depth):
        mod = _ada(c, w_ada[l], b_ada[l])
        mod3 = mod.reshape(b * 6, 1, d)

        x2 = x.reshape(n, d)
        proj = _inproj(x2, mod3, w_in[l].astype(BF16), colscale, s)
        out_a = _attention(proj.reshape(b, s, in_cols), d).reshape(n, SB_WIDTH)

        wr = jnp.zeros((d, LANES), F32)
        wr = wr.at[:, GROUP_LANE0:GROUP_LANE0 + N_GROUPS].set(w_group[l])
        wr = wr.at[:, EXPERT_LANE0:EXPERT_LANE0 + N_EXPERTS].set(w_router[l])
        wr_hi = wr.astype(BF16)
        wr_lo = (wr - wr_hi.astype(F32)).astype(BF16)
        wr = jnp.concatenate([wr_hi, wr_lo], axis=1)
        br = jnp.zeros((1, LANES), F32)
        br = br.at[0, GROUP_LANE0:GROUP_LANE0 + N_GROUPS].set(b_group[l])
        br = br.at[0, EXPERT_LANE0:EXPERT_LANE0 + N_EXPERTS].set(b_router[l])
        sgu_bias_full = jnp.repeat(sgu_b[l].T, SGU_GROUP_DIM, axis=1)

        x1, h2s, route, oh1, oh2 = _mix(
            out_a, proj, x2, mod3, sgu_w[l], sgu_bias_full,
            sgu_ln_g[l].reshape(1, -1), sgu_ln_b[l].reshape(1, -1),
            w_proj_a[l].astype(BF16), w_proj_b[l].astype(BF16), w_out[l].astype(BF16),
            ln1_g[l].reshape(1, d), ln1_b[l].reshape(1, d), wr, br, s, alpha)

        dest, blk = _plan(oh1, oh2, bm, nblk)
        dest1 = dest[:, 0]
        dest2 = dest[:, 1]
        block_expert = blk[:nblk, 0]
        n_active = blk[0:1, 1]

        row_tok = _invert(dest1, dest2, n_rows)
        ys = _experts(block_expert, n_active, row_tok, h2s, w_gate[l], w_up[l], w_down[l], bm)
        out = _combine(dest1, dest2, ys, x1, route, mod3, ln2_g[l].reshape(1, d),
                       ln2_b[l].reshape(1, d), s, alpha)
        x = out.reshape(b, s, d)
    return x
```

```python
import functools
import math

import jax
import jax.numpy as jnp
from jax import lax
from jax.experimental import pallas as pl
from jax.experimental.pallas import tpu as pltpu

F32 = jnp.float32
BF16 = jnp.bfloat16
I32 = jnp.int32

CHUNK = 64
SB_HEADS = 8
SB_HEAD_DIM = 128
SB_WIDTH = SB_HEADS * SB_HEAD_DIM
SGU_GROUPS = 8
SGU_GROUP_DIM = 128
SGU_WIDTH = SGU_GROUPS * SGU_GROUP_DIM
SGU_BLOCK = 128
N_GROUPS = 4
EXPERTS_PER_GROUP = 8
N_EXPERTS = N_GROUPS * EXPERTS_PER_GROUP
LN_EPS = 1e-5

LANES = 128
SUBLANES = 8
VMEM_LIMIT_BYTES = 60000 * 1024

GROUP_LANE0 = 0
EXPERT_LANE0 = N_GROUPS

LOG2E = 1.4426950408889634
NEG_BIG = -1e30

_COL_Q = 0
_COL_K = SB_WIDTH
_COL_V = 2 * SB_WIDTH
_COL_SU = 3 * SB_WIDTH
_COL_SV = 3 * SB_WIDTH + SGU_WIDTH
_COL_GA = 3 * SB_WIDTH + 2 * SGU_WIDTH


def _COL_GB(d):
    return _COL_GA + d


def _cparams(*sem):
    return pltpu.CompilerParams(dimension_semantics=sem, vmem_limit_bytes=VMEM_LIMIT_BYTES)


def _ln_rows(x):
    mu = jnp.mean(x, axis=-1, keepdims=True)
    xc = x - mu
    var = jnp.mean(xc * xc, axis=-1, keepdims=True)
    return xc * lax.rsqrt(var + LN_EPS)


def _sigmoid(x):
    return 1.0 / (1.0 + jnp.exp(-x))


def _ada_kernel(c_ref, w_ref, b_ref, o_ref, *, batch):
    tn = w_ref.shape[1]
    rows = []
    for b in range(batch):
        c = c_ref[b]
        ca = c * _sigmoid(c)
        rows.append(jnp.concatenate(
            [jnp.sum(ca * w_ref[:, j * LANES:(j + 1) * LANES], axis=0, keepdims=True)
             for j in range(tn // LANES)], axis=1) + b_ref[...])
    rows.append(jnp.zeros((SUBLANES - batch, tn), F32))
    o_ref[...] = jnp.concatenate(rows, axis=0)


def _ada(c, w_ada, b_ada, tn=1024):
    b, d = c.shape
    n_out = w_ada.shape[1]
    assert b < SUBLANES
    c_lanes = jnp.broadcast_to(c[:, :, None], (b, d, LANES))
    out = pl.pallas_call(
        functools.partial(_ada_kernel, batch=b),
        grid=(n_out // tn,),
        in_specs=[pl.BlockSpec((b, d, LANES), lambda j: (0, 0, 0)),
                  pl.BlockSpec((d, tn), lambda j: (0, j)),
                  pl.BlockSpec((1, tn), lambda j: (0, j))],
        out_specs=pl.BlockSpec((SUBLANES, tn), lambda j: (0, j)),
        out_shape=jax.ShapeDtypeStruct((SUBLANES, n_out), F32),
        compiler_params=_cparams("arbitrary"),
        name="ada",
    )(c_lanes, w_ada, b_ada.reshape(1, n_out))
    return out[:b]


def _inproj_kernel(x_ref, sh_ref, sc_ref, w_ref, cs_ref, o_ref, h_ref):
    @pl.when(pl.program_id(1) == 0)
    def _():
        h = _ln_rows(x_ref[...]) * (1.0 + sc_ref[0]) + sh_ref[0]
        h_ref[...] = h.astype(BF16)

    acc = jnp.dot(h_ref[...], w_ref[...], preferred_element_type=F32)
    o_ref[...] = (acc * cs_ref[...]).astype(BF16)


def _inproj(x2, mod3, w_in_bf, colscale, seq, tm=1024, tn=2304):
    n, d = x2.shape
    n_cols = w_in_bf.shape[1]
    tm = min(tm, seq)
    return pl.pallas_call(
        _inproj_kernel,
        grid=(n // tm, n_cols // tn),
        in_specs=[pl.BlockSpec((tm, d), lambda i, j: (i, 0)),
                  pl.BlockSpec((1, 1, d), lambda i, j: (((i * tm) // seq) * 6 + 0, 0, 0)),
                  pl.BlockSpec((1, 1, d), lambda i, j: (((i * tm) // seq) * 6 + 1, 0, 0)),
                  pl.BlockSpec((d, tn), lambda i, j: (0, j)),
                  pl.BlockSpec((1, tn), lambda i, j: (0, j))],
        out_specs=pl.BlockSpec((tm, tn), lambda i, j: (i, j)),
        out_shape=jax.ShapeDtypeStruct((n, n_cols), BF16),
        scratch_shapes=[pltpu.VMEM((tm, d), BF16)],
        compiler_params=_cparams("arbitrary", "arbitrary"),
        name="inproj",
    )(x2, mod3, mod3, w_in_bf, colscale)


Z2_MAX = 100.0
STICK_EXIT_LOG2 = 150.0


def _attn_kernel(q_ref, k_ref, v_ref, o_ref, *, tq, tk, heads):
    qi = pl.program_id(2)
    nd = tq // tk
    jd = qi * nd
    trow = lax.broadcasted_iota(I32, (tk, tk), 0)
    tcol = lax.broadcasted_iota(I32, (tk, tk), 1)
    tri = (trow > tcol).astype(BF16)
    hd = SB_HEAD_DIM
    qs = [q_ref[0, :, h * hd:(h + 1) * hd] for h in range(heads)]

    def block(j, runs, accs, causal):
        off = pl.multiple_of(j * tk, tk)
        hs = range(heads)
        z2s = [lax.dot_general(qs[h], k_ref[0, pl.ds(off, tk), h * hd:(h + 1) * hd],
                               (((1,), (1,)), ((), ())), preferred_element_type=F32) for h in hs]
        z2s = [lax.clamp(-Z2_MAX, z2, Z2_MAX) for z2 in z2s]
        sps = [jnp.log(1.0 + jnp.exp2(z2)) for z2 in z2s]
        if causal is not None:
            sps = [jnp.where(causal, sp, 0.0) for sp in sps]
        laters = [jnp.dot(sp.astype(BF16), tri, preferred_element_type=F32) for sp in sps]
        ws = [jnp.exp2(z2s[h] - LOG2E * (sps[h] + (laters[h] + runs[h]))) for h in hs]
        if causal is not None:
            ws = [jnp.where(causal, w, 0.0) for w in ws]
        new_accs = [accs[h] + jnp.dot(ws[h].astype(BF16),
                                      v_ref[0, pl.ds(off, tk), h * hd:(h + 1) * hd],
                                      preferred_element_type=F32) for h in hs]
        new_runs = [runs[h] + jnp.sum(sps[h], axis=1, keepdims=True) for h in hs]
        return tuple(new_runs), tuple(new_accs)

    def mass_left(runs):
        low = runs[0]
        for r in runs[1:]:
            low = jnp.minimum(low, r)
        return (jnp.min(low) * LOG2E < STICK_EXIT_LOG2).astype(I32)

    def diag_body(t, carry):
        c = nd - 1 - t
        row = lax.broadcasted_iota(I32, (tq, tk), 0)
        col = lax.broadcasted_iota(I32, (tq, tk), 1)
        causal = col + c * tk < row
        return block(jd + c, carry[0], carry[1], causal)

    def more(carry):
        t, go, _, _ = carry
        return (t < jd) & (go > 0)

    def body(carry):
        t, _, runs, accs = carry
        runs, accs = block(jd - 1 - t, runs, accs, None)
        return t + 1, mass_left(runs), runs, accs

    runs = tuple(jnp.zeros((tq, 1), F32) for _ in range(heads))
    accs = tuple(jnp.zeros((tq, hd), F32) for _ in range(heads))
    runs, accs = lax.fori_loop(0, nd, diag_body, (runs, accs))
    _, _, runs, accs = lax.while_loop(more, body, (jnp.int32(0), mass_left(runs), runs, accs))
    for h in range(heads):
        o_ref[0, :, h * hd:(h + 1) * hd] = accs[h].astype(BF16)


def _attention(proj3, d, tq=256, tk=256, heads=4):
    b, s, _ = proj3.shape
    tk = min(tk, s)
    tq = min(tq, s)
    assert tq % tk == 0 and s % tq == 0
    width = heads * SB_HEAD_DIM
    qc = _COL_Q // width
    kc = _COL_K // width
    vc = _COL_V // width
    return pl.pallas_call(
        functools.partial(_attn_kernel, tq=tq, tk=tk, heads=heads),
        grid=(b, SB_HEADS // heads, s // tq),
        in_specs=[pl.BlockSpec((1, tq, width), lambda bi, g, qi: (bi, qi, qc + g)),
                  pl.BlockSpec((1, s, width), lambda bi, g, qi: (bi, 0, kc + g)),
                  pl.BlockSpec((1, s, width), lambda bi, g, qi: (bi, 0, vc + g))],
        out_specs=pl.BlockSpec((1, tq, width), lambda bi, g, qi: (bi, qi, g)),
        out_shape=jax.ShapeDtypeStruct((b, s, SB_WIDTH), BF16),
        compiler_params=_cparams("arbitrary", "arbitrary", "arbitrary"),
        name="attn",
    )(proj3, proj3, proj3)


def _erf(x):
    return lax.erf(x)


def _gelu(x):
    return 0.5 * x * (1.0 + _erf(x * (1.0 / math.sqrt(2.0))))


def _mix_kernel(oa_ref, su_ref, sv_ref, ga0_ref, ga1_ref, gb0_ref, gb1_ref, x_ref,
                sw_ref, sbias_ref, sg_ref, sb_ref,
                wpa_ref, wpb_ref, wo_ref,
                g1_ref, ln1g_ref, ln1b_ref, sh2_ref, sc2_ref,
                wr_ref, br_ref,
                x1_ref, h2_ref, route_ref, oh1_ref, oh2_ref, *, tm, alpha):
    u = _gelu(su_ref[...].astype(F32))
    v = _gelu(sv_ref[...].astype(F32))
    v = _ln_rows(v) * sg_ref[...] + sb_ref[...]
    vb = v.astype(BF16)
    pos_t = lax.broadcasted_iota(I32, (SGU_BLOCK, SGU_BLOCK), 0)
    pos_s = lax.broadcasted_iota(I32, (SGU_BLOCK, SGU_BLOCK), 1)
    chunk_causal = (pos_s // CHUNK) <= (pos_t // CHUNK)
    n_pos_blocks = tm // SGU_BLOCK
    per_group = []
    for g in range(SGU_GROUPS):
        c0 = g * SGU_GROUP_DIM
        wg = jnp.where(chunk_causal, sw_ref[g], 0.0).astype(BF16)
        rhs = jnp.concatenate(
            [vb[nb * SGU_BLOCK:(nb + 1) * SGU_BLOCK, c0:c0 + SGU_GROUP_DIM]
             for nb in range(n_pos_blocks)], axis=1)
        per_group.append(jnp.dot(wg, rhs, preferred_element_type=F32))
    mixed = jnp.concatenate(
        [jnp.concatenate([per_group[g][:, nb * SGU_GROUP_DIM:(nb + 1) * SGU_GROUP_DIM]
                          for g in range(SGU_GROUPS)], axis=1) + sbias_ref[...]
         for nb in range(n_pos_blocks)], axis=0)
    out_b = (u * mixed).astype(BF16)

    pa = jnp.dot(oa_ref[...], wpa_ref[...], preferred_element_type=F32)
    pb = jnp.dot(out_b, wpb_ref[...], preferred_element_type=F32)
    ga = jnp.concatenate([ga0_ref[...], ga1_ref[...]], axis=1).astype(F32)
    gb = jnp.concatenate([gb0_ref[...], gb1_ref[...]], axis=1).astype(F32)
    merged = _sigmoid(ga) * pa + _sigmoid(gb) * pb
    y = jnp.dot(merged.astype(BF16), wo_ref[...], preferred_element_type=F32)

    r = alpha * x_ref[...] + g1_ref[0] * y
    x1 = _ln_rows(r) * ln1g_ref[...] + ln1b_ref[...]
    x1_ref[...] = x1
    h2 = _ln_rows(x1) * (1.0 + sc2_ref[0]) + sh2_ref[0]
    chunks = h2.shape[1] // LANES
    for c in range(chunks):
        h2_ref[pl.ds(c, tm, stride=chunks), :] = h2[:, c * LANES:(c + 1) * LANES]

    h_hi = h2.astype(BF16)
    h_lo = (h2 - h_hi.astype(F32)).astype(BF16)
    hi_w = jnp.dot(h_hi, wr_ref[...], preferred_element_type=F32)
    lo_w = jnp.dot(h_lo, wr_ref[:, :LANES], preferred_element_type=F32)
    logits = (hi_w[:, :LANES] + hi_w[:, LANES:]) + lo_w + br_ref[...]
    lane = lax.broadcasted_iota(I32, logits.shape, 1)
    gl = jnp.where(lane < N_GROUPS, logits, NEG_BIG)
    gmax = jnp.max(gl, axis=1, keepdims=True)
    g_sel = jnp.min(jnp.where(gl == gmax, lane, LANES), axis=1, keepdims=True)
    p_group = 1.0 / jnp.sum(jnp.exp(gl - gmax), axis=1, keepdims=True)
    lo_lane = EXPERT_LANE0 + g_sel * EXPERTS_PER_GROUP
    in_group = (lane >= lo_lane) & (lane < lo_lane + EXPERTS_PER_GROUP)
    el = jnp.where(in_group, logits, NEG_BIG)
    m1 = jnp.max(el, axis=1, keepdims=True)
    i1 = jnp.min(jnp.where(el == m1, lane, LANES), axis=1, keepdims=True)
    el2 = jnp.where(lane == i1, NEG_BIG, el)
    m2 = jnp.max(el2, axis=1, keepdims=True)
    i2 = jnp.min(jnp.where(el2 == m2, lane, LANES), axis=1, keepdims=True)
    e21 = jnp.exp(m2 - m1)
    w1 = p_group / (1.0 + e21)
    w2 = p_group * e21 / (1.0 + e21)
    route = jnp.where(lane == 0, w1, 0.0)
    route = jnp.where(lane == 1, w2, route)
    route = jnp.where(lane == 2, (i1 - EXPERT_LANE0).astype(F32), route)
    route = jnp.where(lane == 3, (i2 - EXPERT_LANE0).astype(F32), route)
    route_ref[...] = route
    oh1_ref[...] = jnp.where(lane == i1, 1.0, 0.0).astype(BF16)
    oh2_ref[...] = jnp.where(lane == i2, 1.0, 0.0).astype(BF16)


def _mix(out_a, proj, x2, mod3, sgu_w, sgu_bias_full, sgu_g, sgu_b, wpa, wpb, wo,
         ln1_g, ln1_b, wr, br, seq, alpha, tm=256):
    n, d = x2.shape
    tm = min(tm, seq)
    def pcol(start, width):
        assert start % width == 0
        return start // width
    su_c = pcol(_COL_SU, SGU_WIDTH)
    sv_c = pcol(_COL_SV, SGU_WIDTH)
    half = d // 2
    ga_c = pcol(_COL_GA, half)
    gb_c = pcol(_COL_GB(d), half)
    bidx = lambda i: (i * tm) // seq
    row = lambda i: (i, 0)
    const2 = lambda i: (0, 0)
    const3 = lambda i: (0, 0, 0)
    return pl.pallas_call(
        functools.partial(_mix_kernel, tm=tm, alpha=alpha),
        grid=(n // tm,),
        in_specs=[pl.BlockSpec((tm, SB_WIDTH), row),
                  pl.BlockSpec((tm, SGU_WIDTH), lambda i: (i, su_c)),
                  pl.BlockSpec((tm, SGU_WIDTH), lambda i: (i, sv_c)),
                  pl.BlockSpec((tm, half), lambda i: (i, ga_c)),
                  pl.BlockSpec((tm, half), lambda i: (i, ga_c + 1)),
                  pl.BlockSpec((tm, half), lambda i: (i, gb_c)),
                  pl.BlockSpec((tm, half), lambda i: (i, gb_c + 1)),
                  pl.BlockSpec((tm, d), row),
                  pl.BlockSpec((SGU_GROUPS, SGU_BLOCK, SGU_BLOCK), const3),
                  pl.BlockSpec((SGU_BLOCK, SGU_WIDTH), const2),
                  pl.BlockSpec((1, SGU_WIDTH), const2),
                  pl.BlockSpec((1, SGU_WIDTH), const2),
                  pl.BlockSpec((SB_WIDTH, d), const2),
                  pl.BlockSpec((SGU_WIDTH, d), const2),
                  pl.BlockSpec((d, d), const2),
                  pl.BlockSpec((1, 1, d), lambda i: (bidx(i) * 6 + 2, 0, 0)),
                  pl.BlockSpec((1, d), const2),
                  pl.BlockSpec((1, d), const2),
                  pl.BlockSpec((1, 1, d), lambda i: (bidx(i) * 6 + 3, 0, 0)),
                  pl.BlockSpec((1, 1, d), lambda i: (bidx(i) * 6 + 4, 0, 0)),
                  pl.BlockSpec((d, 2 * LANES), const2),
                  pl.BlockSpec((1, LANES), const2)],
        out_specs=[pl.BlockSpec((tm, d), row),
                   pl.BlockSpec((tm * (d // LANES), LANES), row),
                   pl.BlockSpec((tm, LANES), row),
                   pl.BlockSpec((tm, LANES), row),
                   pl.BlockSpec((tm, LANES), row)],
        out_shape=[jax.ShapeDtypeStruct((n, d), F32),
                   jax.ShapeDtypeStruct((n * (d // LANES), LANES), F32),
                   jax.ShapeDtypeStruct((n, LANES), F32),
                   jax.ShapeDtypeStruct((n, LANES), BF16),
                   jax.ShapeDtypeStruct((n, LANES), BF16)],
        compiler_params=_cparams("arbitrary"),
        name="mix",
    )(out_a, proj, proj, proj, proj, proj, proj, x2, sgu_w, sgu_bias_full, sgu_g, sgu_b,
      wpa, wpb, wo, mod3, ln1_g, ln1_b, mod3, mod3, wr, br)


def _plan_kernel(oh1_ref, oh2_ref, dest_ref, blk_ref, cnt_ref, start_ref, *, tm, bm, nblk_pad):
    phase = pl.program_id(0)
    i = pl.program_id(1)
    oh1 = oh1_ref[...]
    oh2 = oh2_ref[...]
    both = oh1 + oh2
    lane = lax.broadcasted_iota(I32, (SUBLANES, LANES), 1)

    @pl.when((phase == 0) & (i == 0))
    def _():
        cnt_ref[...] = jnp.zeros_like(cnt_ref)

    @pl.when(phase == 0)
    def _():
        ones = jnp.ones((SUBLANES, tm), BF16)
        cnt_ref[...] += jnp.dot(ones, both, preferred_element_type=F32)

    @pl.when((phase == 1) & (i == 0))
    def _():
        cnt = cnt_ref[...]
        padded = jnp.floor((cnt + (bm - 1)) * (1.0 / bm)) * bm
        r = lax.broadcasted_iota(I32, (LANES, LANES), 0)
        c = lax.broadcasted_iota(I32, (LANES, LANES), 1)
        upper = (r < c).astype(F32)
        starts = jnp.dot(padded, upper, preferred_element_type=F32,
                         precision=lax.Precision.HIGHEST)
        start_ref[...] = starts
        cnt_ref[...] = jnp.zeros_like(cnt_ref)
        ends = starts + padded
        is_exp = (lane[0:1] >= EXPERT_LANE0) & (lane[0:1] < EXPERT_LANE0 + N_EXPERTS)
        bstart = (lax.broadcasted_iota(I32, (nblk_pad, LANES), 0) * bm).astype(F32)
        done = jnp.where(is_exp & (ends[0:1] <= bstart), 1.0, 0.0)
        bexp = jnp.minimum(jnp.sum(done, axis=1, keepdims=True), N_EXPERTS - 1.0)
        total = jnp.sum(jnp.where(is_exp, padded[0:1], 0.0), axis=1, keepdims=True)
        blane = lax.broadcasted_iota(I32, (nblk_pad, LANES), 1)
        blk = jnp.where(blane == 0, bexp, jnp.where(blane == 1, total * (1.0 / bm), 0.0))
        blk_ref[...] = blk.astype(I32)

    @pl.when(phase == 1)
    def _():
        r = lax.broadcasted_iota(I32, (tm, tm), 0)
        c = lax.broadcasted_iota(I32, (tm, tm), 1)
        lower = (c < r).astype(BF16)
        before = jnp.dot(lower, both, preferred_element_type=F32) + cnt_ref[0:1]
        pos = before + start_ref[0:1]
        d1 = jnp.sum(pos * oh1.astype(F32), axis=1, keepdims=True)
        d2 = jnp.sum(pos * oh2.astype(F32), axis=1, keepdims=True)
        dlane = lax.broadcasted_iota(I32, (tm, LANES), 1)
        dest = jnp.where(dlane == 0, d1, jnp.where(dlane == 1, d2, 0.0))
        dest_ref[...] = dest.astype(I32)
        ones = jnp.ones((SUBLANES, tm), BF16)
        cnt_ref[...] += jnp.dot(ones, both, preferred_element_type=F32)


def _plan(oh1, oh2, bm, nblk, tm=1024):
    n = oh1.shape[0]
    tm = min(tm, n)
    nblk_pad = -(-nblk // SUBLANES) * SUBLANES
    dest, blk = pl.pallas_call(
        functools.partial(_plan_kernel, tm=tm, bm=bm, nblk_pad=nblk_pad),
        grid=(2, n // tm),
        in_specs=[pl.BlockSpec((tm, LANES), lambda p, i: (i, 0)),
                  pl.BlockSpec((tm, LANES), lambda p, i: (i, 0))],
        out_specs=[pl.BlockSpec((tm, LANES), lambda p, i: (i * p, 0)),
                   pl.BlockSpec((nblk_pad, LANES), lambda p, i: (0, 0))],
        out_shape=[jax.ShapeDtypeStruct((n, LANES), I32),
                   jax.ShapeDtypeStruct((nblk_pad, LANES), I32)],
        scratch_shapes=[pltpu.VMEM((SUBLANES, LANES), F32),
                        pltpu.VMEM((SUBLANES, LANES), F32)],
        compiler_params=_cparams("arbitrary", "arbitrary"),
        name="plan",
    )(oh1, oh2)
    return dest, blk


ROW_UNROLL = 8


def _invert_kernel(d1_ref, d2_ref, zero_ref, rt_ref, sem, *, tb):
    i = pl.program_id(0)

    @pl.when(i == 0)
    def _():
        clear = pltpu.make_async_copy(zero_ref, rt_ref, sem)
        clear.start()
        clear.wait()

    def put(j, carry):
        for u in range(ROW_UNROLL):
            t = j * ROW_UNROLL + u
            rt_ref[d1_ref[0, 0, t]] = i * tb + t
            rt_ref[d2_ref[0, 0, t]] = i * tb + t
        return carry

    lax.fori_loop(0, tb // ROW_UNROLL, put, 0)


def _invert(dest1, dest2, n_rows, tb=2048):
    n = dest1.shape[0]
    tb = min(tb, n)
    assert n % tb == 0 and tb % ROW_UNROLL == 0
    blocks = lambda i: (i, 0, 0)
    return pl.pallas_call(
        functools.partial(_invert_kernel, tb=tb),
        grid=(n // tb,),
        in_specs=[pl.BlockSpec((1, 1, tb), blocks, memory_space=pltpu.SMEM),
                  pl.BlockSpec((1, 1, tb), blocks, memory_space=pltpu.SMEM),
                  pl.BlockSpec(memory_space=pl.ANY)],
        out_specs=pl.BlockSpec(memory_space=pltpu.SMEM),
        out_shape=jax.ShapeDtypeStruct((n_rows,), I32),
        scratch_shapes=[pltpu.SemaphoreType.DMA(())],
        compiler_params=_cparams("arbitrary"),
        name="invert",
    )(dest1.reshape(n // tb, 1, tb), dest2.reshape(n // tb, 1, tb), jnp.zeros((n_rows,), I32))


def _slab(idx, chunks):
    return pl.ds(pl.multiple_of(idx * chunks, chunks), chunks)


def _rows_from_slabs(buf_ref, slot, rows, chunks):
    return jnp.concatenate(
        [buf_ref[slot, pl.ds(c, rows, stride=chunks), :] for c in range(chunks)], axis=1)


GATHER_AHEAD = 6


def _experts_kernel(be_ref, na_ref, first_ref, wslot_ref, nxt_ref, *refs, bm, chunks):
    head_refs = refs[:GATHER_AHEAD]
    (rtn_ref, h_ref, wg_hbm, wu_hbm, wd_hbm, o_ref,
     xbuf, sem, wgf, wuf, wdf, wsem, wgb, wub, wdb) = refs[GATHER_AHEAD:]
    i = pl.program_id(0)
    n_active = na_ref[0]

    def gather(tok_ref, slot):
        def trip(r8, carry):
            for u in range(ROW_UNROLL):
                r = r8 * ROW_UNROLL + u
                pltpu.make_async_copy(h_ref.at[_slab(tok_ref[0, 0, r], chunks), :],
                                      xbuf.at[slot, _slab(r, chunks), :], sem.at[slot]).start()
            return carry
        lax.fori_loop(0, bm // ROW_UNROLL, trip, 0)

    def weight_copies(e, ws):
        return (pltpu.make_async_copy(wg_hbm.at[e], wgf.at[ws], wsem.at[ws]),
                pltpu.make_async_copy(wu_hbm.at[e], wuf.at[ws], wsem.at[ws]),
                pltpu.make_async_copy(wd_hbm.at[e], wdf.at[ws], wsem.at[ws]))

    def fetch_weights(e, ws):
        for cp in weight_copies(e, ws):
            cp.start(priority=1)

    def gather_wait(slot):
        pltpu.make_async_copy(h_ref.at[pl.ds(0, bm * chunks), :], xbuf.at[slot],
                              sem.at[slot]).wait()

    @pl.when(i == 0)
    def _():
        fetch_weights(be_ref[0], wslot_ref[0])
        gather(head_refs[0], 0)
        for k in range(1, GATHER_AHEAD):
            pl.when(k < n_active)(functools.partial(gather, head_refs[k], k))

    @pl.when(i < n_active)
    def _():
        slot = lax.rem(i, GATHER_AHEAD + 1)
        gather_wait(slot)

        @pl.when(i + GATHER_AHEAD < n_active)
        def _():
            gather(rtn_ref, lax.rem(i + GATHER_AHEAD, GATHER_AHEAD + 1))

        @pl.when(first_ref[i] == 1)
        def _():
            ws = wslot_ref[i]
            for cp in weight_copies(be_ref[i], ws):
                cp.wait()

            @pl.when(nxt_ref[i] >= 0)
            def _():
                fetch_weights(nxt_ref[i], 1 - ws)

            wgb[...] = wgf[ws].astype(BF16)
            wub[...] = wuf[ws].astype(BF16)
            wdb[...] = wdf[ws].astype(BF16)

        x = _rows_from_slabs(xbuf, slot, bm, chunks).astype(BF16)
        g = jnp.dot(x, wgb[...], preferred_element_type=F32)
        u = jnp.dot(x, wub[...], preferred_element_type=F32)
        hb = (g * _sigmoid(g) * u).astype(BF16)
        y = jnp.dot(hb, wdb[...], preferred_element_type=F32)
        for c in range(chunks):
            o_ref[pl.ds(c, bm, stride=chunks), :] = y[:, c * LANES:(c + 1) * LANES]

    @pl.when(i >= n_active)
    def _():
        o_ref[...] = jnp.zeros_like(o_ref)


def _experts(block_expert, n_active, row_tok, h2s, w_gate, w_up, w_down, bm):
    n_rows = row_tok.shape[0]
    nblk = n_rows // bm
    d, de = w_gate.shape[1], w_gate.shape[2]
    chunks = d // LANES
    assert bm % ROW_UNROLL == 0
    assert nblk > GATHER_AHEAD
    row_blocks = row_tok.reshape(nblk, 1, bm)

    idx = jnp.arange(nblk, dtype=I32)
    prev = jnp.concatenate([block_expert[:1] - 1, block_expert[:-1]])
    first = ((block_expert != prev) & (idx < n_active[0])).astype(I32)
    wslot = ((jnp.cumsum(first) - 1) % 2).astype(I32)
    run_start = lax.cummin(jnp.where(first > 0, idx, nblk), axis=0, reverse=True)
    next_start = jnp.concatenate([run_start[1:], jnp.full((1,), nblk, I32)])
    nxt = jnp.where(next_start < nblk,
                    block_expert[jnp.minimum(next_start, nblk - 1)], -1).astype(I32)

    return pl.pallas_call(
        functools.partial(_experts_kernel, bm=bm, chunks=chunks),
        grid_spec=pltpu.PrefetchScalarGridSpec(
            num_scalar_prefetch=5,
            grid=(nblk,),
            in_specs=[pl.BlockSpec((1, 1, bm), functools.partial(lambda k, i, *_: (k, 0, 0), k),
                                   memory_space=pltpu.SMEM) for k in range(GATHER_AHEAD)]
                     + [pl.BlockSpec((1, 1, bm),
                                     lambda i, *_: (jnp.minimum(i + GATHER_AHEAD, nblk - 1), 0, 0),
                                     memory_space=pltpu.SMEM)]
                     + [pl.BlockSpec(memory_space=pl.ANY)] * 4,
            out_specs=pl.BlockSpec((bm * chunks, LANES), lambda i, *_: (i, 0)),
            scratch_shapes=[pltpu.VMEM((GATHER_AHEAD + 1, bm * chunks, LANES), F32),
                            pltpu.SemaphoreType.DMA((GATHER_AHEAD + 1,)),
                            pltpu.VMEM((2, d, de), F32),
                            pltpu.VMEM((2, d, de), F32),
                            pltpu.VMEM((2, de, d), F32),
                            pltpu.SemaphoreType.DMA((2,)),
                            pltpu.VMEM((d, de), BF16),
                            pltpu.VMEM((d, de), BF16),
                            pltpu.VMEM((de, d), BF16)],
        ),
        out_shape=jax.ShapeDtypeStruct((n_rows * chunks, LANES), F32),
        compiler_params=_cparams("arbitrary"),
        name="experts",
    )(block_expert, n_active, first, wslot, nxt, *([row_blocks] * (GATHER_AHEAD + 1)),
      h2s, w_gate, w_up, w_down)


COMBINE_BUFS = 3


def _combine_kernel(d1_ref, d2_ref, ys_ref, x1_ref, route_ref, g2_ref, lg_ref, lb_ref,
                    o_ref, y1buf, y2buf, sem, *, tm, chunks, alpha):
    i = pl.program_id(0)
    n_tiles = pl.num_programs(0)

    def row_copies(tile, t, slot):
        tok = tile * tm + t
        return (pltpu.make_async_copy(ys_ref.at[_slab(d1_ref[tok], chunks), :],
                                      y1buf.at[slot, _slab(t, chunks), :], sem.at[slot]),
                pltpu.make_async_copy(ys_ref.at[_slab(d2_ref[tok], chunks), :],
                                      y2buf.at[slot, _slab(t, chunks), :], sem.at[slot]))

    def gather(tile, slot):
        def trip(t8, carry):
            for u in range(ROW_UNROLL):
                for k, cp in enumerate(row_copies(tile, t8 * ROW_UNROLL + u, slot)):
                    cp.start(priority=k)
            return carry
        lax.fori_loop(0, tm // ROW_UNROLL, trip, 0)

    def gather_wait(slot):
        for buf in (y1buf, y2buf):
            pltpu.make_async_copy(ys_ref.at[pl.ds(0, tm * chunks), :], buf.at[slot],
                                  sem.at[slot]).wait()

    @pl.when(i == 0)
    def _():
        for k in range(COMBINE_BUFS - 1):
            pl.when(k < n_tiles)(functools.partial(gather, k, k))

    slot = lax.rem(i, COMBINE_BUFS)
    gather_wait(slot)

    @pl.when(i + COMBINE_BUFS - 1 < n_tiles)
    def _():
        gather(i + COMBINE_BUFS - 1, lax.rem(i + COMBINE_BUFS - 1, COMBINE_BUFS))

    route = route_ref[...]
    y = (route[:, 0:1] * _rows_from_slabs(y1buf, slot, tm, chunks)
         + route[:, 1:2] * _rows_from_slabs(y2buf, slot, tm, chunks))
    r = alpha * x1_ref[...] + g2_ref[0] * y
    o_ref[...] = _ln_rows(r) * lg_ref[...] + lb_ref[...]


def _combine(dest1, dest2, ys, x1, route, mod3, ln2_g, ln2_b, seq, alpha, tm=256):
    n, d = x1.shape
    tm = min(tm, seq)
    chunks = d // LANES
    assert tm % ROW_UNROLL == 0
    tok = lambda i, d1, d2: (i, 0)
    const = lambda i, d1, d2: (0, 0)
    return pl.pallas_call(
        functools.partial(_combine_kernel, tm=tm, chunks=chunks, alpha=alpha),
        grid_spec=pltpu.PrefetchScalarGridSpec(
            num_scalar_prefetch=2,
            grid=(n // tm,),
            in_specs=[pl.BlockSpec(memory_space=pl.ANY),
                      pl.BlockSpec((tm, d), tok),
                      pl.BlockSpec((tm, LANES), tok),
                      pl.BlockSpec((1, 1, d), lambda i, d1, d2: (((i * tm) // seq) * 6 + 5, 0, 0)),
                      pl.BlockSpec((1, d), const),
                      pl.BlockSpec((1, d), const)],
            out_specs=pl.BlockSpec((tm, d), tok),
            scratch_shapes=[pltpu.VMEM((COMBINE_BUFS, tm * chunks, LANES), F32),
                            pltpu.VMEM((COMBINE_BUFS, tm * chunks, LANES), F32),
                            pltpu.SemaphoreType.DMA((COMBINE_BUFS,))],
        ),
        out_shape=jax.ShapeDtypeStruct((n, d), F32),
        compiler_params=_cparams("arbitrary"),
        name="combine",
    )(dest1, dest2, ys, x1, route, mod3, ln2_g, ln2_b)


EXPERT_BLOCK_ROWS = 256


def kernel(x, c, w_ada, b_ada, w_in, sgu_w, sgu_b, sgu_ln_g, sgu_ln_b, w_proj_a, w_proj_b,
           w_out, ln1_g, ln1_b, w_group, b_group, w_router, b_router, w_gate, w_up, w_down,
           ln2_g, ln2_b):
    b, s, d = x.shape
    n = b * s
    depth = w_ada.shape[0]
    alpha = (2.0 * depth) ** 0.25
    sub = d // LANES
    bm = EXPERT_BLOCK_ROWS
    n_assign = 2 * n
    n_rows = n_assign + N_EXPERTS * bm
    nblk = n_rows // bm

    in_cols = w_in.shape[2]
    colscale = jnp.ones((1, in_cols), F32).at[:, _COL_Q:_COL_K].set(
        SB_HEAD_DIM ** -0.5 * LOG2E)

    for l in range(depth):
        mod = _ada(c, w_ada[l], b_ada[l])
        mod3 = mod.reshape(b * 6, 1, d)

        x2 = x.reshape(n, d)
        proj = _inproj(x2, mod3, w_in[l].astype(BF16), colscale, s)
        out_a = _attention(proj.reshape(b, s, in_cols), d).reshape(n, SB_WIDTH)

        wr = jnp.zeros((d, LANES), F32)
        wr = wr.at[:, GROUP_LANE0:GROUP_LANE0 + N_GROUPS].set(w_group[l])
        wr = wr.at[:, EXPERT_LANE0:EXPERT_LANE0 + N_EXPERTS].set(w_router[l])
        wr_hi = wr.astype(BF16)
        wr_lo = (wr - wr_hi.astype(F32)).astype(BF16)
        wr = jnp.concatenate([wr_hi, wr_lo], axis=1)
        br = jnp.zeros((1, LANES), F32)
        br = br.at[0, GROUP_LANE0:GROUP_LANE0 + N_GROUPS].set(b_group[l])
        br = br.at[0, EXPERT_LANE0:EXPERT_LANE0 + N_EXPERTS].set(b_router[l])
        sgu_bias_full = jnp.repeat(sgu_b[l].T, SGU_GROUP_DIM, axis=1)

        x1, h2s, route, oh1, oh2 = _mix(
            out_a, proj, x2, mod3, sgu_w[l], sgu_bias_full,
            sgu_ln_g[l].reshape(1, -1), sgu_ln_b[l].reshape(1, -1),
            w_proj_a[l].astype(BF16), w_proj_b[l].astype(BF16), w_out[l].astype(BF16),
            ln1_g[l].reshape(1, d), ln1_b[l].reshape(1, d), wr, br, s, alpha)

        dest, blk = _plan(oh1, oh2, bm, nblk)
        dest1 = dest[:, 0]
        dest2 = dest[:, 1]
        block_expert = blk[:nblk, 0]
        n_active = blk[0:1, 1]

        row_tok = _invert(dest1, dest2, n_rows)
        ys = _experts(block_expert, n_active, row_tok, h2s, w_gate[l], w_up[l], w_down[l], bm)
        out = _combine(dest1, dest2, ys, x1, route, mod3, ln2_g[l].reshape(1, d),
                       ln2_b[l].reshape(1, d), s, alpha)
        x = out.reshape(b, s, d)
    return x
```
